```python
import jax
import jax.numpy as jnp
from jax import lax
import numpy as np

D_MODEL = 1024
BATCH = 8
SEQ = 4096
DEPTH = 2
DEC_BATCH = 128
DEC_SEQ = 4
PAST_LEN = 16384
PAGE_SIZE = 128

MIX_WIDTH = D_MODEL
GROUP_WIDTH = MIX_WIDTH // 4
HEAD_DIM = 64
A_HEADS = GROUP_WIDTH // HEAD_DIM
A_CONV = 4
A_CHUNK = 64
B_CONV = 3
C_HEADS = GROUP_WIDTH // HEAD_DIM
C_KV_HEADS = 2
WINDOW = 128
ROPE_DIM = HEAD_DIM // 4
ROPE_THETA = 500000.0
D_GROUPS = 4
D_CHUNK = 128
D_FF = 2816
FFN_CONV = 3
EPS = 1e-6
NEG_INF = -1e30
SPLIT_SIZES = (3 * GROUP_WIDTH, GROUP_WIDTH, A_HEADS, A_HEADS, GROUP_WIDTH, GROUP_WIDTH, GROUP_WIDTH, C_HEADS * HEAD_DIM, C_KV_HEADS * HEAD_DIM, C_KV_HEADS * HEAD_DIM, GROUP_WIDTH, GROUP_WIDTH)
P_TOTAL = sum(SPLIT_SIZES)

kernel_name = 'hybrid_parallel_groups_decode_step'


def rmsnorm(x, g):
    xf = x.astype(jnp.float32)
    y = xf * lax.rsqrt(jnp.mean(xf * xf, axis=-1, keepdims=True) + EPS)
    return (y * g.astype(jnp.float32)).astype(x.dtype)


def layernorm(x, g, b):
    xf = x.astype(jnp.float32)
    mu = jnp.mean(xf, axis=-1, keepdims=True)
    xc = xf - mu
    y = xc * lax.rsqrt(jnp.mean(xc * xc, axis=-1, keepdims=True) + EPS)
    return (y * g.astype(jnp.float32) + b.astype(jnp.float32)).astype(x.dtype)


def l2norm(x):
    xf = x.astype(jnp.float32)
    return xf * lax.rsqrt(jnp.sum(xf * xf, axis=-1, keepdims=True) + EPS)


def causal_dwconv(buf, x, w):
    K = w.shape[0]
    t = x.shape[1]
    xp = jnp.concatenate([buf.astype(x.dtype), x], axis=1)
    y = xp[:, 0:t] * w[0]
    for j in range(1, K):
        y = y + xp[:, j:j + t] * w[j]
    return y, xp[:, t:]


def partial_rope(x, pos):
    half = ROPE_DIM // 2
    inv = jnp.power(ROPE_THETA, -jnp.arange(half, dtype=jnp.float32) * (2.0 / ROPE_DIM))
    ang = pos.astype(jnp.float32)[:, None] * inv[None, :]
    cos = jnp.cos(ang)[None, :, None, :]
    sin = jnp.sin(ang)[None, :, None, :]
    xr = x[..., :ROPE_DIM].astype(jnp.float32)
    x1, x2 = xr[..., :half], xr[..., half:]
    rot = jnp.concatenate([x1 * cos - x2 * sin, x2 * cos + x1 * sin], axis=-1).astype(x.dtype)
    return jnp.concatenate([rot, x[..., ROPE_DIM:]], axis=-1)


def sink_softmax(s, sinks):
    sk = jnp.broadcast_to(sinks.astype(jnp.float32)[..., None, None], s.shape[:-1] + (1,))
    return jax.nn.softmax(jnp.concatenate([s, sk], axis=-1), axis=-1)[..., :-1]


def window_attn_banded(q, k, v, sinks):
    bsz, t, hq, hd = q.shape
    hkv = k.shape[2]
    nb = t // WINDOW
    qb = q.reshape(bsz, nb, WINDOW, hkv, hq // hkv, hd)

    def band(a):
        ab = a.reshape(bsz, nb, WINDOW, hkv, hd)
        prev = jnp.concatenate([jnp.zeros_like(ab[:, :1]), ab[:, :-1]], axis=1)
        return jnp.concatenate([prev, ab], axis=2)

    kb, vb = band(k), band(v)
    i = jnp.arange(WINDOW)[:, None]
    j = jnp.arange(2 * WINDOW)[None, :]
    blk = jnp.arange(nb)[:, None, None]
    mask = (j > i) & (j <= i + WINDOW) & (blk * WINDOW + j - WINDOW >= 0)
    s = jnp.einsum('bnqhgd,bnkhd->bnhgqk', qb, kb).astype(jnp.float32) * (hd ** -0.5)
    s = jnp.where(mask[None, :, None, None], s, NEG_INF)
    pr = sink_softmax(s, sinks.reshape(hkv, hq // hkv))
    o = jnp.einsum('bnhgqk,bnkhd->bnqhgd', pr.astype(vb.dtype), vb)
    return o.reshape(bsz, t, hq * hd)


def window_attn_cached(q, k, v, buf_k, buf_v, q_pos, k_pos, sinks):
    bsz, t, hq, hd = q.shape
    hkv = k.shape[2]
    kk = jnp.concatenate([buf_k.astype(k.dtype), k], axis=1)
    vv = jnp.concatenate([buf_v.astype(v.dtype), v], axis=1)
    qg = q.reshape(bsz, t, hkv, hq // hkv, hd)
    s = jnp.einsum('bqhgd,bkhd->bhgqk', qg, kk).astype(jnp.float32) * (hd ** -0.5)
    dpos = q_pos[:, None] - k_pos[None, :]
    s = jnp.where((dpos >= 0) & (dpos < WINDOW), s, NEG_INF)
    pr = sink_softmax(s, sinks.reshape(hkv, hq // hkv))
    o = jnp.einsum('bhgqk,bkhd->bqhgd', pr.astype(vv.dtype), vv).reshape(bsz, t, hq * hd)
    nbuf = buf_k.shape[1]
    return o, kk[:, -nbuf:], vv[:, -nbuf:]


def gated_delta_rule(q, k, v, g, beta, s0):
    f32 = jnp.float32
    bsz, t, h, dk = q.shape
    dv = v.shape[-1]
    c = A_CHUNK if t % A_CHUNK == 0 else t
    n = t // c

    def chunks(a):
        a = a.astype(f32).reshape((bsz, n, c) + a.shape[2:])
        return jnp.moveaxis(jnp.moveaxis(a, 1, 0), 2, 3)

    qc = chunks(q) * (dk ** -0.5)
    kc = chunks(k)
    vc = chunks(v)
    bc = chunks(beta)
    gc = jnp.cumsum(chunks(g), axis=-1)
    causal = jnp.tril(jnp.ones((c, c), bool))
    strict = jnp.tril(jnp.ones((c, c), bool), -1)
    diff = gc[..., :, None] - gc[..., None, :]
    decay = jnp.where(causal, jnp.exp(jnp.where(causal, diff, 0.0)), 0.0)
    kbeta = kc * bc[..., None]
    a_mat = jnp.where(strict, jnp.einsum('nbhcd,nbhsd->nbhcs', kbeta, kc) * decay, 0.0)
    eye = jnp.eye(c, dtype=f32)
    t_inv = lax.linalg.triangular_solve(a_mat + eye, jnp.broadcast_to(eye, a_mat.shape), left_side=True, lower=True)
    u = t_inv @ (vc * bc[..., None])
    w = t_inv @ (kbeta * jnp.exp(gc)[..., None])
    qk = jnp.einsum('nbhcd,nbhsd->nbhcs', qc, kc) * decay
    q_dec = qc * jnp.exp(gc)[..., None]
    g_last = gc[..., -1]
    k_dec = kc * jnp.exp(g_last[..., None] - gc)[..., None]

    def step(s, xs):
        q_i, k_i, u_i, w_i, qk_i, gl_i = xs
        v_new = u_i - w_i @ s
        o_i = q_i @ s + qk_i @ v_new
        s = s * jnp.exp(gl_i)[..., None, None] + jnp.swapaxes(k_i, -1, -2) @ v_new
        return s, o_i

    s_fin, o = lax.scan(step, s0.astype(f32), (q_dec, k_dec, u, w, qk, g_last))
    o = jnp.moveaxis(jnp.moveaxis(o, 3, 2), 0, 1).reshape(bsz, t, h, dv)
    return o.astype(v.dtype), s_fin


def chunk_token_mlp(u, v, ws, bias):
    bsz, t, width = v.shape
    n = min(t, D_CHUNK)
    nc = t // n
    wm = jnp.where(jnp.tril(jnp.ones((n, n), bool)), ws[:, :n, :n], 0.0).astype(v.dtype)
    vc = v.reshape(bsz, nc, n, D_GROUPS, width // D_GROUPS)
    mixed = jnp.einsum('gts,bcsgd->bctgd', wm, vc) + bias[:, :n].T[None, None, :, :, None].astype(v.dtype)
    return u * mixed.reshape(bsz, t, width)


def hybrid_layer(x, pos, k_pos, st, p):
    bsz, t, _ = x.shape
    f32 = jnp.float32
    h = rmsnorm(x, p['norm1_g'])
    proj = h @ p['w_in']
    (a_qkv, a_z, a_b, a_a, b_b, b_c, b_h, c_q, c_k, c_v, d_u, d_v) = jnp.split(proj, [int(s) for s in np.cumsum(SPLIT_SIZES)[:-1]], axis=-1)
    qkv, a_conv_new = causal_dwconv(st['a_conv'], a_qkv, p['a_conv_w'])
    qkv = jax.nn.silu(qkv).reshape(bsz, t, 3, A_HEADS, HEAD_DIM)
    q, k, v = l2norm(qkv[:, :, 0]), l2norm(qkv[:, :, 1]), qkv[:, :, 2]
    beta = jax.nn.sigmoid(a_b.astype(f32))
    g = -jnp.exp(p['a_log'].astype(f32)) * jax.nn.softplus(a_a.astype(f32) + p['a_dt_bias'].astype(f32))
    o, a_state_new = gated_delta_rule(q, k, v, g, beta, st['a_state'])
    o = rmsnorm(o, p['a_norm_g']) * jax.nn.silu(a_z.reshape(bsz, t, A_HEADS, HEAD_DIM))
    out_a = o.reshape(bsz, t, GROUP_WIDTH)
    bx, b_conv_new = causal_dwconv(st['b_conv'], b_c * b_h, p['b_conv_w'])
    out_b = b_b * bx
    cq = partial_rope(c_q.reshape(bsz, t, C_HEADS, HEAD_DIM), pos)
    ck = partial_rope(c_k.reshape(bsz, t, C_KV_HEADS, HEAD_DIM), pos)
    cv = c_v.reshape(bsz, t, C_KV_HEADS, HEAD_DIM)
    if k_pos is None:
        out_c = window_attn_banded(cq, ck, cv, p['c_sinks'])
        c_k_new, c_v_new = ck[:, -WINDOW:], cv[:, -WINDOW:]
    else:
        out_c, c_k_new, c_v_new = window_attn_cached(cq, ck, cv, st['c_k'], st['c_v'], pos, k_pos, p['c_sinks'])
    du = jax.nn.gelu(d_u)
    dvn = layernorm(jax.nn.gelu(d_v), p['d_ln_g'], p['d_ln_b'])
    out_d = chunk_token_mlp(du, dvn, p['d_ws'], p['d_bias'])
    x = x + jnp.concatenate([out_a, out_b, out_c, out_d], axis=-1) @ p['w_out']
    h2 = rmsnorm(x, p['norm2_g'])
    gate, f_conv_new = causal_dwconv(st['f_conv'], h2 @ p['ffn_w_gate'], p['ffn_conv_w'])
    x = x + (jax.nn.silu(gate) * (h2 @ p['ffn_w_up'])) @ p['ffn_w_down']
    new = {'a_state': a_state_new, 'a_conv': a_conv_new, 'b_conv': b_conv_new, 'c_k': c_k_new, 'c_v': c_v_new, 'f_conv': f_conv_new, 'd_v': dvn}
    return x, new


def setup_inputs(seed: int = 0) -> dict:
    key = jax.random.key(seed)
    ks = iter(jax.random.split(key, 40))
    f32 = jnp.float32

    def nrm(shape, scale):
        return jax.random.normal(next(ks), shape, f32) * scale

    win_buf = min(WINDOW, PAST_LEN)
    a_log = jnp.log(jax.random.uniform(next(ks), (DEPTH, A_HEADS), f32, minval=1.0, maxval=16.0))
    dt = jnp.exp(jax.random.uniform(next(ks), (DEPTH, A_HEADS), f32, minval=float(np.log(1e-3)), maxval=float(np.log(1e-1))))
    a_dt_bias = dt + jnp.log(-jnp.expm1(-dt))
    return {
        'x_prompt': nrm((BATCH, SEQ, D_MODEL), 1.0),
        'x_sample': nrm((DEC_BATCH, DEC_SEQ, D_MODEL), 1.0),
        'state_delta': nrm((DEPTH, DEC_BATCH, A_HEADS, HEAD_DIM, HEAD_DIM), 0.3),
        'state_delta_conv': nrm((DEPTH, DEC_BATCH, A_CONV - 1, 3 * GROUP_WIDTH), 1.0),
        'state_shortconv': nrm((DEPTH, DEC_BATCH, B_CONV - 1, GROUP_WIDTH), 1.0),
        'cache_win_k': nrm((DEPTH, DEC_BATCH, win_buf, C_KV_HEADS, HEAD_DIM), 1.0),
        'cache_win_v': nrm((DEPTH, DEC_BATCH, win_buf, C_KV_HEADS, HEAD_DIM), 1.0),
        'state_ffn_conv': nrm((DEPTH, DEC_BATCH, FFN_CONV - 1, D_FF), 1.0),
        'norm1_g': 1.0 + nrm((DEPTH, D_MODEL), 0.1),
        'w_in': nrm((DEPTH, D_MODEL, P_TOTAL), D_MODEL ** -0.5),
        'a_conv_w': nrm((DEPTH, A_CONV, 3 * GROUP_WIDTH), A_CONV ** -0.5),
        'a_log': a_log,
        'a_dt_bias': a_dt_bias,
        'a_norm_g': 1.0 + nrm((DEPTH, HEAD_DIM), 0.1),
        'b_conv_w': nrm((DEPTH, B_CONV, GROUP_WIDTH), B_CONV ** -0.5),
        'c_sinks': nrm((DEPTH, C_HEADS), 1.0),
        'd_ln_g': 1.0 + nrm((DEPTH, GROUP_WIDTH), 0.1),
        'd_ln_b': nrm((DEPTH, GROUP_WIDTH), 0.1),
        'd_ws': nrm((DEPTH, D_GROUPS, D_CHUNK, D_CHUNK), D_CHUNK ** -0.5),
        'd_bias': 1.0 + nrm((DEPTH, D_GROUPS, D_CHUNK), 0.1),
        'w_out': nrm((DEPTH, MIX_WIDTH, D_MODEL), MIX_WIDTH ** -0.5),
        'norm2_g': 1.0 + nrm((DEPTH, D_MODEL), 0.1),
        'ffn_w_gate': nrm((DEPTH, D_MODEL, D_FF), D_MODEL ** -0.5),
        'ffn_w_up': nrm((DEPTH, D_MODEL, D_FF), D_MODEL ** -0.5),
        'ffn_conv_w': nrm((DEPTH, FFN_CONV, D_FF), FFN_CONV ** -0.5),
        'ffn_w_down': nrm((DEPTH, D_FF, D_MODEL), D_FF ** -0.5),
        'final_norm_g': 1.0 + nrm((D_MODEL,), 0.1),
    }


def reference(x_prompt, x_sample, state_delta, state_delta_conv, state_shortconv, cache_win_k, cache_win_v, state_ffn_conv, norm1_g, w_in, a_conv_w, a_log, a_dt_bias, a_norm_g, b_conv_w, c_sinks, d_ln_g, d_ln_b, d_ws, d_bias, w_out, norm2_g, ffn_w_gate, ffn_w_up, ffn_conv_w, ffn_w_down, final_norm_g):
    bp, tp, _ = x_prompt.shape
    bs, ts, _ = x_sample.shape
    dt = x_prompt.dtype
    win_buf = cache_win_k.shape[2]
    pos_p = jnp.arange(tp, dtype=jnp.int32)
    pos_s = PAST_LEN + jnp.arange(ts, dtype=jnp.int32)
    k_pos_s = PAST_LEN - win_buf + jnp.arange(win_buf + ts, dtype=jnp.int32)
    names = ('a_state', 'a_conv', 'b_conv', 'c_k', 'c_v', 'f_conv')
    new_p = {n: [] for n in names}
    new_s = {n: [] for n in names}
    chunk_v_rows = []
    hp, hs = x_prompt, x_sample
    for l in range(DEPTH):
        p = {'norm1_g': norm1_g[l], 'w_in': w_in[l], 'a_conv_w': a_conv_w[l], 'a_log': a_log[l], 'a_dt_bias': a_dt_bias[l], 'a_norm_g': a_norm_g[l], 'b_conv_w': b_conv_w[l], 'c_sinks': c_sinks[l], 'd_ln_g': d_ln_g[l], 'd_ln_b': d_ln_b[l], 'd_ws': d_ws[l], 'd_bias': d_bias[l], 'w_out': w_out[l], 'norm2_g': norm2_g[l], 'ffn_w_gate': ffn_w_gate[l], 'ffn_w_up': ffn_w_up[l], 'ffn_conv_w': ffn_conv_w[l], 'ffn_w_down': ffn_w_down[l]}
        st_p = {'a_state': jnp.zeros((bp, A_HEADS, HEAD_DIM, HEAD_DIM), jnp.float32), 'a_conv': jnp.zeros((bp, A_CONV - 1, 3 * GROUP_WIDTH), dt), 'b_conv': jnp.zeros((bp, B_CONV - 1, GROUP_WIDTH), dt), 'f_conv': jnp.zeros((bp, FFN_CONV - 1, D_FF), dt)}
        hp, np_ = hybrid_layer(hp, pos_p, None, st_p, p)
        st_s = {'a_state': state_delta[l], 'a_conv': state_delta_conv[l], 'b_conv': state_shortconv[l], 'c_k': cache_win_k[l], 'c_v': cache_win_v[l], 'f_conv': state_ffn_conv[l]}
        hs, ns_ = hybrid_layer(hs, pos_s, k_pos_s, st_s, p)
        for n in names:
            new_p[n].append(np_[n])
            new_s[n].append(ns_[n])
        chunk_v_rows.append(ns_['d_v'])
    y_prompt = rmsnorm(hp, final_norm_g)
    y_sample = rmsnorm(hs, final_norm_g)
    sp = {n: jnp.stack(new_p[n]) for n in names}
    ss = {n: jnp.stack(new_s[n]) for n in names}
    chunk_v_s = jnp.stack(chunk_v_rows)
    return (y_prompt, y_sample, sp['a_state'], ss['a_state'], sp['a_conv'], ss['a_conv'], sp['b_conv'], ss['b_conv'], sp['c_k'], ss['c_k'], sp['c_v'], ss['c_v'], sp['f_conv'], ss['f_conv'], chunk_v_s)
```

```python
import functools

import jax
import jax.numpy as jnp
import numpy as np
from jax import lax
from jax.experimental import pallas as pl
from jax.experimental.pallas import tpu as pltpu

F32 = jnp.float32
BF16 = jnp.bfloat16

D_MODEL = 1024
GROUP_WIDTH = 256
HEAD_DIM = 64
A_HEADS = 4
A_CONV = 4
A_CHUNK = 64
B_CONV = 3
C_HEADS = 4
C_KV_HEADS = 2
WINDOW = 128
ROPE_DIM = 16
ROPE_THETA = 500000.0
D_GROUPS = 4
D_CHUNK = 128
D_FF = 2816
FFN_CONV = 3
EPS = 1e-6
NEG_INF = -1e30
PAST_LEN = 16384

COL_A = 0
W_A = 1152
COL_B = 1152
W_B = 768
COL_C = 1920
W_C = 512
COL_D = 2432
W_D = 512
P_PACKED = 2944

TILE_ROWS = 512
CARRY_ROWS = 8
FFN_COL_CHUNK = 1408
VMEM_LIMIT_BYTES = 56 * 1024 * 1024


def _dot(a, b):
    return jnp.dot(a, b, preferred_element_type=F32)


def _dot_nt(a, b):
    return lax.dot_general(a, b, (((1,), (1,)), ((), ())), preferred_element_type=F32)


def _dot_tn(a, b):
    return lax.dot_general(a, b, (((0,), (0,)), ((), ())), preferred_element_type=F32)


def _split3(x):
    hi = x.astype(BF16)
    r1 = x - hi.astype(F32)
    mid = r1.astype(BF16)
    lo = (r1 - mid.astype(F32)).astype(BF16)
    return hi, mid, lo


def _dot_lhs3(x, w01):
    hi, mid, lo = _split3(x)
    return _dot(hi, w01) + _dot(mid, w01) + _dot(lo, w01)


def _dot_rhs3(w01, x):
    hi, mid, lo = _split3(x)
    return _dot(w01, hi) + _dot(w01, mid) + _dot(w01, lo)


def _sigmoid(x):
    return 1.0 / (1.0 + jnp.exp(-x))


def _silu(x):
    return x * _sigmoid(x)


def _softplus(x):
    return jnp.maximum(x, 0.0) + jnp.log(1.0 + jnp.exp(-jnp.abs(x)))


def _gelu_tanh(x):
    return 0.5 * x * (1.0 + jnp.tanh(np.sqrt(2.0 / np.pi).astype(np.float32) * (x + 0.044715 * (x * x * x))))


def _rms(x, g):
    return x * lax.rsqrt(jnp.mean(x * x, axis=-1, keepdims=True) + EPS) * g


def _iota(shape, dim):
    return lax.broadcasted_iota(jnp.int32, shape, dim)


def _block_diag(x, mask_bd):
    return jnp.where(mask_bd, jnp.concatenate([x, x, x, x], axis=0), 0.0)


def _mixer_kernel(sinks_ref, x_ref, n1g_ref, w_in_ref, aconv_w_ref, alog_ref, dtb_ref, anorm_ref,
                  bconv_w_ref, cos_ref, sina_ref, sinb_ref, lng_ref, lnb_ref, ws_ref, dbias_ref,
                  w_out_ref, aconv0_ref, bconv0_ref, s0_ref,
                  x_out_ref, s_out_ref, aconv_out_ref, bconv_out_ref, ck_out_ref, cv_out_ref,
                  abuf, bbuf, kbuf, vbuf, q_s, k_s, v_s, gb_s, bb_s, o_s, s_scr, *, tile):
    T = tile
    si = pl.program_id(1)
    P = CARRY_ROWS

    @pl.when(si == 0)
    def _():
        abuf[0:P, :] = aconv0_ref[0]
        bbuf[0:P, :] = bconv0_ref[0]
        kbuf[0:WINDOW, :] = jnp.zeros((WINDOW, 128), F32)
        vbuf[0:WINDOW, :] = jnp.zeros((WINDOW, 128), F32)
        s_scr[...] = s0_ref[0]

    x = x_ref[0]
    h = _rms(x, n1g_ref[...]).astype(BF16)

    pa = _dot(h, w_in_ref[:, COL_A:COL_A + W_A])
    z = pa[:, 768:1024]
    small = pa[:, 1024:1152]
    abuf[P:P + T, :] = pa[:, 0:768]
    y = (abuf[P:P + T, :] * aconv_w_ref[3:4, :] + abuf[P - 1:P - 1 + T, :] * aconv_w_ref[2:3, :]
         + abuf[P - 2:P - 2 + T, :] * aconv_w_ref[1:2, :] + abuf[P - 3:P - 3 + T, :] * aconv_w_ref[0:1, :])
    tail_a = abuf[T:T + P, :]
    abuf[0:P, :] = tail_a
    aconv_out_ref[0] = tail_a
    qkv = _silu(y)
    q_raw = qkv[:, 0:256]
    k_raw = qkv[:, 256:512]
    v_s[...] = qkv[:, 512:768]

    r256 = _iota((256, 256), 0) >> 6
    c256 = _iota((256, 256), 1) >> 6
    mask_bd = r256 == c256
    ones_bd = jnp.where(mask_bd, 1.0, 0.0).astype(BF16)
    q_s[...] = q_raw * lax.rsqrt(_dot_lhs3(q_raw * q_raw, ones_bd) + EPS) * (HEAD_DIM ** -0.5)
    k_s[...] = k_raw * lax.rsqrt(_dot_lhs3(k_raw * k_raw, ones_bd) + EPS)

    g_log = -jnp.exp(alog_ref[...]) * _softplus(small + dtb_ref[...])
    beta = _sigmoid(small)
    er = _iota((128, 256), 0)
    ec = _iota((128, 256), 1) >> 6
    expand_g = jnp.where(er == ec, 1.0, 0.0).astype(BF16)
    expand_b = jnp.where(er == ec + A_HEADS, 1.0, 0.0).astype(BF16)
    gb_s[...] = _dot_lhs3(g_log, expand_g)
    bb_s[...] = _dot_lhs3(beta, expand_b)

    ri = _iota((A_CHUNK, 256), 0)
    ci = _iota((A_CHUNK, 256), 1) & (A_CHUNK - 1)
    causal_t = ri >= ci
    strict_t = ri > ci
    eye_t = jnp.where(ri == ci, 1.0, 0.0)
    ltri = jnp.where(_iota((A_CHUNK, A_CHUNK), 0) >= _iota((A_CHUNK, A_CHUNK), 1), 1.0, 0.0).astype(BF16)

    def prod(a_row, b_row):
        return _dot(a_row.astype(BF16), _block_diag(b_row, mask_bd).astype(BF16))

    def chunk_body(c, carry):
        rows = pl.ds(pl.multiple_of(c * A_CHUNK, A_CHUNK), A_CHUNK)
        qc = q_s[rows, :]
        kc = k_s[rows, :]
        vc = v_s[rows, :]
        gbc = gb_s[rows, :]
        bbc = bb_s[rows, :]
        cum = _dot_rhs3(ltri, jnp.concatenate([gbc, jnp.where(strict_t, gbc, 0.0)], axis=1))
        gcb = cum[:, 0:256]
        decay = jnp.where(causal_t, jnp.exp(cum[:, 256:512]), 0.0)
        eg = jnp.exp(gcb)
        kb = kc * bbc
        k_bd = _block_diag(kc, mask_bd).astype(BF16)
        aq = _dot_nt(jnp.concatenate([kb, qc], axis=0).astype(BF16), k_bd)
        a_mat = jnp.where(strict_t, aq[0:A_CHUNK] * decay, 0.0)
        qk = aq[A_CHUNK:2 * A_CHUNK] * decay
        t_inv = eye_t - a_mat
        pw = a_mat
        for _ in range(5):
            pw = prod(pw, pw)
            t_inv = t_inv + prod(t_inv, pw)
        rhs = jnp.concatenate([_block_diag(vc * bbc, mask_bd), _block_diag(kb * eg, mask_bd)], axis=1)
        uw = _dot(t_inv.astype(BF16), rhs.astype(BF16))
        u = uw[:, 0:256]
        w = uw[:, 256:512]
        s_bd = s_scr[...]
        wq = _dot(jnp.concatenate([w, qc * eg], axis=0).astype(BF16), s_bd.astype(BF16))
        v_new = u - wq[0:A_CHUNK]
        o_s[rows, :] = wq[A_CHUNK:2 * A_CHUNK] + prod(qk, v_new)
        g_last = gcb[A_CHUNK - 1:A_CHUNK, :]
        kv = _dot_tn(kc.astype(BF16), (v_new * jnp.exp(g_last - gcb)).astype(BF16))
        s_scr[...] = s_bd * jnp.exp(g_last) + jnp.where(mask_bd, kv, 0.0)
        return carry

    lax.fori_loop(0, T // A_CHUNK, chunk_body, 0)
    s_out_ref[0] = s_scr[...]
    o = o_s[...]
    out_a = (o * lax.rsqrt(_dot_lhs3(o * o, ones_bd) * (1.0 / HEAD_DIM) + EPS) * anorm_ref[...]
             * _silu(z))

    pb = _dot(h, w_in_ref[:, COL_B:COL_B + W_B])
    bbuf[P:P + T, :] = pb[:, 256:512] * pb[:, 512:768]
    bx = (bbuf[P:P + T, :] * bconv_w_ref[2:3, :] + bbuf[P - 1:P - 1 + T, :] * bconv_w_ref[1:2, :]
          + bbuf[P - 2:P - 2 + T, :] * bconv_w_ref[0:1, :])
    tail_b = bbuf[T:T + P, :]
    bbuf[0:P, :] = tail_b
    bconv_out_ref[0] = tail_b
    out_b = pb[:, 0:256] * bx

    pc = _dot(h, w_in_ref[:, COL_C:COL_C + W_C])
    cos = cos_ref[...]
    sina = sina_ref[...]
    sinb = sinb_ref[...]
    cq = pc[:, 0:256]
    cq = (cq * jnp.concatenate([cos, cos], axis=1)
          + pltpu.roll(cq, 256 - ROPE_DIM // 2, 1) * jnp.concatenate([sina, sina], axis=1)
          + pltpu.roll(cq, ROPE_DIM // 2, 1) * jnp.concatenate([sinb, sinb], axis=1))
    ck = pc[:, 256:384]
    ck = ck * cos + pltpu.roll(ck, 128 - ROPE_DIM // 2, 1) * sina + pltpu.roll(ck, ROPE_DIM // 2, 1) * sinb
    kbuf[WINDOW:WINDOW + T, :] = ck
    vbuf[WINDOW:WINDOW + T, :] = pc[:, 384:512]
    ck_out_ref[0] = kbuf[T:T + WINDOW, :]
    cv_out_ref[0] = vbuf[T:T + WINDOW, :]

    lane128 = _iota((2 * WINDOW, 128), 1)
    low = _iota((WINDOW, 128), 1) < HEAD_DIM
    qrow = _iota((2 * WINDOW, 2 * WINDOW), 0) & (WINDOW - 1)
    kcol = _iota((2 * WINDOW, 2 * WINDOW), 1)
    band = (kcol > qrow) & (kcol <= qrow + WINDOW)
    top_half = _iota((2 * WINDOW, 1), 0) < WINDOW
    out_c_blocks = []
    for n in range(T // WINDOW):
        first_key = si * T + (n - 1) * WINDOW
        valid = band & (kcol + first_key >= 0)
        kwin = kbuf[n * WINDOW:(n + 2) * WINDOW, :]
        vwin = vbuf[n * WINDOW:(n + 2) * WINDOW, :]
        k_sw = pltpu.roll(kwin, HEAD_DIM, 1)
        v_sw = pltpu.roll(vwin, HEAD_DIM, 1)
        pair_out = []
        for g in range(C_KV_HEADS):
            own = (lane128 < HEAD_DIM) if g == 0 else (lane128 >= HEAD_DIM)
            k_dup = jnp.where(own, kwin, k_sw).astype(BF16)
            v_dup = jnp.where(own, vwin, v_sw).astype(BF16)
            qp = cq[n * WINDOW:(n + 1) * WINDOW, g * 128:(g + 1) * 128]
            q_st = jnp.concatenate([jnp.where(low, qp, 0.0), jnp.where(low, 0.0, qp)], axis=0).astype(BF16)
            s = _dot_nt(q_st, k_dup) * (HEAD_DIM ** -0.5)
            s = jnp.where(valid, s, NEG_INF)
            sink = jnp.where(top_half, sinks_ref[2 * g], sinks_ref[2 * g + 1])
            m = jnp.maximum(jnp.max(s, axis=-1, keepdims=True), sink)
            p = jnp.exp(s - m)
            denom = jnp.sum(p, axis=-1, keepdims=True) + jnp.exp(sink - m)
            o2 = _dot((p / denom).astype(BF16), v_dup)
            pair_out.append(jnp.where(low, o2[0:WINDOW], o2[WINDOW:2 * WINDOW]))
        out_c_blocks.append(jnp.concatenate(pair_out, axis=1))
    out_c = jnp.concatenate(out_c_blocks, axis=0)
    kbuf[0:WINDOW, :] = kbuf[T:T + WINDOW, :]
    vbuf[0:WINDOW, :] = vbuf[T:T + WINDOW, :]

    pd = _dot(h, w_in_ref[:, COL_D:COL_D + W_D])
    du = _gelu_tanh(pd[:, 0:256])
    gv = _gelu_tanh(pd[:, 256:512])
    mu = jnp.mean(gv, axis=-1, keepdims=True)
    xc = gv - mu
    dvn = xc * lax.rsqrt(jnp.mean(xc * xc, axis=-1, keepdims=True) + EPS) * lng_ref[...] + lnb_ref[...]
    wr = _iota((D_GROUPS * D_CHUNK, D_CHUNK), 0) & (D_CHUNK - 1)
    wc = _iota((D_GROUPS * D_CHUNK, D_CHUNK), 1)
    wm = jnp.where(wr >= wc, ws_ref[...], 0.0).astype(BF16)
    lane_grp = _iota((D_CHUNK, 256), 1) >> 6
    out_d_blocks = []
    for n in range(T // D_CHUNK):
        mx = _dot(wm, dvn[n * D_CHUNK:(n + 1) * D_CHUNK, :].astype(BF16))
        mixed = dbias_ref[...]
        for grp in range(D_GROUPS):
            mixed = mixed + jnp.where(lane_grp == grp, mx[grp * D_CHUNK:(grp + 1) * D_CHUNK, :], 0.0)
        out_d_blocks.append(du[n * D_CHUNK:(n + 1) * D_CHUNK, :] * mixed)
    out_d = jnp.concatenate(out_d_blocks, axis=0)

    cat = jnp.concatenate([out_a, out_b, out_c, out_d], axis=1).astype(BF16)
    x_out_ref[0] = x + _dot(cat, w_out_ref[...])


def _mixer_call(x, lw, tabs, aconv0, bconv0, s0):
    bsz, seq, _ = x.shape
    T = TILE_ROWS
    ns = seq // T
    P = CARRY_ROWS
    full = lambda shape: pl.BlockSpec(shape, lambda b, s, *_: (0,) * len(shape),
                                      pipeline_mode=pl.Buffered(1))
    per_b = lambda shape: pl.BlockSpec((1,) + shape, lambda b, s, *_: (b,) + (0,) * len(shape))
    tab = pl.BlockSpec((T, 128), lambda b, s, *_: (s, 0))
    grid_spec = pltpu.PrefetchScalarGridSpec(
        num_scalar_prefetch=1,
        grid=(bsz, ns),
        in_specs=[
            pl.BlockSpec((1, T, D_MODEL), lambda b, s, *_: (b, s, 0)),
            full((1, D_MODEL)), full((D_MODEL, P_PACKED)), full((A_CONV, 768)), full((1, 128)), full((1, 128)),
            full((1, 256)), full((B_CONV, 256)), tab, tab, tab, full((1, 256)), full((1, 256)),
            full((D_GROUPS * D_CHUNK, D_CHUNK)), full((D_CHUNK, 256)), full((D_MODEL, D_MODEL)),
            per_b((P, 768)), per_b((P, 256)), per_b((256, 256)),
        ],
        out_specs=[
            pl.BlockSpec((1, T, D_MODEL), lambda b, s, *_: (b, s, 0)),
            per_b((256, 256)), per_b((P, 768)), per_b((P, 256)), per_b((WINDOW, 128)), per_b((WINDOW, 128)),
        ],
        scratch_shapes=[
            pltpu.VMEM((T + P, 768), F32), pltpu.VMEM((T + P, 256), F32),
            pltpu.VMEM((T + WINDOW, 128), F32), pltpu.VMEM((T + WINDOW, 128), F32),
            pltpu.VMEM((T, 256), F32), pltpu.VMEM((T, 256), F32), pltpu.VMEM((T, 256), F32),
            pltpu.VMEM((T, 256), F32), pltpu.VMEM((T, 256), F32), pltpu.VMEM((T, 256), F32),
            pltpu.VMEM((256, 256), F32),
        ],
    )
    out_shape = [
        jax.ShapeDtypeStruct((bsz, seq, D_MODEL), F32),
        jax.ShapeDtypeStruct((bsz, 256, 256), F32),
        jax.ShapeDtypeStruct((bsz, P, 768), F32),
        jax.ShapeDtypeStruct((bsz, P, 256), F32),
        jax.ShapeDtypeStruct((bsz, WINDOW, 128), F32),
        jax.ShapeDtypeStruct((bsz, WINDOW, 128), F32),
    ]
    return pl.pallas_call(
        functools.partial(_mixer_kernel, tile=T),
        grid_spec=grid_spec,
        out_shape=out_shape,
        compiler_params=pltpu.CompilerParams(
            dimension_semantics=("arbitrary", "arbitrary"), vmem_limit_bytes=VMEM_LIMIT_BYTES),
        name="mixer",
    )(lw['c_sinks'], x, lw['norm1_g'], lw['w_in'], lw['a_conv_w'], lw['a_log'], lw['a_dt_bias'],
      lw['a_norm_g'], lw['b_conv_w'], tabs[0], tabs[1], tabs[2], lw['d_ln_g'], lw['d_ln_b'],
      lw['d_ws'], lw['d_bias'], lw['w_out'], aconv0, bconv0, s0)


def _ffn_kernel(x_ref, n2g_ref, wg_ref, wu_ref, cw_ref, wd_ref, fconv0_ref, fng_ref,
                x_out_ref, fconv_out_ref, gbuf, *, tile, stride, final_norm):
    T = tile
    P = (FFN_CONV - 1) * stride if stride > 1 else CARRY_ROWS
    si = pl.program_id(1)

    @pl.when(si == 0)
    def _():
        gbuf[0:P, :] = fconv0_ref[0]

    x = x_ref[0]
    h = _rms(x, n2g_ref[...]).astype(BF16)
    acc = x
    for c0 in range(0, D_FF, FFN_COL_CHUNK):
        cols = slice(c0, c0 + FFN_COL_CHUNK)
        gbuf[P:P + T, cols] = _dot(h, wg_ref[:, cols])
        up = _dot(h, wu_ref[:, cols])
        gate = (gbuf[P:P + T, cols] * cw_ref[2:3, cols] + gbuf[P - stride:P - stride + T, cols] * cw_ref[1:2, cols]
                + gbuf[P - 2 * stride:P - 2 * stride + T, cols] * cw_ref[0:1, cols])
        acc = acc + _dot((_silu(gate) * up).astype(BF16), wd_ref[cols, :])
    tail = gbuf[T:T + P, :]
    gbuf[0:P, :] = tail
    fconv_out_ref[0] = tail
    if final_norm:
        acc = _rms(acc, fng_ref[...])
    x_out_ref[0] = acc


def _ffn_call(x, lw, fconv0, final_g, *, tile, stride, final_norm):
    bsz, seq, _ = x.shape
    T = tile
    P = (FFN_CONV - 1) * stride if stride > 1 else CARRY_ROWS
    full = lambda shape: pl.BlockSpec(shape, lambda b, s: (0,) * len(shape), pipeline_mode=pl.Buffered(1))
    per_b = lambda shape: pl.BlockSpec((1,) + shape, lambda b, s: (b,) + (0,) * len(shape))
    return pl.pallas_call(
        functools.partial(_ffn_kernel, tile=T, stride=stride, final_norm=final_norm),
        grid=(bsz, seq // T),
        in_specs=[
            pl.BlockSpec((1, T, D_MODEL), lambda b, s: (b, s, 0)),
            full((1, D_MODEL)), full((D_MODEL, D_FF)), full((D_MODEL, D_FF)), full((FFN_CONV, D_FF)),
            full((D_FF, D_MODEL)), per_b((P, D_FF)), full((1, D_MODEL)),
        ],
        out_specs=[pl.BlockSpec((1, T, D_MODEL), lambda b, s: (b, s, 0)), per_b((P, D_FF))],
        out_shape=[jax.ShapeDtypeStruct((bsz, seq, D_MODEL), F32), jax.ShapeDtypeStruct((bsz, P, D_FF), F32)],
        scratch_shapes=[pltpu.VMEM((T + P, D_FF), F32)],
        compiler_params=pltpu.CompilerParams(
            dimension_semantics=("arbitrary", "arbitrary"), vmem_limit_bytes=VMEM_LIMIT_BYTES),
        name="ffn",
    )(x, lw['norm2_g'], lw['ffn_w_gate'], lw['ffn_w_up'], lw['ffn_conv_w'], lw['ffn_w_down'], fconv0, final_g)


def _pack_w_in(w):
    pad = jnp.zeros((D_MODEL, 128 - 2 * A_HEADS), w.dtype)
    small = jnp.concatenate([w[:, 1028:1032], w[:, 1024:1028], pad], axis=1)
    return jnp.concatenate([w[:, 0:1024], small, w[:, 1032:]], axis=1).astype(BF16)


def _pad_lanes(v, width=128):
    return jnp.concatenate([v, jnp.zeros((width - v.shape[0],), v.dtype)])[None, :]


def _layer_weights(l, norm1_g, w_in, a_conv_w, a_log, a_dt_bias, a_norm_g, b_conv_w, c_sinks, d_ln_g, d_ln_b,
                   d_ws, d_bias, w_out, norm2_g, ffn_w_gate, ffn_w_up, ffn_conv_w, ffn_w_down):
    bias_tab = jnp.broadcast_to(d_bias[l].T[:, :, None], (D_CHUNK, D_GROUPS, HEAD_DIM)).reshape(D_CHUNK, 256)
    return {
        'norm1_g': norm1_g[l][None, :], 'w_in': _pack_w_in(w_in[l]), 'a_conv_w': a_conv_w[l],
        'a_log': _pad_lanes(a_log[l]), 'a_dt_bias': _pad_lanes(a_dt_bias[l]),
        'a_norm_g': jnp.tile(a_norm_g[l], A_HEADS)[None, :], 'b_conv_w': b_conv_w[l], 'c_sinks': c_sinks[l],
        'd_ln_g': d_ln_g[l][None, :], 'd_ln_b': d_ln_b[l][None, :],
        'd_ws': d_ws[l].reshape(D_GROUPS * D_CHUNK, D_CHUNK), 'd_bias': bias_tab,
        'w_out': w_out[l].astype(BF16), 'norm2_g': norm2_g[l][None, :],
        'ffn_w_gate': ffn_w_gate[l].astype(BF16), 'ffn_w_up': ffn_w_up[l].astype(BF16),
        'ffn_conv_w': ffn_conv_w[l], 'ffn_w_down': ffn_w_down[l].astype(BF16),
    }


def _rope_tables(pos):
    half = ROPE_DIM // 2
    inv = jnp.power(ROPE_THETA, -jnp.arange(half, dtype=F32) * (2.0 / ROPE_DIM))
    ang = pos.astype(F32)[:, None] * inv[None, :]
    cos, sin = jnp.cos(ang), jnp.sin(ang)
    n = pos.shape[0]
    rest = HEAD_DIM - ROPE_DIM
    cos_h = jnp.concatenate([cos, cos, jnp.ones((n, rest), F32)], axis=1)
    sina_h = jnp.concatenate([-sin, jnp.zeros((n, half + rest), F32)], axis=1)
    sinb_h = jnp.concatenate([jnp.zeros((n, half), F32), sin, jnp.zeros((n, rest), F32)], axis=1)
    return tuple(jnp.concatenate([t, t], axis=1) for t in (cos_h, sina_h, sinb_h))


def _diag_blocks(s_bd):
    return jnp.stack([s_bd[:, HEAD_DIM * i:HEAD_DIM * (i + 1), HEAD_DIM * i:HEAD_DIM * (i + 1)]
                      for i in range(A_HEADS)], axis=1)


def _causal_dwconv(buf, x, w):
    K = w.shape[0]
    t = x.shape[1]
    xp = jnp.concatenate([buf.astype(x.dtype), x], axis=1)
    y = xp[:, 0:t] * w[0]
    for j in range(1, K):
        y = y + xp[:, j:j + t] * w[j]
    return y, xp[:, t:]


def _sample_mixer(x, lw_raw, st, pos, k_pos):
    p = lw_raw
    bsz, t, _ = x.shape
    xf = x
    h = xf * lax.rsqrt(jnp.mean(xf * xf, axis=-1, keepdims=True) + EPS) * p['norm1_g']
    proj = h @ p['w_in']
    sizes = (768, 256, 4, 4, 256, 256, 256, 256, 128, 128, 256, 256)
    (a_qkv, a_z, a_b, a_a, b_b, b_c, b_h, c_q, c_k, c_v, d_u, d_v) = jnp.split(proj, [int(s) for s in np.cumsum(sizes)[:-1]], axis=-1)
    qkv, a_conv_new = _causal_dwconv(st['a_conv'], a_qkv, p['a_conv_w'])
    qkv = jax.nn.silu(qkv).reshape(bsz, t, 3, A_HEADS, HEAD_DIM)
    l2 = lambda a: a * lax.rsqrt(jnp.sum(a * a, axis=-1, keepdims=True) + EPS)
    q, k, v = l2(qkv[:, :, 0]), l2(qkv[:, :, 1]), qkv[:, :, 2]
    beta = jax.nn.sigmoid(a_b)
    g = -jnp.exp(p['a_log']) * jax.nn.softplus(a_a + p['a_dt_bias'])
    s = st['a_state']
    outs = []
    for i in range(t):
        s = s * jnp.exp(g[:, i])[..., None, None]
        ks = jnp.einsum('bhd,bhde->bhe', k[:, i], s)
        vn = (v[:, i] - ks) * beta[:, i][..., None]
        s = s + k[:, i][..., :, None] * vn[..., None, :]
        outs.append(jnp.einsum('bhd,bhde->bhe', q[:, i], s) * (HEAD_DIM ** -0.5))
    o = jnp.stack(outs, axis=1)
    o = o * lax.rsqrt(jnp.mean(o * o, axis=-1, keepdims=True) + EPS) * p['a_norm_g']
    out_a = (o * jax.nn.silu(a_z.reshape(bsz, t, A_HEADS, HEAD_DIM))).reshape(bsz, t, GROUP_WIDTH)
    bx, b_conv_new = _causal_dwconv(st['b_conv'], b_c * b_h, p['b_conv_w'])
    out_b = b_b * bx

    def rope(xx):
        half = ROPE_DIM // 2
        inv = jnp.power(ROPE_THETA, -jnp.arange(half, dtype=F32) * (2.0 / ROPE_DIM))
        ang = pos.astype(F32)[:, None] * inv[None, :]
        cos = jnp.cos(ang)[None, :, None, :]
        sin = jnp.sin(ang)[None, :, None, :]
        x1, x2 = xx[..., :half], xx[..., half:ROPE_DIM]
        return jnp.concatenate([x1 * cos - x2 * sin, x2 * cos + x1 * sin, xx[..., ROPE_DIM:]], axis=-1)

    cq = rope(c_q.reshape(bsz, t, C_HEADS, HEAD_DIM))
    ck = rope(c_k.reshape(bsz, t, C_KV_HEADS, HEAD_DIM))
    cv = c_v.reshape(bsz, t, C_KV_HEADS, HEAD_DIM)
    kk = jnp.concatenate([st['c_k'], ck], axis=1)
    vv = jnp.concatenate([st['c_v'], cv], axis=1)
    qg = cq.reshape(bsz, t, C_KV_HEADS, C_HEADS // C_KV_HEADS, HEAD_DIM)
    sc = jnp.einsum('bqhgd,bkhd->bhgqk', qg, kk) * (HEAD_DIM ** -0.5)
    dpos = pos[:, None] - k_pos[None, :]
    sc = jnp.where((dpos >= 0) & (dpos < WINDOW), sc, NEG_INF)
    sk = jnp.broadcast_to(p['c_sinks'].reshape(C_KV_HEADS, 2)[..., None, None], sc.shape[:-1] + (1,))
    pr = jax.nn.softmax(jnp.concatenate([sc, sk], axis=-1), axis=-1)[..., :-1]
    out_c = jnp.einsum('bhgqk,bkhd->bqhgd', pr, vv).reshape(bsz, t, C_HEADS * HEAD_DIM)
    nbuf = st['c_k'].shape[1]
    du = jax.nn.gelu(d_u)
    gv = jax.nn.gelu(d_v)
    mu = jnp.mean(gv, axis=-1, keepdims=True)
    xc = gv - mu
    dvn = xc * lax.rsqrt(jnp.mean(xc * xc, axis=-1, keepdims=True) + EPS) * p['d_ln_g'] + p['d_ln_b']
    wm = jnp.where(jnp.tril(jnp.ones((t, t), bool)), p['d_ws'][:, :t, :t], 0.0)
    vc = dvn.reshape(bsz, 1, t, D_GROUPS, HEAD_DIM)
    mixed = jnp.einsum('gts,bcsgd->bctgd', wm, vc) + p['d_bias'][:, :t].T[None, None, :, :, None]
    out_d = du * mixed.reshape(bsz, t, GROUP_WIDTH)
    x = x + jnp.concatenate([out_a, out_b, out_c, out_d], axis=-1) @ p['w_out']
    new = {'a_state': s, 'a_conv': a_conv_new, 'b_conv': b_conv_new, 'c_k': kk[:, -nbuf:], 'c_v': vv[:, -nbuf:],
           'd_v': dvn}
    return x, new


def kernel(x_prompt, x_sample, state_delta, state_delta_conv, state_shortconv, cache_win_k, cache_win_v,
           state_ffn_conv, norm1_g, w_in, a_conv_w, a_log, a_dt_bias, a_norm_g, b_conv_w, c_sinks, d_ln_g,
           d_ln_b, d_ws, d_bias, w_out, norm2_g, ffn_w_gate, ffn_w_up, ffn_conv_w, ffn_w_down, final_norm_g):
    bp, tp, _ = x_prompt.shape
    bs, ts, _ = x_sample.shape
    depth = w_in.shape[0]
    win_buf = cache_win_k.shape[2]
    pos_p = jnp.arange(tp, dtype=jnp.int32)
    pos_s = PAST_LEN + jnp.arange(ts, dtype=jnp.int32)
    k_pos_s = PAST_LEN - win_buf + jnp.arange(win_buf + ts, dtype=jnp.int32)
    tabs_p = _rope_tables(pos_p)
    fng = final_norm_g[None, :]
    P = CARRY_ROWS

    hp, hs = x_prompt, x_sample
    outs = {k: [] for k in ('sp', 'ss', 'acp', 'acs', 'bcp', 'bcs', 'ckp', 'cks', 'cvp', 'cvs', 'fcp', 'fcs', 'dv')}
    for l in range(depth):
        lw = _layer_weights(l, norm1_g, w_in, a_conv_w, a_log, a_dt_bias, a_norm_g, b_conv_w, c_sinks, d_ln_g,
                            d_ln_b, d_ws, d_bias, w_out, norm2_g, ffn_w_gate, ffn_w_up, ffn_conv_w, ffn_w_down)
        last = l == depth - 1
        hp, s_bd, acv, bcv, ckn, cvn = _mixer_call(
            hp, lw, tabs_p, jnp.zeros((bp, P, 768), F32), jnp.zeros((bp, P, 256), F32),
            jnp.zeros((bp, 256, 256), F32))
        hp, fcv = _ffn_call(hp, lw, jnp.zeros((bp, P, D_FF), F32), fng, tile=TILE_ROWS, stride=1, final_norm=last)
        outs['sp'].append(_diag_blocks(s_bd))
        outs['acp'].append(acv[:, P - (A_CONV - 1):])
        outs['bcp'].append(bcv[:, P - (B_CONV - 1):])
        outs['ckp'].append(ckn.reshape(bp, WINDOW, C_KV_HEADS, HEAD_DIM))
        outs['cvp'].append(cvn.reshape(bp, WINDOW, C_KV_HEADS, HEAD_DIM))
        outs['fcp'].append(fcv[:, P - (FFN_CONV - 1):])
        raw = {'norm1_g': norm1_g[l], 'w_in': w_in[l], 'a_conv_w': a_conv_w[l], 'a_log': a_log[l],
               'a_dt_bias': a_dt_bias[l], 'a_norm_g': a_norm_g[l], 'b_conv_w': b_conv_w[l], 'c_sinks': c_sinks[l],
               'd_ln_g': d_ln_g[l], 'd_ln_b': d_ln_b[l], 'd_ws': d_ws[l], 'd_bias': d_bias[l], 'w_out': w_out[l]}
        st_s = {'a_state': state_delta[l], 'a_conv': state_delta_conv[l], 'b_conv': state_shortconv[l],
                'c_k': cache_win_k[l], 'c_v': cache_win_v[l]}
        hs, ns = _sample_mixer(hs, raw, st_s, pos_s, k_pos_s)
        xs_tm = jnp.swapaxes(hs, 0, 1).reshape(1, ts * bs, D_MODEL)
        f0 = jnp.swapaxes(state_ffn_conv[l], 0, 1).reshape(1, (FFN_CONV - 1) * bs, D_FF)
        ys_tm, fcs = _ffn_call(xs_tm, lw, f0, fng, tile=ts * bs, stride=bs, final_norm=last)
        hs = jnp.swapaxes(ys_tm.reshape(ts, bs, D_MODEL), 0, 1)
        outs['ss'].append(ns['a_state'])
        outs['acs'].append(ns['a_conv'])
        outs['bcs'].append(ns['b_conv'])
        outs['cks'].append(ns['c_k'])
        outs['cvs'].append(ns['c_v'])
        outs['fcs'].append(jnp.swapaxes(fcs.reshape(FFN_CONV - 1, bs, D_FF), 0, 1))
        outs['dv'].append(ns['d_v'])
    st = {k: jnp.stack(v) for k, v in outs.items()}
    return (hp, hs, st['sp'], st['ss'], st['acp'], st['acs'], st['bcp'], st['bcs'], st['ckp'], st['cks'],
            st['cvp'], st['cvs'], st['fcp'], st['fcs'], st['dv'])
```

```python
import functools

import jax
import jax.numpy as jnp
import numpy as np
from jax import lax
from jax.experimental import pallas as pl
from jax.experimental.pallas import tpu as pltpu

F32 = jnp.float32
BF16 = jnp.bfloat16

D_MODEL = 1024
GROUP_WIDTH = 256
HEAD_DIM = 64
A_HEADS = 4
A_CONV = 4
A_CHUNK = 64
B_CONV = 3
C_HEADS = 4
C_KV_HEADS = 2
WINDOW = 128
ROPE_DIM = 16
ROPE_THETA = 500000.0
D_GROUPS = 4
D_CHUNK = 128
D_FF = 2816
FFN_CONV = 3
EPS = 1e-6
NEG_INF = -1e30
PAST_LEN = 16384

COL_A = 0
W_A = 1152
COL_B = 1152
W_B = 768
COL_C = 1920
W_C = 512
COL_D = 2432
W_D = 512
P_PACKED = 2944

TILE_ROWS = 512
CARRY_ROWS = 8
FFN_COL_CHUNK = 1408
VMEM_LIMIT_BYTES = 56 * 1024 * 1024


def _dot(a, b):
    return jnp.dot(a, b, preferred_element_type=F32)


def _dot_nt(a, b):
    return lax.dot_general(a, b, (((1,), (1,)), ((), ())), preferred_element_type=F32)


def _dot_tn(a, b):
    return lax.dot_general(a, b, (((0,), (0,)), ((), ())), preferred_element_type=F32)


def _split3(x):
    hi = x.astype(BF16)
    r1 = x - hi.astype(F32)
    mid = r1.astype(BF16)
    lo = (r1 - mid.astype(F32)).astype(BF16)
    return hi, mid, lo


def _dot_lhs3(x, w01):
    hi, mid, lo = _split3(x)
    return _dot(hi, w01) + _dot(mid, w01) + _dot(lo, w01)


def _dot_rhs3(w01, x):
    hi, mid, lo = _split3(x)
    return _dot(w01, hi) + _dot(w01, mid) + _dot(w01, lo)


def _sigmoid(x):
    return 1.0 / (1.0 + jnp.exp(-x))


def _silu(x):
    return x * _sigmoid(x)


def _softplus(x):
    return jnp.maximum(x, 0.0) + jnp.log(1.0 + jnp.exp(-jnp.abs(x)))


def _gelu_tanh(x):
    return 0.5 * x * (1.0 + jnp.tanh(np.sqrt(2.0 / np.pi).astype(np.float32) * (x + 0.044715 * (x * x * x))))


def _rms(x, g):
    return x * lax.rsqrt(jnp.mean(x * x, axis=-1, keepdims=True) + EPS) * g


def _iota(shape, dim):
    return lax.broadcasted_iota(jnp.int32, shape, dim)


def _block_diag(x, mask_bd):
    return jnp.where(mask_bd, jnp.concatenate([x, x, x, x], axis=0), 0.0)


def _mixer_kernel(sinks_ref, x_ref, n1g_ref, w_in_ref, aconv_w_ref, alog_ref, dtb_ref, anorm_ref,
                  bconv_w_ref, cos_ref, sina_ref, sinb_ref, lng_ref, lnb_ref, ws_ref, dbias_ref,
                  w_out_ref, aconv0_ref, bconv0_ref, s0_ref,
                  x_out_ref, s_out_ref, aconv_out_ref, bconv_out_ref, ck_out_ref, cv_out_ref,
                  abuf, bbuf, kbuf, vbuf, q_s, k_s, v_s, gb_s, bb_s, o_s, s_scr, *, tile):
    T = tile
    si = pl.program_id(1)
    P = CARRY_ROWS

    @pl.when(si == 0)
    def _():
        abuf[0:P, :] = aconv0_ref[0]
        bbuf[0:P, :] = bconv0_ref[0]
        kbuf[0:WINDOW, :] = jnp.zeros((WINDOW, 128), F32)
        vbuf[0:WINDOW, :] = jnp.zeros((WINDOW, 128), F32)
        s_scr[...] = s0_ref[0]

    x = x_ref[0]
    h = _rms(x, n1g_ref[...]).astype(BF16)

    pa = _dot(h, w_in_ref[:, COL_A:COL_A + W_A])
    z = pa[:, 768:1024]
    small = pa[:, 1024:1152]
    abuf[P:P + T, :] = pa[:, 0:768]
    y = (abuf[P:P + T, :] * aconv_w_ref[3:4, :] + abuf[P - 1:P - 1 + T, :] * aconv_w_ref[2:3, :]
         + abuf[P - 2:P - 2 + T, :] * aconv_w_ref[1:2, :] + abuf[P - 3:P - 3 + T, :] * aconv_w_ref[0:1, :])
    tail_a = abuf[T:T + P, :]
    abuf[0:P, :] = tail_a
    aconv_out_ref[0] = tail_a
    qkv = _silu(y)
    q_raw = qkv[:, 0:256]
    k_raw = qkv[:, 256:512]
    v_s[...] = qkv[:, 512:768]

    r256 = _iota((256, 256), 0) >> 6
    c256 = _iota((256, 256), 1) >> 6
    mask_bd = r256 == c256
    ones_bd = jnp.where(mask_bd, 1.0, 0.0).astype(BF16)
    q_s[...] = q_raw * lax.rsqrt(_dot_lhs3(q_raw * q_raw, ones_bd) + EPS) * (HEAD_DIM ** -0.5)
    k_s[...] = k_raw * lax.rsqrt(_dot_lhs3(k_raw * k_raw, ones_bd) + EPS)

    g_log = -jnp.exp(alog_ref[...]) * _softplus(small + dtb_ref[...])
    beta = _sigmoid(small)
    er = _iota((128, 256), 0)
    ec = _iota((128, 256), 1) >> 6
    expand_g = jnp.where(er == ec, 1.0, 0.0).astype(BF16)
    expand_b = jnp.where(er == ec + A_HEADS, 1.0, 0.0).astype(BF16)
    gb_s[...] = _dot_lhs3(g_log, expand_g)
    bb_s[...] = _dot_lhs3(beta, expand_b)

    ri = _iota((A_CHUNK, 256), 0)
    ci = _iota((A_CHUNK, 256), 1) & (A_CHUNK - 1)
    causal_t = ri >= ci
    strict_t = ri > ci
    eye_t = jnp.where(ri == ci, 1.0, 0.0)
    ltri = jnp.where(_iota((A_CHUNK, A_CHUNK), 0) >= _iota((A_CHUNK, A_CHUNK), 1), 1.0, 0.0).astype(BF16)

    def prod(a_row, b_row):
        return _dot(a_row.astype(BF16), _block_diag(b_row, mask_bd).astype(BF16))

    def chunk_body(c, carry):
        rows = pl.ds(pl.multiple_of(c * A_CHUNK, A_CHUNK), A_CHUNK)
        qc = q_s[rows, :]
        kc = k_s[rows, :]
        vc = v_s[rows, :]
        gbc = gb_s[rows, :]
        bbc = bb_s[rows, :]
        cum = _dot_rhs3(ltri, jnp.concatenate([gbc, jnp.where(strict_t, gbc, 0.0)], axis=1))
        gcb = cum[:, 0:256]
        decay = jnp.where(causal_t, jnp.exp(cum[:, 256:512]), 0.0)
        eg = jnp.exp(gcb)
        kb = kc * bbc
        k_bd = _block_diag(kc, mask_bd).astype(BF16)
        aq = _dot_nt(jnp.concatenate([kb, qc], axis=0).astype(BF16), k_bd)
        a_mat = jnp.where(strict_t, aq[0:A_CHUNK] * decay, 0.0)
        qk = aq[A_CHUNK:2 * A_CHUNK] * decay
        t_inv = eye_t - a_mat
        pw = a_mat
        for _ in range(5):
            pw = prod(pw, pw)
            t_inv = t_inv + prod(t_inv, pw)
        rhs = jnp.concatenate([_block_diag(vc * bbc, mask_bd), _block_diag(kb * eg, mask_bd)], axis=1)
        uw = _dot(t_inv.astype(BF16), rhs.astype(BF16))
        u = uw[:, 0:256]
        w = uw[:, 256:512]
        s_bd = s_scr[...]
        wq = _dot(jnp.concatenate([w, qc * eg], axis=0).astype(BF16), s_bd.astype(BF16))
        v_new = u - wq[0:A_CHUNK]
        o_s[rows, :] = wq[A_CHUNK:2 * A_CHUNK] + prod(qk, v_new)
        g_last = gcb[A_CHUNK - 1:A_CHUNK, :]
        kv = _dot_tn(kc.astype(BF16), (v_new * jnp.exp(g_last - gcb)).astype(BF16))
        s_scr[...] = s_bd * jnp.exp(g_last) + jnp.where(mask_bd, kv, 0.0)
        return carry

    lax.fori_loop(0, T // A_CHUNK, chunk_body, 0)
    s_out_ref[0] = s_scr[...]
    o = o_s[...]
    out_a = (o * lax.rsqrt(_dot_lhs3(o * o, ones_bd) * (1.0 / HEAD_DIM) + EPS) * anorm_ref[...]
             * _silu(z))

    pb = _dot(h, w_in_ref[:, COL_B:COL_B + W_B])
    bbuf[P:P + T, :] = pb[:, 256:512] * pb[:, 512:768]
    bx = (bbuf[P:P + T, :] * bconv_w_ref[2:3, :] + bbuf[P - 1:P - 1 + T, :] * bconv_w_ref[1:2, :]
          + bbuf[P - 2:P - 2 + T, :] * bconv_w_ref[0:1, :])
    tail_b = bbuf[T:T + P, :]
    bbuf[0:P, :] = tail_b
    bconv_out_ref[0] = tail_b
    out_b = pb[:, 0:256] * bx

    pc = _dot(h, w_in_ref[:, COL_C:COL_C + W_C])
    cos = cos_ref[...]
    sina = sina_ref[...]
    sinb = sinb_ref[...]
    cq = pc[:, 0:256]
    cq = (cq * jnp.concatenate([cos, cos], axis=1)
          + pltpu.roll(cq, 256 - ROPE_DIM // 2, 1) * jnp.concatenate([sina, sina], axis=1)
          + pltpu.roll(cq, ROPE_DIM // 2, 1) * jnp.concatenate([sinb, sinb], axis=1))
    ck = pc[:, 256:384]
    ck = ck * cos + pltpu.roll(ck, 128 - ROPE_DIM // 2, 1) * sina + pltpu.roll(ck, ROPE_DIM // 2, 1) * sinb
    kbuf[WINDOW:WINDOW + T, :] = ck
    vbuf[WINDOW:WINDOW + T, :] = pc[:, 384:512]
    ck_out_ref[0] = kbuf[T:T + WINDOW, :]
    cv_out_ref[0] = vbuf[T:T + WINDOW, :]

    lane128 = _iota((2 * WINDOW, 128), 1)
    low = _iota((WINDOW, 128), 1) < HEAD_DIM
    qrow = _iota((2 * WINDOW, 2 * WINDOW), 0) & (WINDOW - 1)
    kcol = _iota((2 * WINDOW, 2 * WINDOW), 1)
    band = (kcol > qrow) & (kcol <= qrow + WINDOW)
    top_half = _iota((2 * WINDOW, 1), 0) < WINDOW
    out_c_blocks = []
    for n in range(T // WINDOW):
        first_key = si * T + (n - 1) * WINDOW
        valid = band & (kcol + first_key >= 0)
        kwin = kbuf[n * WINDOW:(n + 2) * WINDOW, :]
        vwin = vbuf[n * WINDOW:(n + 2) * WINDOW, :]
        k_sw = pltpu.roll(kwin, HEAD_DIM, 1)
        v_sw = pltpu.roll(vwin, HEAD_DIM, 1)
        pair_out = []
        for g in range(C_KV_HEADS):
            own = (lane128 < HEAD_DIM) if g == 0 else (lane128 >= HEAD_DIM)
            k_dup = jnp.where(own, kwin, k_sw).astype(BF16)
            v_dup = jnp.where(own, vwin, v_sw).astype(BF16)
            qp = cq[n * WINDOW:(n + 1) * WINDOW, g * 128:(g + 1) * 128]
            q_st = jnp.concatenate([jnp.where(low, qp, 0.0), jnp.where(low, 0.0, qp)], axis=0).astype(BF16)
            s = _dot_nt(q_st, k_dup) * (HEAD_DIM ** -0.5)
            s = jnp.where(valid, s, NEG_INF)
            sink = jnp.where(top_half, sinks_ref[2 * g], sinks_ref[2 * g + 1])
            m = jnp.maximum(jnp.max(s, axis=-1, keepdims=True), sink)
            p = jnp.exp(s - m)
            denom = jnp.sum(p, axis=-1, keepdims=True) + jnp.exp(sink - m)
            o2 = _dot((p / denom).astype(BF16), v_dup)
            pair_out.append(jnp.where(low, o2[0:WINDOW], o2[WINDOW:2 * WINDOW]))
        out_c_blocks.append(jnp.concatenate(pair_out, axis=1))
    out_c = jnp.concatenate(out_c_blocks, axis=0)
    kbuf[0:WINDOW, :] = kbuf[T:T + WINDOW, :]
    vbuf[0:WINDOW, :] = vbuf[T:T + WINDOW, :]

    pd = _dot(h, w_in_ref[:, COL_D:COL_D + W_D])
    du = _gelu_tanh(pd[:, 0:256])
    gv = _gelu_tanh(pd[:, 256:512])
    mu = jnp.mean(gv, axis=-1, keepdims=True)
    xc = gv - mu
    dvn = xc * lax.rsqrt(jnp.mean(xc * xc, axis=-1, keepdims=True) + EPS) * lng_ref[...] + lnb_ref[...]
    wr = _iota((D_GROUPS * D_CHUNK, D_CHUNK), 0) & (D_CHUNK - 1)
    wc = _iota((D_GROUPS * D_CHUNK, D_CHUNK), 1)
    wm = jnp.where(wr >= wc, ws_ref[...], 0.0).astype(BF16)
    lane_grp = _iota((D_CHUNK, 256), 1) >> 6
    out_d_blocks = []
    for n in range(T // D_CHUNK):
        mx = _dot(wm, dvn[n * D_CHUNK:(n + 1) * D_CHUNK, :].astype(BF16))
        mixed = dbias_ref[...]
        for grp in range(D_GROUPS):
            mixed = mixed + jnp.where(lane_grp == grp, mx[grp * D_CHUNK:(grp + 1) * D_CHUNK, :], 0.0)
        out_d_blocks.append(du[n * D_CHUNK:(n + 1) * D_CHUNK, :] * mixed)
    out_d = jnp.concatenate(out_d_blocks, axis=0)

    cat = jnp.concatenate([out_a, out_b, out_c, out_d], axis=1).astype(BF16)
    x_out_ref[0] = x + _dot(cat, w_out_ref[...])


def _mixer_call(x, lw, tabs, aconv0, bconv0, s0):
    bsz, seq, _ = x.shape
    T = TILE_ROWS
    ns = seq // T
    P = CARRY_ROWS
    full = lambda shape: pl.BlockSpec(shape, lambda b, s, *_: (0,) * len(shape),
                                      pipeline_mode=pl.Buffered(1))
    per_b = lambda shape: pl.BlockSpec((1,) + shape, lambda b, s, *_: (b,) + (0,) * len(shape))
    tab = pl.BlockSpec((T, 128), lambda b, s, *_: (s, 0))
    grid_spec = pltpu.PrefetchScalarGridSpec(
        num_scalar_prefetch=1,
        grid=(bsz, ns),
        in_specs=[
            pl.BlockSpec((1, T, D_MODEL), lambda b, s, *_: (b, s, 0)),
            full((1, D_MODEL)), full((D_MODEL, P_PACKED)), full((A_CONV, 768)), full((1, 128)), full((1, 128)),
            full((1, 256)), full((B_CONV, 256)), tab, tab, tab, full((1, 256)), full((1, 256)),
            full((D_GROUPS * D_CHUNK, D_CHUNK)), full((D_CHUNK, 256)), full((D_MODEL, D_MODEL)),
            per_b((P, 768)), per_b((P, 256)), per_b((256, 256)),
        ],
        out_specs=[
            pl.BlockSpec((1, T, D_MODEL), lambda b, s, *_: (b, s, 0)),
            per_b((256, 256)), per_b((P, 768)), per_b((P, 256)), per_b((WINDOW, 128)), per_b((WINDOW, 128)),
        ],
        scratch_shapes=[
            pltpu.VMEM((T + P, 768), F32), pltpu.VMEM((T + P, 256), F32),
            pltpu.VMEM((T + WINDOW, 128), F32), pltpu.VMEM((T + WINDOW, 128), F32),
            pltpu.VMEM((T, 256), F32), pltpu.VMEM((T, 256), F32), pltpu.VMEM((T, 256), F32),
            pltpu.VMEM((T, 256), F32), pltpu.VMEM((T, 256), F32), pltpu.VMEM((T, 256), F32),
            pltpu.VMEM((256, 256), F32),
        ],
    )
    out_shape = [
        jax.ShapeDtypeStruct((bsz, seq, D_MODEL), F32),
        jax.ShapeDtypeStruct((bsz, 256, 256), F32),
        jax.ShapeDtypeStruct((bsz, P, 768), F32),
        jax.ShapeDtypeStruct((bsz, P, 256), F32),
        jax.ShapeDtypeStruct((bsz, WINDOW, 128), F32),
        jax.ShapeDtypeStruct((bsz, WINDOW, 128), F32),
    ]
    return pl.pallas_call(
        functools.partial(_mixer_kernel, tile=T),
        grid_spec=grid_spec,
        out_shape=out_shape,
        compiler_params=pltpu.CompilerParams(
            dimension_semantics=("arbitrary", "arbitrary"), vmem_limit_bytes=VMEM_LIMIT_BYTES),
        name="mixer",
    )(lw['c_sinks'], x, lw['norm1_g'], lw['w_in'], lw['a_conv_w'], lw['a_log'], lw['a_dt_bias'],
      lw['a_norm_g'], lw['b_conv_w'], tabs[0], tabs[1], tabs[2], lw['d_ln_g'], lw['d_ln_b'],
      lw['d_ws'], lw['d_bias'], lw['w_out'], aconv0, bconv0, s0)


def _ffn_kernel(x_ref, n2g_ref, wg_ref, wu_ref, cw_ref, wd_ref, fconv0_ref, fng_ref,
                x_out_ref, fconv_out_ref, gbuf, *, tile, stride, final_norm):
    T = tile
    P = (FFN_CONV - 1) * stride if stride > 1 else CARRY_ROWS
    si = pl.program_id(1)

    @pl.when(si == 0)
    def _():
        gbuf[0:P, :] = fconv0_ref[0]

    x = x_ref[0]
    h = _rms(x, n2g_ref[...]).astype(BF16)
    acc = x
    for c0 in range(0, D_FF, FFN_COL_CHUNK):
        cols = slice(c0, c0 + FFN_COL_CHUNK)
        gbuf[P:P + T, cols] = _dot(h, wg_ref[:, cols])
        up = _dot(h, wu_ref[:, cols])
        gate = (gbuf[P:P + T, cols] * cw_ref[2:3, cols] + gbuf[P - stride:P - stride + T, cols] * cw_ref[1:2, cols]
                + gbuf[P - 2 * stride:P - 2 * stride + T, cols] * cw_ref[0:1, cols])
        acc = acc + _dot((_silu(gate) * up).astype(BF16), wd_ref[cols, :])
    tail = gbuf[T:T + P, :]
    gbuf[0:P, :] = tail
    fconv_out_ref[0] = tail
    if final_norm:
        acc = _rms(acc, fng_ref[...])
    x_out_ref[0] = acc


def _ffn_call(x, lw, fconv0, final_g, *, tile, stride, final_norm):
    bsz, seq, _ = x.shape
    T = tile
    P = (FFN_CONV - 1) * stride if stride > 1 else CARRY_ROWS
    full = lambda shape: pl.BlockSpec(shape, lambda b, s: (0,) * len(shape), pipeline_mode=pl.Buffered(1))
    per_b = lambda shape: pl.BlockSpec((1,) + shape, lambda b, s: (b,) + (0,) * len(shape))
    return pl.pallas_call(
        functools.partial(_ffn_kernel, tile=T, stride=stride, final_norm=final_norm),
        grid=(bsz, seq // T),
        in_specs=[
            pl.BlockSpec((1, T, D_MODEL), lambda b, s: (b, s, 0)),
            full((1, D_MODEL)), full((D_MODEL, D_FF)), full((D_MODEL, D_FF)), full((FFN_CONV, D_FF)),
            full((D_FF, D_MODEL)), per_b((P, D_FF)), full((1, D_MODEL)),
        ],
        out_specs=[pl.BlockSpec((1, T, D_MODEL), lambda b, s: (b, s, 0)), per_b((P, D_FF))],
        out_shape=[jax.ShapeDtypeStruct((bsz, seq, D_MODEL), F32), jax.ShapeDtypeStruct((bsz, P, D_FF), F32)],
        scratch_shapes=[pltpu.VMEM((T + P, D_FF), F32)],
        compiler_params=pltpu.CompilerParams(
            dimension_semantics=("arbitrary", "arbitrary"), vmem_limit_bytes=VMEM_LIMIT_BYTES),
        name="ffn",
    )(x, lw['norm2_g'], lw['ffn_w_gate'], lw['ffn_w_up'], lw['ffn_conv_w'], lw['ffn_w_down'], fconv0, final_g)


def _pack_w_in(w):
    pad = jnp.zeros((D_MODEL, 128 - 2 * A_HEADS), w.dtype)
    small = jnp.concatenate([w[:, 1028:1032], w[:, 1024:1028], pad], axis=1)
    return jnp.concatenate([w[:, 0:1024], small, w[:, 1032:]], axis=1).astype(BF16)


def _pad_lanes(v, width=128):
    return jnp.concatenate([v, jnp.zeros((width - v.shape[0],), v.dtype)])[None, :]


def _layer_weights(l, norm1_g, w_in, a_conv_w, a_log, a_dt_bias, a_norm_g, b_conv_w, c_sinks, d_ln_g, d_ln_b,
                   d_ws, d_bias, w_out, norm2_g, ffn_w_gate, ffn_w_up, ffn_conv_w, ffn_w_down):
    bias_tab = jnp.broadcast_to(d_bias[l].T[:, :, None], (D_CHUNK, D_GROUPS, HEAD_DIM)).reshape(D_CHUNK, 256)
    return {
        'norm1_g': norm1_g[l][None, :], 'w_in': _pack_w_in(w_in[l]), 'a_conv_w': a_conv_w[l],
        'a_log': _pad_lanes(a_log[l]), 'a_dt_bias': _pad_lanes(a_dt_bias[l]),
        'a_norm_g': jnp.tile(a_norm_g[l], A_HEADS)[None, :], 'b_conv_w': b_conv_w[l], 'c_sinks': c_sinks[l],
        'd_ln_g': d_ln_g[l][None, :], 'd_ln_b': d_ln_b[l][None, :],
        'd_ws': d_ws[l].reshape(D_GROUPS * D_CHUNK, D_CHUNK), 'd_bias': bias_tab,
        'w_out': w_out[l].astype(BF16), 'norm2_g': norm2_g[l][None, :],
        'ffn_w_gate': ffn_w_gate[l].astype(BF16), 'ffn_w_up': ffn_w_up[l].astype(BF16),
        'ffn_conv_w': ffn_conv_w[l], 'ffn_w_down': ffn_w_down[l].astype(BF16),
    }


def _rope_tables(pos):
    half = ROPE_DIM // 2
    inv = jnp.power(ROPE_THETA, -jnp.arange(half, dtype=F32) * (2.0 / ROPE_DIM))
    ang = pos.astype(F32)[:, None] * inv[None, :]
    cos, sin = jnp.cos(ang), jnp.sin(ang)
    n = pos.shape[0]
    rest = HEAD_DIM - ROPE_DIM
    cos_h = jnp.concatenate([cos, cos, jnp.ones((n, rest), F32)], axis=1)
    sina_h = jnp.concatenate([-sin, jnp.zeros((n, half + rest), F32)], axis=1)
    sinb_h = jnp.concatenate([jnp.zeros((n, half), F32), sin, jnp.zeros((n, rest), F32)], axis=1)
    return tuple(jnp.concatenate([t, t], axis=1) for t in (cos_h, sina_h, sinb_h))


def _diag_blocks(s_bd):
    return jnp.stack([s_bd[:, HEAD_DIM * i:HEAD_DIM * (i + 1), HEAD_DIM * i:HEAD_DIM * (i + 1)]
                      for i in range(A_HEADS)], axis=1)


def _sample_pre_kernel(x_ref, n1g_ref, w_in_ref, aconv_w_ref, alog_ref, dtb_ref, bconv_w_ref, cos_ref, sina_ref,
                       sinb_ref, lng_ref, lnb_ref, wtab_ref, btab_ref, aconv0_ref, bconv0_ref,
                       q_ref, k_ref, v_ref, g_ref, beta_ref, z_ref, outb_ref, cq_ref, ck_ref, cv_ref, outd_ref,
                       dvn_ref, aconv_out_ref, bconv_out_ref, abuf, bbuf, *, rows, stride):
    R, S = rows, stride
    steps = R // S
    pa_rows = (A_CONV - 1) * S
    pb_rows = (B_CONV - 1) * S
    x = x_ref[...]
    h = _rms(x, n1g_ref[...]).astype(BF16)

    pa = _dot(h, w_in_ref[:, COL_A:COL_A + W_A])
    z_ref[...] = pa[:, 768:1024]
    small = pa[:, 1024:1152]
    abuf[0:pa_rows, :] = aconv0_ref[...]
    abuf[pa_rows:pa_rows + R, :] = pa[:, 0:768]
    y = abuf[0:R, :] * aconv_w_ref[0:1, :]
    for j in range(1, A_CONV):
        y = y + abuf[j * S:j * S + R, :] * aconv_w_ref[j:j + 1, :]
    aconv_out_ref[...] = abuf[R:R + pa_rows, :]
    qkv = _silu(y)
    q_raw = qkv[:, 0:256]
    k_raw = qkv[:, 256:512]
    v_ref[...] = qkv[:, 512:768]
    mask_bd = (_iota((256, 256), 0) >> 6) == (_iota((256, 256), 1) >> 6)
    ones_bd = jnp.where(mask_bd, 1.0, 0.0).astype(BF16)
    q_ref[...] = q_raw * lax.rsqrt(_dot_lhs3(q_raw * q_raw, ones_bd) + EPS) * (HEAD_DIM ** -0.5)
    k_ref[...] = k_raw * lax.rsqrt(_dot_lhs3(k_raw * k_raw, ones_bd) + EPS)
    g_ref[...] = -jnp.exp(alog_ref[...]) * _softplus(small + dtb_ref[...])
    beta_ref[...] = _sigmoid(small)

    pb = _dot(h, w_in_ref[:, COL_B:COL_B + W_B])
    bbuf[0:pb_rows, :] = bconv0_ref[...]
    bbuf[pb_rows:pb_rows + R, :] = pb[:, 256:512] * pb[:, 512:768]
    bx = bbuf[0:R, :] * bconv_w_ref[0:1, :]
    for j in range(1, B_CONV):
        bx = bx + bbuf[j * S:j * S + R, :] * bconv_w_ref[j:j + 1, :]
    bconv_out_ref[...] = bbuf[R:R + pb_rows, :]
    outb_ref[...] = pb[:, 0:256] * bx

    pc = _dot(h, w_in_ref[:, COL_C:COL_C + W_C])
    cos = cos_ref[...]
    sina = sina_ref[...]
    sinb = sinb_ref[...]
    cq = pc[:, 0:256]
    cq_ref[...] = (cq * jnp.concatenate([cos, cos], axis=1)
                   + pltpu.roll(cq, 256 - ROPE_DIM // 2, 1) * jnp.concatenate([sina, sina], axis=1)
                   + pltpu.roll(cq, ROPE_DIM // 2, 1) * jnp.concatenate([sinb, sinb], axis=1))
    ck = pc[:, 256:384]
    ck_ref[...] = (ck * cos + pltpu.roll(ck, 128 - ROPE_DIM // 2, 1) * sina
                   + pltpu.roll(ck, ROPE_DIM // 2, 1) * sinb)
    cv_ref[...] = pc[:, 384:512]

    pd = _dot(h, w_in_ref[:, COL_D:COL_D + W_D])
    du = _gelu_tanh(pd[:, 0:256])
    gv = _gelu_tanh(pd[:, 256:512])
    mu = jnp.mean(gv, axis=-1, keepdims=True)
    xc = gv - mu
    dvn = xc * lax.rsqrt(jnp.mean(xc * xc, axis=-1, keepdims=True) + EPS) * lng_ref[...] + lnb_ref[...]
    dvn_ref[...] = dvn
    for t in range(steps):
        mixed = btab_ref[t:t + 1, :]
        for s in range(t + 1):
            mixed = mixed + wtab_ref[t * steps + s:t * steps + s + 1, :] * dvn[s * S:(s + 1) * S, :]
        outd_ref[t * S:(t + 1) * S, :] = du[t * S:(t + 1) * S, :] * mixed


def _sample_pre_call(x, lw, tabs, wtab, btab, aconv0, bconv0, stride):
    R = x.shape[0]
    f = lambda *shape: jax.ShapeDtypeStruct(shape, F32)
    out_shape = [f(R, 256), f(R, 256), f(R, 256), f(R, 128), f(R, 128), f(R, 256), f(R, 256), f(R, 256),
                 f(R, 128), f(R, 128), f(R, 256), f(R, 256), f((A_CONV - 1) * stride, 768),
                 f((B_CONV - 1) * stride, 256)]
    return pl.pallas_call(
        functools.partial(_sample_pre_kernel, rows=R, stride=stride),
        out_shape=out_shape,
        scratch_shapes=[pltpu.VMEM((R + (A_CONV - 1) * stride, 768), F32),
                        pltpu.VMEM((R + (B_CONV - 1) * stride, 256), F32)],
        compiler_params=pltpu.CompilerParams(vmem_limit_bytes=VMEM_LIMIT_BYTES),
        name="sample_pre",
    )(x, lw['norm1_g'], lw['w_in'], lw['a_conv_w'], lw['a_log'], lw['a_dt_bias'], lw['b_conv_w'],
      tabs[0], tabs[1], tabs[2], lw['d_ln_g'], lw['d_ln_b'], wtab, btab, aconv0, bconv0)


def _sample_delta_kernel(g_ref, beta_ref, q_ref, k_ref, v_ref, s_ref, o_ref, snew_ref, *, steps):
    n_i = HEAD_DIM
    nb = s_ref.shape[-1]
    zeros = jnp.zeros((HEAD_DIM, nb), F32)

    def rows_of(i):
        return pl.ds(pl.multiple_of(i * HEAD_DIM, HEAD_DIM), HEAD_DIM)

    dec0 = jnp.exp(g_ref[0, 0])

    def first_pass(i, acc):
        return acc + k_ref[0, 0, pl.ds(i, 1), :] * (s_ref[0, rows_of(i), :] * dec0)

    ks = lax.fori_loop(0, n_i, first_pass, zeros)
    for t in range(steps):
        dec = jnp.exp(g_ref[0, t])
        v_new = beta_ref[0, t] * (v_ref[0, t] - ks)
        src = s_ref if t == 0 else snew_ref
        dec_next = jnp.exp(g_ref[0, t + 1]) if t + 1 < steps else None

        def update(i, carry, t=t, dec=dec, v_new=v_new, src=src, dec_next=dec_next):
            o_acc, ks_acc = carry
            blk = src[0, rows_of(i), :] * dec + k_ref[0, t, pl.ds(i, 1), :] * v_new
            snew_ref[0, rows_of(i), :] = blk
            o_acc = o_acc + q_ref[0, t, pl.ds(i, 1), :] * blk
            if dec_next is not None:
                ks_acc = ks_acc + k_ref[0, t + 1, pl.ds(i, 1), :] * (blk * dec_next)
            return o_acc, ks_acc

        o_acc, ks = lax.fori_loop(0, n_i, update, (zeros, zeros))
        o_ref[0, t] = o_acc


def _sample_delta_call(g, beta, q, k, v, s):
    nh, steps, _, nb = q.shape
    vec = pl.BlockSpec((1, steps, 1, nb), lambda h: (h, 0, 0, 0))
    mat = pl.BlockSpec((1, steps, HEAD_DIM, nb), lambda h: (h, 0, 0, 0))
    st = pl.BlockSpec((1, HEAD_DIM * HEAD_DIM, nb), lambda h: (h, 0, 0))
    return pl.pallas_call(
        functools.partial(_sample_delta_kernel, steps=steps),
        grid=(nh,),
        in_specs=[vec, vec, mat, mat, mat, st],
        out_specs=[mat, st],
        out_shape=[jax.ShapeDtypeStruct((nh, steps, HEAD_DIM, nb), F32),
                   jax.ShapeDtypeStruct((nh, HEAD_DIM * HEAD_DIM, nb), F32)],
        compiler_params=pltpu.CompilerParams(dimension_semantics=("arbitrary",),
                                             vmem_limit_bytes=VMEM_LIMIT_BYTES),
        name="sample_delta",
    )(g, beta, q, k, v, s)


SAMPLE_ATTN_BLOCK = 8
NEW_KEY_ROWS = 8


def _sample_attn_kernel(sinks_ref, qm_ref, kc_ref, kn_ref, vc_ref, vn_ref, o_ref, *, steps):
    nq = C_HEADS * steps
    nk = WINDOW + NEW_KEY_ROWS
    row = _iota((nq, nk), 0)
    col = _iota((nq, nk), 1)
    t_q = row & (steps - 1)
    valid = ((col < WINDOW) & (col > t_q)) | ((col >= WINDOW) & (col - WINDOW <= t_q))
    head = _iota((nq, 1), 0) >> 2
    sink = jnp.where(head == 0, sinks_ref[0],
                     jnp.where(head == 1, sinks_ref[1], jnp.where(head == 2, sinks_ref[2], sinks_ref[3])))
    for b in range(SAMPLE_ATTN_BLOCK):
        k_all = jnp.concatenate([kc_ref[b], kn_ref[b]], axis=0).astype(BF16)
        v_all = jnp.concatenate([vc_ref[b], vn_ref[b]], axis=0).astype(BF16)
        s = _dot_nt(qm_ref[b].astype(BF16), k_all) * (HEAD_DIM ** -0.5)
        s = jnp.where(valid, s, NEG_INF)
        m = jnp.maximum(jnp.max(s, axis=-1, keepdims=True), sink)
        p = jnp.exp(s - m)
        denom = jnp.sum(p, axis=-1, keepdims=True) + jnp.exp(sink - m)
        o_ref[b] = _dot((p / denom).astype(BF16), v_all)


def _sample_attn_call(sinks, qm, kc, kn, vc, vn, steps):
    bs, nq, _ = qm.shape
    BB = SAMPLE_ATTN_BLOCK
    blk = lambda r: pl.BlockSpec((BB, r, 128), lambda i, *_: (i, 0, 0))
    grid_spec = pltpu.PrefetchScalarGridSpec(
        num_scalar_prefetch=1, grid=(bs // BB,),
        in_specs=[blk(nq), blk(WINDOW), blk(NEW_KEY_ROWS), blk(WINDOW), blk(NEW_KEY_ROWS)],
        out_specs=blk(nq))
    return pl.pallas_call(
        functools.partial(_sample_attn_kernel, steps=steps),
        grid_spec=grid_spec,
        out_shape=jax.ShapeDtypeStruct((bs, nq, 128), F32),
        compiler_params=pltpu.CompilerParams(dimension_semantics=("arbitrary",)),
        name="sample_attn",
    )(sinks, qm, kc, kn, vc, vn)


def _sample_post_kernel(x_ref, o_ref, z_ref, outb_ref, outc_ref, outd_ref, anorm_ref, w_out_ref, x_out_ref):
    mask_bd = (_iota((256, 256), 0) >> 6) == (_iota((256, 256), 1) >> 6)
    ones_bd = jnp.where(mask_bd, 1.0, 0.0).astype(BF16)
    o = o_ref[...]
    out_a = (o * lax.rsqrt(_dot_lhs3(o * o, ones_bd) * (1.0 / HEAD_DIM) + EPS) * anorm_ref[...]
             * _silu(z_ref[...]))
    cat = jnp.concatenate([out_a, outb_ref[...], outc_ref[...], outd_ref[...]], axis=1).astype(BF16)
    x_out_ref[...] = x_ref[...] + _dot(cat, w_out_ref[...])


def _sample_post_call(x, o, z, out_b, out_c, out_d, lw):
    return pl.pallas_call(
        _sample_post_kernel,
        out_shape=jax.ShapeDtypeStruct(x.shape, F32),
        compiler_params=pltpu.CompilerParams(vmem_limit_bytes=VMEM_LIMIT_BYTES),
        name="sample_post",
    )(x, o, z, out_b, out_c, out_d, lw['a_norm_g'], lw['w_out'])


def _sample_mixer(x_tm, lw, tabs, d_ws_l, d_bias_l, a_state, a_conv, b_conv, c_k, c_v, bs, ts):
    to_tm = lambda a: jnp.swapaxes(a, 0, 1).reshape(a.shape[1] * bs, a.shape[2])
    from_tm = lambda a, n: jnp.swapaxes(a.reshape(n, bs, a.shape[-1]), 0, 1)
    wtab = jnp.repeat(d_ws_l[:, :ts, :ts].transpose(1, 2, 0).reshape(ts * ts, D_GROUPS), HEAD_DIM, axis=1)
    btab = jnp.repeat(d_bias_l[:, :ts].T, HEAD_DIM, axis=1)
    (q, k, v, g, beta, z, out_b, cq, ck, cv, out_d, dvn, a_tail, b_tail) = _sample_pre_call(
        x_tm, lw, tabs, wtab, btab, to_tm(a_conv), to_tm(b_conv), bs)

    heads_t = lambda a: a.reshape(ts, bs, A_HEADS, HEAD_DIM).transpose(2, 0, 3, 1)
    vec_t = lambda a, lo: a[:, lo:lo + A_HEADS].reshape(ts, bs, A_HEADS).transpose(2, 0, 1)[:, :, None, :]
    s_t = a_state.reshape(bs, A_HEADS, HEAD_DIM * HEAD_DIM).transpose(1, 2, 0)
    o_t, s_new_t = _sample_delta_call(vec_t(g, 0), vec_t(beta, A_HEADS), heads_t(q), heads_t(k), heads_t(v), s_t)
    o = o_t.transpose(1, 3, 0, 2).reshape(ts * bs, GROUP_WIDTH)
    a_state_new = s_new_t.transpose(2, 0, 1).reshape(bs, A_HEADS, HEAD_DIM, HEAD_DIM)

    cq4 = cq.reshape(ts, bs, C_HEADS, HEAD_DIM).transpose(1, 2, 0, 3)
    zq = jnp.zeros_like(cq4[:, 0])
    qm = jnp.concatenate(
        [jnp.concatenate([cq4[:, hh], zq] if hh // 2 == 0 else [zq, cq4[:, hh]], axis=-1) for hh in range(C_HEADS)],
        axis=1)
    pad_new = lambda a: jnp.concatenate([from_tm(a, ts), jnp.zeros((bs, NEW_KEY_ROWS - ts, 128), F32)], axis=1)
    kc = c_k.reshape(bs, WINDOW, 128)
    vc = c_v.reshape(bs, WINDOW, 128)
    o_att = _sample_attn_call(lw['c_sinks'], qm, kc, pad_new(ck), vc, pad_new(cv), ts)
    out_c = jnp.concatenate(
        [o_att[:, hh * ts:(hh + 1) * ts, (hh // 2) * HEAD_DIM:(hh // 2 + 1) * HEAD_DIM] for hh in range(C_HEADS)],
        axis=-1)
    out_c = jnp.swapaxes(out_c, 0, 1).reshape(ts * bs, GROUP_WIDTH)
    c_k_new = jnp.concatenate([kc[:, ts:], from_tm(ck, ts)], axis=1).reshape(bs, WINDOW, C_KV_HEADS, HEAD_DIM)
    c_v_new = jnp.concatenate([vc[:, ts:], from_tm(cv, ts)], axis=1).reshape(bs, WINDOW, C_KV_HEADS, HEAD_DIM)

    x2 = _sample_post_call(x_tm, o, z, out_b, out_c, out_d, lw)
    new = {'a_state': a_state_new, 'a_conv': from_tm(a_tail, A_CONV - 1), 'b_conv': from_tm(b_tail, B_CONV - 1),
           'c_k': c_k_new, 'c_v': c_v_new, 'd_v': from_tm(dvn, ts)}
    return x2, new


def kernel(x_prompt, x_sample, state_delta, state_delta_conv, state_shortconv, cache_win_k, cache_win_v,
           state_ffn_conv, norm1_g, w_in, a_conv_w, a_log, a_dt_bias, a_norm_g, b_conv_w, c_sinks, d_ln_g,
           d_ln_b, d_ws, d_bias, w_out, norm2_g, ffn_w_gate, ffn_w_up, ffn_conv_w, ffn_w_down, final_norm_g):
    bp, tp, _ = x_prompt.shape
    bs, ts, _ = x_sample.shape
    depth = w_in.shape[0]
    win_buf = cache_win_k.shape[2]
    pos_p = jnp.arange(tp, dtype=jnp.int32)
    pos_s = PAST_LEN + jnp.arange(ts, dtype=jnp.int32)
    assert win_buf == WINDOW and ts == 4
    tabs_p = _rope_tables(pos_p)
    tabs_s = tuple(jnp.repeat(t, bs, axis=0) for t in _rope_tables(pos_s))
    fng = final_norm_g[None, :]
    P = CARRY_ROWS

    hp = x_prompt
    hs = jnp.swapaxes(x_sample, 0, 1).reshape(ts * bs, D_MODEL)
    outs = {k: [] for k in ('sp', 'ss', 'acp', 'acs', 'bcp', 'bcs', 'ckp', 'cks', 'cvp', 'cvs', 'fcp', 'fcs', 'dv')}
    for l in range(depth):
        lw = _layer_weights(l, norm1_g, w_in, a_conv_w, a_log, a_dt_bias, a_norm_g, b_conv_w, c_sinks, d_ln_g,
                            d_ln_b, d_ws, d_bias, w_out, norm2_g, ffn_w_gate, ffn_w_up, ffn_conv_w, ffn_w_down)
        last = l == depth - 1
        hp, s_bd, acv, bcv, ckn, cvn = _mixer_call(
            hp, lw, tabs_p, jnp.zeros((bp, P, 768), F32), jnp.zeros((bp, P, 256), F32),
            jnp.zeros((bp, 256, 256), F32))
        hp, fcv = _ffn_call(hp, lw, jnp.zeros((bp, P, D_FF), F32), fng, tile=TILE_ROWS, stride=1, final_norm=last)
        outs['sp'].append(_diag_blocks(s_bd))
        outs['acp'].append(acv[:, P - (A_CONV - 1):])
        outs['bcp'].append(bcv[:, P - (B_CONV - 1):])
        outs['ckp'].append(ckn.reshape(bp, WINDOW, C_KV_HEADS, HEAD_DIM))
        outs['cvp'].append(cvn.reshape(bp, WINDOW, C_KV_HEADS, HEAD_DIM))
        outs['fcp'].append(fcv[:, P - (FFN_CONV - 1):])
        hs, ns = _sample_mixer(hs, lw, tabs_s, d_ws[l], d_bias[l], state_delta[l], state_delta_conv[l],
                               state_shortconv[l], cache_win_k[l], cache_win_v[l], bs, ts)
        f0 = jnp.swapaxes(state_ffn_conv[l], 0, 1).reshape(1, (FFN_CONV - 1) * bs, D_FF)
        ys_tm, fcs = _ffn_call(hs[None], lw, f0, fng, tile=ts * bs, stride=bs, final_norm=last)
        hs = ys_tm[0]
        outs['ss'].append(ns['a_state'])
        outs['acs'].append(ns['a_conv'])
        outs['bcs'].append(ns['b_conv'])
        outs['cks'].append(ns['c_k'])
        outs['cvs'].append(ns['c_v'])
        outs['fcs'].append(jnp.swapaxes(fcs.reshape(FFN_CONV - 1, bs, D_FF), 0, 1))
        outs['dv'].append(ns['d_v'])
    st = {k: jnp.stack(v) for k, v in outs.items()}
    hs = jnp.swapaxes(hs.reshape(ts, bs, D_MODEL), 0, 1)
    return (hp, hs, st['sp'], st['ss'], st['acp'], st['acs'], st['bcp'], st['bcs'], st['ckp'], st['cks'],
            st['cvp'], st['cvs'], st['fcp'], st['fcs'], st['dv'])
```

```python
import functools

import jax
import jax.numpy as jnp
import numpy as np
from jax import lax
from jax.experimental import pallas as pl
from jax.experimental.pallas import tpu as pltpu

F32 = jnp.float32
BF16 = jnp.bfloat16

D_MODEL = 1024
GROUP_WIDTH = 256
HEAD_DIM = 64
A_HEADS = 4
A_CONV = 4
A_CHUNK = 64
B_CONV = 3
C_HEADS = 4
C_KV_HEADS = 2
WINDOW = 128
ROPE_DIM = 16
ROPE_THETA = 500000.0
D_GROUPS = 4
D_CHUNK = 128
D_FF = 2816
FFN_CONV = 3
EPS = 1e-6
NEG_INF = -1e30
PAST_LEN = 16384

COL_A = 0
W_A = 1152
COL_B = 1152
W_B = 768
COL_C = 1920
W_C = 512
COL_D = 2432
W_D = 512
P_PACKED = 2944

TILE_ROWS = 512
CARRY_ROWS = 8
FFN_COL_CHUNK = 1408
VMEM_LIMIT_BYTES = 56 * 1024 * 1024


def _dot(a, b):
    return jnp.dot(a, b, preferred_element_type=F32)


def _dot_nt(a, b):
    return lax.dot_general(a, b, (((1,), (1,)), ((), ())), preferred_element_type=F32)


def _dot_tn(a, b):
    return lax.dot_general(a, b, (((0,), (0,)), ((), ())), preferred_element_type=F32)


def _split3(x):
    hi = x.astype(BF16)
    r1 = x - hi.astype(F32)
    mid = r1.astype(BF16)
    lo = (r1 - mid.astype(F32)).astype(BF16)
    return hi, mid, lo


def _dot_lhs3(x, w01):
    hi, mid, lo = _split3(x)
    return _dot(hi, w01) + _dot(mid, w01) + _dot(lo, w01)


def _dot_lhs2(x, w01):
    hi = x.astype(BF16)
    mid = (x - hi.astype(F32)).astype(BF16)
    return _dot(hi, w01) + _dot(mid, w01)


def _sigmoid(x):
    return 1.0 / (1.0 + jnp.exp(-x))


def _silu(x):
    return x * _sigmoid(x)


def _softplus(x):
    return jnp.maximum(x, 0.0) + jnp.log(1.0 + jnp.exp(-jnp.abs(x)))


def _gelu_tanh(x):
    return 0.5 * x * (1.0 + jnp.tanh(np.sqrt(2.0 / np.pi).astype(np.float32) * (x + 0.044715 * (x * x * x))))


def _rms(x, g):
    return x * lax.rsqrt(jnp.mean(x * x, axis=-1, keepdims=True) + EPS) * g


def _iota(shape, dim):
    return lax.broadcasted_iota(jnp.int32, shape, dim)


def _mixer_kernel(sinks_ref, x_ref, n1g_ref, w_in_ref, aconv_w_ref, alog_ref, dtb_ref, anorm_ref,
                  bconv_w_ref, cos_ref, sina_ref, sinb_ref, lng_ref, lnb_ref, ws_ref, dbias_ref,
                  w_out_ref, aconv0_ref, bconv0_ref, s0_ref,
                  x_out_ref, s_out_ref, aconv_out_ref, bconv_out_ref, ck_out_ref, cv_out_ref,
                  abuf, bbuf, kbuf, vbuf, q_s, k_s, v_s, gcb_s, bb_s, o_s, s_scr, *, tile):
    T = tile
    si = pl.program_id(1)
    P = CARRY_ROWS

    @pl.when(si == 0)
    def _():
        abuf[0:P, :] = aconv0_ref[0]
        bbuf[0:P, :] = bconv0_ref[0]
        kbuf[0:WINDOW, :] = jnp.zeros((WINDOW, 128), F32)
        vbuf[0:WINDOW, :] = jnp.zeros((WINDOW, 128), F32)
        s_scr[...] = s0_ref[0]

    x = x_ref[0]
    h = _rms(x, n1g_ref[...]).astype(BF16)

    pa = _dot(h, w_in_ref[:, COL_A:COL_A + W_A])
    z = pa[:, 768:1024]
    small = pa[:, 1024:1152]
    abuf[P:P + T, :] = pa[:, 0:768]
    y = (abuf[P:P + T, :] * aconv_w_ref[3:4, :] + abuf[P - 1:P - 1 + T, :] * aconv_w_ref[2:3, :]
         + abuf[P - 2:P - 2 + T, :] * aconv_w_ref[1:2, :] + abuf[P - 3:P - 3 + T, :] * aconv_w_ref[0:1, :])
    tail_a = abuf[T:T + P, :]
    abuf[0:P, :] = tail_a
    aconv_out_ref[0] = tail_a
    qkv = _silu(y)
    q_raw = qkv[:, 0:256]
    k_raw = qkv[:, 256:512]
    v_s[...] = qkv[:, 512:768]

    r256 = _iota((256, 256), 0) >> 6
    c256 = _iota((256, 256), 1) >> 6
    mask_bd = r256 == c256
    ones_bd = jnp.where(mask_bd, 1.0, 0.0).astype(BF16)
    q_s[...] = q_raw * lax.rsqrt(_dot_lhs2(q_raw * q_raw, ones_bd) + EPS) * (HEAD_DIM ** -0.5)
    k_s[...] = k_raw * lax.rsqrt(_dot_lhs2(k_raw * k_raw, ones_bd) + EPS)

    g_log = -jnp.exp(alog_ref[...]) * _softplus(small + dtb_ref[...])
    beta = _sigmoid(small)
    gbeta = jnp.where(_iota((T, 128), 1) < A_HEADS, g_log, beta)
    expand = jnp.where(_iota((128, 512), 0) == (_iota((128, 512), 1) >> 6), 1.0, 0.0).astype(BF16)
    gbb = _dot_lhs3(gbeta, expand)
    bb_s[...] = gbb[:, 256:512]

    def chunk_cumsum(xv):
        row_in_chunk = _iota(xv.shape, 0) & (A_CHUNK - 1)
        step = 1
        while step < A_CHUNK:
            xv = xv + jnp.where(row_in_chunk >= step, pltpu.roll(xv, step, 0), 0.0)
            step *= 2
        return xv

    gcb_s[...] = chunk_cumsum(gbb[:, 0:256])
    gct = chunk_cumsum(g_log).T[0:8, :]
    low_half = (_iota((1, T), 1) & A_CHUNK) == 0
    gct_r = pltpu.roll(gct, A_CHUNK, 1)
    gct_l = pltpu.roll(gct, T - A_CHUNK, 1)
    even_rows = [jnp.where(low_half, gct[a:a + 1, :], gct_r[a + 1:a + 2, :]) for a in (0, 2)]
    odd_rows = [jnp.where(low_half, gct_l[a:a + 1, :], gct[a + 1:a + 2, :]) for a in (0, 2)]

    ri = _iota((A_CHUNK, 256), 0)
    ci = _iota((A_CHUNK, 256), 1) & (A_CHUNK - 1)
    causal_t = ri >= ci
    strict_t = ri > ci
    eye_t = jnp.where(ri == ci, 1.0, 0.0)

    def bd16(x16):
        return jnp.concatenate([x16, x16, x16, x16], axis=0) * ones_bd

    n_chunks = T // A_CHUNK
    t_inv, pw, qk, qdec, e_tail, vb16, kbe16 = [], [], [], [], [], [], []
    for c in range(n_chunks):
        rows = slice(c * A_CHUNK, (c + 1) * A_CHUNK)
        blk = slice((c // 2) * 128, (c // 2) * 128 + 128)
        src = even_rows if c % 2 == 0 else odd_rows
        gc_row = jnp.concatenate([src[0][:, blk], src[1][:, blk]], axis=1)
        qc = q_s[rows, :]
        kc = k_s[rows, :]
        bbc = bb_s[rows, :]
        gcb = gcb_s[rows, :]
        decay = jnp.where(causal_t, jnp.exp(jnp.where(causal_t, gcb - gc_row, 0.0)), 0.0)
        eg = jnp.exp(gcb)
        kb = kc * bbc
        aq = _dot_nt(jnp.concatenate([kb, qc], axis=0).astype(BF16), bd16(kc.astype(BF16)))
        a_mat = jnp.where(strict_t, aq[0:A_CHUNK] * decay, 0.0)
        qk.append(aq[A_CHUNK:2 * A_CHUNK] * decay)
        t_inv.append(eye_t - a_mat)
        pw.append(a_mat)
        qdec.append(qc * eg)
        e_tail.append(jnp.exp(gcb[A_CHUNK - 1:A_CHUNK, :] - gcb))
        vb16.append((v_s[rows, :] * bbc).astype(BF16))
        kbe16.append((kb * eg).astype(BF16))
    for c in range(n_chunks):
        p16 = pw[c].astype(BF16)
        pw[c] = _dot(p16, bd16(p16))
    for _ in range(4):
        for c in range(n_chunks):
            p16 = pw[c].astype(BF16)
            res = _dot(jnp.concatenate([p16, t_inv[c].astype(BF16)], axis=0), bd16(p16))
            pw[c] = res[0:A_CHUNK]
            t_inv[c] = t_inv[c] + res[A_CHUNK:2 * A_CHUNK]
    u, w = [], []
    for c in range(n_chunks):
        t_c = t_inv[c] + _dot(t_inv[c].astype(BF16), bd16(pw[c].astype(BF16)))
        uw = _dot(t_c.astype(BF16), jnp.concatenate([bd16(vb16[c]), bd16(kbe16[c])], axis=1))
        u.append(uw[:, 0:256])
        w.append(uw[:, 256:512])
    s_bd = s_scr[...]
    for c in range(n_chunks):
        rows = slice(c * A_CHUNK, (c + 1) * A_CHUNK)
        wq = _dot(jnp.concatenate([w[c], qdec[c]], axis=0).astype(BF16), s_bd.astype(BF16))
        v_new = u[c] - wq[0:A_CHUNK]
        o_s[rows, :] = wq[A_CHUNK:2 * A_CHUNK] + _dot(qk[c].astype(BF16), bd16(v_new.astype(BF16)))
        kv = _dot_tn(k_s[rows, :].astype(BF16), (v_new * e_tail[c]).astype(BF16))
        g_last = gcb_s[(c + 1) * A_CHUNK - 1:(c + 1) * A_CHUNK, :]
        s_bd = s_bd * jnp.exp(g_last) + jnp.where(mask_bd, kv, 0.0)
    s_scr[...] = s_bd
    s_out_ref[0] = s_bd
    o = o_s[...]
    out_a = (o * lax.rsqrt(_dot_lhs2(o * o, ones_bd) * (1.0 / HEAD_DIM) + EPS) * anorm_ref[...]
             * _silu(z))

    pb = _dot(h, w_in_ref[:, COL_B:COL_B + W_B])
    bbuf[P:P + T, :] = pb[:, 256:512] * pb[:, 512:768]
    bx = (bbuf[P:P + T, :] * bconv_w_ref[2:3, :] + bbuf[P - 1:P - 1 + T, :] * bconv_w_ref[1:2, :]
          + bbuf[P - 2:P - 2 + T, :] * bconv_w_ref[0:1, :])
    tail_b = bbuf[T:T + P, :]
    bbuf[0:P, :] = tail_b
    bconv_out_ref[0] = tail_b
    out_b = pb[:, 0:256] * bx

    pc = _dot(h, w_in_ref[:, COL_C:COL_C + W_C])
    cos = cos_ref[...]
    sina = sina_ref[...]
    sinb = sinb_ref[...]
    cq = pc[:, 0:256]
    cq = (cq * jnp.concatenate([cos, cos], axis=1)
          + pltpu.roll(cq, 256 - ROPE_DIM // 2, 1) * jnp.concatenate([sina, sina], axis=1)
          + pltpu.roll(cq, ROPE_DIM // 2, 1) * jnp.concatenate([sinb, sinb], axis=1))
    ck = pc[:, 256:384]
    ck = ck * cos + pltpu.roll(ck, 128 - ROPE_DIM // 2, 1) * sina + pltpu.roll(ck, ROPE_DIM // 2, 1) * sinb
    kbuf[WINDOW:WINDOW + T, :] = ck
    vbuf[WINDOW:WINDOW + T, :] = pc[:, 384:512]
    ck_out_ref[0] = kbuf[T:T + WINDOW, :]
    cv_out_ref[0] = vbuf[T:T + WINDOW, :]

    lane128 = _iota((2 * WINDOW, 128), 1)
    low = _iota((WINDOW, 128), 1) < HEAD_DIM
    qrow = _iota((2 * WINDOW, 2 * WINDOW), 0) & (WINDOW - 1)
    kcol = _iota((2 * WINDOW, 2 * WINDOW), 1)
    band = (kcol > qrow) & (kcol <= qrow + WINDOW)
    top_half = _iota((2 * WINDOW, 1), 0) < WINDOW
    out_c_blocks = []
    for n in range(T // WINDOW):
        first_key = si * T + (n - 1) * WINDOW
        valid = band & (kcol + first_key >= 0)
        kwin = kbuf[n * WINDOW:(n + 2) * WINDOW, :]
        vwin = vbuf[n * WINDOW:(n + 2) * WINDOW, :]
        k_sw = pltpu.roll(kwin, HEAD_DIM, 1)
        v_sw = pltpu.roll(vwin, HEAD_DIM, 1)
        pair_out = []
        for g in range(C_KV_HEADS):
            own = (lane128 < HEAD_DIM) if g == 0 else (lane128 >= HEAD_DIM)
            k_dup = jnp.where(own, kwin, k_sw).astype(BF16)
            v_dup = jnp.where(own, vwin, v_sw).astype(BF16)
            qp = cq[n * WINDOW:(n + 1) * WINDOW, g * 128:(g + 1) * 128]
            q_st = jnp.concatenate([jnp.where(low, qp, 0.0), jnp.where(low, 0.0, qp)], axis=0).astype(BF16)
            s = _dot_nt(q_st, k_dup) * (HEAD_DIM ** -0.5)
            s = jnp.where(valid, s, NEG_INF)
            sink = jnp.where(top_half, sinks_ref[2 * g], sinks_ref[2 * g + 1])
            m = jnp.maximum(jnp.max(s, axis=-1, keepdims=True), sink)
            p = jnp.exp(s - m)
            denom = jnp.sum(p, axis=-1, keepdims=True) + jnp.exp(sink - m)
            o2 = _dot((p / denom).astype(BF16), v_dup)
            pair_out.append(jnp.where(low, o2[0:WINDOW], o2[WINDOW:2 * WINDOW]))
        out_c_blocks.append(jnp.concatenate(pair_out, axis=1))
    out_c = jnp.concatenate(out_c_blocks, axis=0)
    kbuf[0:WINDOW, :] = kbuf[T:T + WINDOW, :]
    vbuf[0:WINDOW, :] = vbuf[T:T + WINDOW, :]

    pd = _dot(h, w_in_ref[:, COL_D:COL_D + W_D])
    du = _gelu_tanh(pd[:, 0:256])
    gv = _gelu_tanh(pd[:, 256:512])
    mu = jnp.mean(gv, axis=-1, keepdims=True)
    xc = gv - mu
    dvn = xc * lax.rsqrt(jnp.mean(xc * xc, axis=-1, keepdims=True) + EPS) * lng_ref[...] + lnb_ref[...]
    wr = _iota((D_GROUPS * D_CHUNK, D_CHUNK), 0) & (D_CHUNK - 1)
    wc = _iota((D_GROUPS * D_CHUNK, D_CHUNK), 1)
    wm = jnp.where(wr >= wc, ws_ref[...], 0.0).astype(BF16)
    lane_grp = _iota((D_CHUNK, 256), 1) >> 6
    out_d_blocks = []
    for n in range(T // D_CHUNK):
        mx = _dot(wm, dvn[n * D_CHUNK:(n + 1) * D_CHUNK, :].astype(BF16))
        mixed = dbias_ref[...]
        for grp in range(D_GROUPS):
            mixed = mixed + jnp.where(lane_grp == grp, mx[grp * D_CHUNK:(grp + 1) * D_CHUNK, :], 0.0)
        out_d_blocks.append(du[n * D_CHUNK:(n + 1) * D_CHUNK, :] * mixed)
    out_d = jnp.concatenate(out_d_blocks, axis=0)

    cat = jnp.concatenate([out_a, out_b, out_c, out_d], axis=1).astype(BF16)
    x_out_ref[0] = x + _dot(cat, w_out_ref[...])


def _mixer_call(x, lw, tabs, aconv0, bconv0, s0):
    bsz, seq, _ = x.shape
    T = TILE_ROWS
    ns = seq // T
    P = CARRY_ROWS
    full = lambda shape: pl.BlockSpec(shape, lambda b, s, *_: (0,) * len(shape),
                                      pipeline_mode=pl.Buffered(1))
    per_b = lambda shape: pl.BlockSpec((1,) + shape, lambda b, s, *_: (b,) + (0,) * len(shape))
    tab = pl.BlockSpec((T, 128), lambda b, s, *_: (s, 0))
    grid_spec = pltpu.PrefetchScalarGridSpec(
        num_scalar_prefetch=1,
        grid=(bsz, ns),
        in_specs=[
            pl.BlockSpec((1, T, D_MODEL), lambda b, s, *_: (b, s, 0)),
            full((1, D_MODEL)), full((D_MODEL, P_PACKED)), full((A_CONV, 768)), full((1, 128)), full((1, 128)),
            full((1, 256)), full((B_CONV, 256)), tab, tab, tab, full((1, 256)), full((1, 256)),
            full((D_GROUPS * D_CHUNK, D_CHUNK)), full((D_CHUNK, 256)), full((D_MODEL, D_MODEL)),
            per_b((P, 768)), per_b((P, 256)), per_b((256, 256)),
        ],
        out_specs=[
            pl.BlockSpec((1, T, D_MODEL), lambda b, s, *_: (b, s, 0)),
            per_b((256, 256)), per_b((P, 768)), per_b((P, 256)), per_b((WINDOW, 128)), per_b((WINDOW, 128)),
        ],
        scratch_shapes=[
            pltpu.VMEM((T + P, 768), F32), pltpu.VMEM((T + P, 256), F32),
            pltpu.VMEM((T + WINDOW, 128), F32), pltpu.VMEM((T + WINDOW, 128), F32),
            pltpu.VMEM((T, 256), F32), pltpu.VMEM((T, 256), F32), pltpu.VMEM((T, 256), F32),
            pltpu.VMEM((T, 256), F32), pltpu.VMEM((T, 256), F32), pltpu.VMEM((T, 256), F32),
            pltpu.VMEM((256, 256), F32),
        ],
    )
    out_shape = [
        jax.ShapeDtypeStruct((bsz, seq, D_MODEL), F32),
        jax.ShapeDtypeStruct((bsz, 256, 256), F32),
        jax.ShapeDtypeStruct((bsz, P, 768), F32),
        jax.ShapeDtypeStruct((bsz, P, 256), F32),
        jax.ShapeDtypeStruct((bsz, WINDOW, 128), F32),
        jax.ShapeDtypeStruct((bsz, WINDOW, 128), F32),
    ]
    return pl.pallas_call(
        functools.partial(_mixer_kernel, tile=T),
        grid_spec=grid_spec,
        out_shape=out_shape,
        compiler_params=pltpu.CompilerParams(
            dimension_semantics=("arbitrary", "arbitrary"), vmem_limit_bytes=VMEM_LIMIT_BYTES),
        name="mixer",
    )(lw['c_sinks'], x, lw['norm1_g'], lw['w_in'], lw['a_conv_w'], lw['a_log'], lw['a_dt_bias'],
      lw['a_norm_g'], lw['b_conv_w'], tabs[0], tabs[1], tabs[2], lw['d_ln_g'], lw['d_ln_b'],
      lw['d_ws'], lw['d_bias'], lw['w_out'], aconv0, bconv0, s0)


def _ffn_kernel(x_ref, n2g_ref, wg_ref, wu_ref, cw_ref, wd_ref, fconv0_ref, fng_ref,
                x_out_ref, fconv_out_ref, gbuf, *, tile, stride, final_norm):
    T = tile
    P = (FFN_CONV - 1) * stride if stride > 1 else CARRY_ROWS
    si = pl.program_id(1)

    @pl.when(si == 0)
    def _():
        gbuf[0:P, :] = fconv0_ref[0]

    x = x_ref[0]
    h = _rms(x, n2g_ref[...]).astype(BF16)
    acc = x
    for c0 in range(0, D_FF, FFN_COL_CHUNK):
        cols = slice(c0, c0 + FFN_COL_CHUNK)
        gbuf[P:P + T, cols] = _dot(h, wg_ref[:, cols])
        up = _dot(h, wu_ref[:, cols])
        gate = (gbuf[P:P + T, cols] * cw_ref[2:3, cols] + gbuf[P - stride:P - stride + T, cols] * cw_ref[1:2, cols]
                + gbuf[P - 2 * stride:P - 2 * stride + T, cols] * cw_ref[0:1, cols])
        acc = acc + _dot((_silu(gate) * up).astype(BF16), wd_ref[cols, :])
    tail = gbuf[T:T + P, :]
    gbuf[0:P, :] = tail
    fconv_out_ref[0] = tail
    if final_norm:
        acc = _rms(acc, fng_ref[...])
    x_out_ref[0] = acc


def _ffn_call(x, lw, fconv0, final_g, *, tile, stride, final_norm):
    bsz, seq, _ = x.shape
    T = tile
    P = (FFN_CONV - 1) * stride if stride > 1 else CARRY_ROWS
    full = lambda shape: pl.BlockSpec(shape, lambda b, s: (0,) * len(shape), pipeline_mode=pl.Buffered(1))
    per_b = lambda shape: pl.BlockSpec((1,) + shape, lambda b, s: (b,) + (0,) * len(shape))
    return pl.pallas_call(
        functools.partial(_ffn_kernel, tile=T, stride=stride, final_norm=final_norm),
        grid=(bsz, seq // T),
        in_specs=[
            pl.BlockSpec((1, T, D_MODEL), lambda b, s: (b, s, 0)),
            full((1, D_MODEL)), full((D_MODEL, D_FF)), full((D_MODEL, D_FF)), full((FFN_CONV, D_FF)),
            full((D_FF, D_MODEL)), per_b((P, D_FF)), full((1, D_MODEL)),
        ],
        out_specs=[pl.BlockSpec((1, T, D_MODEL), lambda b, s: (b, s, 0)), per_b((P, D_FF))],
        out_shape=[jax.ShapeDtypeStruct((bsz, seq, D_MODEL), F32), jax.ShapeDtypeStruct((bsz, P, D_FF), F32)],
        scratch_shapes=[pltpu.VMEM((T + P, D_FF), F32)],
        compiler_params=pltpu.CompilerParams(
            dimension_semantics=("arbitrary", "arbitrary"), vmem_limit_bytes=VMEM_LIMIT_BYTES),
        name="ffn",
    )(x, lw['norm2_g'], lw['ffn_w_gate'], lw['ffn_w_up'], lw['ffn_conv_w'], lw['ffn_w_down'], fconv0, final_g)


def _pack_w_in(w):
    pad = jnp.zeros((D_MODEL, 128 - 2 * A_HEADS), w.dtype)
    small = jnp.concatenate([w[:, 1028:1032], w[:, 1024:1028], pad], axis=1)
    return jnp.concatenate([w[:, 0:1024], small, w[:, 1032:]], axis=1).astype(BF16)


def _pad_lanes(v, width=128):
    return jnp.concatenate([v, jnp.zeros((width - v.shape[0],), v.dtype)])[None, :]


def _layer_weights(l, norm1_g, w_in, a_conv_w, a_log, a_dt_bias, a_norm_g, b_conv_w, c_sinks, d_ln_g, d_ln_b,
                   d_ws, d_bias, w_out, norm2_g, ffn_w_gate, ffn_w_up, ffn_conv_w, ffn_w_down):
    bias_tab = jnp.broadcast_to(d_bias[l].T[:, :, None], (D_CHUNK, D_GROUPS, HEAD_DIM)).reshape(D_CHUNK, 256)
    return {
        'norm1_g': norm1_g[l][None, :], 'w_in': _pack_w_in(w_in[l]), 'a_conv_w': a_conv_w[l],
        'a_log': _pad_lanes(a_log[l]), 'a_dt_bias': _pad_lanes(a_dt_bias[l]),
        'a_norm_g': jnp.tile(a_norm_g[l], A_HEADS)[None, :], 'b_conv_w': b_conv_w[l], 'c_sinks': c_sinks[l],
        'd_ln_g': d_ln_g[l][None, :], 'd_ln_b': d_ln_b[l][None, :],
        'd_ws': d_ws[l].reshape(D_GROUPS * D_CHUNK, D_CHUNK), 'd_bias': bias_tab,
        'w_out': w_out[l].astype(BF16), 'norm2_g': norm2_g[l][None, :],
        'ffn_w_gate': ffn_w_gate[l].astype(BF16), 'ffn_w_up': ffn_w_up[l].astype(BF16),
        'ffn_conv_w': ffn_conv_w[l], 'ffn_w_down': ffn_w_down[l].astype(BF16),
    }


def _rope_tables(pos):
    half = ROPE_DIM // 2
    inv = jnp.power(ROPE_THETA, -jnp.arange(half, dtype=F32) * (2.0 / ROPE_DIM))
    ang = pos.astype(F32)[:, None] * inv[None, :]
    cos, sin = jnp.cos(ang), jnp.sin(ang)
    n = pos.shape[0]
    rest = HEAD_DIM - ROPE_DIM
    cos_h = jnp.concatenate([cos, cos, jnp.ones((n, rest), F32)], axis=1)
    sina_h = jnp.concatenate([-sin, jnp.zeros((n, half + rest), F32)], axis=1)
    sinb_h = jnp.concatenate([jnp.zeros((n, half), F32), sin, jnp.zeros((n, rest), F32)], axis=1)
    return tuple(jnp.concatenate([t, t], axis=1) for t in (cos_h, sina_h, sinb_h))


def _diag_blocks(s_bd):
    return jnp.stack([s_bd[:, HEAD_DIM * i:HEAD_DIM * (i + 1), HEAD_DIM * i:HEAD_DIM * (i + 1)]
                      for i in range(A_HEADS)], axis=1)


def _sample_pre_kernel(x_ref, n1g_ref, w_in_ref, aconv_w_ref, alog_ref, dtb_ref, bconv_w_ref, cos_ref, sina_ref,
                       sinb_ref, lng_ref, lnb_ref, wtab_ref, btab_ref, aconv0_ref, bconv0_ref,
                       q_ref, k_ref, v_ref, g_ref, beta_ref, z_ref, outb_ref, cq_ref, ck_ref, cv_ref, outd_ref,
                       dvn_ref, aconv_out_ref, bconv_out_ref, abuf, bbuf, *, rows, stride):
    R, S = rows, stride
    steps = R // S
    pa_rows = (A_CONV - 1) * S
    pb_rows = (B_CONV - 1) * S
    x = x_ref[...]
    h = _rms(x, n1g_ref[...]).astype(BF16)

    pa = _dot(h, w_in_ref[:, COL_A:COL_A + W_A])
    z_ref[...] = pa[:, 768:1024]
    small = pa[:, 1024:1152]
    abuf[0:pa_rows, :] = aconv0_ref[...]
    abuf[pa_rows:pa_rows + R, :] = pa[:, 0:768]
    y = abuf[0:R, :] * aconv_w_ref[0:1, :]
    for j in range(1, A_CONV):
        y = y + abuf[j * S:j * S + R, :] * aconv_w_ref[j:j + 1, :]
    aconv_out_ref[...] = abuf[R:R + pa_rows, :]
    qkv = _silu(y)
    q_raw = qkv[:, 0:256]
    k_raw = qkv[:, 256:512]
    v_ref[...] = qkv[:, 512:768]
    mask_bd = (_iota((256, 256), 0) >> 6) == (_iota((256, 256), 1) >> 6)
    ones_bd = jnp.where(mask_bd, 1.0, 0.0).astype(BF16)
    q_ref[...] = q_raw * lax.rsqrt(_dot_lhs3(q_raw * q_raw, ones_bd) + EPS) * (HEAD_DIM ** -0.5)
    k_ref[...] = k_raw * lax.rsqrt(_dot_lhs3(k_raw * k_raw, ones_bd) + EPS)
    g_ref[...] = -jnp.exp(alog_ref[...]) * _softplus(small + dtb_ref[...])
    beta_ref[...] = _sigmoid(small)

    pb = _dot(h, w_in_ref[:, COL_B:COL_B + W_B])
    bbuf[0:pb_rows, :] = bconv0_ref[...]
    bbuf[pb_rows:pb_rows + R, :] = pb[:, 256:512] * pb[:, 512:768]
    bx = bbuf[0:R, :] * bconv_w_ref[0:1, :]
    for j in range(1, B_CONV):
        bx = bx + bbuf[j * S:j * S + R, :] * bconv_w_ref[j:j + 1, :]
    bconv_out_ref[...] = bbuf[R:R + pb_rows, :]
    outb_ref[...] = pb[:, 0:256] * bx

    pc = _dot(h, w_in_ref[:, COL_C:COL_C + W_C])
    cos = cos_ref[...]
    sina = sina_ref[...]
    sinb = sinb_ref[...]
    cq = pc[:, 0:256]
    cq_ref[...] = (cq * jnp.concatenate([cos, cos], axis=1)
                   + pltpu.roll(cq, 256 - ROPE_DIM // 2, 1) * jnp.concatenate([sina, sina], axis=1)
                   + pltpu.roll(cq, ROPE_DIM // 2, 1) * jnp.concatenate([sinb, sinb], axis=1))
    ck = pc[:, 256:384]
    ck_ref[...] = (ck * cos + pltpu.roll(ck, 128 - ROPE_DIM // 2, 1) * sina
                   + pltpu.roll(ck, ROPE_DIM // 2, 1) * sinb)
    cv_ref[...] = pc[:, 384:512]

    pd = _dot(h, w_in_ref[:, COL_D:COL_D + W_D])
    du = _gelu_tanh(pd[:, 0:256])
    gv = _gelu_tanh(pd[:, 256:512])
    mu = jnp.mean(gv, axis=-1, keepdims=True)
    xc = gv - mu
    dvn = xc * lax.rsqrt(jnp.mean(xc * xc, axis=-1, keepdims=True) + EPS) * lng_ref[...] + lnb_ref[...]
    dvn_ref[...] = dvn
    for t in range(steps):
        mixed = btab_ref[t:t + 1, :]
        for s in range(t + 1):
            mixed = mixed + wtab_ref[t * steps + s:t * steps + s + 1, :] * dvn[s * S:(s + 1) * S, :]
        outd_ref[t * S:(t + 1) * S, :] = du[t * S:(t + 1) * S, :] * mixed


def _sample_pre_call(x, lw, tabs, wtab, btab, aconv0, bconv0, stride):
    R = x.shape[0]
    f = lambda *shape: jax.ShapeDtypeStruct(shape, F32)
    out_shape = [f(R, 256), f(R, 256), f(R, 256), f(R, 128), f(R, 128), f(R, 256), f(R, 256), f(R, 256),
                 f(R, 128), f(R, 128), f(R, 256), f(R, 256), f((A_CONV - 1) * stride, 768),
                 f((B_CONV - 1) * stride, 256)]
    return pl.pallas_call(
        functools.partial(_sample_pre_kernel, rows=R, stride=stride),
        out_shape=out_shape,
        scratch_shapes=[pltpu.VMEM((R + (A_CONV - 1) * stride, 768), F32),
                        pltpu.VMEM((R + (B_CONV - 1) * stride, 256), F32)],
        compiler_params=pltpu.CompilerParams(vmem_limit_bytes=VMEM_LIMIT_BYTES),
        name="sample_pre",
    )(x, lw['norm1_g'], lw['w_in'], lw['a_conv_w'], lw['a_log'], lw['a_dt_bias'], lw['b_conv_w'],
      tabs[0], tabs[1], tabs[2], lw['d_ln_g'], lw['d_ln_b'], wtab, btab, aconv0, bconv0)


def _sample_delta_kernel(g_ref, beta_ref, q_ref, k_ref, v_ref, s_ref, o_ref, snew_ref, *, steps):
    n_i = HEAD_DIM
    nb = s_ref.shape[-1]
    zeros = jnp.zeros((HEAD_DIM, nb), F32)

    def rows_of(i):
        return pl.ds(pl.multiple_of(i * HEAD_DIM, HEAD_DIM), HEAD_DIM)

    dec0 = jnp.exp(g_ref[0, 0])

    def first_pass(i, acc):
        return acc + k_ref[0, 0, pl.ds(i, 1), :] * (s_ref[0, rows_of(i), :] * dec0)

    ks = lax.fori_loop(0, n_i, first_pass, zeros)
    for t in range(steps):
        dec = jnp.exp(g_ref[0, t])
        v_new = beta_ref[0, t] * (v_ref[0, t] - ks)
        src = s_ref if t == 0 else snew_ref
        dec_next = jnp.exp(g_ref[0, t + 1]) if t + 1 < steps else None

        def update(i, carry, t=t, dec=dec, v_new=v_new, src=src, dec_next=dec_next):
            o_acc, ks_acc = carry
            blk = src[0, rows_of(i), :] * dec + k_ref[0, t, pl.ds(i, 1), :] * v_new
            snew_ref[0, rows_of(i), :] = blk
            o_acc = o_acc + q_ref[0, t, pl.ds(i, 1), :] * blk
            if dec_next is not None:
                ks_acc = ks_acc + k_ref[0, t + 1, pl.ds(i, 1), :] * (blk * dec_next)
            return o_acc, ks_acc

        o_acc, ks = lax.fori_loop(0, n_i, update, (zeros, zeros))
        o_ref[0, t] = o_acc


def _sample_delta_call(g, beta, q, k, v, s):
    nh, steps, _, nb = q.shape
    vec = pl.BlockSpec((1, steps, 1, nb), lambda h: (h, 0, 0, 0))
    mat = pl.BlockSpec((1, steps, HEAD_DIM, nb), lambda h: (h, 0, 0, 0))
    st = pl.BlockSpec((1, HEAD_DIM * HEAD_DIM, nb), lambda h: (h, 0, 0))
    return pl.pallas_call(
        functools.partial(_sample_delta_kernel, steps=steps),
        grid=(nh,),
        in_specs=[vec, vec, mat, mat, mat, st],
        out_specs=[mat, st],
        out_shape=[jax.ShapeDtypeStruct((nh, steps, HEAD_DIM, nb), F32),
                   jax.ShapeDtypeStruct((nh, HEAD_DIM * HEAD_DIM, nb), F32)],
        compiler_params=pltpu.CompilerParams(dimension_semantics=("arbitrary",),
                                             vmem_limit_bytes=VMEM_LIMIT_BYTES),
        name="sample_delta",
    )(g, beta, q, k, v, s)


SAMPLE_ATTN_BLOCK = 8
NEW_KEY_ROWS = 8


def _sample_attn_kernel(sinks_ref, qm_ref, kc_ref, kn_ref, vc_ref, vn_ref, o_ref, *, steps):
    nq = C_HEADS * steps
    nk = WINDOW + NEW_KEY_ROWS
    row = _iota((nq, nk), 0)
    col = _iota((nq, nk), 1)
    t_q = row & (steps - 1)
    valid = ((col < WINDOW) & (col > t_q)) | ((col >= WINDOW) & (col - WINDOW <= t_q))
    head = _iota((nq, 1), 0) >> 2
    sink = jnp.where(head == 0, sinks_ref[0],
                     jnp.where(head == 1, sinks_ref[1], jnp.where(head == 2, sinks_ref[2], sinks_ref[3])))
    for b in range(SAMPLE_ATTN_BLOCK):
        k_all = jnp.concatenate([kc_ref[b], kn_ref[b]], axis=0).astype(BF16)
        v_all = jnp.concatenate([vc_ref[b], vn_ref[b]], axis=0).astype(BF16)
        s = _dot_nt(qm_ref[b].astype(BF16), k_all) * (HEAD_DIM ** -0.5)
        s = jnp.where(valid, s, NEG_INF)
        m = jnp.maximum(jnp.max(s, axis=-1, keepdims=True), sink)
        p = jnp.exp(s - m)
        denom = jnp.sum(p, axis=-1, keepdims=True) + jnp.exp(sink - m)
        o_ref[b] = _dot((p / denom).astype(BF16), v_all)


def _sample_attn_call(sinks, qm, kc, kn, vc, vn, steps):
    bs, nq, _ = qm.shape
    BB = SAMPLE_ATTN_BLOCK
    blk = lambda r: pl.BlockSpec((BB, r, 128), lambda i, *_: (i, 0, 0))
    grid_spec = pltpu.PrefetchScalarGridSpec(
        num_scalar_prefetch=1, grid=(bs // BB,),
        in_specs=[blk(nq), blk(WINDOW), blk(NEW_KEY_ROWS), blk(WINDOW), blk(NEW_KEY_ROWS)],
        out_specs=blk(nq))
    return pl.pallas_call(
        functools.partial(_sample_attn_kernel, steps=steps),
        grid_spec=grid_spec,
        out_shape=jax.ShapeDtypeStruct((bs, nq, 128), F32),
        compiler_params=pltpu.CompilerParams(dimension_semantics=("arbitrary",)),
        name="sample_attn",
    )(sinks, qm, kc, kn, vc, vn)


def _sample_post_kernel(x_ref, o_ref, z_ref, outb_ref, outc_ref, outd_ref, anorm_ref, w_out_ref, x_out_ref):
    mask_bd = (_iota((256, 256), 0) >> 6) == (_iota((256, 256), 1) >> 6)
    ones_bd = jnp.where(mask_bd, 1.0, 0.0).astype(BF16)
    o = o_ref[...]
    out_a = (o * lax.rsqrt(_dot_lhs3(o * o, ones_bd) * (1.0 / HEAD_DIM) + EPS) * anorm_ref[...]
             * _silu(z_ref[...]))
    cat = jnp.concatenate([out_a, outb_ref[...], outc_ref[...], outd_ref[...]], axis=1).astype(BF16)
    x_out_ref[...] = x_ref[...] + _dot(cat, w_out_ref[...])


def _sample_post_call(x, o, z, out_b, out_c, out_d, lw):
    return pl.pallas_call(
        _sample_post_kernel,
        out_shape=jax.ShapeDtypeStruct(x.shape, F32),
        compiler_params=pltpu.CompilerParams(vmem_limit_bytes=VMEM_LIMIT_BYTES),
        name="sample_post",
    )(x, o, z, out_b, out_c, out_d, lw['a_norm_g'], lw['w_out'])


def _sample_mixer(x_tm, lw, tabs, d_ws_l, d_bias_l, a_state, a_conv, b_conv, c_k, c_v, bs, ts):
    to_tm = lambda a: jnp.swapaxes(a, 0, 1).reshape(a.shape[1] * bs, a.shape[2])
    from_tm = lambda a, n: jnp.swapaxes(a.reshape(n, bs, a.shape[-1]), 0, 1)
    wtab = jnp.repeat(d_ws_l[:, :ts, :ts].transpose(1, 2, 0).reshape(ts * ts, D_GROUPS), HEAD_DIM, axis=1)
    btab = jnp.repeat(d_bias_l[:, :ts].T, HEAD_DIM, axis=1)
    (q, k, v, g, beta, z, out_b, cq, ck, cv, out_d, dvn, a_tail, b_tail) = _sample_pre_call(
        x_tm, lw, tabs, wtab, btab, to_tm(a_conv), to_tm(b_conv), bs)

    heads_t = lambda a: a.reshape(ts, bs, A_HEADS, HEAD_DIM).transpose(2, 0, 3, 1)
    vec_t = lambda a, lo: a[:, lo:lo + A_HEADS].reshape(ts, bs, A_HEADS).transpose(2, 0, 1)[:, :, None, :]
    s_t = a_state.reshape(bs, A_HEADS, HEAD_DIM * HEAD_DIM).transpose(1, 2, 0)
    o_t, s_new_t = _sample_delta_call(vec_t(g, 0), vec_t(beta, A_HEADS), heads_t(q), heads_t(k), heads_t(v), s_t)
    o = o_t.transpose(1, 3, 0, 2).reshape(ts * bs, GROUP_WIDTH)
    a_state_new = s_new_t.transpose(2, 0, 1).reshape(bs, A_HEADS, HEAD_DIM, HEAD_DIM)

    cq4 = cq.reshape(ts, bs, C_HEADS, HEAD_DIM).transpose(1, 2, 0, 3)
    zq = jnp.zeros_like(cq4[:, 0])
    qm = jnp.concatenate(
        [jnp.concatenate([cq4[:, hh], zq] if hh // 2 == 0 else [zq, cq4[:, hh]], axis=-1) for hh in range(C_HEADS)],
        axis=1)
    pad_new = lambda a: jnp.concatenate([from_tm(a, ts), jnp.zeros((bs, NEW_KEY_ROWS - ts, 128), F32)], axis=1)
    kc = c_k.reshape(bs, WINDOW, 128)
    vc = c_v.reshape(bs, WINDOW, 128)
    o_att = _sample_attn_call(lw['c_sinks'], qm, kc, pad_new(ck), vc, pad_new(cv), ts)
    out_c = jnp.concatenate(
        [o_att[:, hh * ts:(hh + 1) * ts, (hh // 2) * HEAD_DIM:(hh // 2 + 1) * HEAD_DIM] for hh in range(C_HEADS)],
        axis=-1)
    out_c = jnp.swapaxes(out_c, 0, 1).reshape(ts * bs, GROUP_WIDTH)
    c_k_new = jnp.concatenate([kc[:, ts:], from_tm(ck, ts)], axis=1).reshape(bs, WINDOW, C_KV_HEADS, HEAD_DIM)
    c_v_new = jnp.concatenate([vc[:, ts:], from_tm(cv, ts)], axis=1).reshape(bs, WINDOW, C_KV_HEADS, HEAD_DIM)

    x2 = _sample_post_call(x_tm, o, z, out_b, out_c, out_d, lw)
    new = {'a_state': a_state_new, 'a_conv': from_tm(a_tail, A_CONV - 1), 'b_conv': from_tm(b_tail, B_CONV - 1),
           'c_k': c_k_new, 'c_v': c_v_new, 'd_v': from_tm(dvn, ts)}
    return x2, new


def kernel(x_prompt, x_sample, state_delta, state_delta_conv, state_shortconv, cache_win_k, cache_win_v,
           state_ffn_conv, norm1_g, w_in, a_conv_w, a_log, a_dt_bias, a_norm_g, b_conv_w, c_sinks, d_ln_g,
           d_ln_b, d_ws, d_bias, w_out, norm2_g, ffn_w_gate, ffn_w_up, ffn_conv_w, ffn_w_down, final_norm_g):
    bp, tp, _ = x_prompt.shape
    bs, ts, _ = x_sample.shape
    depth = w_in.shape[0]
    win_buf = cache_win_k.shape[2]
    pos_p = jnp.arange(tp, dtype=jnp.int32)
    pos_s = PAST_LEN + jnp.arange(ts, dtype=jnp.int32)
    assert win_buf == WINDOW and ts == 4
    tabs_p = _rope_tables(pos_p)
    tabs_s = tuple(jnp.repeat(t, bs, axis=0) for t in _rope_tables(pos_s))
    fng = final_norm_g[None, :]
    P = CARRY_ROWS

    hp = x_prompt
    hs = jnp.swapaxes(x_sample, 0, 1).reshape(ts * bs, D_MODEL)
    outs = {k: [] for k in ('sp', 'ss', 'acp', 'acs', 'bcp', 'bcs', 'ckp', 'cks', 'cvp', 'cvs', 'fcp', 'fcs', 'dv')}
    for l in range(depth):
        lw = _layer_weights(l, norm1_g, w_in, a_conv_w, a_log, a_dt_bias, a_norm_g, b_conv_w, c_sinks, d_ln_g,
                            d_ln_b, d_ws, d_bias, w_out, norm2_g, ffn_w_gate, ffn_w_up, ffn_conv_w, ffn_w_down)
        last = l == depth - 1
        hp, s_bd, acv, bcv, ckn, cvn = _mixer_call(
            hp, lw, tabs_p, jnp.zeros((bp, P, 768), F32), jnp.zeros((bp, P, 256), F32),
            jnp.zeros((bp, 256, 256), F32))
        hp, fcv = _ffn_call(hp, lw, jnp.zeros((bp, P, D_FF), F32), fng, tile=TILE_ROWS, stride=1, final_norm=last)
        outs['sp'].append(_diag_blocks(s_bd))
        outs['acp'].append(acv[:, P - (A_CONV - 1):])
        outs['bcp'].append(bcv[:, P - (B_CONV - 1):])
        outs['ckp'].append(ckn.reshape(bp, WINDOW, C_KV_HEADS, HEAD_DIM))
        outs['cvp'].append(cvn.reshape(bp, WINDOW, C_KV_HEADS, HEAD_DIM))
        outs['fcp'].append(fcv[:, P - (FFN_CONV - 1):])
        hs, ns = _sample_mixer(hs, lw, tabs_s, d_ws[l], d_bias[l], state_delta[l], state_delta_conv[l],
                               state_shortconv[l], cache_win_k[l], cache_win_v[l], bs, ts)
        f0 = jnp.swapaxes(state_ffn_conv[l], 0, 1).reshape(1, (FFN_CONV - 1) * bs, D_FF)
        ys_tm, fcs = _ffn_call(hs[None], lw, f0, fng, tile=ts * bs, stride=bs, final_norm=last)
        hs = ys_tm[0]
        outs['ss'].append(ns['a_state'])
        outs['acs'].append(ns['a_conv'])
        outs['bcs'].append(ns['b_conv'])
        outs['cks'].append(ns['c_k'])
        outs['cvs'].append(ns['c_v'])
        outs['fcs'].append(jnp.swapaxes(fcs.reshape(FFN_CONV - 1, bs, D_FF), 0, 1))
        outs['dv'].append(ns['d_v'])
    st = {k: jnp.stack(v) for k, v in outs.items()}
    hs = jnp.swapaxes(hs.reshape(ts, bs, D_MODEL), 0, 1)
    return (hp, hs, st['sp'], st['ss'], st['acp'], st['acs'], st['bcp'], st['bcs'], st['ckp'], st['cks'],
            st['cvp'], st['cvs'], st['fcp'], st['fcs'], st['dv'])
```

```python
import functools

import jax
import jax.numpy as jnp
import numpy as np
from jax import lax
from jax.experimental import pallas as pl
from jax.experimental.pallas import tpu as pltpu

F32 = jnp.float32
BF16 = jnp.bfloat16

D_MODEL = 1024
GROUP_WIDTH = 256
HEAD_DIM = 64
A_HEADS = 4
A_CONV = 4
A_CHUNK = 64
B_CONV = 3
C_HEADS = 4
C_KV_HEADS = 2
WINDOW = 128
ROPE_DIM = 16
ROPE_THETA = 500000.0
D_GROUPS = 4
D_CHUNK = 128
D_FF = 2816
FFN_CONV = 3
EPS = 1e-6
NEG_INF = -1e30
PAST_LEN = 16384

COL_A = 0
W_A = 1152
COL_B = 1152
W_B = 768
COL_C = 1920
W_C = 512
COL_D = 2432
W_D = 512
P_PACKED = 2944

TILE_ROWS = 512
CARRY_ROWS = 8
FFN_COL_CHUNK = 1408
VMEM_LIMIT_BYTES = 56 * 1024 * 1024


def _dot(a, b):
    return jnp.dot(a, b, preferred_element_type=F32)


def _dot_nt(a, b):
    return lax.dot_general(a, b, (((1,), (1,)), ((), ())), preferred_element_type=F32)


def _dot_tn(a, b):
    return lax.dot_general(a, b, (((0,), (0,)), ((), ())), preferred_element_type=F32)


def _split3(x):
    hi = x.astype(BF16)
    r1 = x - hi.astype(F32)
    mid = r1.astype(BF16)
    lo = (r1 - mid.astype(F32)).astype(BF16)
    return hi, mid, lo


def _dot_lhs3(x, w01):
    hi, mid, lo = _split3(x)
    return _dot(hi, w01) + _dot(mid, w01) + _dot(lo, w01)


def _dot_lhs2(x, w01):
    hi = x.astype(BF16)
    mid = (x - hi.astype(F32)).astype(BF16)
    return _dot(hi, w01) + _dot(mid, w01)


def _sigmoid(x):
    return 1.0 / (1.0 + jnp.exp(-x))


def _silu(x):
    return x * _sigmoid(x)


def _softplus(x):
    return jnp.maximum(x, 0.0) + jnp.log(1.0 + jnp.exp(-jnp.abs(x)))


def _gelu_tanh(x):
    return 0.5 * x * (1.0 + jnp.tanh(np.sqrt(2.0 / np.pi).astype(np.float32) * (x + 0.044715 * (x * x * x))))


def _rms(x, g):
    return x * lax.rsqrt(jnp.mean(x * x, axis=-1, keepdims=True) + EPS) * g


def _iota(shape, dim):
    return lax.broadcasted_iota(jnp.int32, shape, dim)


def _mixer_kernel(sinks_ref, x_ref, n1g_ref, w_in_ref, aconv_w_ref, alog_ref, dtb_ref, anorm_ref,
                  bconv_w_ref, cos_ref, sina_ref, sinb_ref, lng_ref, lnb_ref, ws_ref, dbias_ref,
                  w_out_ref, aconv0_ref, bconv0_ref, s0_ref,
                  x_out_ref, s_out_ref, aconv_out_ref, bconv_out_ref, ck_out_ref, cv_out_ref,
                  abuf, bbuf, kbuf, vbuf, q_s, k_s, v_s, gcb_s, bb_s, o_s, s_scr, *, tile):
    T = tile
    si = pl.program_id(1)
    P = CARRY_ROWS

    @pl.when(si == 0)
    def _():
        abuf[0:P, :] = aconv0_ref[0]
        bbuf[0:P, :] = bconv0_ref[0]
        kbuf[0:WINDOW, :] = jnp.zeros((WINDOW, 128), F32)
        vbuf[0:WINDOW, :] = jnp.zeros((WINDOW, 128), F32)
        s_scr[...] = s0_ref[0]

    x = x_ref[0]
    h = _rms(x, n1g_ref[...]).astype(BF16)

    pa = _dot(h, w_in_ref[:, COL_A:COL_A + W_A])
    pb = _dot(h, w_in_ref[:, COL_B:COL_B + W_B])
    pc = _dot(h, w_in_ref[:, COL_C:COL_C + W_C])
    pd = _dot(h, w_in_ref[:, COL_D:COL_D + W_D])

    z = pa[:, 768:1024]
    small = pa[:, 1024:1152]
    abuf[P:P + T, :] = pa[:, 0:768]
    y = (abuf[P:P + T, :] * aconv_w_ref[3:4, :] + abuf[P - 1:P - 1 + T, :] * aconv_w_ref[2:3, :]
         + abuf[P - 2:P - 2 + T, :] * aconv_w_ref[1:2, :] + abuf[P - 3:P - 3 + T, :] * aconv_w_ref[0:1, :])
    tail_a = abuf[T:T + P, :]
    abuf[0:P, :] = tail_a
    aconv_out_ref[0] = tail_a
    qkv = _silu(y)
    q_raw = qkv[:, 0:256]
    k_raw = qkv[:, 256:512]
    v_s[...] = qkv[:, 512:768]

    r256 = _iota((256, 256), 0) >> 6
    c256 = _iota((256, 256), 1) >> 6
    mask_bd = r256 == c256
    ones_bd = jnp.where(mask_bd, 1.0, 0.0).astype(BF16)
    q_s[...] = q_raw * lax.rsqrt(_dot_lhs2(q_raw * q_raw, ones_bd) + EPS) * (HEAD_DIM ** -0.5)
    k_s[...] = k_raw * lax.rsqrt(_dot_lhs2(k_raw * k_raw, ones_bd) + EPS)

    g_log = -jnp.exp(alog_ref[...]) * _softplus(small + dtb_ref[...])
    beta = _sigmoid(small)
    gbeta = jnp.where(_iota((T, 128), 1) < A_HEADS, g_log, beta)
    expand = jnp.where(_iota((128, 512), 0) == (_iota((128, 512), 1) >> 6), 1.0, 0.0).astype(BF16)
    gbb = _dot_lhs3(gbeta, expand)
    bb_s[...] = gbb[:, 256:512]

    def chunk_cumsum(xv):
        row_in_chunk = _iota(xv.shape, 0) & (A_CHUNK - 1)
        step = 1
        while step < A_CHUNK:
            xv = xv + jnp.where(row_in_chunk >= step, pltpu.roll(xv, step, 0), 0.0)
            step *= 2
        return xv

    gcb_s[...] = chunk_cumsum(gbb[:, 0:256])
    gct = chunk_cumsum(g_log).T[0:8, :]
    low_half = (_iota((1, T), 1) & A_CHUNK) == 0
    gct_r = pltpu.roll(gct, A_CHUNK, 1)
    gct_l = pltpu.roll(gct, T - A_CHUNK, 1)
    even_rows = [jnp.where(low_half, gct[a:a + 1, :], gct_r[a + 1:a + 2, :]) for a in (0, 2)]
    odd_rows = [jnp.where(low_half, gct_l[a:a + 1, :], gct[a + 1:a + 2, :]) for a in (0, 2)]

    ri = _iota((A_CHUNK, 256), 0)
    ci = _iota((A_CHUNK, 256), 1) & (A_CHUNK - 1)
    causal_t = ri >= ci
    strict_t = ri > ci
    eye_t = jnp.where(ri == ci, 1.0, 0.0)

    def bd16(x16):
        return jnp.concatenate([x16, x16, x16, x16], axis=0) * ones_bd

    n_chunks = T // A_CHUNK
    t_inv, pw, qk, qdec, e_tail, vb16, kbe16 = [], [], [], [], [], [], []
    for c in range(n_chunks):
        rows = slice(c * A_CHUNK, (c + 1) * A_CHUNK)
        blk = slice((c // 2) * 128, (c // 2) * 128 + 128)
        src = even_rows if c % 2 == 0 else odd_rows
        gc_row = jnp.concatenate([src[0][:, blk], src[1][:, blk]], axis=1)
        qc = q_s[rows, :]
        kc = k_s[rows, :]
        bbc = bb_s[rows, :]
        gcb = gcb_s[rows, :]
        decay = jnp.where(causal_t, jnp.exp(jnp.where(causal_t, gcb - gc_row, 0.0)), 0.0)
        eg = jnp.exp(gcb)
        kb = kc * bbc
        aq = _dot_nt(jnp.concatenate([kb, qc], axis=0).astype(BF16), bd16(kc.astype(BF16)))
        a_mat = jnp.where(strict_t, aq[0:A_CHUNK] * decay, 0.0)
        qk.append(aq[A_CHUNK:2 * A_CHUNK] * decay)
        t_inv.append(eye_t - a_mat)
        pw.append(a_mat)
        qdec.append(qc * eg)
        e_tail.append(jnp.exp(gcb[A_CHUNK - 1:A_CHUNK, :] - gcb))
        vb16.append((v_s[rows, :] * bbc).astype(BF16))
        kbe16.append((kb * eg).astype(BF16))
    for c in range(n_chunks):
        p16 = pw[c].astype(BF16)
        pw[c] = _dot(p16, bd16(p16))
    for _ in range(4):
        for c in range(n_chunks):
            p16 = pw[c].astype(BF16)
            res = _dot(jnp.concatenate([p16, t_inv[c].astype(BF16)], axis=0), bd16(p16))
            pw[c] = res[0:A_CHUNK]
            t_inv[c] = t_inv[c] + res[A_CHUNK:2 * A_CHUNK]
    u, w = [], []
    for c in range(n_chunks):
        t_c = t_inv[c] + _dot(t_inv[c].astype(BF16), bd16(pw[c].astype(BF16)))
        uw = _dot(t_c.astype(BF16), jnp.concatenate([bd16(vb16[c]), bd16(kbe16[c])], axis=1))
        u.append(uw[:, 0:256])
        w.append(uw[:, 256:512])
    state = {'s': s_scr[...]}

    def scan_step(c):
        rows = slice(c * A_CHUNK, (c + 1) * A_CHUNK)
        s_bd = state['s']
        wq = _dot(jnp.concatenate([w[c], qdec[c]], axis=0).astype(BF16), s_bd.astype(BF16))
        v_new = u[c] - wq[0:A_CHUNK]
        o_s[rows, :] = wq[A_CHUNK:2 * A_CHUNK] + _dot(qk[c].astype(BF16), bd16(v_new.astype(BF16)))
        kv = _dot_tn(k_s[rows, :].astype(BF16), (v_new * e_tail[c]).astype(BF16))
        g_last = gcb_s[(c + 1) * A_CHUNK - 1:(c + 1) * A_CHUNK, :]
        state['s'] = s_bd * jnp.exp(g_last) + jnp.where(mask_bd, kv, 0.0)

    scan_order = iter(range(n_chunks))
    scan_step(next(scan_order))

    bbuf[P:P + T, :] = pb[:, 256:512] * pb[:, 512:768]
    bx = (bbuf[P:P + T, :] * bconv_w_ref[2:3, :] + bbuf[P - 1:P - 1 + T, :] * bconv_w_ref[1:2, :]
          + bbuf[P - 2:P - 2 + T, :] * bconv_w_ref[0:1, :])
    tail_b = bbuf[T:T + P, :]
    bbuf[0:P, :] = tail_b
    bconv_out_ref[0] = tail_b
    out_b = pb[:, 0:256] * bx
    scan_step(next(scan_order))

    cos = cos_ref[...]
    sina = sina_ref[...]
    sinb = sinb_ref[...]
    cq = pc[:, 0:256]
    cq = (cq * jnp.concatenate([cos, cos], axis=1)
          + pltpu.roll(cq, 256 - ROPE_DIM // 2, 1) * jnp.concatenate([sina, sina], axis=1)
          + pltpu.roll(cq, ROPE_DIM // 2, 1) * jnp.concatenate([sinb, sinb], axis=1))
    ck = pc[:, 256:384]
    ck = ck * cos + pltpu.roll(ck, 128 - ROPE_DIM // 2, 1) * sina + pltpu.roll(ck, ROPE_DIM // 2, 1) * sinb
    kbuf[WINDOW:WINDOW + T, :] = ck
    vbuf[WINDOW:WINDOW + T, :] = pc[:, 384:512]
    ck_out_ref[0] = kbuf[T:T + WINDOW, :]
    cv_out_ref[0] = vbuf[T:T + WINDOW, :]

    lane128 = _iota((2 * WINDOW, 128), 1)
    low = _iota((WINDOW, 128), 1) < HEAD_DIM
    qrow = _iota((2 * WINDOW, 2 * WINDOW), 0) & (WINDOW - 1)
    kcol = _iota((2 * WINDOW, 2 * WINDOW), 1)
    band = (kcol > qrow) & (kcol <= qrow + WINDOW)
    top_half = _iota((2 * WINDOW, 1), 0) < WINDOW
    out_c_blocks = []
    for n in range(T // WINDOW):
        scan_step(next(scan_order))
        first_key = si * T + (n - 1) * WINDOW
        valid = band & (kcol + first_key >= 0)
        kwin = kbuf[n * WINDOW:(n + 2) * WINDOW, :]
        vwin = vbuf[n * WINDOW:(n + 2) * WINDOW, :]
        k_sw = pltpu.roll(kwin, HEAD_DIM, 1)
        v_sw = pltpu.roll(vwin, HEAD_DIM, 1)
        pair_out = []
        for g in range(C_KV_HEADS):
            own = (lane128 < HEAD_DIM) if g == 0 else (lane128 >= HEAD_DIM)
            k_dup = jnp.where(own, kwin, k_sw).astype(BF16)
            v_dup = jnp.where(own, vwin, v_sw).astype(BF16)
            qp = cq[n * WINDOW:(n + 1) * WINDOW, g * 128:(g + 1) * 128]
            q_st = jnp.concatenate([jnp.where(low, qp, 0.0), jnp.where(low, 0.0, qp)], axis=0).astype(BF16)
            s = _dot_nt(q_st, k_dup) * (HEAD_DIM ** -0.5)
            s = jnp.where(valid, s, NEG_INF)
            sink = jnp.where(top_half, sinks_ref[2 * g], sinks_ref[2 * g + 1])
            m = jnp.maximum(jnp.max(s, axis=-1, keepdims=True), sink)
            p = jnp.exp(s - m)
            denom = jnp.sum(p, axis=-1, keepdims=True) + jnp.exp(sink - m)
            o2 = _dot((p / denom).astype(BF16), v_dup)
            pair_out.append(jnp.where(low, o2[0:WINDOW], o2[WINDOW:2 * WINDOW]))
        out_c_blocks.append(jnp.concatenate(pair_out, axis=1))
    out_c = jnp.concatenate(out_c_blocks, axis=0)
    kbuf[0:WINDOW, :] = kbuf[T:T + WINDOW, :]
    vbuf[0:WINDOW, :] = vbuf[T:T + WINDOW, :]
    scan_step(next(scan_order))

    du = _gelu_tanh(pd[:, 0:256])
    gv = _gelu_tanh(pd[:, 256:512])
    mu = jnp.mean(gv, axis=-1, keepdims=True)
    xc = gv - mu
    dvn = xc * lax.rsqrt(jnp.mean(xc * xc, axis=-1, keepdims=True) + EPS) * lng_ref[...] + lnb_ref[...]
    wr = _iota((D_GROUPS * D_CHUNK, D_CHUNK), 0) & (D_CHUNK - 1)
    wc = _iota((D_GROUPS * D_CHUNK, D_CHUNK), 1)
    wm = jnp.where(wr >= wc, ws_ref[...], 0.0).astype(BF16)
    lane_grp = _iota((D_CHUNK, 256), 1) >> 6
    scan_step(next(scan_order))
    out_d_blocks = []
    for n in range(T // D_CHUNK):
        mx = _dot(wm, dvn[n * D_CHUNK:(n + 1) * D_CHUNK, :].astype(BF16))
        mixed = dbias_ref[...]
        for grp in range(D_GROUPS):
            mixed = mixed + jnp.where(lane_grp == grp, mx[grp * D_CHUNK:(grp + 1) * D_CHUNK, :], 0.0)
        out_d_blocks.append(du[n * D_CHUNK:(n + 1) * D_CHUNK, :] * mixed)
    out_d = jnp.concatenate(out_d_blocks, axis=0)

    assert next(scan_order, None) is None
    s_scr[...] = state['s']
    s_out_ref[0] = state['s']
    o = o_s[...]
    out_a = (o * lax.rsqrt(_dot_lhs2(o * o, ones_bd) * (1.0 / HEAD_DIM) + EPS) * anorm_ref[...]
             * _silu(z))

    cat = jnp.concatenate([out_a, out_b, out_c, out_d], axis=1).astype(BF16)
    x_out_ref[0] = x + _dot(cat, w_out_ref[...])


def _mixer_call(x, lw, tabs, aconv0, bconv0, s0):
    bsz, seq, _ = x.shape
    T = TILE_ROWS
    ns = seq // T
    P = CARRY_ROWS
    full = lambda shape: pl.BlockSpec(shape, lambda b, s, *_: (0,) * len(shape),
                                      pipeline_mode=pl.Buffered(1))
    per_b = lambda shape: pl.BlockSpec((1,) + shape, lambda b, s, *_: (b,) + (0,) * len(shape))
    tab = pl.BlockSpec((T, 128), lambda b, s, *_: (s, 0))
    grid_spec = pltpu.PrefetchScalarGridSpec(
        num_scalar_prefetch=1,
        grid=(bsz, ns),
        in_specs=[
            pl.BlockSpec((1, T, D_MODEL), lambda b, s, *_: (b, s, 0)),
            full((1, D_MODEL)), full((D_MODEL, P_PACKED)), full((A_CONV, 768)), full((1, 128)), full((1, 128)),
            full((1, 256)), full((B_CONV, 256)), tab, tab, tab, full((1, 256)), full((1, 256)),
            full((D_GROUPS * D_CHUNK, D_CHUNK)), full((D_CHUNK, 256)), full((D_MODEL, D_MODEL)),
            per_b((P, 768)), per_b((P, 256)), per_b((256, 256)),
        ],
        out_specs=[
            pl.BlockSpec((1, T, D_MODEL), lambda b, s, *_: (b, s, 0)),
            per_b((256, 256)), per_b((P, 768)), per_b((P, 256)), per_b((WINDOW, 128)), per_b((WINDOW, 128)),
        ],
        scratch_shapes=[
            pltpu.VMEM((T + P, 768), F32), pltpu.VMEM((T + P, 256), F32),
            pltpu.VMEM((T + WINDOW, 128), F32), pltpu.VMEM((T + WINDOW, 128), F32),
            pltpu.VMEM((T, 256), F32), pltpu.VMEM((T, 256), F32), pltpu.VMEM((T, 256), F32),
            pltpu.VMEM((T, 256), F32), pltpu.VMEM((T, 256), F32), pltpu.VMEM((T, 256), F32),
            pltpu.VMEM((256, 256), F32),
        ],
    )
    out_shape = [
        jax.ShapeDtypeStruct((bsz, seq, D_MODEL), F32),
        jax.ShapeDtypeStruct((bsz, 256, 256), F32),
        jax.ShapeDtypeStruct((bsz, P, 768), F32),
        jax.ShapeDtypeStruct((bsz, P, 256), F32),
        jax.ShapeDtypeStruct((bsz, WINDOW, 128), F32),
        jax.ShapeDtypeStruct((bsz, WINDOW, 128), F32),
    ]
    return pl.pallas_call(
        functools.partial(_mixer_kernel, tile=T),
        grid_spec=grid_spec,
        out_shape=out_shape,
        compiler_params=pltpu.CompilerParams(
            dimension_semantics=("arbitrary", "arbitrary"), vmem_limit_bytes=VMEM_LIMIT_BYTES),
        name="mixer",
    )(lw['c_sinks'], x, lw['norm1_g'], lw['w_in'], lw['a_conv_w'], lw['a_log'], lw['a_dt_bias'],
      lw['a_norm_g'], lw['b_conv_w'], tabs[0], tabs[1], tabs[2], lw['d_ln_g'], lw['d_ln_b'],
      lw['d_ws'], lw['d_bias'], lw['w_out'], aconv0, bconv0, s0)


def _ffn_kernel(x_ref, n2g_ref, wg_ref, wu_ref, cw_ref, wd_ref, fconv0_ref, fng_ref,
                x_out_ref, fconv_out_ref, gbuf, *, tile, stride, final_norm):
    T = tile
    P = (FFN_CONV - 1) * stride if stride > 1 else CARRY_ROWS
    si = pl.program_id(1)

    @pl.when(si == 0)
    def _():
        gbuf[0:P, :] = fconv0_ref[0]

    x = x_ref[0]
    h = _rms(x, n2g_ref[...]).astype(BF16)
    acc = x
    for c0 in range(0, D_FF, FFN_COL_CHUNK):
        cols = slice(c0, c0 + FFN_COL_CHUNK)
        gbuf[P:P + T, cols] = _dot(h, wg_ref[:, cols])
        up = _dot(h, wu_ref[:, cols])
        gate = (gbuf[P:P + T, cols] * cw_ref[2:3, cols] + gbuf[P - stride:P - stride + T, cols] * cw_ref[1:2, cols]
                + gbuf[P - 2 * stride:P - 2 * stride + T, cols] * cw_ref[0:1, cols])
        acc = acc + _dot((_silu(gate) * up).astype(BF16), wd_ref[cols, :])
    tail = gbuf[T:T + P, :]
    gbuf[0:P, :] = tail
    fconv_out_ref[0] = tail
    if final_norm:
        acc = _rms(acc, fng_ref[...])
    x_out_ref[0] = acc


def _ffn_call(x, lw, fconv0, final_g, *, tile, stride, final_norm):
    bsz, seq, _ = x.shape
    T = tile
    P = (FFN_CONV - 1) * stride if stride > 1 else CARRY_ROWS
    full = lambda shape: pl.BlockSpec(shape, lambda b, s: (0,) * len(shape), pipeline_mode=pl.Buffered(1))
    per_b = lambda shape: pl.BlockSpec((1,) + shape, lambda b, s: (b,) + (0,) * len(shape))
    return pl.pallas_call(
        functools.partial(_ffn_kernel, tile=T, stride=stride, final_norm=final_norm),
        grid=(bsz, seq // T),
        in_specs=[
            pl.BlockSpec((1, T, D_MODEL), lambda b, s: (b, s, 0)),
            full((1, D_MODEL)), full((D_MODEL, D_FF)), full((D_MODEL, D_FF)), full((FFN_CONV, D_FF)),
            full((D_FF, D_MODEL)), per_b((P, D_FF)), full((1, D_MODEL)),
        ],
        out_specs=[pl.BlockSpec((1, T, D_MODEL), lambda b, s: (b, s, 0)), per_b((P, D_FF))],
        out_shape=[jax.ShapeDtypeStruct((bsz, seq, D_MODEL), F32), jax.ShapeDtypeStruct((bsz, P, D_FF), F32)],
        scratch_shapes=[pltpu.VMEM((T + P, D_FF), F32)],
        compiler_params=pltpu.CompilerParams(
            dimension_semantics=("arbitrary", "arbitrary"), vmem_limit_bytes=VMEM_LIMIT_BYTES),
        name="ffn",
    )(x, lw['norm2_g'], lw['ffn_w_gate'], lw['ffn_w_up'], lw['ffn_conv_w'], lw['ffn_w_down'], fconv0, final_g)


def _pack_w_in(w):
    pad = jnp.zeros((D_MODEL, 128 - 2 * A_HEADS), w.dtype)
    small = jnp.concatenate([w[:, 1028:1032], w[:, 1024:1028], pad], axis=1)
    return jnp.concatenate([w[:, 0:1024], small, w[:, 1032:]], axis=1).astype(BF16)


def _pad_lanes(v, width=128):
    return jnp.concatenate([v, jnp.zeros((width - v.shape[0],), v.dtype)])[None, :]


def _layer_weights(l, norm1_g, w_in, a_conv_w, a_log, a_dt_bias, a_norm_g, b_conv_w, c_sinks, d_ln_g, d_ln_b,
                   d_ws, d_bias, w_out, norm2_g, ffn_w_gate, ffn_w_up, ffn_conv_w, ffn_w_down):
    bias_tab = jnp.broadcast_to(d_bias[l].T[:, :, None], (D_CHUNK, D_GROUPS, HEAD_DIM)).reshape(D_CHUNK, 256)
    return {
        'norm1_g': norm1_g[l][None, :], 'w_in': _pack_w_in(w_in[l]), 'a_conv_w': a_conv_w[l],
        'a_log': _pad_lanes(a_log[l]), 'a_dt_bias': _pad_lanes(a_dt_bias[l]),
        'a_norm_g': jnp.tile(a_norm_g[l], A_HEADS)[None, :], 'b_conv_w': b_conv_w[l], 'c_sinks': c_sinks[l],
        'd_ln_g': d_ln_g[l][None, :], 'd_ln_b': d_ln_b[l][None, :],
        'd_ws': d_ws[l].reshape(D_GROUPS * D_CHUNK, D_CHUNK), 'd_bias': bias_tab,
        'w_out': w_out[l].astype(BF16), 'norm2_g': norm2_g[l][None, :],
        'ffn_w_gate': ffn_w_gate[l].astype(BF16), 'ffn_w_up': ffn_w_up[l].astype(BF16),
        'ffn_conv_w': ffn_conv_w[l], 'ffn_w_down': ffn_w_down[l].astype(BF16),
    }


def _rope_tables(pos):
    half = ROPE_DIM // 2
    inv = jnp.power(ROPE_THETA, -jnp.arange(half, dtype=F32) * (2.0 / ROPE_DIM))
    ang = pos.astype(F32)[:, None] * inv[None, :]
    cos, sin = jnp.cos(ang), jnp.sin(ang)
    n = pos.shape[0]
    rest = HEAD_DIM - ROPE_DIM
    cos_h = jnp.concatenate([cos, cos, jnp.ones((n, rest), F32)], axis=1)
    sina_h = jnp.concatenate([-sin, jnp.zeros((n, half + rest), F32)], axis=1)
    sinb_h = jnp.concatenate([jnp.zeros((n, half), F32), sin, jnp.zeros((n, rest), F32)], axis=1)
    return tuple(jnp.concatenate([t, t], axis=1) for t in (cos_h, sina_h, sinb_h))


def _diag_blocks(s_bd):
    return jnp.stack([s_bd[:, HEAD_DIM * i:HEAD_DIM * (i + 1), HEAD_DIM * i:HEAD_DIM * (i + 1)]
                      for i in range(A_HEADS)], axis=1)


def _sample_pre_kernel(x_ref, n1g_ref, w_in_ref, aconv_w_ref, alog_ref, dtb_ref, bconv_w_ref, cos_ref, sina_ref,
                       sinb_ref, lng_ref, lnb_ref, wtab_ref, btab_ref, aconv0_ref, bconv0_ref,
                       q_ref, k_ref, v_ref, g_ref, beta_ref, z_ref, outb_ref, cq_ref, ck_ref, cv_ref, outd_ref,
                       dvn_ref, aconv_out_ref, bconv_out_ref, abuf, bbuf, *, rows, stride):
    R, S = rows, stride
    steps = R // S
    pa_rows = (A_CONV - 1) * S
    pb_rows = (B_CONV - 1) * S
    x = x_ref[...]
    h = _rms(x, n1g_ref[...]).astype(BF16)

    pa = _dot(h, w_in_ref[:, COL_A:COL_A + W_A])
    z_ref[...] = pa[:, 768:1024]
    small = pa[:, 1024:1152]
    abuf[0:pa_rows, :] = aconv0_ref[...]
    abuf[pa_rows:pa_rows + R, :] = pa[:, 0:768]
    y = abuf[0:R, :] * aconv_w_ref[0:1, :]
    for j in range(1, A_CONV):
        y = y + abuf[j * S:j * S + R, :] * aconv_w_ref[j:j + 1, :]
    aconv_out_ref[...] = abuf[R:R + pa_rows, :]
    qkv = _silu(y)
    q_raw = qkv[:, 0:256]
    k_raw = qkv[:, 256:512]
    v_ref[...] = qkv[:, 512:768]
    mask_bd = (_iota((256, 256), 0) >> 6) == (_iota((256, 256), 1) >> 6)
    ones_bd = jnp.where(mask_bd, 1.0, 0.0).astype(BF16)
    q_ref[...] = q_raw * lax.rsqrt(_dot_lhs3(q_raw * q_raw, ones_bd) + EPS) * (HEAD_DIM ** -0.5)
    k_ref[...] = k_raw * lax.rsqrt(_dot_lhs3(k_raw * k_raw, ones_bd) + EPS)
    g_ref[...] = -jnp.exp(alog_ref[...]) * _softplus(small + dtb_ref[...])
    beta_ref[...] = _sigmoid(small)

    pb = _dot(h, w_in_ref[:, COL_B:COL_B + W_B])
    bbuf[0:pb_rows, :] = bconv0_ref[...]
    bbuf[pb_rows:pb_rows + R, :] = pb[:, 256:512] * pb[:, 512:768]
    bx = bbuf[0:R, :] * bconv_w_ref[0:1, :]
    for j in range(1, B_CONV):
        bx = bx + bbuf[j * S:j * S + R, :] * bconv_w_ref[j:j + 1, :]
    bconv_out_ref[...] = bbuf[R:R + pb_rows, :]
    outb_ref[...] = pb[:, 0:256] * bx

    pc = _dot(h, w_in_ref[:, COL_C:COL_C + W_C])
    cos = cos_ref[...]
    sina = sina_ref[...]
    sinb = sinb_ref[...]
    cq = pc[:, 0:256]
    cq_ref[...] = (cq * jnp.concatenate([cos, cos], axis=1)
                   + pltpu.roll(cq, 256 - ROPE_DIM // 2, 1) * jnp.concatenate([sina, sina], axis=1)
                   + pltpu.roll(cq, ROPE_DIM // 2, 1) * jnp.concatenate([sinb, sinb], axis=1))
    ck = pc[:, 256:384]
    ck_ref[...] = (ck * cos + pltpu.roll(ck, 128 - ROPE_DIM // 2, 1) * sina
                   + pltpu.roll(ck, ROPE_DIM // 2, 1) * sinb)
    cv_ref[...] = pc[:, 384:512]

    pd = _dot(h, w_in_ref[:, COL_D:COL_D + W_D])
    du = _gelu_tanh(pd[:, 0:256])
    gv = _gelu_tanh(pd[:, 256:512])
    mu = jnp.mean(gv, axis=-1, keepdims=True)
    xc = gv - mu
    dvn = xc * lax.rsqrt(jnp.mean(xc * xc, axis=-1, keepdims=True) + EPS) * lng_ref[...] + lnb_ref[...]
    dvn_ref[...] = dvn
    for t in range(steps):
        mixed = btab_ref[t:t + 1, :]
        for s in range(t + 1):
            mixed = mixed + wtab_ref[t * steps + s:t * steps + s + 1, :] * dvn[s * S:(s + 1) * S, :]
        outd_ref[t * S:(t + 1) * S, :] = du[t * S:(t + 1) * S, :] * mixed


def _sample_pre_call(x, lw, tabs, wtab, btab, aconv0, bconv0, stride):
    R = x.shape[0]
    f = lambda *shape: jax.ShapeDtypeStruct(shape, F32)
    out_shape = [f(R, 256), f(R, 256), f(R, 256), f(R, 128), f(R, 128), f(R, 256), f(R, 256), f(R, 256),
                 f(R, 128), f(R, 128), f(R, 256), f(R, 256), f((A_CONV - 1) * stride, 768),
                 f((B_CONV - 1) * stride, 256)]
    return pl.pallas_call(
        functools.partial(_sample_pre_kernel, rows=R, stride=stride),
        out_shape=out_shape,
        scratch_shapes=[pltpu.VMEM((R + (A_CONV - 1) * stride, 768), F32),
                        pltpu.VMEM((R + (B_CONV - 1) * stride, 256), F32)],
        compiler_params=pltpu.CompilerParams(vmem_limit_bytes=VMEM_LIMIT_BYTES),
        name="sample_pre",
    )(x, lw['norm1_g'], lw['w_in'], lw['a_conv_w'], lw['a_log'], lw['a_dt_bias'], lw['b_conv_w'],
      tabs[0], tabs[1], tabs[2], lw['d_ln_g'], lw['d_ln_b'], wtab, btab, aconv0, bconv0)


def _sample_delta_kernel(g_ref, beta_ref, q_ref, k_ref, v_ref, s_ref, o_ref, snew_ref, *, steps):
    n_i = HEAD_DIM
    nb = s_ref.shape[-1]
    zeros = jnp.zeros((HEAD_DIM, nb), F32)

    def rows_of(i):
        return pl.ds(pl.multiple_of(i * HEAD_DIM, HEAD_DIM), HEAD_DIM)

    dec0 = jnp.exp(g_ref[0, 0])

    def first_pass(i, acc):
        return acc + k_ref[0, 0, pl.ds(i, 1), :] * (s_ref[0, rows_of(i), :] * dec0)

    ks = lax.fori_loop(0, n_i, first_pass, zeros)
    for t in range(steps):
        dec = jnp.exp(g_ref[0, t])
        v_new = beta_ref[0, t] * (v_ref[0, t] - ks)
        src = s_ref if t == 0 else snew_ref
        dec_next = jnp.exp(g_ref[0, t + 1]) if t + 1 < steps else None

        def update(i, carry, t=t, dec=dec, v_new=v_new, src=src, dec_next=dec_next):
            o_acc, ks_acc = carry
            blk = src[0, rows_of(i), :] * dec + k_ref[0, t, pl.ds(i, 1), :] * v_new
            snew_ref[0, rows_of(i), :] = blk
            o_acc = o_acc + q_ref[0, t, pl.ds(i, 1), :] * blk
            if dec_next is not None:
                ks_acc = ks_acc + k_ref[0, t + 1, pl.ds(i, 1), :] * (blk * dec_next)
            return o_acc, ks_acc

        o_acc, ks = lax.fori_loop(0, n_i, update, (zeros, zeros))
        o_ref[0, t] = o_acc


def _sample_delta_call(g, beta, q, k, v, s):
    nh, steps, _, nb = q.shape
    vec = pl.BlockSpec((1, steps, 1, nb), lambda h: (h, 0, 0, 0))
    mat = pl.BlockSpec((1, steps, HEAD_DIM, nb), lambda h: (h, 0, 0, 0))
    st = pl.BlockSpec((1, HEAD_DIM * HEAD_DIM, nb), lambda h: (h, 0, 0))
    return pl.pallas_call(
        functools.partial(_sample_delta_kernel, steps=steps),
        grid=(nh,),
        in_specs=[vec, vec, mat, mat, mat, st],
        out_specs=[mat, st],
        out_shape=[jax.ShapeDtypeStruct((nh, steps, HEAD_DIM, nb), F32),
                   jax.ShapeDtypeStruct((nh, HEAD_DIM * HEAD_DIM, nb), F32)],
        compiler_params=pltpu.CompilerParams(dimension_semantics=("arbitrary",),
                                             vmem_limit_bytes=VMEM_LIMIT_BYTES),
        name="sample_delta",
    )(g, beta, q, k, v, s)


SAMPLE_ATTN_BLOCK = 8
NEW_KEY_ROWS = 8


def _sample_attn_kernel(sinks_ref, qm_ref, kc_ref, kn_ref, vc_ref, vn_ref, o_ref, *, steps):
    nq = C_HEADS * steps
    nk = WINDOW + NEW_KEY_ROWS
    row = _iota((nq, nk), 0)
    col = _iota((nq, nk), 1)
    t_q = row & (steps - 1)
    valid = ((col < WINDOW) & (col > t_q)) | ((col >= WINDOW) & (col - WINDOW <= t_q))
    head = _iota((nq, 1), 0) >> 2
    sink = jnp.where(head == 0, sinks_ref[0],
                     jnp.where(head == 1, sinks_ref[1], jnp.where(head == 2, sinks_ref[2], sinks_ref[3])))
    for b in range(SAMPLE_ATTN_BLOCK):
        k_all = jnp.concatenate([kc_ref[b], kn_ref[b]], axis=0).astype(BF16)
        v_all = jnp.concatenate([vc_ref[b], vn_ref[b]], axis=0).astype(BF16)
        s = _dot_nt(qm_ref[b].astype(BF16), k_all) * (HEAD_DIM ** -0.5)
        s = jnp.where(valid, s, NEG_INF)
        m = jnp.maximum(jnp.max(s, axis=-1, keepdims=True), sink)
        p = jnp.exp(s - m)
        denom = jnp.sum(p, axis=-1, keepdims=True) + jnp.exp(sink - m)
        o_ref[b] = _dot((p / denom).astype(BF16), v_all)


def _sample_attn_call(sinks, qm, kc, kn, vc, vn, steps):
    bs, nq, _ = qm.shape
    BB = SAMPLE_ATTN_BLOCK
    blk = lambda r: pl.BlockSpec((BB, r, 128), lambda i, *_: (i, 0, 0))
    grid_spec = pltpu.PrefetchScalarGridSpec(
        num_scalar_prefetch=1, grid=(bs // BB,),
        in_specs=[blk(nq), blk(WINDOW), blk(NEW_KEY_ROWS), blk(WINDOW), blk(NEW_KEY_ROWS)],
        out_specs=blk(nq))
    return pl.pallas_call(
        functools.partial(_sample_attn_kernel, steps=steps),
        grid_spec=grid_spec,
        out_shape=jax.ShapeDtypeStruct((bs, nq, 128), F32),
        compiler_params=pltpu.CompilerParams(dimension_semantics=("arbitrary",)),
        name="sample_attn",
    )(sinks, qm, kc, kn, vc, vn)


def _sample_post_kernel(x_ref, o_ref, z_ref, outb_ref, outc_ref, outd_ref, anorm_ref, w_out_ref, x_out_ref):
    mask_bd = (_iota((256, 256), 0) >> 6) == (_iota((256, 256), 1) >> 6)
    ones_bd = jnp.where(mask_bd, 1.0, 0.0).astype(BF16)
    o = o_ref[...]
    out_a = (o * lax.rsqrt(_dot_lhs3(o * o, ones_bd) * (1.0 / HEAD_DIM) + EPS) * anorm_ref[...]
             * _silu(z_ref[...]))
    cat = jnp.concatenate([out_a, outb_ref[...], outc_ref[...], outd_ref[...]], axis=1).astype(BF16)
    x_out_ref[...] = x_ref[...] + _dot(cat, w_out_ref[...])


def _sample_post_call(x, o, z, out_b, out_c, out_d, lw):
    return pl.pallas_call(
        _sample_post_kernel,
        out_shape=jax.ShapeDtypeStruct(x.shape, F32),
        compiler_params=pltpu.CompilerParams(vmem_limit_bytes=VMEM_LIMIT_BYTES),
        name="sample_post",
    )(x, o, z, out_b, out_c, out_d, lw['a_norm_g'], lw['w_out'])


def _sample_mixer(x_tm, lw, tabs, d_ws_l, d_bias_l, a_state, a_conv, b_conv, c_k, c_v, bs, ts):
    to_tm = lambda a: jnp.swapaxes(a, 0, 1).reshape(a.shape[1] * bs, a.shape[2])
    from_tm = lambda a, n: jnp.swapaxes(a.reshape(n, bs, a.shape[-1]), 0, 1)
    wtab = jnp.repeat(d_ws_l[:, :ts, :ts].transpose(1, 2, 0).reshape(ts * ts, D_GROUPS), HEAD_DIM, axis=1)
    btab = jnp.repeat(d_bias_l[:, :ts].T, HEAD_DIM, axis=1)
    (q, k, v, g, beta, z, out_b, cq, ck, cv, out_d, dvn, a_tail, b_tail) = _sample_pre_call(
        x_tm, lw, tabs, wtab, btab, to_tm(a_conv), to_tm(b_conv), bs)

    heads_t = lambda a: a.reshape(ts, bs, A_HEADS, HEAD_DIM).transpose(2, 0, 3, 1)
    vec_t = lambda a, lo: a[:, lo:lo + A_HEADS].reshape(ts, bs, A_HEADS).transpose(2, 0, 1)[:, :, None, :]
    s_t = a_state.reshape(bs, A_HEADS, HEAD_DIM * HEAD_DIM).transpose(1, 2, 0)
    o_t, s_new_t = _sample_delta_call(vec_t(g, 0), vec_t(beta, A_HEADS), heads_t(q), heads_t(k), heads_t(v), s_t)
    o = o_t.transpose(1, 3, 0, 2).reshape(ts * bs, GROUP_WIDTH)
    a_state_new = s_new_t.transpose(2, 0, 1).reshape(bs, A_HEADS, HEAD_DIM, HEAD_DIM)

    cq4 = cq.reshape(ts, bs, C_HEADS, HEAD_DIM).transpose(1, 2, 0, 3)
    zq = jnp.zeros_like(cq4[:, 0])
    qm = jnp.concatenate(
        [jnp.concatenate([cq4[:, hh], zq] if hh // 2 == 0 else [zq, cq4[:, hh]], axis=-1) for hh in range(C_HEADS)],
        axis=1)
    pad_new = lambda a: jnp.concatenate([from_tm(a, ts), jnp.zeros((bs, NEW_KEY_ROWS - ts, 128), F32)], axis=1)
    kc = c_k.reshape(bs, WINDOW, 128)
    vc = c_v.reshape(bs, WINDOW, 128)
    o_att = _sample_attn_call(lw['c_sinks'], qm, kc, pad_new(ck), vc, pad_new(cv), ts)
    out_c = jnp.concatenate(
        [o_att[:, hh * ts:(hh + 1) * ts, (hh // 2) * HEAD_DIM:(hh // 2 + 1) * HEAD_DIM] for hh in range(C_HEADS)],
        axis=-1)
    out_c = jnp.swapaxes(out_c, 0, 1).reshape(ts * bs, GROUP_WIDTH)
    c_k_new = jnp.concatenate([kc[:, ts:], from_tm(ck, ts)], axis=1).reshape(bs, WINDOW, C_KV_HEADS, HEAD_DIM)
    c_v_new = jnp.concatenate([vc[:, ts:], from_tm(cv, ts)], axis=1).reshape(bs, WINDOW, C_KV_HEADS, HEAD_DIM)

    x2 = _sample_post_call(x_tm, o, z, out_b, out_c, out_d, lw)
    new = {'a_state': a_state_new, 'a_conv': from_tm(a_tail, A_CONV - 1), 'b_conv': from_tm(b_tail, B_CONV - 1),
           'c_k': c_k_new, 'c_v': c_v_new, 'd_v': from_tm(dvn, ts)}
    return x2, new


def kernel(x_prompt, x_sample, state_delta, state_delta_conv, state_shortconv, cache_win_k, cache_win_v,
           state_ffn_conv, norm1_g, w_in, a_conv_w, a_log, a_dt_bias, a_norm_g, b_conv_w, c_sinks, d_ln_g,
           d_ln_b, d_ws, d_bias, w_out, norm2_g, ffn_w_gate, ffn_w_up, ffn_conv_w, ffn_w_down, final_norm_g):
    bp, tp, _ = x_prompt.shape
    bs, ts, _ = x_sample.shape
    depth = w_in.shape[0]
    win_buf = cache_win_k.shape[2]
    pos_p = jnp.arange(tp, dtype=jnp.int32)
    pos_s = PAST_LEN + jnp.arange(ts, dtype=jnp.int32)
    assert win_buf == WINDOW and ts == 4
    tabs_p = _rope_tables(pos_p)
    tabs_s = tuple(jnp.repeat(t, bs, axis=0) for t in _rope_tables(pos_s))
    fng = final_norm_g[None, :]
    P = CARRY_ROWS

    hp = x_prompt
    hs = jnp.swapaxes(x_sample, 0, 1).reshape(ts * bs, D_MODEL)
    outs = {k: [] for k in ('sp', 'ss', 'acp', 'acs', 'bcp', 'bcs', 'ckp', 'cks', 'cvp', 'cvs', 'fcp', 'fcs', 'dv')}
    for l in range(depth):
        lw = _layer_weights(l, norm1_g, w_in, a_conv_w, a_log, a_dt_bias, a_norm_g, b_conv_w, c_sinks, d_ln_g,
                            d_ln_b, d_ws, d_bias, w_out, norm2_g, ffn_w_gate, ffn_w_up, ffn_conv_w, ffn_w_down)
        last = l == depth - 1
        hp, s_bd, acv, bcv, ckn, cvn = _mixer_call(
            hp, lw, tabs_p, jnp.zeros((bp, P, 768), F32), jnp.zeros((bp, P, 256), F32),
            jnp.zeros((bp, 256, 256), F32))
        hp, fcv = _ffn_call(hp, lw, jnp.zeros((bp, P, D_FF), F32), fng, tile=TILE_ROWS, stride=1, final_norm=last)
        outs['sp'].append(_diag_blocks(s_bd))
        outs['acp'].append(acv[:, P - (A_CONV - 1):])
        outs['bcp'].append(bcv[:, P - (B_CONV - 1):])
        outs['ckp'].append(ckn.reshape(bp, WINDOW, C_KV_HEADS, HEAD_DIM))
        outs['cvp'].append(cvn.reshape(bp, WINDOW, C_KV_HEADS, HEAD_DIM))
        outs['fcp'].append(fcv[:, P - (FFN_CONV - 1):])
        hs, ns = _sample_mixer(hs, lw, tabs_s, d_ws[l], d_bias[l], state_delta[l], state_delta_conv[l],
                               state_shortconv[l], cache_win_k[l], cache_win_v[l], bs, ts)
        f0 = jnp.swapaxes(state_ffn_conv[l], 0, 1).reshape(1, (FFN_CONV - 1) * bs, D_FF)
        ys_tm, fcs = _ffn_call(hs[None], lw, f0, fng, tile=ts * bs, stride=bs, final_norm=last)
        hs = ys_tm[0]
        outs['ss'].append(ns['a_state'])
        outs['acs'].append(ns['a_conv'])
        outs['bcs'].append(ns['b_conv'])
        outs['cks'].append(ns['c_k'])
        outs['cvs'].append(ns['c_v'])
        outs['fcs'].append(jnp.swapaxes(fcs.reshape(FFN_CONV - 1, bs, D_FF), 0, 1))
        outs['dv'].append(ns['d_v'])
    st = {k: jnp.stack(v) for k, v in outs.items()}
    hs = jnp.swapaxes(hs.reshape(ts, bs, D_MODEL), 0, 1)
    return (hp, hs, st['sp'], st['ss'], st['acp'], st['acs'], st['bcp'], st['bcs'], st['ckp'], st['cks'],
            st['cvp'], st['cvs'], st['fcp'], st['fcs'], st['dv'])
```

```python
import functools

import jax
import jax.numpy as jnp
import numpy as np
from jax import lax
from jax.experimental import pallas as pl
from jax.experimental.pallas import tpu as pltpu

F32 = jnp.float32
BF16 = jnp.bfloat16

D_MODEL = 1024
GROUP_WIDTH = 256
HEAD_DIM = 64
A_HEADS = 4
A_CONV = 4
A_CHUNK = 64
B_CONV = 3
C_HEADS = 4
C_KV_HEADS = 2
WINDOW = 128
ROPE_DIM = 16
ROPE_THETA = 500000.0
D_GROUPS = 4
D_CHUNK = 128
D_FF = 2816
FFN_CONV = 3
EPS = 1e-6
NEG_INF = -1e30
PAST_LEN = 16384

COL_A = 0
W_A = 1152
COL_B = 1152
W_B = 768
COL_C = 1920
W_C = 512
COL_D = 2432
W_D = 512
P_PACKED = 2944

TILE_ROWS = 512
CARRY_ROWS = 8
FFN_COL_CHUNK = 1408
VMEM_LIMIT_BYTES = 56 * 1024 * 1024
MIXER_VMEM_LIMIT_BYTES = 60 * 1024 * 1024


def _dot(a, b):
    return jnp.dot(a, b, preferred_element_type=F32)


def _dot_nt(a, b):
    return lax.dot_general(a, b, (((1,), (1,)), ((), ())), preferred_element_type=F32)


def _dot_tn(a, b):
    return lax.dot_general(a, b, (((0,), (0,)), ((), ())), preferred_element_type=F32)


def _split3(x):
    hi = x.astype(BF16)
    r1 = x - hi.astype(F32)
    mid = r1.astype(BF16)
    lo = (r1 - mid.astype(F32)).astype(BF16)
    return hi, mid, lo


def _dot_lhs3(x, w01):
    hi, mid, lo = _split3(x)
    return _dot(hi, w01) + _dot(mid, w01) + _dot(lo, w01)


def _dot_lhs2(x, w01):
    hi = x.astype(BF16)
    mid = (x - hi.astype(F32)).astype(BF16)
    return _dot(hi, w01) + _dot(mid, w01)


def _sigmoid(x):
    return 1.0 / (1.0 + jnp.exp(-x))


def _silu(x):
    return x * _sigmoid(x)


def _softplus(x):
    return jnp.maximum(x, 0.0) + jnp.log(1.0 + jnp.exp(-jnp.abs(x)))


def _gelu_tanh(x):
    return 0.5 * x * (1.0 + jnp.tanh(np.sqrt(2.0 / np.pi).astype(np.float32) * (x + 0.044715 * (x * x * x))))


def _rms(x, g):
    return x * lax.rsqrt(jnp.mean(x * x, axis=-1, keepdims=True) + EPS) * g


def _iota(shape, dim):
    return lax.broadcasted_iota(jnp.int32, shape, dim)


MIXER_SEQS = 2


def _mixer_kernel(sinks_ref, x_ref, n1g_ref, w_in_ref, aconv_w_ref, alog_ref, dtb_ref, anorm_ref,
                  bconv_w_ref, cos_ref, sina_ref, sinb_ref, lng_ref, lnb_ref, ws_ref, dbias_ref,
                  w_out_ref, aconv0_ref, bconv0_ref, s0_ref,
                  x_out_ref, s_out_ref, aconv_out_ref, bconv_out_ref, ck_out_ref, cv_out_ref,
                  *scratch, tile):
    shared = (sinks_ref, n1g_ref, w_in_ref, aconv_w_ref, alog_ref, dtb_ref, anorm_ref, bconv_w_ref, cos_ref,
              sina_ref, sinb_ref, lng_ref, lnb_ref, ws_ref, dbias_ref, w_out_ref)
    per_seq = (x_ref, aconv0_ref, bconv0_ref, s0_ref, x_out_ref, s_out_ref, aconv_out_ref, bconv_out_ref,
               ck_out_ref, cv_out_ref)
    n_scr = len(scratch) // MIXER_SEQS
    gens = [_mixer_seq(*shared, *(r.at[b] for r in per_seq), *scratch[b * n_scr:(b + 1) * n_scr], tile=tile)
            for b in range(MIXER_SEQS)]
    started, live = 1, list(gens[:1])
    while live:
        for gen in list(live):
            if next(gen, 'done') == 'done':
                live.remove(gen)
        if started < len(gens):
            live.append(gens[started])
            started += 1


def _mixer_seq(sinks_ref, n1g_ref, w_in_ref, aconv_w_ref, alog_ref, dtb_ref, anorm_ref, bconv_w_ref, cos_ref,
               sina_ref, sinb_ref, lng_ref, lnb_ref, ws_ref, dbias_ref, w_out_ref,
               x_ref, aconv0_ref, bconv0_ref, s0_ref, x_out_ref, s_out_ref, aconv_out_ref, bconv_out_ref,
               ck_out_ref, cv_out_ref,
               abuf, bbuf, kbuf, vbuf, q_s, k_s, v_s, gcb_s, bb_s, o_s, s_scr, proj_s, *, tile):
    T = tile
    si = pl.program_id(1)
    P = CARRY_ROWS

    @pl.when(si == 0)
    def _():
        abuf[0:P, :] = aconv0_ref[...]
        bbuf[0:P, :] = bconv0_ref[...]
        kbuf[0:WINDOW, :] = jnp.zeros((WINDOW, 128), F32)
        vbuf[0:WINDOW, :] = jnp.zeros((WINDOW, 128), F32)
        s_scr[...] = s0_ref[...]

    res = {}
    h = _rms(x_ref[...], n1g_ref[...]).astype(BF16)
    for lo, width in ((COL_A, W_A), (COL_B, W_B), (COL_C, W_C), (COL_D, W_D)):
        proj_s[:, lo:lo + width] = _dot(h, w_in_ref[:, lo:lo + width])
    yield

    small = proj_s[:, 1024:1152]
    abuf[P:P + T, :] = proj_s[:, 0:768]

    def group_b():
        bbuf[P:P + T, :] = proj_s[:, COL_B + 256:COL_B + 512] * proj_s[:, COL_B + 512:COL_B + 768]
        bx = (bbuf[P:P + T, :] * bconv_w_ref[2:3, :] + bbuf[P - 1:P - 1 + T, :] * bconv_w_ref[1:2, :]
              + bbuf[P - 2:P - 2 + T, :] * bconv_w_ref[0:1, :])
        tail_b = bbuf[T:T + P, :]
        bbuf[0:P, :] = tail_b
        bconv_out_ref[...] = tail_b
        res['out_b'] = proj_s[:, COL_B:COL_B + 256] * bx

    def group_c_rope():
        cos = cos_ref[...]
        sina = sina_ref[...]
        sinb = sinb_ref[...]
        cq = proj_s[:, COL_C:COL_C + 256]
        res['cq'] = (cq * jnp.concatenate([cos, cos], axis=1)
                     + pltpu.roll(cq, 256 - ROPE_DIM // 2, 1) * jnp.concatenate([sina, sina], axis=1)
                     + pltpu.roll(cq, ROPE_DIM // 2, 1) * jnp.concatenate([sinb, sinb], axis=1))
        ck = proj_s[:, COL_C + 256:COL_C + 384]
        ck = (ck * cos + pltpu.roll(ck, 128 - ROPE_DIM // 2, 1) * sina
              + pltpu.roll(ck, ROPE_DIM // 2, 1) * sinb)
        kbuf[WINDOW:WINDOW + T, :] = ck
        vbuf[WINDOW:WINDOW + T, :] = proj_s[:, COL_C + 384:COL_C + 512]
        ck_out_ref[...] = kbuf[T:T + WINDOW, :]
        cv_out_ref[...] = vbuf[T:T + WINDOW, :]

    def group_d_norm():
        res['du'] = _gelu_tanh(proj_s[:, COL_D:COL_D + 256])
        gv = _gelu_tanh(proj_s[:, COL_D + 256:COL_D + 512])
        mu = jnp.mean(gv, axis=-1, keepdims=True)
        xc = gv - mu
        res['dvn'] = (xc * lax.rsqrt(jnp.mean(xc * xc, axis=-1, keepdims=True) + EPS) * lng_ref[...]
                      + lnb_ref[...])
        wr = _iota((D_GROUPS * D_CHUNK, D_CHUNK), 0) & (D_CHUNK - 1)
        wc = _iota((D_GROUPS * D_CHUNK, D_CHUNK), 1)
        res['wm'] = jnp.where(wr >= wc, ws_ref[...], 0.0).astype(BF16)

    y = (abuf[P:P + T, :] * aconv_w_ref[3:4, :] + abuf[P - 1:P - 1 + T, :] * aconv_w_ref[2:3, :]
         + abuf[P - 2:P - 2 + T, :] * aconv_w_ref[1:2, :] + abuf[P - 3:P - 3 + T, :] * aconv_w_ref[0:1, :])
    tail_a = abuf[T:T + P, :]
    abuf[0:P, :] = tail_a
    aconv_out_ref[...] = tail_a
    qkv = _silu(y)
    q_raw = qkv[:, 0:256]
    k_raw = qkv[:, 256:512]
    v_s[...] = qkv[:, 512:768]

    r256 = _iota((256, 256), 0) >> 6
    c256 = _iota((256, 256), 1) >> 6
    mask_bd = r256 == c256
    ones_bd = jnp.where(mask_bd, 1.0, 0.0).astype(BF16)
    q_s[...] = q_raw * lax.rsqrt(_dot_lhs2(q_raw * q_raw, ones_bd) + EPS) * (HEAD_DIM ** -0.5)
    k_s[...] = k_raw * lax.rsqrt(_dot_lhs2(k_raw * k_raw, ones_bd) + EPS)
    yield

    g_log = -jnp.exp(alog_ref[...]) * _softplus(small + dtb_ref[...])
    beta = _sigmoid(small)
    gbeta = jnp.where(_iota((T, 128), 1) < A_HEADS, g_log, beta)
    expand = jnp.where(_iota((128, 512), 0) == (_iota((128, 512), 1) >> 6), 1.0, 0.0).astype(BF16)
    gbb = _dot_lhs3(gbeta, expand)
    bb_s[...] = gbb[:, 256:512]

    def chunk_cumsum(xv):
        row_in_chunk = _iota(xv.shape, 0) & (A_CHUNK - 1)
        step = 1
        while step < A_CHUNK:
            xv = xv + jnp.where(row_in_chunk >= step, pltpu.roll(xv, step, 0), 0.0)
            step *= 2
        return xv

    gcb_s[...] = chunk_cumsum(gbb[:, 0:256])
    gct = chunk_cumsum(g_log).T[0:8, :]
    low_half = (_iota((1, T), 1) & A_CHUNK) == 0
    gct_r = pltpu.roll(gct, A_CHUNK, 1)
    gct_l = pltpu.roll(gct, T - A_CHUNK, 1)
    even_rows = [jnp.where(low_half, gct[a:a + 1, :], gct_r[a + 1:a + 2, :]) for a in (0, 2)]
    odd_rows = [jnp.where(low_half, gct_l[a:a + 1, :], gct[a + 1:a + 2, :]) for a in (0, 2)]
    yield

    ri = _iota((A_CHUNK, 256), 0)
    ci = _iota((A_CHUNK, 256), 1) & (A_CHUNK - 1)
    causal_t = ri >= ci
    strict_t = ri > ci
    eye_t = jnp.where(ri == ci, 1.0, 0.0)

    def bd16(x16):
        return jnp.concatenate([x16, x16, x16, x16], axis=0) * ones_bd

    n_chunks = T // A_CHUNK
    t_inv, pw, qk, qdec, e_tail, vb16, kbe16 = [], [], [], [], [], [], []
    for c in range(n_chunks):
        rows = slice(c * A_CHUNK, (c + 1) * A_CHUNK)
        blk = slice((c // 2) * 128, (c // 2) * 128 + 128)
        src = even_rows if c % 2 == 0 else odd_rows
        gc_row = jnp.concatenate([src[0][:, blk], src[1][:, blk]], axis=1)
        qc = q_s[rows, :]
        kc = k_s[rows, :]
        bbc = bb_s[rows, :]
        gcb = gcb_s[rows, :]
        decay = jnp.where(causal_t, jnp.exp(jnp.where(causal_t, gcb - gc_row, 0.0)), 0.0)
        eg = jnp.exp(gcb)
        kb = kc * bbc
        aq = _dot_nt(jnp.concatenate([kb, qc], axis=0).astype(BF16), bd16(kc.astype(BF16)))
        a_mat = jnp.where(strict_t, aq[0:A_CHUNK] * decay, 0.0)
        qk.append(aq[A_CHUNK:2 * A_CHUNK] * decay)
        t_inv.append(eye_t - a_mat)
        pw.append(a_mat)
        qdec.append(qc * eg)
        e_tail.append(jnp.exp(gcb[A_CHUNK - 1:A_CHUNK, :] - gcb))
        vb16.append((v_s[rows, :] * bbc).astype(BF16))
        kbe16.append((kb * eg).astype(BF16))
    yield
    u, w = [], []
    state = {'s': s_scr[...]}

    def level_first():
        for c in range(n_chunks):
            p16 = pw[c].astype(BF16)
            pw[c] = _dot(p16, bd16(p16))

    def level_mid():
        for c in range(n_chunks):
            p16 = pw[c].astype(BF16)
            res = _dot(jnp.concatenate([p16, t_inv[c].astype(BF16)], axis=0), bd16(p16))
            pw[c] = res[0:A_CHUNK]
            t_inv[c] = t_inv[c] + res[A_CHUNK:2 * A_CHUNK]

    def level_last():
        for c in range(n_chunks):
            t_c = t_inv[c] + _dot(t_inv[c].astype(BF16), bd16(pw[c].astype(BF16)))
            uw = _dot(t_c.astype(BF16), jnp.concatenate([bd16(vb16[c]), bd16(kbe16[c])], axis=1))
            u.append(uw[:, 0:256])
            w.append(uw[:, 256:512])

    def scan_step(c):
        rows = slice(c * A_CHUNK, (c + 1) * A_CHUNK)
        s_bd = state['s']
        wq = _dot(jnp.concatenate([w[c], qdec[c]], axis=0).astype(BF16), s_bd.astype(BF16))
        v_new = u[c] - wq[0:A_CHUNK]
        o_s[rows, :] = wq[A_CHUNK:2 * A_CHUNK] + _dot(qk[c].astype(BF16), bd16(v_new.astype(BF16)))
        kv = _dot_tn(k_s[rows, :].astype(BF16), (v_new * e_tail[c]).astype(BF16))
        g_last = gcb_s[(c + 1) * A_CHUNK - 1:(c + 1) * A_CHUNK, :]
        state['s'] = s_bd * jnp.exp(g_last) + jnp.where(mask_bd, kv, 0.0)

    chain = [level_first] + [level_mid] * 4 + [level_last] + [functools.partial(scan_step, c) for c in range(n_chunks)]

    lane128 = _iota((2 * WINDOW, 128), 1)
    low = _iota((WINDOW, 128), 1) < HEAD_DIM
    qrow = _iota((2 * WINDOW, 2 * WINDOW), 0) & (WINDOW - 1)
    kcol = _iota((2 * WINDOW, 2 * WINDOW), 1)
    band = (kcol > qrow) & (kcol <= qrow + WINDOW)
    top_half = _iota((2 * WINDOW, 1), 0) < WINDOW
    out_c_blocks = []

    def group_c_block(n):
        first_key = si * T + (n - 1) * WINDOW
        valid = band & (kcol + first_key >= 0)
        kwin = kbuf[n * WINDOW:(n + 2) * WINDOW, :]
        vwin = vbuf[n * WINDOW:(n + 2) * WINDOW, :]
        k_sw = pltpu.roll(kwin, HEAD_DIM, 1)
        v_sw = pltpu.roll(vwin, HEAD_DIM, 1)
        pair_out = []
        for g in range(C_KV_HEADS):
            own = (lane128 < HEAD_DIM) if g == 0 else (lane128 >= HEAD_DIM)
            k_dup = jnp.where(own, kwin, k_sw).astype(BF16)
            v_dup = jnp.where(own, vwin, v_sw).astype(BF16)
            qp = res['cq'][n * WINDOW:(n + 1) * WINDOW, g * 128:(g + 1) * 128]
            q_st = jnp.concatenate([jnp.where(low, qp, 0.0), jnp.where(low, 0.0, qp)], axis=0).astype(BF16)
            s = _dot_nt(q_st, k_dup) * (HEAD_DIM ** -0.5)
            s = jnp.where(valid, s, NEG_INF)
            sink = jnp.where(top_half, sinks_ref[2 * g], sinks_ref[2 * g + 1])
            m = jnp.maximum(jnp.max(s, axis=-1, keepdims=True), sink)
            p = jnp.exp(s - m)
            denom = jnp.sum(p, axis=-1, keepdims=True) + jnp.exp(sink - m)
            o2 = _dot((p / denom).astype(BF16), v_dup)
            pair_out.append(jnp.where(low, o2[0:WINDOW], o2[WINDOW:2 * WINDOW]))
        out_c_blocks.append(jnp.concatenate(pair_out, axis=1))
        if n == T // WINDOW - 1:
            kbuf[0:WINDOW, :] = kbuf[T:T + WINDOW, :]
            vbuf[0:WINDOW, :] = vbuf[T:T + WINDOW, :]

    lane_grp = _iota((D_CHUNK, 256), 1) >> 6
    out_d_blocks = []

    def group_d_block(n):
        mx = _dot(res['wm'], res['dvn'][n * D_CHUNK:(n + 1) * D_CHUNK, :].astype(BF16))
        mixed = dbias_ref[...]
        for grp in range(D_GROUPS):
            mixed = mixed + jnp.where(lane_grp == grp, mx[grp * D_CHUNK:(grp + 1) * D_CHUNK, :], 0.0)
        out_d_blocks.append(res['du'][n * D_CHUNK:(n + 1) * D_CHUNK, :] * mixed)

    c_blocks = [functools.partial(group_c_block, n) for n in range(T // WINDOW)]
    d_blocks = [functools.partial(group_d_block, n) for n in range(T // D_CHUNK)]
    fill = [group_b, group_c_rope, group_d_norm] + c_blocks + d_blocks
    for i in range(max(len(chain), len(fill))):
        if i < len(chain):
            chain[i]()
        if i < len(fill):
            fill[i]()
        yield
    out_b = res['out_b']
    out_c = jnp.concatenate(out_c_blocks, axis=0)
    out_d = jnp.concatenate(out_d_blocks, axis=0)

    s_scr[...] = state['s']
    s_out_ref[...] = state['s']
    o = o_s[...]
    out_a = (o * lax.rsqrt(_dot_lhs2(o * o, ones_bd) * (1.0 / HEAD_DIM) + EPS) * anorm_ref[...]
             * _silu(proj_s[:, 768:1024]))

    cat = jnp.concatenate([out_a, out_b, out_c, out_d], axis=1).astype(BF16)
    x_out_ref[...] = x_ref[...] + _dot(cat, w_out_ref[...])


def _mixer_call(x, lw, tabs, aconv0, bconv0, s0):
    bsz, seq, _ = x.shape
    T = TILE_ROWS
    ns = seq // T
    NB = MIXER_SEQS
    P = CARRY_ROWS
    full = lambda shape: pl.BlockSpec(shape, lambda b, s, *_: (0,) * len(shape), pipeline_mode=pl.Buffered(1))
    per_b = lambda shape: pl.BlockSpec((NB,) + shape, lambda b, s, *_: (b,) + (0,) * len(shape))
    tab = pl.BlockSpec((T, 128), lambda b, s, *_: (s, 0))
    seq_scratch = [pltpu.VMEM(shape, F32) for shape in (
        (T + P, 768), (T + P, 256), (T + WINDOW, 128), (T + WINDOW, 128), (T, 256), (T, 256), (T, 256),
        (T, 256), (T, 256), (T, 256), (256, 256), (T, P_PACKED))]
    grid_spec = pltpu.PrefetchScalarGridSpec(
        num_scalar_prefetch=1,
        grid=(bsz // NB, ns),
        in_specs=[
            pl.BlockSpec((NB, T, D_MODEL), lambda b, s, *_: (b, s, 0)),
            full((1, D_MODEL)), full((D_MODEL, P_PACKED)), full((A_CONV, 768)), full((1, 128)), full((1, 128)),
            full((1, 256)), full((B_CONV, 256)), tab, tab, tab, full((1, 256)), full((1, 256)),
            full((D_GROUPS * D_CHUNK, D_CHUNK)), full((D_CHUNK, 256)), full((D_MODEL, D_MODEL)),
            per_b((P, 768)), per_b((P, 256)), per_b((256, 256)),
        ],
        out_specs=[
            pl.BlockSpec((NB, T, D_MODEL), lambda b, s, *_: (b, s, 0)),
            per_b((256, 256)), per_b((P, 768)), per_b((P, 256)), per_b((WINDOW, 128)), per_b((WINDOW, 128)),
        ],
        scratch_shapes=seq_scratch * NB,
    )
    out_shape = [
        jax.ShapeDtypeStruct((bsz, seq, D_MODEL), F32),
        jax.ShapeDtypeStruct((bsz, 256, 256), F32),
        jax.ShapeDtypeStruct((bsz, P, 768), F32),
        jax.ShapeDtypeStruct((bsz, P, 256), F32),
        jax.ShapeDtypeStruct((bsz, WINDOW, 128), F32),
        jax.ShapeDtypeStruct((bsz, WINDOW, 128), F32),
    ]
    return pl.pallas_call(
        functools.partial(_mixer_kernel, tile=T),
        grid_spec=grid_spec,
        out_shape=out_shape,
        compiler_params=pltpu.CompilerParams(
            dimension_semantics=("arbitrary", "arbitrary"), vmem_limit_bytes=MIXER_VMEM_LIMIT_BYTES),
        name="mixer",
    )(lw['c_sinks'], x, lw['norm1_g'], lw['w_in'], lw['a_conv_w'], lw['a_log'], lw['a_dt_bias'],
      lw['a_norm_g'], lw['b_conv_w'], tabs[0], tabs[1], tabs[2], lw['d_ln_g'], lw['d_ln_b'],
      lw['d_ws'], lw['d_bias'], lw['w_out'], aconv0, bconv0, s0)


def _ffn_kernel(x_ref, n2g_ref, wg_ref, wu_ref, cw_ref, wd_ref, fconv0_ref, fng_ref,
                x_out_ref, fconv_out_ref, gbuf, *, tile, stride, final_norm):
    T = tile
    P = (FFN_CONV - 1) * stride if stride > 1 else CARRY_ROWS
    si = pl.program_id(1)

    @pl.when(si == 0)
    def _():
        gbuf[0:P, :] = fconv0_ref[0]

    x = x_ref[0]
    h = _rms(x, n2g_ref[...]).astype(BF16)
    acc = x
    for c0 in range(0, D_FF, FFN_COL_CHUNK):
        cols = slice(c0, c0 + FFN_COL_CHUNK)
        gbuf[P:P + T, cols] = _dot(h, wg_ref[:, cols])
        up = _dot(h, wu_ref[:, cols])
        gate = (gbuf[P:P + T, cols] * cw_ref[2:3, cols] + gbuf[P - stride:P - stride + T, cols] * cw_ref[1:2, cols]
                + gbuf[P - 2 * stride:P - 2 * stride + T, cols] * cw_ref[0:1, cols])
        acc = acc + _dot((_silu(gate) * up).astype(BF16), wd_ref[cols, :])
    tail = gbuf[T:T + P, :]
    gbuf[0:P, :] = tail
    fconv_out_ref[0] = tail
    if final_norm:
        acc = _rms(acc, fng_ref[...])
    x_out_ref[0] = acc


def _ffn_call(x, lw, fconv0, final_g, *, tile, stride, final_norm):
    bsz, seq, _ = x.shape
    T = tile
    P = (FFN_CONV - 1) * stride if stride > 1 else CARRY_ROWS
    full = lambda shape: pl.BlockSpec(shape, lambda b, s: (0,) * len(shape), pipeline_mode=pl.Buffered(1))
    per_b = lambda shape: pl.BlockSpec((1,) + shape, lambda b, s: (b,) + (0,) * len(shape))
    return pl.pallas_call(
        functools.partial(_ffn_kernel, tile=T, stride=stride, final_norm=final_norm),
        grid=(bsz, seq // T),
        in_specs=[
            pl.BlockSpec((1, T, D_MODEL), lambda b, s: (b, s, 0)),
            full((1, D_MODEL)), full((D_MODEL, D_FF)), full((D_MODEL, D_FF)), full((FFN_CONV, D_FF)),
            full((D_FF, D_MODEL)), per_b((P, D_FF)), full((1, D_MODEL)),
        ],
        out_specs=[pl.BlockSpec((1, T, D_MODEL), lambda b, s: (b, s, 0)), per_b((P, D_FF))],
        out_shape=[jax.ShapeDtypeStruct((bsz, seq, D_MODEL), F32), jax.ShapeDtypeStruct((bsz, P, D_FF), F32)],
        scratch_shapes=[pltpu.VMEM((T + P, D_FF), F32)],
        compiler_params=pltpu.CompilerParams(
            dimension_semantics=("arbitrary", "arbitrary"), vmem_limit_bytes=VMEM_LIMIT_BYTES),
        name="ffn",
    )(x, lw['norm2_g'], lw['ffn_w_gate'], lw['ffn_w_up'], lw['ffn_conv_w'], lw['ffn_w_down'], fconv0, final_g)


def _pack_w_in(w):
    pad = jnp.zeros((D_MODEL, 128 - 2 * A_HEADS), w.dtype)
    small = jnp.concatenate([w[:, 1028:1032], w[:, 1024:1028], pad], axis=1)
    return jnp.concatenate([w[:, 0:1024], small, w[:, 1032:]], axis=1).astype(BF16)


def _pad_lanes(v, width=128):
    return jnp.concatenate([v, jnp.zeros((width - v.shape[0],), v.dtype)])[None, :]


def _layer_weights(l, norm1_g, w_in, a_conv_w, a_log, a_dt_bias, a_norm_g, b_conv_w, c_sinks, d_ln_g, d_ln_b,
                   d_ws, d_bias, w_out, norm2_g, ffn_w_gate, ffn_w_up, ffn_conv_w, ffn_w_down):
    bias_tab = jnp.broadcast_to(d_bias[l].T[:, :, None], (D_CHUNK, D_GROUPS, HEAD_DIM)).reshape(D_CHUNK, 256)
    return {
        'norm1_g': norm1_g[l][None, :], 'w_in': _pack_w_in(w_in[l]), 'a_conv_w': a_conv_w[l],
        'a_log': _pad_lanes(a_log[l]), 'a_dt_bias': _pad_lanes(a_dt_bias[l]),
        'a_norm_g': jnp.tile(a_norm_g[l], A_HEADS)[None, :], 'b_conv_w': b_conv_w[l], 'c_sinks': c_sinks[l],
        'd_ln_g': d_ln_g[l][None, :], 'd_ln_b': d_ln_b[l][None, :],
        'd_ws': d_ws[l].reshape(D_GROUPS * D_CHUNK, D_CHUNK), 'd_bias': bias_tab,
        'w_out': w_out[l].astype(BF16), 'norm2_g': norm2_g[l][None, :],
        'ffn_w_gate': ffn_w_gate[l].astype(BF16), 'ffn_w_up': ffn_w_up[l].astype(BF16),
        'ffn_conv_w': ffn_conv_w[l], 'ffn_w_down': ffn_w_down[l].astype(BF16),
    }


def _rope_tables(pos):
    half = ROPE_DIM // 2
    inv = jnp.power(ROPE_THETA, -jnp.arange(half, dtype=F32) * (2.0 / ROPE_DIM))
    ang = pos.astype(F32)[:, None] * inv[None, :]
    cos, sin = jnp.cos(ang), jnp.sin(ang)
    n = pos.shape[0]
    rest = HEAD_DIM - ROPE_DIM
    cos_h = jnp.concatenate([cos, cos, jnp.ones((n, rest), F32)], axis=1)
    sina_h = jnp.concatenate([-sin, jnp.zeros((n, half + rest), F32)], axis=1)
    sinb_h = jnp.concatenate([jnp.zeros((n, half), F32), sin, jnp.zeros((n, rest), F32)], axis=1)
    return tuple(jnp.concatenate([t, t], axis=1) for t in (cos_h, sina_h, sinb_h))


def _diag_blocks(s_bd):
    return jnp.stack([s_bd[:, HEAD_DIM * i:HEAD_DIM * (i + 1), HEAD_DIM * i:HEAD_DIM * (i + 1)]
                      for i in range(A_HEADS)], axis=1)


def _sample_pre_kernel(x_ref, n1g_ref, w_in_ref, aconv_w_ref, alog_ref, dtb_ref, bconv_w_ref, cos_ref, sina_ref,
                       sinb_ref, lng_ref, lnb_ref, wtab_ref, btab_ref, aconv0_ref, bconv0_ref,
                       q_ref, k_ref, v_ref, g_ref, beta_ref, z_ref, outb_ref, cq_ref, ck_ref, cv_ref, outd_ref,
                       dvn_ref, aconv_out_ref, bconv_out_ref, abuf, bbuf, *, rows, stride):
    R, S = rows, stride
    steps = R // S
    pa_rows = (A_CONV - 1) * S
    pb_rows = (B_CONV - 1) * S
    x = x_ref[...]
    h = _rms(x, n1g_ref[...]).astype(BF16)

    pa = _dot(h, w_in_ref[:, COL_A:COL_A + W_A])
    z_ref[...] = pa[:, 768:1024]
    small = pa[:, 1024:1152]
    abuf[0:pa_rows, :] = aconv0_ref[...]
    abuf[pa_rows:pa_rows + R, :] = pa[:, 0:768]
    y = abuf[0:R, :] * aconv_w_ref[0:1, :]
    for j in range(1, A_CONV):
        y = y + abuf[j * S:j * S + R, :] * aconv_w_ref[j:j + 1, :]
    aconv_out_ref[...] = abuf[R:R + pa_rows, :]
    qkv = _silu(y)
    q_raw = qkv[:, 0:256]
    k_raw = qkv[:, 256:512]
    v_ref[...] = qkv[:, 512:768]
    mask_bd = (_iota((256, 256), 0) >> 6) == (_iota((256, 256), 1) >> 6)
    ones_bd = jnp.where(mask_bd, 1.0, 0.0).astype(BF16)
    q_ref[...] = q_raw * lax.rsqrt(_dot_lhs3(q_raw * q_raw, ones_bd) + EPS) * (HEAD_DIM ** -0.5)
    k_ref[...] = k_raw * lax.rsqrt(_dot_lhs3(k_raw * k_raw, ones_bd) + EPS)
    g_ref[...] = -jnp.exp(alog_ref[...]) * _softplus(small + dtb_ref[...])
    beta_ref[...] = _sigmoid(small)

    pb = _dot(h, w_in_ref[:, COL_B:COL_B + W_B])
    bbuf[0:pb_rows, :] = bconv0_ref[...]
    bbuf[pb_rows:pb_rows + R, :] = pb[:, 256:512] * pb[:, 512:768]
    bx = bbuf[0:R, :] * bconv_w_ref[0:1, :]
    for j in range(1, B_CONV):
        bx = bx + bbuf[j * S:j * S + R, :] * bconv_w_ref[j:j + 1, :]
    bconv_out_ref[...] = bbuf[R:R + pb_rows, :]
    outb_ref[...] = pb[:, 0:256] * bx

    pc = _dot(h, w_in_ref[:, COL_C:COL_C + W_C])
    cos = cos_ref[...]
    sina = sina_ref[...]
    sinb = sinb_ref[...]
    cq = pc[:, 0:256]
    cq_ref[...] = (cq * jnp.concatenate([cos, cos], axis=1)
                   + pltpu.roll(cq, 256 - ROPE_DIM // 2, 1) * jnp.concatenate([sina, sina], axis=1)
                   + pltpu.roll(cq, ROPE_DIM // 2, 1) * jnp.concatenate([sinb, sinb], axis=1))
    ck = pc[:, 256:384]
    ck_ref[...] = (ck * cos + pltpu.roll(ck, 128 - ROPE_DIM // 2, 1) * sina
                   + pltpu.roll(ck, ROPE_DIM // 2, 1) * sinb)
    cv_ref[...] = pc[:, 384:512]

    pd = _dot(h, w_in_ref[:, COL_D:COL_D + W_D])
    du = _gelu_tanh(pd[:, 0:256])
    gv = _gelu_tanh(pd[:, 256:512])
    mu = jnp.mean(gv, axis=-1, keepdims=True)
    xc = gv - mu
    dvn = xc * lax.rsqrt(jnp.mean(xc * xc, axis=-1, keepdims=True) + EPS) * lng_ref[...] + lnb_ref[...]
    dvn_ref[...] = dvn
    for t in range(steps):
        mixed = btab_ref[t:t + 1, :]
        for s in range(t + 1):
            mixed = mixed + wtab_ref[t * steps + s:t * steps + s + 1, :] * dvn[s * S:(s + 1) * S, :]
        outd_ref[t * S:(t + 1) * S, :] = du[t * S:(t + 1) * S, :] * mixed


def _sample_pre_call(x, lw, tabs, wtab, btab, aconv0, bconv0, stride):
    R = x.shape[0]
    f = lambda *shape: jax.ShapeDtypeStruct(shape, F32)
    out_shape = [f(R, 256), f(R, 256), f(R, 256), f(R, 128), f(R, 128), f(R, 256), f(R, 256), f(R, 256),
                 f(R, 128), f(R, 128), f(R, 256), f(R, 256), f((A_CONV - 1) * stride, 768),
                 f((B_CONV - 1) * stride, 256)]
    return pl.pallas_call(
        functools.partial(_sample_pre_kernel, rows=R, stride=stride),
        out_shape=out_shape,
        scratch_shapes=[pltpu.VMEM((R + (A_CONV - 1) * stride, 768), F32),
                        pltpu.VMEM((R + (B_CONV - 1) * stride, 256), F32)],
        compiler_params=pltpu.CompilerParams(vmem_limit_bytes=VMEM_LIMIT_BYTES),
        name="sample_pre",
    )(x, lw['norm1_g'], lw['w_in'], lw['a_conv_w'], lw['a_log'], lw['a_dt_bias'], lw['b_conv_w'],
      tabs[0], tabs[1], tabs[2], lw['d_ln_g'], lw['d_ln_b'], wtab, btab, aconv0, bconv0)


def _sample_delta_kernel(g_ref, beta_ref, q_ref, k_ref, v_ref, s_ref, o_ref, snew_ref, *, steps):
    n_i = HEAD_DIM
    nb = s_ref.shape[-1]
    zeros = jnp.zeros((HEAD_DIM, nb), F32)

    def rows_of(i):
        return pl.ds(pl.multiple_of(i * HEAD_DIM, HEAD_DIM), HEAD_DIM)

    dec0 = jnp.exp(g_ref[0, 0])

    def first_pass(i, acc):
        return acc + k_ref[0, 0, pl.ds(i, 1), :] * (s_ref[0, rows_of(i), :] * dec0)

    ks = lax.fori_loop(0, n_i, first_pass, zeros)
    for t in range(steps):
        dec = jnp.exp(g_ref[0, t])
        v_new = beta_ref[0, t] * (v_ref[0, t] - ks)
        src = s_ref if t == 0 else snew_ref
        dec_next = jnp.exp(g_ref[0, t + 1]) if t + 1 < steps else None

        def update(i, carry, t=t, dec=dec, v_new=v_new, src=src, dec_next=dec_next):
            o_acc, ks_acc = carry
            blk = src[0, rows_of(i), :] * dec + k_ref[0, t, pl.ds(i, 1), :] * v_new
            snew_ref[0, rows_of(i), :] = blk
            o_acc = o_acc + q_ref[0, t, pl.ds(i, 1), :] * blk
            if dec_next is not None:
                ks_acc = ks_acc + k_ref[0, t + 1, pl.ds(i, 1), :] * (blk * dec_next)
            return o_acc, ks_acc

        o_acc, ks = lax.fori_loop(0, n_i, update, (zeros, zeros))
        o_ref[0, t] = o_acc


def _sample_delta_call(g, beta, q, k, v, s):
    nh, steps, _, nb = q.shape
    vec = pl.BlockSpec((1, steps, 1, nb), lambda h: (h, 0, 0, 0))
    mat = pl.BlockSpec((1, steps, HEAD_DIM, nb), lambda h: (h, 0, 0, 0))
    st = pl.BlockSpec((1, HEAD_DIM * HEAD_DIM, nb), lambda h: (h, 0, 0))
    return pl.pallas_call(
        functools.partial(_sample_delta_kernel, steps=steps),
        grid=(nh,),
        in_specs=[vec, vec, mat, mat, mat, st],
        out_specs=[mat, st],
        out_shape=[jax.ShapeDtypeStruct((nh, steps, HEAD_DIM, nb), F32),
                   jax.ShapeDtypeStruct((nh, HEAD_DIM * HEAD_DIM, nb), F32)],
        compiler_params=pltpu.CompilerParams(dimension_semantics=("arbitrary",),
                                             vmem_limit_bytes=VMEM_LIMIT_BYTES),
        name="sample_delta",
    )(g, beta, q, k, v, s)


SAMPLE_ATTN_BLOCK = 8
NEW_KEY_ROWS = 8


def _sample_attn_kernel(sinks_ref, qm_ref, kc_ref, kn_ref, vc_ref, vn_ref, o_ref, *, steps):
    nq = C_HEADS * steps
    nk = WINDOW + NEW_KEY_ROWS
    row = _iota((nq, nk), 0)
    col = _iota((nq, nk), 1)
    t_q = row & (steps - 1)
    valid = ((col < WINDOW) & (col > t_q)) | ((col >= WINDOW) & (col - WINDOW <= t_q))
    head = _iota((nq, 1), 0) >> 2
    sink = jnp.where(head == 0, sinks_ref[0],
                     jnp.where(head == 1, sinks_ref[1], jnp.where(head == 2, sinks_ref[2], sinks_ref[3])))
    for b in range(SAMPLE_ATTN_BLOCK):
        k_all = jnp.concatenate([kc_ref[b], kn_ref[b]], axis=0).astype(BF16)
        v_all = jnp.concatenate([vc_ref[b], vn_ref[b]], axis=0).astype(BF16)
        s = _dot_nt(qm_ref[b].astype(BF16), k_all) * (HEAD_DIM ** -0.5)
        s = jnp.where(valid, s, NEG_INF)
        m = jnp.maximum(jnp.max(s, axis=-1, keepdims=True), sink)
        p = jnp.exp(s - m)
        denom = jnp.sum(p, axis=-1, keepdims=True) + jnp.exp(sink - m)
        o_ref[b] = _dot((p / denom).astype(BF16), v_all)


def _sample_attn_call(sinks, qm, kc, kn, vc, vn, steps):
    bs, nq, _ = qm.shape
    BB = SAMPLE_ATTN_BLOCK
    blk = lambda r: pl.BlockSpec((BB, r, 128), lambda i, *_: (i, 0, 0))
    grid_spec = pltpu.PrefetchScalarGridSpec(
        num_scalar_prefetch=1, grid=(bs // BB,),
        in_specs=[blk(nq), blk(WINDOW), blk(NEW_KEY_ROWS), blk(WINDOW), blk(NEW_KEY_ROWS)],
        out_specs=blk(nq))
    return pl.pallas_call(
        functools.partial(_sample_attn_kernel, steps=steps),
        grid_spec=grid_spec,
        out_shape=jax.ShapeDtypeStruct((bs, nq, 128), F32),
        compiler_params=pltpu.CompilerParams(dimension_semantics=("arbitrary",)),
        name="sample_attn",
    )(sinks, qm, kc, kn, vc, vn)


def _sample_post_kernel(x_ref, o_ref, z_ref, outb_ref, outc_ref, outd_ref, anorm_ref, w_out_ref, x_out_ref):
    mask_bd = (_iota((256, 256), 0) >> 6) == (_iota((256, 256), 1) >> 6)
    ones_bd = jnp.where(mask_bd, 1.0, 0.0).astype(BF16)
    o = o_ref[...]
    out_a = (o * lax.rsqrt(_dot_lhs3(o * o, ones_bd) * (1.0 / HEAD_DIM) + EPS) * anorm_ref[...]
             * _silu(z_ref[...]))
    cat = jnp.concatenate([out_a, outb_ref[...], outc_ref[...], outd_ref[...]], axis=1).astype(BF16)
    x_out_ref[...] = x_ref[...] + _dot(cat, w_out_ref[...])


def _sample_post_call(x, o, z, out_b, out_c, out_d, lw):
    return pl.pallas_call(
        _sample_post_kernel,
        out_shape=jax.ShapeDtypeStruct(x.shape, F32),
        compiler_params=pltpu.CompilerParams(vmem_limit_bytes=VMEM_LIMIT_BYTES),
        name="sample_post",
    )(x, o, z, out_b, out_c, out_d, lw['a_norm_g'], lw['w_out'])


def _sample_mixer(x_tm, lw, tabs, d_ws_l, d_bias_l, a_state, a_conv, b_conv, c_k, c_v, bs, ts):
    to_tm = lambda a: jnp.swapaxes(a, 0, 1).reshape(a.shape[1] * bs, a.shape[2])
    from_tm = lambda a, n: jnp.swapaxes(a.reshape(n, bs, a.shape[-1]), 0, 1)
    wtab = jnp.repeat(d_ws_l[:, :ts, :ts].transpose(1, 2, 0).reshape(ts * ts, D_GROUPS), HEAD_DIM, axis=1)
    btab = jnp.repeat(d_bias_l[:, :ts].T, HEAD_DIM, axis=1)
    (q, k, v, g, beta, z, out_b, cq, ck, cv, out_d, dvn, a_tail, b_tail) = _sample_pre_call(
        x_tm, lw, tabs, wtab, btab, to_tm(a_conv), to_tm(b_conv), bs)

    heads_t = lambda a: a.reshape(ts, bs, A_HEADS, HEAD_DIM).transpose(2, 0, 3, 1)
    vec_t = lambda a, lo: a[:, lo:lo + A_HEADS].reshape(ts, bs, A_HEADS).transpose(2, 0, 1)[:, :, None, :]
    s_t = a_state.reshape(bs, A_HEADS, HEAD_DIM * HEAD_DIM).transpose(1, 2, 0)
    o_t, s_new_t = _sample_delta_call(vec_t(g, 0), vec_t(beta, A_HEADS), heads_t(q), heads_t(k), heads_t(v), s_t)
    o = o_t.transpose(1, 3, 0, 2).reshape(ts * bs, GROUP_WIDTH)
    a_state_new = s_new_t.transpose(2, 0, 1).reshape(bs, A_HEADS, HEAD_DIM, HEAD_DIM)

    cq4 = cq.reshape(ts, bs, C_HEADS, HEAD_DIM).transpose(1, 2, 0, 3)
    zq = jnp.zeros_like(cq4[:, 0])
    qm = jnp.concatenate(
        [jnp.concatenate([cq4[:, hh], zq] if hh // 2 == 0 else [zq, cq4[:, hh]], axis=-1) for hh in range(C_HEADS)],
        axis=1)
    pad_new = lambda a: jnp.concatenate([from_tm(a, ts), jnp.zeros((bs, NEW_KEY_ROWS - ts, 128), F32)], axis=1)
    kc = c_k.reshape(bs, WINDOW, 128)
    vc = c_v.reshape(bs, WINDOW, 128)
    o_att = _sample_attn_call(lw['c_sinks'], qm, kc, pad_new(ck), vc, pad_new(cv), ts)
    out_c = jnp.concatenate(
        [o_att[:, hh * ts:(hh + 1) * ts, (hh // 2) * HEAD_DIM:(hh // 2 + 1) * HEAD_DIM] for hh in range(C_HEADS)],
        axis=-1)
    out_c = jnp.swapaxes(out_c, 0, 1).reshape(ts * bs, GROUP_WIDTH)
    c_k_new = jnp.concatenate([kc[:, ts:], from_tm(ck, ts)], axis=1).reshape(bs, WINDOW, C_KV_HEADS, HEAD_DIM)
    c_v_new = jnp.concatenate([vc[:, ts:], from_tm(cv, ts)], axis=1).reshape(bs, WINDOW, C_KV_HEADS, HEAD_DIM)

    x2 = _sample_post_call(x_tm, o, z, out_b, out_c, out_d, lw)
    new = {'a_state': a_state_new, 'a_conv': from_tm(a_tail, A_CONV - 1), 'b_conv': from_tm(b_tail, B_CONV - 1),
           'c_k': c_k_new, 'c_v': c_v_new, 'd_v': from_tm(dvn, ts)}
    return x2, new


def kernel(x_prompt, x_sample, state_delta, state_delta_conv, state_shortconv, cache_win_k, cache_win_v,
           state_ffn_conv, norm1_g, w_in, a_conv_w, a_log, a_dt_bias, a_norm_g, b_conv_w, c_sinks, d_ln_g,
           d_ln_b, d_ws, d_bias, w_out, norm2_g, ffn_w_gate, ffn_w_up, ffn_conv_w, ffn_w_down, final_norm_g):
    bp, tp, _ = x_prompt.shape
    bs, ts, _ = x_sample.shape
    depth = w_in.shape[0]
    win_buf = cache_win_k.shape[2]
    pos_p = jnp.arange(tp, dtype=jnp.int32)
    pos_s = PAST_LEN + jnp.arange(ts, dtype=jnp.int32)
    assert win_buf == WINDOW and ts == 4
    tabs_p = _rope_tables(pos_p)
    tabs_s = tuple(jnp.repeat(t, bs, axis=0) for t in _rope_tables(pos_s))
    fng = final_norm_g[None, :]
    P = CARRY_ROWS

    hp = x_prompt
    hs = jnp.swapaxes(x_sample, 0, 1).reshape(ts * bs, D_MODEL)
    outs = {k: [] for k in ('sp', 'ss', 'acp', 'acs', 'bcp', 'bcs', 'ckp', 'cks', 'cvp', 'cvs', 'fcp', 'fcs', 'dv')}
    for l in range(depth):
        lw = _layer_weights(l, norm1_g, w_in, a_conv_w, a_log, a_dt_bias, a_norm_g, b_conv_w, c_sinks, d_ln_g,
                            d_ln_b, d_ws, d_bias, w_out, norm2_g, ffn_w_gate, ffn_w_up, ffn_conv_w, ffn_w_down)
        last = l == depth - 1
        hp, s_bd, acv, bcv, ckn, cvn = _mixer_call(
            hp, lw, tabs_p, jnp.zeros((bp, P, 768), F32), jnp.zeros((bp, P, 256), F32),
            jnp.zeros((bp, 256, 256), F32))
        hp, fcv = _ffn_call(hp, lw, jnp.zeros((bp, P, D_FF), F32), fng, tile=TILE_ROWS, stride=1, final_norm=last)
        outs['sp'].append(_diag_blocks(s_bd))
        outs['acp'].append(acv[:, P - (A_CONV - 1):])
        outs['bcp'].append(bcv[:, P - (B_CONV - 1):])
        outs['ckp'].append(ckn.reshape(bp, WINDOW, C_KV_HEADS, HEAD_DIM))
        outs['cvp'].append(cvn.reshape(bp, WINDOW, C_KV_HEADS, HEAD_DIM))
        outs['fcp'].append(fcv[:, P - (FFN_CONV - 1):])
        hs, ns = _sample_mixer(hs, lw, tabs_s, d_ws[l], d_bias[l], state_delta[l], state_delta_conv[l],
                               state_shortconv[l], cache_win_k[l], cache_win_v[l], bs, ts)
        f0 = jnp.swapaxes(state_ffn_conv[l], 0, 1).reshape(1, (FFN_CONV - 1) * bs, D_FF)
        ys_tm, fcs = _ffn_call(hs[None], lw, f0, fng, tile=ts * bs, stride=bs, final_norm=last)
        hs = ys_tm[0]
        outs['ss'].append(ns['a_state'])
        outs['acs'].append(ns['a_conv'])
        outs['bcs'].append(ns['b_conv'])
        outs['cks'].append(ns['c_k'])
        outs['cvs'].append(ns['c_v'])
        outs['fcs'].append(jnp.swapaxes(fcs.reshape(FFN_CONV - 1, bs, D_FF), 0, 1))
        outs['dv'].append(ns['d_v'])
    st = {k: jnp.stack(v) for k, v in outs.items()}
    hs = jnp.swapaxes(hs.reshape(ts, bs, D_MODEL), 0, 1)
    return (hp, hs, st['sp'], st['ss'], st['acp'], st['acs'], st['bcp'], st['bcs'], st['ckp'], st['cks'],
            st['cvp'], st['cvs'], st['fcp'], st['fcs'], st['dv'])
```

```python
import functools

import jax
import jax.numpy as jnp
import numpy as np
from jax import lax
from jax.experimental import pallas as pl
from jax.experimental.pallas import tpu as pltpu

F32 = jnp.float32
BF16 = jnp.bfloat16

D_MODEL = 1024
GROUP_WIDTH = 256
HEAD_DIM = 64
A_HEADS = 4
A_CONV = 4
A_CHUNK = 64
B_CONV = 3
C_HEADS = 4
C_KV_HEADS = 2
WINDOW = 128
ROPE_DIM = 16
ROPE_THETA = 500000.0
D_GROUPS = 4
D_CHUNK = 128
D_FF = 2816
FFN_CONV = 3
EPS = 1e-6
NEG_INF = -1e30
PAST_LEN = 16384

COL_A = 0
W_A = 1152
COL_B = 1152
W_B = 768
COL_C = 1920
W_C = 512
COL_D = 2432
W_D = 512
P_PACKED = 2944

TILE_ROWS = 512
CARRY_ROWS = 8
FFN_COL_CHUNK = 1408
VMEM_LIMIT_BYTES = 56 * 1024 * 1024
MIXER_VMEM_LIMIT_BYTES = 60 * 1024 * 1024


def _dot(a, b):
    return jnp.dot(a, b, preferred_element_type=F32)


def _dot_nt(a, b):
    return lax.dot_general(a, b, (((1,), (1,)), ((), ())), preferred_element_type=F32)


def _dot_tn(a, b):
    return lax.dot_general(a, b, (((0,), (0,)), ((), ())), preferred_element_type=F32)


def _split3(x):
    hi = x.astype(BF16)
    r1 = x - hi.astype(F32)
    mid = r1.astype(BF16)
    lo = (r1 - mid.astype(F32)).astype(BF16)
    return hi, mid, lo


def _dot_lhs3(x, w01):
    hi, mid, lo = _split3(x)
    return _dot(hi, w01) + _dot(mid, w01) + _dot(lo, w01)


def _head_sumsq(x, ones_bd):
    return _dot((x * x).astype(BF16), ones_bd)


def _sigmoid(x):
    return 1.0 / (1.0 + jnp.exp(-x))


def _silu(x):
    return x * _sigmoid(x)


def _softplus(x):
    return jnp.maximum(x, 0.0) + jnp.log(1.0 + jnp.exp(-jnp.abs(x)))


def _gelu_tanh(x):
    return 0.5 * x * (1.0 + jnp.tanh(np.sqrt(2.0 / np.pi).astype(np.float32) * (x + 0.044715 * (x * x * x))))


def _rms(x, g):
    return x * lax.rsqrt(jnp.mean(x * x, axis=-1, keepdims=True) + EPS) * g


def _iota(shape, dim):
    return lax.broadcasted_iota(jnp.int32, shape, dim)


MIXER_SEQS = 2


def _mixer_kernel(sinks_ref, x_ref, n1g_ref, w_in_ref, aconv_w_ref, alog_ref, dtb_ref, anorm_ref,
                  bconv_w_ref, cos_ref, sina_ref, sinb_ref, lng_ref, lnb_ref, ws_ref, dbias_ref,
                  w_out_ref, aconv0_ref, bconv0_ref, s0_ref,
                  x_out_ref, s_out_ref, aconv_out_ref, bconv_out_ref, ck_out_ref, cv_out_ref,
                  *scratch, tile):
    shared = (sinks_ref, n1g_ref, w_in_ref, aconv_w_ref, alog_ref, dtb_ref, anorm_ref, bconv_w_ref, cos_ref,
              sina_ref, sinb_ref, lng_ref, lnb_ref, ws_ref, dbias_ref, w_out_ref)
    per_seq = (x_ref, aconv0_ref, bconv0_ref, s0_ref, x_out_ref, s_out_ref, aconv_out_ref, bconv_out_ref,
               ck_out_ref, cv_out_ref)
    n_scr = len(scratch) // MIXER_SEQS
    gens = [_mixer_seq(*shared, *(r.at[b] for r in per_seq), *scratch[b * n_scr:(b + 1) * n_scr], tile=tile)
            for b in range(MIXER_SEQS)]
    started, live = 1, list(gens[:1])
    while live:
        for gen in list(live):
            if next(gen, 'done') == 'done':
                live.remove(gen)
        if started < len(gens):
            live.append(gens[started])
            started += 1


def _mixer_seq(sinks_ref, n1g_ref, w_in_ref, aconv_w_ref, alog_ref, dtb_ref, anorm_ref, bconv_w_ref, cos_ref,
               sina_ref, sinb_ref, lng_ref, lnb_ref, ws_ref, dbias_ref, w_out_ref,
               x_ref, aconv0_ref, bconv0_ref, s0_ref, x_out_ref, s_out_ref, aconv_out_ref, bconv_out_ref,
               ck_out_ref, cv_out_ref,
               abuf, bbuf, kbuf, vbuf, q_s, k_s, v_s, gcb_s, bb_s, o_s, s_scr, proj_s, *, tile):
    T = tile
    si = pl.program_id(1)
    P = CARRY_ROWS

    @pl.when(si == 0)
    def _():
        abuf[0:P, :] = aconv0_ref[...]
        bbuf[0:P, :] = bconv0_ref[...]
        kbuf[0:WINDOW, :] = jnp.zeros((WINDOW, 128), F32)
        vbuf[0:WINDOW, :] = jnp.zeros((WINDOW, 128), F32)
        s_scr[...] = s0_ref[...]

    res = {}
    h = _rms(x_ref[...], n1g_ref[...]).astype(BF16)
    for lo, width in ((COL_A, W_A), (COL_B, W_B), (COL_C, W_C), (COL_D, W_D)):
        proj_s[:, lo:lo + width] = _dot(h, w_in_ref[:, lo:lo + width])
    yield

    small = proj_s[:, 1024:1152]
    abuf[P:P + T, :] = proj_s[:, 0:768]

    def group_b():
        bbuf[P:P + T, :] = proj_s[:, COL_B + 256:COL_B + 512] * proj_s[:, COL_B + 512:COL_B + 768]
        bx = (bbuf[P:P + T, :] * bconv_w_ref[2:3, :] + bbuf[P - 1:P - 1 + T, :] * bconv_w_ref[1:2, :]
              + bbuf[P - 2:P - 2 + T, :] * bconv_w_ref[0:1, :])
        tail_b = bbuf[T:T + P, :]
        bbuf[0:P, :] = tail_b
        bconv_out_ref[...] = tail_b
        res['out_b'] = proj_s[:, COL_B:COL_B + 256] * bx

    def group_c_rope():
        cos = cos_ref[...]
        sina = sina_ref[...]
        sinb = sinb_ref[...]
        cq = proj_s[:, COL_C:COL_C + 256]
        res['cq'] = (cq * jnp.concatenate([cos, cos], axis=1)
                     + pltpu.roll(cq, 256 - ROPE_DIM // 2, 1) * jnp.concatenate([sina, sina], axis=1)
                     + pltpu.roll(cq, ROPE_DIM // 2, 1) * jnp.concatenate([sinb, sinb], axis=1))
        ck = proj_s[:, COL_C + 256:COL_C + 384]
        ck = (ck * cos + pltpu.roll(ck, 128 - ROPE_DIM // 2, 1) * sina
              + pltpu.roll(ck, ROPE_DIM // 2, 1) * sinb)
        kbuf[WINDOW:WINDOW + T, :] = ck
        vbuf[WINDOW:WINDOW + T, :] = proj_s[:, COL_C + 384:COL_C + 512]
        ck_out_ref[...] = kbuf[T:T + WINDOW, :]
        cv_out_ref[...] = vbuf[T:T + WINDOW, :]

    def group_d_norm():
        res['du'] = _gelu_tanh(proj_s[:, COL_D:COL_D + 256])
        gv = _gelu_tanh(proj_s[:, COL_D + 256:COL_D + 512])
        mu = jnp.mean(gv, axis=-1, keepdims=True)
        xc = gv - mu
        res['dvn'] = (xc * lax.rsqrt(jnp.mean(xc * xc, axis=-1, keepdims=True) + EPS) * lng_ref[...]
                      + lnb_ref[...])
        wr = _iota((D_GROUPS * D_CHUNK, D_CHUNK), 0) & (D_CHUNK - 1)
        wc = _iota((D_GROUPS * D_CHUNK, D_CHUNK), 1)
        res['wm'] = jnp.where(wr >= wc, ws_ref[...], 0.0).astype(BF16)

    y = (abuf[P:P + T, :] * aconv_w_ref[3:4, :] + abuf[P - 1:P - 1 + T, :] * aconv_w_ref[2:3, :]
         + abuf[P - 2:P - 2 + T, :] * aconv_w_ref[1:2, :] + abuf[P - 3:P - 3 + T, :] * aconv_w_ref[0:1, :])
    tail_a = abuf[T:T + P, :]
    abuf[0:P, :] = tail_a
    aconv_out_ref[...] = tail_a
    qkv = _silu(y)
    q_raw = qkv[:, 0:256]
    k_raw = qkv[:, 256:512]
    v_s[...] = qkv[:, 512:768]

    r256 = _iota((256, 256), 0) >> 6
    c256 = _iota((256, 256), 1) >> 6
    mask_bd = r256 == c256
    ones_bd = jnp.where(mask_bd, 1.0, 0.0).astype(BF16)
    q_s[...] = q_raw * lax.rsqrt(_head_sumsq(q_raw, ones_bd) + EPS) * (HEAD_DIM ** -0.5)
    k_s[...] = k_raw * lax.rsqrt(_head_sumsq(k_raw, ones_bd) + EPS)
    yield

    g_log = -jnp.exp(alog_ref[...]) * _softplus(small + dtb_ref[...])
    beta = _sigmoid(small)
    gbeta = jnp.where(_iota((T, 128), 1) < A_HEADS, g_log, beta)
    expand = jnp.where(_iota((128, 512), 0) == (_iota((128, 512), 1) >> 6), 1.0, 0.0).astype(BF16)
    gbb = _dot_lhs3(gbeta, expand)
    bb_s[...] = gbb[:, 256:512]

    def chunk_cumsum(xv):
        row_in_chunk = _iota(xv.shape, 0) & (A_CHUNK - 1)
        step = 1
        while step < A_CHUNK:
            xv = xv + jnp.where(row_in_chunk >= step, pltpu.roll(xv, step, 0), 0.0)
            step *= 2
        return xv

    gcb_s[...] = chunk_cumsum(gbb[:, 0:256])
    gct = chunk_cumsum(g_log).T[0:8, :]
    low_half = (_iota((1, T), 1) & A_CHUNK) == 0
    gct_r = pltpu.roll(gct, A_CHUNK, 1)
    gct_l = pltpu.roll(gct, T - A_CHUNK, 1)
    even_rows = [jnp.where(low_half, gct[a:a + 1, :], gct_r[a + 1:a + 2, :]) for a in (0, 2)]
    odd_rows = [jnp.where(low_half, gct_l[a:a + 1, :], gct[a + 1:a + 2, :]) for a in (0, 2)]
    yield

    ri = _iota((A_CHUNK, 256), 0)
    ci = _iota((A_CHUNK, 256), 1) & (A_CHUNK - 1)
    causal_t = ri >= ci
    strict_t = ri > ci
    eye_t = jnp.where(ri == ci, 1.0, 0.0)

    def bd16(x16):
        return jnp.concatenate([x16, x16, x16, x16], axis=0) * ones_bd

    n_chunks = T // A_CHUNK
    t_inv, pw, qk, qdec, e_tail, vb16, kbe16 = [], [], [], [], [], [], []
    for c in range(n_chunks):
        rows = slice(c * A_CHUNK, (c + 1) * A_CHUNK)
        blk = slice((c // 2) * 128, (c // 2) * 128 + 128)
        src = even_rows if c % 2 == 0 else odd_rows
        gc_row = jnp.concatenate([src[0][:, blk], src[1][:, blk]], axis=1)
        qc = q_s[rows, :]
        kc = k_s[rows, :]
        bbc = bb_s[rows, :]
        gcb = gcb_s[rows, :]
        decay = jnp.where(causal_t, jnp.exp(jnp.where(causal_t, gcb - gc_row, 0.0)), 0.0)
        eg = jnp.exp(gcb)
        kb = kc * bbc
        aq = _dot_nt(jnp.concatenate([kb, qc], axis=0).astype(BF16), bd16(kc.astype(BF16)))
        a_mat = jnp.where(strict_t, aq[0:A_CHUNK] * decay, 0.0)
        qk.append(aq[A_CHUNK:2 * A_CHUNK] * decay)
        t_inv.append(eye_t - a_mat)
        pw.append(a_mat)
        qdec.append(qc * eg)
        e_tail.append(jnp.exp(gcb[A_CHUNK - 1:A_CHUNK, :] - gcb))
        vb16.append((v_s[rows, :] * bbc).astype(BF16))
        kbe16.append((kb * eg).astype(BF16))
    yield
    u, w = [], []
    state = {'s': s_scr[...]}

    def level_first():
        for c in range(n_chunks):
            p16 = pw[c].astype(BF16)
            pw[c] = _dot(p16, bd16(p16))

    def level_mid():
        for c in range(n_chunks):
            p16 = pw[c].astype(BF16)
            res = _dot(jnp.concatenate([p16, t_inv[c].astype(BF16)], axis=0), bd16(p16))
            pw[c] = res[0:A_CHUNK]
            t_inv[c] = t_inv[c] + res[A_CHUNK:2 * A_CHUNK]

    def level_last():
        for c in range(n_chunks):
            t_c = t_inv[c] + _dot(t_inv[c].astype(BF16), bd16(pw[c].astype(BF16)))
            uw = _dot(t_c.astype(BF16), jnp.concatenate([bd16(vb16[c]), bd16(kbe16[c])], axis=1))
            u.append(uw[:, 0:256])
            w.append(uw[:, 256:512])

    def scan_step(c):
        rows = slice(c * A_CHUNK, (c + 1) * A_CHUNK)
        s_bd = state['s']
        wq = _dot(jnp.concatenate([w[c], qdec[c]], axis=0).astype(BF16), s_bd.astype(BF16))
        v_new = u[c] - wq[0:A_CHUNK]
        o_s[rows, :] = wq[A_CHUNK:2 * A_CHUNK] + _dot(qk[c].astype(BF16), bd16(v_new.astype(BF16)))
        kv = _dot_tn(k_s[rows, :].astype(BF16), (v_new * e_tail[c]).astype(BF16))
        g_last = gcb_s[(c + 1) * A_CHUNK - 1:(c + 1) * A_CHUNK, :]
        state['s'] = s_bd * jnp.exp(g_last) + jnp.where(mask_bd, kv, 0.0)

    chain = [level_first] + [level_mid] * 4 + [level_last] + [functools.partial(scan_step, c) for c in range(n_chunks)]

    lane128 = _iota((2 * WINDOW, 128), 1)
    low = _iota((WINDOW, 128), 1) < HEAD_DIM
    qrow = _iota((2 * WINDOW, 2 * WINDOW), 0) & (WINDOW - 1)
    kcol = _iota((2 * WINDOW, 2 * WINDOW), 1)
    band = (kcol > qrow) & (kcol <= qrow + WINDOW)
    top_half = _iota((2 * WINDOW, 1), 0) < WINDOW
    out_c_blocks = []

    def group_c_block(n):
        first_key = si * T + (n - 1) * WINDOW
        valid = band & (kcol + first_key >= 0)
        kwin = kbuf[n * WINDOW:(n + 2) * WINDOW, :]
        vwin = vbuf[n * WINDOW:(n + 2) * WINDOW, :]
        k_sw = pltpu.roll(kwin, HEAD_DIM, 1)
        v_sw = pltpu.roll(vwin, HEAD_DIM, 1)
        pair_out = []
        for g in range(C_KV_HEADS):
            own = (lane128 < HEAD_DIM) if g == 0 else (lane128 >= HEAD_DIM)
            k_dup = jnp.where(own, kwin, k_sw).astype(BF16)
            v_dup = jnp.where(own, vwin, v_sw).astype(BF16)
            qp = res['cq'][n * WINDOW:(n + 1) * WINDOW, g * 128:(g + 1) * 128]
            q_st = jnp.concatenate([jnp.where(low, qp, 0.0), jnp.where(low, 0.0, qp)], axis=0).astype(BF16)
            s = _dot_nt(q_st, k_dup) * (HEAD_DIM ** -0.5)
            s = jnp.where(valid, s, NEG_INF)
            sink = jnp.where(top_half, sinks_ref[2 * g], sinks_ref[2 * g + 1])
            m = jnp.maximum(jnp.max(s, axis=-1, keepdims=True), sink)
            p = jnp.exp(s - m)
            denom = jnp.sum(p, axis=-1, keepdims=True) + jnp.exp(sink - m)
            o2 = _dot((p / denom).astype(BF16), v_dup)
            pair_out.append(jnp.where(low, o2[0:WINDOW], o2[WINDOW:2 * WINDOW]))
        out_c_blocks.append(jnp.concatenate(pair_out, axis=1))
        if n == T // WINDOW - 1:
            kbuf[0:WINDOW, :] = kbuf[T:T + WINDOW, :]
            vbuf[0:WINDOW, :] = vbuf[T:T + WINDOW, :]

    lane_grp = _iota((D_CHUNK, 256), 1) >> 6
    out_d_blocks = []

    def group_d_block(n):
        mx = _dot(res['wm'], res['dvn'][n * D_CHUNK:(n + 1) * D_CHUNK, :].astype(BF16))
        mixed = dbias_ref[...]
        for grp in range(D_GROUPS):
            mixed = mixed + jnp.where(lane_grp == grp, mx[grp * D_CHUNK:(grp + 1) * D_CHUNK, :], 0.0)
        out_d_blocks.append(res['du'][n * D_CHUNK:(n + 1) * D_CHUNK, :] * mixed)

    c_blocks = [functools.partial(group_c_block, n) for n in range(T // WINDOW)]
    d_blocks = [functools.partial(group_d_block, n) for n in range(T // D_CHUNK)]
    fill = [group_b, group_c_rope, group_d_norm] + c_blocks + d_blocks
    for i in range(max(len(chain), len(fill))):
        if i < len(chain):
            chain[i]()
        if i < len(fill):
            fill[i]()
        yield
    out_b = res['out_b']
    out_c = jnp.concatenate(out_c_blocks, axis=0)
    out_d = jnp.concatenate(out_d_blocks, axis=0)

    s_scr[...] = state['s']
    s_out_ref[...] = state['s']
    o = o_s[...]
    out_a = (o * lax.rsqrt(_head_sumsq(o, ones_bd) * (1.0 / HEAD_DIM) + EPS) * anorm_ref[...]
             * _silu(proj_s[:, 768:1024]))

    cat = jnp.concatenate([out_a, out_b, out_c, out_d], axis=1).astype(BF16)
    x_out_ref[...] = x_ref[...] + _dot(cat, w_out_ref[...])


def _mixer_call(x, lw, tabs, aconv0, bconv0, s0):
    bsz, seq, _ = x.shape
    T = TILE_ROWS
    ns = seq // T
    NB = MIXER_SEQS
    P = CARRY_ROWS
    full = lambda shape: pl.BlockSpec(shape, lambda b, s, *_: (0,) * len(shape), pipeline_mode=pl.Buffered(1))
    per_b = lambda shape: pl.BlockSpec((NB,) + shape, lambda b, s, *_: (b,) + (0,) * len(shape))
    tab = pl.BlockSpec((T, 128), lambda b, s, *_: (s, 0))
    seq_scratch = [pltpu.VMEM(shape, F32) for shape in (
        (T + P, 768), (T + P, 256), (T + WINDOW, 128), (T + WINDOW, 128), (T, 256), (T, 256), (T, 256),
        (T, 256), (T, 256), (T, 256), (256, 256), (T, P_PACKED))]
    grid_spec = pltpu.PrefetchScalarGridSpec(
        num_scalar_prefetch=1,
        grid=(bsz // NB, ns),
        in_specs=[
            pl.BlockSpec((NB, T, D_MODEL), lambda b, s, *_: (b, s, 0)),
            full((1, D_MODEL)), full((D_MODEL, P_PACKED)), full((A_CONV, 768)), full((1, 128)), full((1, 128)),
            full((1, 256)), full((B_CONV, 256)), tab, tab, tab, full((1, 256)), full((1, 256)),
            full((D_GROUPS * D_CHUNK, D_CHUNK)), full((D_CHUNK, 256)), full((D_MODEL, D_MODEL)),
            per_b((P, 768)), per_b((P, 256)), per_b((256, 256)),
        ],
        out_specs=[
            pl.BlockSpec((NB, T, D_MODEL), lambda b, s, *_: (b, s, 0)),
            per_b((256, 256)), per_b((P, 768)), per_b((P, 256)), per_b((WINDOW, 128)), per_b((WINDOW, 128)),
        ],
        scratch_shapes=seq_scratch * NB,
    )
    out_shape = [
        jax.ShapeDtypeStruct((bsz, seq, D_MODEL), F32),
        jax.ShapeDtypeStruct((bsz, 256, 256), F32),
        jax.ShapeDtypeStruct((bsz, P, 768), F32),
        jax.ShapeDtypeStruct((bsz, P, 256), F32),
        jax.ShapeDtypeStruct((bsz, WINDOW, 128), F32),
        jax.ShapeDtypeStruct((bsz, WINDOW, 128), F32),
    ]
    return pl.pallas_call(
        functools.partial(_mixer_kernel, tile=T),
        grid_spec=grid_spec,
        out_shape=out_shape,
        compiler_params=pltpu.CompilerParams(
            dimension_semantics=("arbitrary", "arbitrary"), vmem_limit_bytes=MIXER_VMEM_LIMIT_BYTES),
        name="mixer",
    )(lw['c_sinks'], x, lw['norm1_g'], lw['w_in'], lw['a_conv_w'], lw['a_log'], lw['a_dt_bias'],
      lw['a_norm_g'], lw['b_conv_w'], tabs[0], tabs[1], tabs[2], lw['d_ln_g'], lw['d_ln_b'],
      lw['d_ws'], lw['d_bias'], lw['w_out'], aconv0, bconv0, s0)


def _ffn_kernel(x_ref, n2g_ref, wg_ref, wu_ref, cw_ref, wd_ref, fconv0_ref, fng_ref,
                x_out_ref, fconv_out_ref, gbuf, *, tile, stride, final_norm):
    T = tile
    P = (FFN_CONV - 1) * stride if stride > 1 else CARRY_ROWS
    si = pl.program_id(1)

    @pl.when(si == 0)
    def _():
        gbuf[0:P, :] = fconv0_ref[0]

    x = x_ref[0]
    h = _rms(x, n2g_ref[...]).astype(BF16)
    acc = x
    for c0 in range(0, D_FF, FFN_COL_CHUNK):
        cols = slice(c0, c0 + FFN_COL_CHUNK)
        gbuf[P:P + T, cols] = _dot(h, wg_ref[:, cols])
        up = _dot(h, wu_ref[:, cols])
        gate = (gbuf[P:P + T, cols] * cw_ref[2:3, cols] + gbuf[P - stride:P - stride + T, cols] * cw_ref[1:2, cols]
                + gbuf[P - 2 * stride:P - 2 * stride + T, cols] * cw_ref[0:1, cols])
        acc = acc + _dot((_silu(gate) * up).astype(BF16), wd_ref[cols, :])
    tail = gbuf[T:T + P, :]
    gbuf[0:P, :] = tail
    fconv_out_ref[0] = tail
    if final_norm:
        acc = _rms(acc, fng_ref[...])
    x_out_ref[0] = acc


def _ffn_call(x, lw, fconv0, final_g, *, tile, stride, final_norm):
    bsz, seq, _ = x.shape
    T = tile
    P = (FFN_CONV - 1) * stride if stride > 1 else CARRY_ROWS
    full = lambda shape: pl.BlockSpec(shape, lambda b, s: (0,) * len(shape), pipeline_mode=pl.Buffered(1))
    per_b = lambda shape: pl.BlockSpec((1,) + shape, lambda b, s: (b,) + (0,) * len(shape))
    return pl.pallas_call(
        functools.partial(_ffn_kernel, tile=T, stride=stride, final_norm=final_norm),
        grid=(bsz, seq // T),
        in_specs=[
            pl.BlockSpec((1, T, D_MODEL), lambda b, s: (b, s, 0)),
            full((1, D_MODEL)), full((D_MODEL, D_FF)), full((D_MODEL, D_FF)), full((FFN_CONV, D_FF)),
            full((D_FF, D_MODEL)), per_b((P, D_FF)), full((1, D_MODEL)),
        ],
        out_specs=[pl.BlockSpec((1, T, D_MODEL), lambda b, s: (b, s, 0)), per_b((P, D_FF))],
        out_shape=[jax.ShapeDtypeStruct((bsz, seq, D_MODEL), F32), jax.ShapeDtypeStruct((bsz, P, D_FF), F32)],
        scratch_shapes=[pltpu.VMEM((T + P, D_FF), F32)],
        compiler_params=pltpu.CompilerParams(
            dimension_semantics=("arbitrary", "arbitrary"), vmem_limit_bytes=VMEM_LIMIT_BYTES),
        name="ffn",
    )(x, lw['norm2_g'], lw['ffn_w_gate'], lw['ffn_w_up'], lw['ffn_conv_w'], lw['ffn_w_down'], fconv0, final_g)


def _pack_w_in(w):
    pad = jnp.zeros((D_MODEL, 128 - 2 * A_HEADS), w.dtype)
    small = jnp.concatenate([w[:, 1028:1032], w[:, 1024:1028], pad], axis=1)
    return jnp.concatenate([w[:, 0:1024], small, w[:, 1032:]], axis=1).astype(BF16)


def _pad_lanes(v, width=128):
    return jnp.concatenate([v, jnp.zeros((width - v.shape[0],), v.dtype)])[None, :]


def _layer_weights(l, norm1_g, w_in, a_conv_w, a_log, a_dt_bias, a_norm_g, b_conv_w, c_sinks, d_ln_g, d_ln_b,
                   d_ws, d_bias, w_out, norm2_g, ffn_w_gate, ffn_w_up, ffn_conv_w, ffn_w_down):
    bias_tab = jnp.broadcast_to(d_bias[l].T[:, :, None], (D_CHUNK, D_GROUPS, HEAD_DIM)).reshape(D_CHUNK, 256)
    return {
        'norm1_g': norm1_g[l][None, :], 'w_in': _pack_w_in(w_in[l]), 'a_conv_w': a_conv_w[l],
        'a_log': _pad_lanes(a_log[l]), 'a_dt_bias': _pad_lanes(a_dt_bias[l]),
        'a_norm_g': jnp.tile(a_norm_g[l], A_HEADS)[None, :], 'b_conv_w': b_conv_w[l], 'c_sinks': c_sinks[l],
        'd_ln_g': d_ln_g[l][None, :], 'd_ln_b': d_ln_b[l][None, :],
        'd_ws': d_ws[l].reshape(D_GROUPS * D_CHUNK, D_CHUNK), 'd_bias': bias_tab,
        'w_out': w_out[l].astype(BF16), 'norm2_g': norm2_g[l][None, :],
        'ffn_w_gate': ffn_w_gate[l].astype(BF16), 'ffn_w_up': ffn_w_up[l].astype(BF16),
        'ffn_conv_w': ffn_conv_w[l], 'ffn_w_down': ffn_w_down[l].astype(BF16),
    }


def _rope_tables(pos):
    half = ROPE_DIM // 2
    inv = jnp.power(ROPE_THETA, -jnp.arange(half, dtype=F32) * (2.0 / ROPE_DIM))
    ang = pos.astype(F32)[:, None] * inv[None, :]
    cos, sin = jnp.cos(ang), jnp.sin(ang)
    n = pos.shape[0]
    rest = HEAD_DIM - ROPE_DIM
    cos_h = jnp.concatenate([cos, cos, jnp.ones((n, rest), F32)], axis=1)
    sina_h = jnp.concatenate([-sin, jnp.zeros((n, half + rest), F32)], axis=1)
    sinb_h = jnp.concatenate([jnp.zeros((n, half), F32), sin, jnp.zeros((n, rest), F32)], axis=1)
    return tuple(jnp.concatenate([t, t], axis=1) for t in (cos_h, sina_h, sinb_h))


def _diag_blocks(s_bd):
    return jnp.stack([s_bd[:, HEAD_DIM * i:HEAD_DIM * (i + 1), HEAD_DIM * i:HEAD_DIM * (i + 1)]
                      for i in range(A_HEADS)], axis=1)


def _sample_pre_kernel(x_ref, n1g_ref, w_in_ref, aconv_w_ref, alog_ref, dtb_ref, bconv_w_ref, cos_ref, sina_ref,
                       sinb_ref, lng_ref, lnb_ref, wtab_ref, btab_ref, aconv0_ref, bconv0_ref,
                       q_ref, k_ref, v_ref, g_ref, beta_ref, z_ref, outb_ref, cq_ref, ck_ref, cv_ref, outd_ref,
                       dvn_ref, aconv_out_ref, bconv_out_ref, abuf, bbuf, *, rows, stride):
    R, S = rows, stride
    steps = R // S
    pa_rows = (A_CONV - 1) * S
    pb_rows = (B_CONV - 1) * S
    x = x_ref[...]
    h = _rms(x, n1g_ref[...]).astype(BF16)

    pa = _dot(h, w_in_ref[:, COL_A:COL_A + W_A])
    z_ref[...] = pa[:, 768:1024]
    small = pa[:, 1024:1152]
    abuf[0:pa_rows, :] = aconv0_ref[...]
    abuf[pa_rows:pa_rows + R, :] = pa[:, 0:768]
    y = abuf[0:R, :] * aconv_w_ref[0:1, :]
    for j in range(1, A_CONV):
        y = y + abuf[j * S:j * S + R, :] * aconv_w_ref[j:j + 1, :]
    aconv_out_ref[...] = abuf[R:R + pa_rows, :]
    qkv = _silu(y)
    q_raw = qkv[:, 0:256]
    k_raw = qkv[:, 256:512]
    v_ref[...] = qkv[:, 512:768]
    mask_bd = (_iota((256, 256), 0) >> 6) == (_iota((256, 256), 1) >> 6)
    ones_bd = jnp.where(mask_bd, 1.0, 0.0).astype(BF16)
    q_ref[...] = q_raw * lax.rsqrt(_dot_lhs3(q_raw * q_raw, ones_bd) + EPS) * (HEAD_DIM ** -0.5)
    k_ref[...] = k_raw * lax.rsqrt(_dot_lhs3(k_raw * k_raw, ones_bd) + EPS)
    g_ref[...] = -jnp.exp(alog_ref[...]) * _softplus(small + dtb_ref[...])
    beta_ref[...] = _sigmoid(small)

    pb = _dot(h, w_in_ref[:, COL_B:COL_B + W_B])
    bbuf[0:pb_rows, :] = bconv0_ref[...]
    bbuf[pb_rows:pb_rows + R, :] = pb[:, 256:512] * pb[:, 512:768]
    bx = bbuf[0:R, :] * bconv_w_ref[0:1, :]
    for j in range(1, B_CONV):
        bx = bx + bbuf[j * S:j * S + R, :] * bconv_w_ref[j:j + 1, :]
    bconv_out_ref[...] = bbuf[R:R + pb_rows, :]
    outb_ref[...] = pb[:, 0:256] * bx

    pc = _dot(h, w_in_ref[:, COL_C:COL_C + W_C])
    cos = cos_ref[...]
    sina = sina_ref[...]
    sinb = sinb_ref[...]
    cq = pc[:, 0:256]
    cq_ref[...] = (cq * jnp.concatenate([cos, cos], axis=1)
                   + pltpu.roll(cq, 256 - ROPE_DIM // 2, 1) * jnp.concatenate([sina, sina], axis=1)
                   + pltpu.roll(cq, ROPE_DIM // 2, 1) * jnp.concatenate([sinb, sinb], axis=1))
    ck = pc[:, 256:384]
    ck_ref[...] = (ck * cos + pltpu.roll(ck, 128 - ROPE_DIM // 2, 1) * sina
                   + pltpu.roll(ck, ROPE_DIM // 2, 1) * sinb)
    cv_ref[...] = pc[:, 384:512]

    pd = _dot(h, w_in_ref[:, COL_D:COL_D + W_D])
    du = _gelu_tanh(pd[:, 0:256])
    gv = _gelu_tanh(pd[:, 256:512])
    mu = jnp.mean(gv, axis=-1, keepdims=True)
    xc = gv - mu
    dvn = xc * lax.rsqrt(jnp.mean(xc * xc, axis=-1, keepdims=True) + EPS) * lng_ref[...] + lnb_ref[...]
    dvn_ref[...] = dvn
    for t in range(steps):
        mixed = btab_ref[t:t + 1, :]
        for s in range(t + 1):
            mixed = mixed + wtab_ref[t * steps + s:t * steps + s + 1, :] * dvn[s * S:(s + 1) * S, :]
        outd_ref[t * S:(t + 1) * S, :] = du[t * S:(t + 1) * S, :] * mixed


def _sample_pre_call(x, lw, tabs, wtab, btab, aconv0, bconv0, stride):
    R = x.shape[0]
    f = lambda *shape: jax.ShapeDtypeStruct(shape, F32)
    out_shape = [f(R, 256), f(R, 256), f(R, 256), f(R, 128), f(R, 128), f(R, 256), f(R, 256), f(R, 256),
                 f(R, 128), f(R, 128), f(R, 256), f(R, 256), f((A_CONV - 1) * stride, 768),
                 f((B_CONV - 1) * stride, 256)]
    return pl.pallas_call(
        functools.partial(_sample_pre_kernel, rows=R, stride=stride),
        out_shape=out_shape,
        scratch_shapes=[pltpu.VMEM((R + (A_CONV - 1) * stride, 768), F32),
                        pltpu.VMEM((R + (B_CONV - 1) * stride, 256), F32)],
        compiler_params=pltpu.CompilerParams(vmem_limit_bytes=VMEM_LIMIT_BYTES),
        name="sample_pre",
    )(x, lw['norm1_g'], lw['w_in'], lw['a_conv_w'], lw['a_log'], lw['a_dt_bias'], lw['b_conv_w'],
      tabs[0], tabs[1], tabs[2], lw['d_ln_g'], lw['d_ln_b'], wtab, btab, aconv0, bconv0)


def _sample_delta_kernel(g_ref, beta_ref, q_ref, k_ref, v_ref, s_ref, o_ref, snew_ref,
                         qt_s, kt_s, vt_s, gt_s, bt_s, ot_s, *, steps, nb):
    h = pl.program_id(0)
    n_i = HEAD_DIM
    zeros = jnp.zeros((HEAD_DIM, nb), F32)

    @pl.when(h == 0)
    def _():
        for t in range(steps):
            rows = slice(t * nb, (t + 1) * nb)
            qt_s[t] = q_ref[rows, :].T
            kt_s[t] = k_ref[rows, :].T
            vt_s[t] = v_ref[rows, :].T
            gt_s[t] = g_ref[rows, :].T
            bt_s[t] = beta_ref[rows, :].T

    base = pl.multiple_of(h * HEAD_DIM, HEAD_DIM)
    head_rows = pl.ds(base, HEAD_DIM)

    def rows_of(i):
        return pl.ds(pl.multiple_of(i * HEAD_DIM, HEAD_DIM), HEAD_DIM)

    def decay(t):
        return jnp.exp(gt_s[t, pl.ds(h, 1), :])

    dec0 = decay(0)

    def first_pass(i, acc):
        return acc + kt_s[0, pl.ds(base + i, 1), :] * (s_ref[0, rows_of(i), :] * dec0)

    ks = lax.fori_loop(0, n_i, first_pass, zeros)
    for t in range(steps):
        dec = decay(t)
        v_new = bt_s[t, pl.ds(A_HEADS + h, 1), :] * (vt_s[t, head_rows, :] - ks)
        src = s_ref if t == 0 else snew_ref
        dec_next = decay(t + 1) if t + 1 < steps else None

        def update(i, carry, t=t, dec=dec, v_new=v_new, src=src, dec_next=dec_next):
            o_acc, ks_acc = carry
            blk = src[0, rows_of(i), :] * dec + kt_s[t, pl.ds(base + i, 1), :] * v_new
            snew_ref[0, rows_of(i), :] = blk
            o_acc = o_acc + qt_s[t, pl.ds(base + i, 1), :] * blk
            if dec_next is not None:
                ks_acc = ks_acc + kt_s[t + 1, pl.ds(base + i, 1), :] * (blk * dec_next)
            return o_acc, ks_acc

        o_acc, ks = lax.fori_loop(0, n_i, update, (zeros, zeros))
        ot_s[t, head_rows, :] = o_acc

    @pl.when(h == A_HEADS - 1)
    def _():
        for t in range(steps):
            o_ref[t * nb:(t + 1) * nb, :] = ot_s[t].T


def _sample_delta_call(g, beta, q, k, v, s, steps):
    nh, _, nb = s.shape
    rows = steps * nb
    whole = lambda width: pl.BlockSpec((rows, width), lambda h: (0, 0))
    st = pl.BlockSpec((1, HEAD_DIM * HEAD_DIM, nb), lambda h: (h, 0, 0))
    wide = pltpu.VMEM((steps, GROUP_WIDTH, nb), F32)
    narrow = pltpu.VMEM((steps, 128, nb), F32)
    return pl.pallas_call(
        functools.partial(_sample_delta_kernel, steps=steps, nb=nb),
        grid=(nh,),
        in_specs=[whole(128), whole(128), whole(GROUP_WIDTH), whole(GROUP_WIDTH), whole(GROUP_WIDTH), st],
        out_specs=[whole(GROUP_WIDTH), st],
        out_shape=[jax.ShapeDtypeStruct((rows, GROUP_WIDTH), F32),
                   jax.ShapeDtypeStruct((nh, HEAD_DIM * HEAD_DIM, nb), F32)],
        scratch_shapes=[wide, wide, wide, narrow, narrow, wide],
        compiler_params=pltpu.CompilerParams(dimension_semantics=("arbitrary",),
                                             vmem_limit_bytes=VMEM_LIMIT_BYTES),
        name="sample_delta",
    )(g, beta, q, k, v, s)


SAMPLE_ATTN_BLOCK = 8
NEW_KEY_ROWS = 8


def _sample_attn_kernel(sinks_ref, qm_ref, kc_ref, kn_ref, vc_ref, vn_ref, o_ref, *, steps):
    nq = C_HEADS * steps
    nk = WINDOW + NEW_KEY_ROWS
    BB = SAMPLE_ATTN_BLOCK
    row = _iota((BB * nq, nk), 0)
    col = _iota((BB * nq, nk), 1)
    t_q = row & (steps - 1)
    valid = ((col < WINDOW) & (col > t_q)) | ((col >= WINDOW) & (col - WINDOW <= t_q))
    head = (_iota((BB * nq, 1), 0) >> 2) & (C_HEADS - 1)
    sink = jnp.where(head == 0, sinks_ref[0],
                     jnp.where(head == 1, sinks_ref[1], jnp.where(head == 2, sinks_ref[2], sinks_ref[3])))
    scores = [_dot_nt(qm_ref[b].astype(BF16), jnp.concatenate([kc_ref[b], kn_ref[b]], axis=0).astype(BF16))
              for b in range(BB)]
    s = jnp.where(valid, jnp.concatenate(scores, axis=0) * (HEAD_DIM ** -0.5), NEG_INF)
    m = jnp.maximum(jnp.max(s, axis=-1, keepdims=True), sink)
    p = jnp.exp(s - m)
    denom = jnp.sum(p, axis=-1, keepdims=True) + jnp.exp(sink - m)
    p16 = (p / denom).astype(BF16)
    for b in range(BB):
        v_all = jnp.concatenate([vc_ref[b], vn_ref[b]], axis=0).astype(BF16)
        o_ref[b] = _dot(p16[b * nq:(b + 1) * nq, :], v_all)


def _sample_attn_call(sinks, qm, kc, kn, vc, vn, steps):
    bs, nq, _ = qm.shape
    BB = SAMPLE_ATTN_BLOCK
    blk = lambda r: pl.BlockSpec((BB, r, 128), lambda i, *_: (i, 0, 0))
    grid_spec = pltpu.PrefetchScalarGridSpec(
        num_scalar_prefetch=1, grid=(bs // BB,),
        in_specs=[blk(nq), blk(WINDOW), blk(NEW_KEY_ROWS), blk(WINDOW), blk(NEW_KEY_ROWS)],
        out_specs=blk(nq))
    return pl.pallas_call(
        functools.partial(_sample_attn_kernel, steps=steps),
        grid_spec=grid_spec,
        out_shape=jax.ShapeDtypeStruct((bs, nq, 128), F32),
        compiler_params=pltpu.CompilerParams(dimension_semantics=("arbitrary",)),
        name="sample_attn",
    )(sinks, qm, kc, kn, vc, vn)


def _sample_post_kernel(x_ref, o_ref, z_ref, outb_ref, outc_ref, outd_ref, anorm_ref, w_out_ref, x_out_ref):
    mask_bd = (_iota((256, 256), 0) >> 6) == (_iota((256, 256), 1) >> 6)
    ones_bd = jnp.where(mask_bd, 1.0, 0.0).astype(BF16)
    o = o_ref[...]
    out_a = (o * lax.rsqrt(_dot_lhs3(o * o, ones_bd) * (1.0 / HEAD_DIM) + EPS) * anorm_ref[...]
             * _silu(z_ref[...]))
    cat = jnp.concatenate([out_a, outb_ref[...], outc_ref[...], outd_ref[...]], axis=1).astype(BF16)
    x_out_ref[...] = x_ref[...] + _dot(cat, w_out_ref[...])


def _sample_post_call(x, o, z, out_b, out_c, out_d, lw):
    return pl.pallas_call(
        _sample_post_kernel,
        out_shape=jax.ShapeDtypeStruct(x.shape, F32),
        compiler_params=pltpu.CompilerParams(vmem_limit_bytes=VMEM_LIMIT_BYTES),
        name="sample_post",
    )(x, o, z, out_b, out_c, out_d, lw['a_norm_g'], lw['w_out'])


def _sample_mixer(x_tm, lw, tabs, d_ws_l, d_bias_l, a_state, a_conv, b_conv, c_k, c_v, bs, ts):
    to_tm = lambda a: jnp.swapaxes(a, 0, 1).reshape(a.shape[1] * bs, a.shape[2])
    from_tm = lambda a, n: jnp.swapaxes(a.reshape(n, bs, a.shape[-1]), 0, 1)
    wtab = jnp.repeat(d_ws_l[:, :ts, :ts].transpose(1, 2, 0).reshape(ts * ts, D_GROUPS), HEAD_DIM, axis=1)
    btab = jnp.repeat(d_bias_l[:, :ts].T, HEAD_DIM, axis=1)
    (q, k, v, g, beta, z, out_b, cq, ck, cv, out_d, dvn, a_tail, b_tail) = _sample_pre_call(
        x_tm, lw, tabs, wtab, btab, to_tm(a_conv), to_tm(b_conv), bs)

    s_t = a_state.reshape(bs, A_HEADS, HEAD_DIM * HEAD_DIM).transpose(1, 2, 0)
    o, s_new_t = _sample_delta_call(g, beta, q, k, v, s_t, ts)
    a_state_new = s_new_t.transpose(2, 0, 1).reshape(bs, A_HEADS, HEAD_DIM, HEAD_DIM)

    cq4 = cq.reshape(ts, bs, C_HEADS, HEAD_DIM).transpose(1, 2, 0, 3)
    zq = jnp.zeros_like(cq4[:, 0])
    qm = jnp.concatenate(
        [jnp.concatenate([cq4[:, hh], zq] if hh // 2 == 0 else [zq, cq4[:, hh]], axis=-1) for hh in range(C_HEADS)],
        axis=1)
    pad_new = lambda a: jnp.concatenate([from_tm(a, ts), jnp.zeros((bs, NEW_KEY_ROWS - ts, 128), F32)], axis=1)
    kc = c_k.reshape(bs, WINDOW, 128)
    vc = c_v.reshape(bs, WINDOW, 128)
    o_att = _sample_attn_call(lw['c_sinks'], qm, kc, pad_new(ck), vc, pad_new(cv), ts)
    out_c = jnp.concatenate(
        [o_att[:, hh * ts:(hh + 1) * ts, (hh // 2) * HEAD_DIM:(hh // 2 + 1) * HEAD_DIM] for hh in range(C_HEADS)],
        axis=-1)
    out_c = jnp.swapaxes(out_c, 0, 1).reshape(ts * bs, GROUP_WIDTH)
    c_k_new = jnp.concatenate([kc[:, ts:], from_tm(ck, ts)], axis=1).reshape(bs, WINDOW, C_KV_HEADS, HEAD_DIM)
    c_v_new = jnp.concatenate([vc[:, ts:], from_tm(cv, ts)], axis=1).reshape(bs, WINDOW, C_KV_HEADS, HEAD_DIM)

    x2 = _sample_post_call(x_tm, o, z, out_b, out_c, out_d, lw)
    new = {'a_state': a_state_new, 'a_conv': from_tm(a_tail, A_CONV - 1), 'b_conv': from_tm(b_tail, B_CONV - 1),
           'c_k': c_k_new, 'c_v': c_v_new, 'd_v': from_tm(dvn, ts)}
    return x2, new


def kernel(x_prompt, x_sample, state_delta, state_delta_conv, state_shortconv, cache_win_k, cache_win_v,
           state_ffn_conv, norm1_g, w_in, a_conv_w, a_log, a_dt_bias, a_norm_g, b_conv_w, c_sinks, d_ln_g,
           d_ln_b, d_ws, d_bias, w_out, norm2_g, ffn_w_gate, ffn_w_up, ffn_conv_w, ffn_w_down, final_norm_g):
    bp, tp, _ = x_prompt.shape
    bs, ts, _ = x_sample.shape
    depth = w_in.shape[0]
    win_buf = cache_win_k.shape[2]
    pos_p = jnp.arange(tp, dtype=jnp.int32)
    pos_s = PAST_LEN + jnp.arange(ts, dtype=jnp.int32)
    assert win_buf == WINDOW and ts == 4
    tabs_p = _rope_tables(pos_p)
    tabs_s = tuple(jnp.repeat(t, bs, axis=0) for t in _rope_tables(pos_s))
    fng = final_norm_g[None, :]
    P = CARRY_ROWS

    hp = x_prompt
    hs = jnp.swapaxes(x_sample, 0, 1).reshape(ts * bs, D_MODEL)
    outs = {k: [] for k in ('sp', 'ss', 'acp', 'acs', 'bcp', 'bcs', 'ckp', 'cks', 'cvp', 'cvs', 'fcp', 'fcs', 'dv')}
    for l in range(depth):
        lw = _layer_weights(l, norm1_g, w_in, a_conv_w, a_log, a_dt_bias, a_norm_g, b_conv_w, c_sinks, d_ln_g,
                            d_ln_b, d_ws, d_bias, w_out, norm2_g, ffn_w_gate, ffn_w_up, ffn_conv_w, ffn_w_down)
        last = l == depth - 1
        hp, s_bd, acv, bcv, ckn, cvn = _mixer_call(
            hp, lw, tabs_p, jnp.zeros((bp, P, 768), F32), jnp.zeros((bp, P, 256), F32),
            jnp.zeros((bp, 256, 256), F32))
        hp, fcv = _ffn_call(hp, lw, jnp.zeros((bp, P, D_FF), F32), fng, tile=TILE_ROWS, stride=1, final_norm=last)
        outs['sp'].append(_diag_blocks(s_bd))
        outs['acp'].append(acv[:, P - (A_CONV - 1):])
        outs['bcp'].append(bcv[:, P - (B_CONV - 1):])
        outs['ckp'].append(ckn.reshape(bp, WINDOW, C_KV_HEADS, HEAD_DIM))
        outs['cvp'].append(cvn.reshape(bp, WINDOW, C_KV_HEADS, HEAD_DIM))
        outs['fcp'].append(fcv[:, P - (FFN_CONV - 1):])
        hs, ns = _sample_mixer(hs, lw, tabs_s, d_ws[l], d_bias[l], state_delta[l], state_delta_conv[l],
                               state_shortconv[l], cache_win_k[l], cache_win_v[l], bs, ts)
        f0 = jnp.swapaxes(state_ffn_conv[l], 0, 1).reshape(1, (FFN_CONV - 1) * bs, D_FF)
        ys_tm, fcs = _ffn_call(hs[None], lw, f0, fng, tile=ts * bs, stride=bs, final_norm=last)
        hs = ys_tm[0]
        outs['ss'].append(ns['a_state'])
        outs['acs'].append(ns['a_conv'])
        outs['bcs'].append(ns['b_conv'])
        outs['cks'].append(ns['c_k'])
        outs['cvs'].append(ns['c_v'])
        outs['fcs'].append(jnp.swapaxes(fcs.reshape(FFN_CONV - 1, bs, D_FF), 0, 1))
        outs['dv'].append(ns['d_v'])
    st = {k: jnp.stack(v) for k, v in outs.items()}
    hs = jnp.swapaxes(hs.reshape(ts, bs, D_MODEL), 0, 1)
    return (hp, hs, st['sp'], st['ss'], st['acp'], st['acs'], st['bcp'], st['bcs'], st['ckp'], st['cks'],
            st['cvp'], st['cvs'], st['fcp'], st['fcs'], st['dv'])
```

```python
import functools

import jax
import jax.numpy as jnp
import numpy as np
from jax import lax
from jax.experimental import pallas as pl
from jax.experimental.pallas import tpu as pltpu

F32 = jnp.float32
BF16 = jnp.bfloat16

D_MODEL = 1024
GROUP_WIDTH = 256
HEAD_DIM = 64
A_HEADS = 4
A_CONV = 4
A_CHUNK = 64
B_CONV = 3
C_HEADS = 4
C_KV_HEADS = 2
WINDOW = 128
ROPE_DIM = 16
ROPE_THETA = 500000.0
D_GROUPS = 4
D_CHUNK = 128
D_FF = 2816
FFN_CONV = 3
EPS = 1e-6
NEG_INF = -1e30
PAST_LEN = 16384

COL_A = 0
W_A = 1152
COL_B = 1152
W_B = 768
COL_C = 1920
W_C = 512
COL_D = 2432
W_D = 512
P_PACKED = 2944

TILE_ROWS = 512
CARRY_ROWS = 8
FFN_COL_CHUNK = 1408
VMEM_LIMIT_BYTES = 56 * 1024 * 1024
MIXER_VMEM_LIMIT_BYTES = 60 * 1024 * 1024


def _dot(a, b):
    return jnp.dot(a, b, preferred_element_type=F32)


def _dot_nt(a, b):
    return lax.dot_general(a, b, (((1,), (1,)), ((), ())), preferred_element_type=F32)


def _dot_tn(a, b):
    return lax.dot_general(a, b, (((0,), (0,)), ((), ())), preferred_element_type=F32)


def _split3(x):
    hi = x.astype(BF16)
    r1 = x - hi.astype(F32)
    mid = r1.astype(BF16)
    lo = (r1 - mid.astype(F32)).astype(BF16)
    return hi, mid, lo


def _dot_lhs3(x, w01):
    hi, mid, lo = _split3(x)
    return _dot(hi, w01) + _dot(mid, w01) + _dot(lo, w01)


def _head_sumsq(x, ones_bd):
    return _dot((x * x).astype(BF16), ones_bd)


def _sigmoid(x):
    return 1.0 / (1.0 + jnp.exp(-x))


def _silu(x):
    return x * _sigmoid(x)


def _softplus(x):
    return jnp.maximum(x, 0.0) + jnp.log(1.0 + jnp.exp(-jnp.abs(x)))


def _gelu_tanh(x):
    return 0.5 * x * (1.0 + jnp.tanh(np.sqrt(2.0 / np.pi).astype(np.float32) * (x + 0.044715 * (x * x * x))))


def _rms(x, g):
    return x * lax.rsqrt(jnp.mean(x * x, axis=-1, keepdims=True) + EPS) * g


def _iota(shape, dim):
    return lax.broadcasted_iota(jnp.int32, shape, dim)


MIXER_SEQS = 2


def _mixer_kernel(sinks_ref, x_ref, n1g_ref, w_in_ref, aconv_w_ref, alog_ref, dtb_ref, anorm_ref,
                  bconv_w_ref, cos_ref, sina_ref, sinb_ref, lng_ref, lnb_ref, ws_ref, dbias_ref,
                  w_out_ref, aconv0_ref, bconv0_ref, s0_ref,
                  x_out_ref, s_out_ref, aconv_out_ref, bconv_out_ref, ck_out_ref, cv_out_ref,
                  *scratch, tile):
    shared = (sinks_ref, n1g_ref, w_in_ref, aconv_w_ref, alog_ref, dtb_ref, anorm_ref, bconv_w_ref, cos_ref,
              sina_ref, sinb_ref, lng_ref, lnb_ref, ws_ref, dbias_ref, w_out_ref)
    per_seq = (x_ref, aconv0_ref, bconv0_ref, s0_ref, x_out_ref, s_out_ref, aconv_out_ref, bconv_out_ref,
               ck_out_ref, cv_out_ref)
    n_scr = len(scratch) // MIXER_SEQS
    gens = [_mixer_seq(*shared, *(r.at[b] for r in per_seq), *scratch[b * n_scr:(b + 1) * n_scr], tile=tile)
            for b in range(MIXER_SEQS)]
    started, live = 1, list(gens[:1])
    while live:
        for gen in list(live):
            if next(gen, 'done') == 'done':
                live.remove(gen)
        if started < len(gens):
            live.append(gens[started])
            started += 1


def _mixer_seq(sinks_ref, n1g_ref, w_in_ref, aconv_w_ref, alog_ref, dtb_ref, anorm_ref, bconv_w_ref, cos_ref,
               sina_ref, sinb_ref, lng_ref, lnb_ref, ws_ref, dbias_ref, w_out_ref,
               x_ref, aconv0_ref, bconv0_ref, s0_ref, x_out_ref, s_out_ref, aconv_out_ref, bconv_out_ref,
               ck_out_ref, cv_out_ref,
               abuf, bbuf, kbuf, vbuf, q_s, k_s, v_s, gcb_s, bb_s, o_s, s_scr, proj_s, *, tile):
    T = tile
    si = pl.program_id(1)
    P = CARRY_ROWS

    @pl.when(si == 0)
    def _():
        abuf[0:P, :] = aconv0_ref[...]
        bbuf[0:P, :] = bconv0_ref[...]
        kbuf[0:WINDOW, :] = jnp.zeros((WINDOW, 128), F32)
        vbuf[0:WINDOW, :] = jnp.zeros((WINDOW, 128), F32)
        s_scr[...] = s0_ref[...]

    res = {}
    h = _rms(x_ref[...], n1g_ref[...]).astype(BF16)
    for lo, width in ((COL_A, W_A), (COL_B, W_B), (COL_C, W_C), (COL_D, W_D)):
        proj_s[:, lo:lo + width] = _dot(h, w_in_ref[:, lo:lo + width])
    yield

    small = proj_s[:, 1024:1152]
    abuf[P:P + T, :] = proj_s[:, 0:768]

    def group_b():
        bbuf[P:P + T, :] = proj_s[:, COL_B + 256:COL_B + 512] * proj_s[:, COL_B + 512:COL_B + 768]
        bx = (bbuf[P:P + T, :] * bconv_w_ref[2:3, :] + bbuf[P - 1:P - 1 + T, :] * bconv_w_ref[1:2, :]
              + bbuf[P - 2:P - 2 + T, :] * bconv_w_ref[0:1, :])
        tail_b = bbuf[T:T + P, :]
        bbuf[0:P, :] = tail_b
        bconv_out_ref[...] = tail_b
        res['out_b'] = proj_s[:, COL_B:COL_B + 256] * bx

    def group_c_rope():
        cos = cos_ref[...]
        sina = sina_ref[...]
        sinb = sinb_ref[...]
        cq = proj_s[:, COL_C:COL_C + 256]
        res['cq'] = (cq * jnp.concatenate([cos, cos], axis=1)
                     + pltpu.roll(cq, 256 - ROPE_DIM // 2, 1) * jnp.concatenate([sina, sina], axis=1)
                     + pltpu.roll(cq, ROPE_DIM // 2, 1) * jnp.concatenate([sinb, sinb], axis=1))
        ck = proj_s[:, COL_C + 256:COL_C + 384]
        ck = (ck * cos + pltpu.roll(ck, 128 - ROPE_DIM // 2, 1) * sina
              + pltpu.roll(ck, ROPE_DIM // 2, 1) * sinb)
        kbuf[WINDOW:WINDOW + T, :] = ck
        vbuf[WINDOW:WINDOW + T, :] = proj_s[:, COL_C + 384:COL_C + 512]
        ck_out_ref[...] = kbuf[T:T + WINDOW, :]
        cv_out_ref[...] = vbuf[T:T + WINDOW, :]

    def group_d_norm():
        res['du'] = _gelu_tanh(proj_s[:, COL_D:COL_D + 256])
        gv = _gelu_tanh(proj_s[:, COL_D + 256:COL_D + 512])
        mu = jnp.mean(gv, axis=-1, keepdims=True)
        xc = gv - mu
        res['dvn'] = (xc * lax.rsqrt(jnp.mean(xc * xc, axis=-1, keepdims=True) + EPS) * lng_ref[...]
                      + lnb_ref[...])
        wr = _iota((D_GROUPS * D_CHUNK, D_CHUNK), 0) & (D_CHUNK - 1)
        wc = _iota((D_GROUPS * D_CHUNK, D_CHUNK), 1)
        res['wm'] = jnp.where(wr >= wc, ws_ref[...], 0.0).astype(BF16)

    y = (abuf[P:P + T, :] * aconv_w_ref[3:4, :] + abuf[P - 1:P - 1 + T, :] * aconv_w_ref[2:3, :]
         + abuf[P - 2:P - 2 + T, :] * aconv_w_ref[1:2, :] + abuf[P - 3:P - 3 + T, :] * aconv_w_ref[0:1, :])
    tail_a = abuf[T:T + P, :]
    abuf[0:P, :] = tail_a
    aconv_out_ref[...] = tail_a
    qkv = _silu(y)
    q_raw = qkv[:, 0:256]
    k_raw = qkv[:, 256:512]
    v_s[...] = qkv[:, 512:768]

    r256 = _iota((256, 256), 0) >> 6
    c256 = _iota((256, 256), 1) >> 6
    mask_bd = r256 == c256
    ones_bd = jnp.where(mask_bd, 1.0, 0.0).astype(BF16)
    q_s[...] = q_raw * lax.rsqrt(_head_sumsq(q_raw, ones_bd) + EPS) * (HEAD_DIM ** -0.5)
    k_s[...] = k_raw * lax.rsqrt(_head_sumsq(k_raw, ones_bd) + EPS)
    yield

    g_log = -jnp.exp(alog_ref[...]) * _softplus(small + dtb_ref[...])
    beta = _sigmoid(small)
    gbeta = jnp.where(_iota((T, 128), 1) < A_HEADS, g_log, beta)
    expand = jnp.where(_iota((128, 512), 0) == (_iota((128, 512), 1) >> 6), 1.0, 0.0).astype(BF16)
    gbb = _dot_lhs3(gbeta, expand)
    bb_s[...] = gbb[:, 256:512]

    def chunk_cumsum(xv):
        row_in_chunk = _iota(xv.shape, 0) & (A_CHUNK - 1)
        step = 1
        while step < A_CHUNK:
            xv = xv + jnp.where(row_in_chunk >= step, pltpu.roll(xv, step, 0), 0.0)
            step *= 2
        return xv

    gcb_s[...] = chunk_cumsum(gbb[:, 0:256])
    gct = chunk_cumsum(g_log).T[0:8, :]
    low_half = (_iota((1, T), 1) & A_CHUNK) == 0
    gct_r = pltpu.roll(gct, A_CHUNK, 1)
    gct_l = pltpu.roll(gct, T - A_CHUNK, 1)
    even_rows = [jnp.where(low_half, gct[a:a + 1, :], gct_r[a + 1:a + 2, :]) for a in (0, 2)]
    odd_rows = [jnp.where(low_half, gct_l[a:a + 1, :], gct[a + 1:a + 2, :]) for a in (0, 2)]
    yield

    ri = _iota((A_CHUNK, 256), 0)
    ci = _iota((A_CHUNK, 256), 1) & (A_CHUNK - 1)
    causal_t = ri >= ci
    strict_t = ri > ci
    eye_t = jnp.where(ri == ci, 1.0, 0.0)

    def bd16(x16):
        return jnp.concatenate([x16, x16, x16, x16], axis=0) * ones_bd

    n_chunks = T // A_CHUNK
    t_inv, pw, qk, qdec, e_tail, vb16, kbe16 = [], [], [], [], [], [], []
    for c in range(n_chunks):
        rows = slice(c * A_CHUNK, (c + 1) * A_CHUNK)
        blk = slice((c // 2) * 128, (c // 2) * 128 + 128)
        src = even_rows if c % 2 == 0 else odd_rows
        gc_row = jnp.concatenate([src[0][:, blk], src[1][:, blk]], axis=1)
        qc = q_s[rows, :]
        kc = k_s[rows, :]
        bbc = bb_s[rows, :]
        gcb = gcb_s[rows, :]
        decay = jnp.where(causal_t, jnp.exp(jnp.where(causal_t, gcb - gc_row, 0.0)), 0.0)
        eg = jnp.exp(gcb)
        kb = kc * bbc
        aq = _dot_nt(jnp.concatenate([kb, qc], axis=0).astype(BF16), bd16(kc.astype(BF16)))
        a_mat = jnp.where(strict_t, aq[0:A_CHUNK] * decay, 0.0)
        qk.append(aq[A_CHUNK:2 * A_CHUNK] * decay)
        t_inv.append(eye_t - a_mat)
        pw.append(a_mat)
        qdec.append(qc * eg)
        e_tail.append(jnp.exp(gcb[A_CHUNK - 1:A_CHUNK, :] - gcb))
        vb16.append((v_s[rows, :] * bbc).astype(BF16))
        kbe16.append((kb * eg).astype(BF16))
    yield
    u, w = [], []
    state = {'s': s_scr[...]}

    def level_first():
        for c in range(n_chunks):
            p16 = pw[c].astype(BF16)
            pw[c] = _dot(p16, bd16(p16))

    def level_mid():
        for c in range(n_chunks):
            p16 = pw[c].astype(BF16)
            res = _dot(jnp.concatenate([p16, t_inv[c].astype(BF16)], axis=0), bd16(p16))
            pw[c] = res[0:A_CHUNK]
            t_inv[c] = t_inv[c] + res[A_CHUNK:2 * A_CHUNK]

    def level_last():
        for c in range(n_chunks):
            t_c = t_inv[c] + _dot(t_inv[c].astype(BF16), bd16(pw[c].astype(BF16)))
            uw = _dot(t_c.astype(BF16), jnp.concatenate([bd16(vb16[c]), bd16(kbe16[c])], axis=1))
            u.append(uw[:, 0:256])
            w.append(uw[:, 256:512])

    def scan_step(c):
        rows = slice(c * A_CHUNK, (c + 1) * A_CHUNK)
        s_bd = state['s']
        wq = _dot(jnp.concatenate([w[c], qdec[c]], axis=0).astype(BF16), s_bd.astype(BF16))
        v_new = u[c] - wq[0:A_CHUNK]
        o_s[rows, :] = wq[A_CHUNK:2 * A_CHUNK] + _dot(qk[c].astype(BF16), bd16(v_new.astype(BF16)))
        kv = _dot_tn(k_s[rows, :].astype(BF16), (v_new * e_tail[c]).astype(BF16))
        g_last = gcb_s[(c + 1) * A_CHUNK - 1:(c + 1) * A_CHUNK, :]
        state['s'] = s_bd * jnp.exp(g_last) + jnp.where(mask_bd, kv, 0.0)

    chain = [level_first] + [level_mid] * 4 + [level_last] + [functools.partial(scan_step, c) for c in range(n_chunks)]

    lane128 = _iota((2 * WINDOW, 128), 1)
    low = _iota((WINDOW, 128), 1) < HEAD_DIM
    qrow = _iota((2 * WINDOW, 2 * WINDOW), 0) & (WINDOW - 1)
    kcol = _iota((2 * WINDOW, 2 * WINDOW), 1)
    band = (kcol > qrow) & (kcol <= qrow + WINDOW)
    top_half = _iota((2 * WINDOW, 1), 0) < WINDOW
    out_c_blocks = []

    def group_c_block(n):
        first_key = si * T + (n - 1) * WINDOW
        valid = band & (kcol + first_key >= 0)
        kwin = kbuf[n * WINDOW:(n + 2) * WINDOW, :]
        vwin = vbuf[n * WINDOW:(n + 2) * WINDOW, :]
        k_sw = pltpu.roll(kwin, HEAD_DIM, 1)
        v_sw = pltpu.roll(vwin, HEAD_DIM, 1)
        pair_out = []
        for g in range(C_KV_HEADS):
            own = (lane128 < HEAD_DIM) if g == 0 else (lane128 >= HEAD_DIM)
            k_dup = jnp.where(own, kwin, k_sw).astype(BF16)
            v_dup = jnp.where(own, vwin, v_sw).astype(BF16)
            qp = res['cq'][n * WINDOW:(n + 1) * WINDOW, g * 128:(g + 1) * 128]
            q_st = jnp.concatenate([jnp.where(low, qp, 0.0), jnp.where(low, 0.0, qp)], axis=0).astype(BF16)
            s = _dot_nt(q_st, k_dup) * (HEAD_DIM ** -0.5)
            s = jnp.where(valid, s, NEG_INF)
            sink = jnp.where(top_half, sinks_ref[2 * g], sinks_ref[2 * g + 1])
            m = jnp.maximum(jnp.max(s, axis=-1, keepdims=True), sink)
            p = jnp.exp(s - m)
            denom = jnp.sum(p, axis=-1, keepdims=True) + jnp.exp(sink - m)
            o2 = _dot((p / denom).astype(BF16), v_dup)
            pair_out.append(jnp.where(low, o2[0:WINDOW], o2[WINDOW:2 * WINDOW]))
        out_c_blocks.append(jnp.concatenate(pair_out, axis=1))
        if n == T // WINDOW - 1:
            kbuf[0:WINDOW, :] = kbuf[T:T + WINDOW, :]
            vbuf[0:WINDOW, :] = vbuf[T:T + WINDOW, :]

    lane_grp = _iota((D_CHUNK, 256), 1) >> 6
    out_d_blocks = []

    def group_d_block(n):
        mx = _dot(res['wm'], res['dvn'][n * D_CHUNK:(n + 1) * D_CHUNK, :].astype(BF16))
        mixed = dbias_ref[...]
        for grp in range(D_GROUPS):
            mixed = mixed + jnp.where(lane_grp == grp, mx[grp * D_CHUNK:(grp + 1) * D_CHUNK, :], 0.0)
        out_d_blocks.append(res['du'][n * D_CHUNK:(n + 1) * D_CHUNK, :] * mixed)

    c_blocks = [functools.partial(group_c_block, n) for n in range(T // WINDOW)]
    d_blocks = [functools.partial(group_d_block, n) for n in range(T // D_CHUNK)]
    fill = [group_b, group_c_rope, group_d_norm] + c_blocks + d_blocks
    for i in range(max(len(chain), len(fill))):
        if i < len(chain):
            chain[i]()
        if i < len(fill):
            fill[i]()
        yield
    out_b = res['out_b']
    out_c = jnp.concatenate(out_c_blocks, axis=0)
    out_d = jnp.concatenate(out_d_blocks, axis=0)

    s_scr[...] = state['s']
    s_out_ref[...] = state['s']
    o = o_s[...]
    out_a = (o * lax.rsqrt(_head_sumsq(o, ones_bd) * (1.0 / HEAD_DIM) + EPS) * anorm_ref[...]
             * _silu(proj_s[:, 768:1024]))

    cat = jnp.concatenate([out_a, out_b, out_c, out_d], axis=1).astype(BF16)
    x_out_ref[...] = x_ref[...] + _dot(cat, w_out_ref[...])


def _mixer_call(x, lw, tabs, aconv0, bconv0, s0):
    bsz, seq, _ = x.shape
    T = TILE_ROWS
    ns = seq // T
    NB = MIXER_SEQS
    P = CARRY_ROWS
    full = lambda shape: pl.BlockSpec(shape, lambda b, s, *_: (0,) * len(shape), pipeline_mode=pl.Buffered(1))
    per_b = lambda shape: pl.BlockSpec((NB,) + shape, lambda b, s, *_: (b,) + (0,) * len(shape))
    tab = pl.BlockSpec((T, 128), lambda b, s, *_: (s, 0))
    seq_scratch = [pltpu.VMEM(shape, F32) for shape in (
        (T + P, 768), (T + P, 256), (T + WINDOW, 128), (T + WINDOW, 128), (T, 256), (T, 256), (T, 256),
        (T, 256), (T, 256), (T, 256), (256, 256), (T, P_PACKED))]
    grid_spec = pltpu.PrefetchScalarGridSpec(
        num_scalar_prefetch=1,
        grid=(bsz // NB, ns),
        in_specs=[
            pl.BlockSpec((NB, T, D_MODEL), lambda b, s, *_: (b, s, 0)),
            full((1, D_MODEL)), full((D_MODEL, P_PACKED)), full((A_CONV, 768)), full((1, 128)), full((1, 128)),
            full((1, 256)), full((B_CONV, 256)), tab, tab, tab, full((1, 256)), full((1, 256)),
            full((D_GROUPS * D_CHUNK, D_CHUNK)), full((D_CHUNK, 256)), full((D_MODEL, D_MODEL)),
            per_b((P, 768)), per_b((P, 256)), per_b((256, 256)),
        ],
        out_specs=[
            pl.BlockSpec((NB, T, D_MODEL), lambda b, s, *_: (b, s, 0)),
            per_b((256, 256)), per_b((P, 768)), per_b((P, 256)), per_b((WINDOW, 128)), per_b((WINDOW, 128)),
        ],
        scratch_shapes=seq_scratch * NB,
    )
    out_shape = [
        jax.ShapeDtypeStruct((bsz, seq, D_MODEL), F32),
        jax.ShapeDtypeStruct((bsz, 256, 256), F32),
        jax.ShapeDtypeStruct((bsz, P, 768), F32),
        jax.ShapeDtypeStruct((bsz, P, 256), F32),
        jax.ShapeDtypeStruct((bsz, WINDOW, 128), F32),
        jax.ShapeDtypeStruct((bsz, WINDOW, 128), F32),
    ]
    return pl.pallas_call(
        functools.partial(_mixer_kernel, tile=T),
        grid_spec=grid_spec,
        out_shape=out_shape,
        compiler_params=pltpu.CompilerParams(
            dimension_semantics=("arbitrary", "arbitrary"), vmem_limit_bytes=MIXER_VMEM_LIMIT_BYTES),
        name="mixer",
    )(lw['c_sinks'], x, lw['norm1_g'], lw['w_in'], lw['a_conv_w'], lw['a_log'], lw['a_dt_bias'],
      lw['a_norm_g'], lw['b_conv_w'], tabs[0], tabs[1], tabs[2], lw['d_ln_g'], lw['d_ln_b'],
      lw['d_ws'], lw['d_bias'], lw['w_out'], aconv0, bconv0, s0)


def _ffn_kernel(x_ref, n2g_ref, wg_ref, wu_ref, cw_ref, wd_ref, fconv0_ref, fng_ref,
                x_out_ref, fconv_out_ref, gbuf, *, tile, stride, final_norm):
    T = tile
    P = (FFN_CONV - 1) * stride if stride > 1 else CARRY_ROWS
    si = pl.program_id(1)

    @pl.when(si == 0)
    def _():
        gbuf[0:P, :] = fconv0_ref[0]

    x = x_ref[0]
    h = _rms(x, n2g_ref[...]).astype(BF16)
    acc = x
    for c0 in range(0, D_FF, FFN_COL_CHUNK):
        cols = slice(c0, c0 + FFN_COL_CHUNK)
        gbuf[P:P + T, cols] = _dot(h, wg_ref[:, cols])
        up = _dot(h, wu_ref[:, cols])
        gate = (gbuf[P:P + T, cols] * cw_ref[2:3, cols] + gbuf[P - stride:P - stride + T, cols] * cw_ref[1:2, cols]
                + gbuf[P - 2 * stride:P - 2 * stride + T, cols] * cw_ref[0:1, cols])
        acc = acc + _dot((_silu(gate) * up).astype(BF16), wd_ref[cols, :])
    tail = gbuf[T:T + P, :]
    gbuf[0:P, :] = tail
    fconv_out_ref[0] = tail
    if final_norm:
        acc = _rms(acc, fng_ref[...])
    x_out_ref[0] = acc


def _ffn_call(x, lw, fconv0, final_g, *, tile, stride, final_norm):
    bsz, seq, _ = x.shape
    T = tile
    P = (FFN_CONV - 1) * stride if stride > 1 else CARRY_ROWS
    full = lambda shape: pl.BlockSpec(shape, lambda b, s: (0,) * len(shape), pipeline_mode=pl.Buffered(1))
    per_b = lambda shape: pl.BlockSpec((1,) + shape, lambda b, s: (b,) + (0,) * len(shape))
    return pl.pallas_call(
        functools.partial(_ffn_kernel, tile=T, stride=stride, final_norm=final_norm),
        grid=(bsz, seq // T),
        in_specs=[
            pl.BlockSpec((1, T, D_MODEL), lambda b, s: (b, s, 0)),
            full((1, D_MODEL)), full((D_MODEL, D_FF)), full((D_MODEL, D_FF)), full((FFN_CONV, D_FF)),
            full((D_FF, D_MODEL)), per_b((P, D_FF)), full((1, D_MODEL)),
        ],
        out_specs=[pl.BlockSpec((1, T, D_MODEL), lambda b, s: (b, s, 0)), per_b((P, D_FF))],
        out_shape=[jax.ShapeDtypeStruct((bsz, seq, D_MODEL), F32), jax.ShapeDtypeStruct((bsz, P, D_FF), F32)],
        scratch_shapes=[pltpu.VMEM((T + P, D_FF), F32)],
        compiler_params=pltpu.CompilerParams(
            dimension_semantics=("arbitrary", "arbitrary"), vmem_limit_bytes=VMEM_LIMIT_BYTES),
        name="ffn",
    )(x, lw['norm2_g'], lw['ffn_w_gate'], lw['ffn_w_up'], lw['ffn_conv_w'], lw['ffn_w_down'], fconv0, final_g)


def _pack_w_in(w):
    pad = jnp.zeros((D_MODEL, 128 - 2 * A_HEADS), w.dtype)
    small = jnp.concatenate([w[:, 1028:1032], w[:, 1024:1028], pad], axis=1)
    return jnp.concatenate([w[:, 0:1024], small, w[:, 1032:]], axis=1).astype(BF16)


def _pad_lanes(v, width=128):
    return jnp.concatenate([v, jnp.zeros((width - v.shape[0],), v.dtype)])[None, :]


def _layer_weights(l, norm1_g, w_in, a_conv_w, a_log, a_dt_bias, a_norm_g, b_conv_w, c_sinks, d_ln_g, d_ln_b,
                   d_ws, d_bias, w_out, norm2_g, ffn_w_gate, ffn_w_up, ffn_conv_w, ffn_w_down):
    bias_tab = jnp.broadcast_to(d_bias[l].T[:, :, None], (D_CHUNK, D_GROUPS, HEAD_DIM)).reshape(D_CHUNK, 256)
    return {
        'norm1_g': norm1_g[l][None, :], 'w_in': _pack_w_in(w_in[l]), 'a_conv_w': a_conv_w[l],
        'a_log': _pad_lanes(a_log[l]), 'a_dt_bias': _pad_lanes(a_dt_bias[l]),
        'a_norm_g': jnp.tile(a_norm_g[l], A_HEADS)[None, :], 'b_conv_w': b_conv_w[l], 'c_sinks': c_sinks[l],
        'd_ln_g': d_ln_g[l][None, :], 'd_ln_b': d_ln_b[l][None, :],
        'd_ws': d_ws[l].reshape(D_GROUPS * D_CHUNK, D_CHUNK), 'd_bias': bias_tab,
        'w_out': w_out[l].astype(BF16), 'norm2_g': norm2_g[l][None, :],
        'ffn_w_gate': ffn_w_gate[l].astype(BF16), 'ffn_w_up': ffn_w_up[l].astype(BF16),
        'ffn_conv_w': ffn_conv_w[l], 'ffn_w_down': ffn_w_down[l].astype(BF16),
    }


def _rope_tables(pos):
    half = ROPE_DIM // 2
    inv = np.power(ROPE_THETA, -np.arange(half, dtype=np.float64) * (2.0 / ROPE_DIM))
    ang = pos.astype(np.float64)[:, None] * inv[None, :]
    cos, sin = jnp.asarray(np.cos(ang), F32), jnp.asarray(np.sin(ang), F32)
    n = pos.shape[0]
    rest = HEAD_DIM - ROPE_DIM
    cos_h = jnp.concatenate([cos, cos, jnp.ones((n, rest), F32)], axis=1)
    sina_h = jnp.concatenate([-sin, jnp.zeros((n, half + rest), F32)], axis=1)
    sinb_h = jnp.concatenate([jnp.zeros((n, half), F32), sin, jnp.zeros((n, rest), F32)], axis=1)
    return tuple(jnp.concatenate([t, t], axis=1) for t in (cos_h, sina_h, sinb_h))


def _diag_blocks(s_bd):
    return jnp.stack([s_bd[:, HEAD_DIM * i:HEAD_DIM * (i + 1), HEAD_DIM * i:HEAD_DIM * (i + 1)]
                      for i in range(A_HEADS)], axis=1)


def _sample_pre_kernel(x_ref, n1g_ref, w_in_ref, aconv_w_ref, alog_ref, dtb_ref, bconv_w_ref, cos_ref, sina_ref,
                       sinb_ref, lng_ref, lnb_ref, wtab_ref, btab_ref, aconv0_ref, bconv0_ref,
                       q_ref, k_ref, v_ref, g_ref, beta_ref, z_ref, outb_ref, cq_ref, ck_ref, cv_ref, outd_ref,
                       dvn_ref, aconv_out_ref, bconv_out_ref, abuf, bbuf, *, rows, stride):
    R, S = rows, stride
    steps = R // S
    pa_rows = (A_CONV - 1) * S
    pb_rows = (B_CONV - 1) * S
    x = x_ref[...]
    h = _rms(x, n1g_ref[...]).astype(BF16)

    pa = _dot(h, w_in_ref[:, COL_A:COL_A + W_A])
    z_ref[...] = pa[:, 768:1024]
    small = pa[:, 1024:1152]
    abuf[0:pa_rows, :] = aconv0_ref[...]
    abuf[pa_rows:pa_rows + R, :] = pa[:, 0:768]
    y = abuf[0:R, :] * aconv_w_ref[0:1, :]
    for j in range(1, A_CONV):
        y = y + abuf[j * S:j * S + R, :] * aconv_w_ref[j:j + 1, :]
    aconv_out_ref[...] = abuf[R:R + pa_rows, :]
    qkv = _silu(y)
    q_raw = qkv[:, 0:256]
    k_raw = qkv[:, 256:512]
    v_ref[...] = qkv[:, 512:768]
    mask_bd = (_iota((256, 256), 0) >> 6) == (_iota((256, 256), 1) >> 6)
    ones_bd = jnp.where(mask_bd, 1.0, 0.0).astype(BF16)
    q_ref[...] = q_raw * lax.rsqrt(_dot_lhs3(q_raw * q_raw, ones_bd) + EPS) * (HEAD_DIM ** -0.5)
    k_ref[...] = k_raw * lax.rsqrt(_dot_lhs3(k_raw * k_raw, ones_bd) + EPS)
    g_ref[...] = -jnp.exp(alog_ref[...]) * _softplus(small + dtb_ref[...])
    beta_ref[...] = _sigmoid(small)

    pb = _dot(h, w_in_ref[:, COL_B:COL_B + W_B])
    bbuf[0:pb_rows, :] = bconv0_ref[...]
    bbuf[pb_rows:pb_rows + R, :] = pb[:, 256:512] * pb[:, 512:768]
    bx = bbuf[0:R, :] * bconv_w_ref[0:1, :]
    for j in range(1, B_CONV):
        bx = bx + bbuf[j * S:j * S + R, :] * bconv_w_ref[j:j + 1, :]
    bconv_out_ref[...] = bbuf[R:R + pb_rows, :]
    outb_ref[...] = pb[:, 0:256] * bx

    pc = _dot(h, w_in_ref[:, COL_C:COL_C + W_C])
    cos = cos_ref[...]
    sina = sina_ref[...]
    sinb = sinb_ref[...]
    cq = pc[:, 0:256]
    cq_ref[...] = (cq * jnp.concatenate([cos, cos], axis=1)
                   + pltpu.roll(cq, 256 - ROPE_DIM // 2, 1) * jnp.concatenate([sina, sina], axis=1)
                   + pltpu.roll(cq, ROPE_DIM // 2, 1) * jnp.concatenate([sinb, sinb], axis=1))
    ck = pc[:, 256:384]
    ck_ref[...] = (ck * cos + pltpu.roll(ck, 128 - ROPE_DIM // 2, 1) * sina
                   + pltpu.roll(ck, ROPE_DIM // 2, 1) * sinb)
    cv_ref[...] = pc[:, 384:512]

    pd = _dot(h, w_in_ref[:, COL_D:COL_D + W_D])
    du = _gelu_tanh(pd[:, 0:256])
    gv = _gelu_tanh(pd[:, 256:512])
    mu = jnp.mean(gv, axis=-1, keepdims=True)
    xc = gv - mu
    dvn = xc * lax.rsqrt(jnp.mean(xc * xc, axis=-1, keepdims=True) + EPS) * lng_ref[...] + lnb_ref[...]
    dvn_ref[...] = dvn
    for t in range(steps):
        mixed = btab_ref[t:t + 1, :]
        for s in range(t + 1):
            mixed = mixed + wtab_ref[t * steps + s:t * steps + s + 1, :] * dvn[s * S:(s + 1) * S, :]
        outd_ref[t * S:(t + 1) * S, :] = du[t * S:(t + 1) * S, :] * mixed


def _sample_pre_call(x, lw, tabs, wtab, btab, aconv0, bconv0, stride):
    R = x.shape[0]
    f = lambda *shape: jax.ShapeDtypeStruct(shape, F32)
    out_shape = [f(R, 256), f(R, 256), f(R, 256), f(R, 128), f(R, 128), f(R, 256), f(R, 256), f(R, 256),
                 f(R, 128), f(R, 128), f(R, 256), f(R, 256), f((A_CONV - 1) * stride, 768),
                 f((B_CONV - 1) * stride, 256)]
    return pl.pallas_call(
        functools.partial(_sample_pre_kernel, rows=R, stride=stride),
        out_shape=out_shape,
        scratch_shapes=[pltpu.VMEM((R + (A_CONV - 1) * stride, 768), F32),
                        pltpu.VMEM((R + (B_CONV - 1) * stride, 256), F32)],
        compiler_params=pltpu.CompilerParams(vmem_limit_bytes=VMEM_LIMIT_BYTES),
        name="sample_pre",
    )(x, lw['norm1_g'], lw['w_in'], lw['a_conv_w'], lw['a_log'], lw['a_dt_bias'], lw['b_conv_w'],
      tabs[0], tabs[1], tabs[2], lw['d_ln_g'], lw['d_ln_b'], wtab, btab, aconv0, bconv0)


def _sample_delta_kernel(g_ref, beta_ref, q_ref, k_ref, v_ref, s_ref, o_ref, snew_ref,
                         qt_s, kt_s, vt_s, gt_s, bt_s, ot_s, *, steps, nb):
    h = pl.program_id(0)
    n_i = HEAD_DIM
    zeros = jnp.zeros((HEAD_DIM, nb), F32)

    @pl.when(h == 0)
    def _():
        for t in range(steps):
            rows = slice(t * nb, (t + 1) * nb)
            qt_s[t] = q_ref[rows, :].T
            kt_s[t] = k_ref[rows, :].T
            vt_s[t] = v_ref[rows, :].T
            gt_s[t] = g_ref[rows, :].T
            bt_s[t] = beta_ref[rows, :].T

    base = pl.multiple_of(h * HEAD_DIM, HEAD_DIM)
    head_rows = pl.ds(base, HEAD_DIM)

    def rows_of(i):
        return pl.ds(pl.multiple_of(i * HEAD_DIM, HEAD_DIM), HEAD_DIM)

    def decay(t):
        return jnp.exp(gt_s[t, pl.ds(h, 1), :])

    dec0 = decay(0)

    def first_pass(i, acc):
        return acc + kt_s[0, pl.ds(base + i, 1), :] * (s_ref[0, rows_of(i), :] * dec0)

    ks = lax.fori_loop(0, n_i, first_pass, zeros)
    for t in range(steps):
        dec = decay(t)
        v_new = bt_s[t, pl.ds(A_HEADS + h, 1), :] * (vt_s[t, head_rows, :] - ks)
        src = s_ref if t == 0 else snew_ref
        dec_next = decay(t + 1) if t + 1 < steps else None

        def update(i, carry, t=t, dec=dec, v_new=v_new, src=src, dec_next=dec_next):
            o_acc, ks_acc = carry
            blk = src[0, rows_of(i), :] * dec + kt_s[t, pl.ds(base + i, 1), :] * v_new
            snew_ref[0, rows_of(i), :] = blk
            o_acc = o_acc + qt_s[t, pl.ds(base + i, 1), :] * blk
            if dec_next is not None:
                ks_acc = ks_acc + kt_s[t + 1, pl.ds(base + i, 1), :] * (blk * dec_next)
            return o_acc, ks_acc

        o_acc, ks = lax.fori_loop(0, n_i, update, (zeros, zeros))
        ot_s[t, head_rows, :] = o_acc

    @pl.when(h == A_HEADS - 1)
    def _():
        for t in range(steps):
            o_ref[t * nb:(t + 1) * nb, :] = ot_s[t].T


def _sample_delta_call(g, beta, q, k, v, s, steps):
    nh, _, nb = s.shape
    rows = steps * nb
    whole = lambda width: pl.BlockSpec((rows, width), lambda h: (0, 0))
    st = pl.BlockSpec((1, HEAD_DIM * HEAD_DIM, nb), lambda h: (h, 0, 0))
    wide = pltpu.VMEM((steps, GROUP_WIDTH, nb), F32)
    narrow = pltpu.VMEM((steps, 128, nb), F32)
    return pl.pallas_call(
        functools.partial(_sample_delta_kernel, steps=steps, nb=nb),
        grid=(nh,),
        in_specs=[whole(128), whole(128), whole(GROUP_WIDTH), whole(GROUP_WIDTH), whole(GROUP_WIDTH), st],
        out_specs=[whole(GROUP_WIDTH), st],
        out_shape=[jax.ShapeDtypeStruct((rows, GROUP_WIDTH), F32),
                   jax.ShapeDtypeStruct((nh, HEAD_DIM * HEAD_DIM, nb), F32)],
        scratch_shapes=[wide, wide, wide, narrow, narrow, wide],
        compiler_params=pltpu.CompilerParams(dimension_semantics=("arbitrary",),
                                             vmem_limit_bytes=VMEM_LIMIT_BYTES),
        name="sample_delta",
    )(g, beta, q, k, v, s)


SAMPLE_ATTN_BLOCK = 8
NEW_KEY_ROWS = 8


def _sample_attn_kernel(sinks_ref, qm_ref, kc_ref, kn_ref, vc_ref, vn_ref, o_ref, k_out_ref, v_out_ref, *, steps):
    nq = C_HEADS * steps
    nk = WINDOW + NEW_KEY_ROWS
    BB = SAMPLE_ATTN_BLOCK
    row = _iota((BB * nq, nk), 0)
    col = _iota((BB * nq, nk), 1)
    t_q = row & (steps - 1)
    valid = ((col < WINDOW) & (col > t_q)) | ((col >= WINDOW) & (col - WINDOW <= t_q))
    head = (_iota((BB * nq, 1), 0) >> 2) & (C_HEADS - 1)
    sink = jnp.where(head == 0, sinks_ref[0],
                     jnp.where(head == 1, sinks_ref[1], jnp.where(head == 2, sinks_ref[2], sinks_ref[3])))
    scores = [_dot_nt(qm_ref[b].astype(BF16), jnp.concatenate([kc_ref[0, b], kn_ref[b]], axis=0).astype(BF16))
              for b in range(BB)]
    s = jnp.where(valid, jnp.concatenate(scores, axis=0) * (HEAD_DIM ** -0.5), NEG_INF)
    m = jnp.maximum(jnp.max(s, axis=-1, keepdims=True), sink)
    p = jnp.exp(s - m)
    denom = jnp.sum(p, axis=-1, keepdims=True) + jnp.exp(sink - m)
    p16 = (p / denom).astype(BF16)
    is_new_row = _iota((WINDOW, 128), 0) >= WINDOW - steps
    pad_rows = jnp.zeros((WINDOW - NEW_KEY_ROWS, 128), F32)

    def slide(cache, new_rows):
        kept = pltpu.roll(cache, WINDOW - steps, 0)
        tail = jnp.concatenate([pad_rows, pltpu.roll(new_rows, NEW_KEY_ROWS - steps, 0)], axis=0)
        return jnp.where(is_new_row, tail, kept)

    for b in range(BB):
        v_all = jnp.concatenate([vc_ref[0, b], vn_ref[b]], axis=0).astype(BF16)
        o_ref[b] = _dot(p16[b * nq:(b + 1) * nq, :], v_all)
        k_out_ref[b] = slide(kc_ref[0, b], kn_ref[b])
        v_out_ref[b] = slide(vc_ref[0, b], vn_ref[b])


def _sample_attn_call(sinks, qm, kc_all, kn, vc_all, vn, layer, steps):
    bs, nq, _ = qm.shape
    BB = SAMPLE_ATTN_BLOCK
    blk = lambda r: pl.BlockSpec((BB, r, 128), lambda i, *_: (i, 0, 0))
    cache = pl.BlockSpec((1, BB, WINDOW, 128), lambda i, *_: (layer, i, 0, 0))
    grid_spec = pltpu.PrefetchScalarGridSpec(
        num_scalar_prefetch=1, grid=(bs // BB,),
        in_specs=[blk(nq), cache, blk(NEW_KEY_ROWS), cache, blk(NEW_KEY_ROWS)],
        out_specs=[blk(nq), blk(WINDOW), blk(WINDOW)])
    return pl.pallas_call(
        functools.partial(_sample_attn_kernel, steps=steps),
        grid_spec=grid_spec,
        out_shape=[jax.ShapeDtypeStruct((bs, nq, 128), F32), jax.ShapeDtypeStruct((bs, WINDOW, 128), F32),
                   jax.ShapeDtypeStruct((bs, WINDOW, 128), F32)],
        compiler_params=pltpu.CompilerParams(dimension_semantics=("arbitrary",)),
        name="sample_attn",
    )(sinks, qm, kc_all, kn, vc_all, vn)


def _sample_post_kernel(x_ref, o_ref, z_ref, outb_ref, outc_ref, outd_ref, anorm_ref, w_out_ref, x_out_ref):
    mask_bd = (_iota((256, 256), 0) >> 6) == (_iota((256, 256), 1) >> 6)
    ones_bd = jnp.where(mask_bd, 1.0, 0.0).astype(BF16)
    o = o_ref[...]
    out_a = (o * lax.rsqrt(_dot_lhs3(o * o, ones_bd) * (1.0 / HEAD_DIM) + EPS) * anorm_ref[...]
             * _silu(z_ref[...]))
    cat = jnp.concatenate([out_a, outb_ref[...], outc_ref[...], outd_ref[...]], axis=1).astype(BF16)
    x_out_ref[...] = x_ref[...] + _dot(cat, w_out_ref[...])


def _sample_post_call(x, o, z, out_b, out_c, out_d, lw):
    return pl.pallas_call(
        _sample_post_kernel,
        out_shape=jax.ShapeDtypeStruct(x.shape, F32),
        compiler_params=pltpu.CompilerParams(vmem_limit_bytes=VMEM_LIMIT_BYTES),
        name="sample_post",
    )(x, o, z, out_b, out_c, out_d, lw['a_norm_g'], lw['w_out'])


def _sample_mixer(x_tm, lw, tabs, d_ws_l, d_bias_l, a_state, a_conv, b_conv, c_k_all, c_v_all, layer, bs, ts):
    to_tm = lambda a: jnp.swapaxes(a, 0, 1).reshape(a.shape[1] * bs, a.shape[2])
    from_tm = lambda a, n: jnp.swapaxes(a.reshape(n, bs, a.shape[-1]), 0, 1)
    wtab = jnp.repeat(d_ws_l[:, :ts, :ts].transpose(1, 2, 0).reshape(ts * ts, D_GROUPS), HEAD_DIM, axis=1)
    btab = jnp.repeat(d_bias_l[:, :ts].T, HEAD_DIM, axis=1)
    (q, k, v, g, beta, z, out_b, cq, ck, cv, out_d, dvn, a_tail, b_tail) = _sample_pre_call(
        x_tm, lw, tabs, wtab, btab, to_tm(a_conv), to_tm(b_conv), bs)

    s_t = a_state.reshape(bs, A_HEADS, HEAD_DIM * HEAD_DIM).transpose(1, 2, 0)
    o, s_new_t = _sample_delta_call(g, beta, q, k, v, s_t, ts)
    a_state_new = s_new_t.transpose(2, 0, 1).reshape(bs, A_HEADS, HEAD_DIM, HEAD_DIM)

    cq4 = cq.reshape(ts, bs, C_HEADS, HEAD_DIM).transpose(1, 2, 0, 3)
    zq = jnp.zeros_like(cq4[:, 0])
    qm = jnp.concatenate(
        [jnp.concatenate([cq4[:, hh], zq] if hh // 2 == 0 else [zq, cq4[:, hh]], axis=-1) for hh in range(C_HEADS)],
        axis=1)
    pad_new = lambda a: jnp.concatenate([from_tm(a, ts), jnp.zeros((bs, NEW_KEY_ROWS - ts, 128), F32)], axis=1)
    o_att, c_k_new, c_v_new = _sample_attn_call(lw['c_sinks'], qm, c_k_all, pad_new(ck), c_v_all, pad_new(cv),
                                                layer, ts)
    out_c = jnp.concatenate(
        [o_att[:, hh * ts:(hh + 1) * ts, (hh // 2) * HEAD_DIM:(hh // 2 + 1) * HEAD_DIM] for hh in range(C_HEADS)],
        axis=-1)
    out_c = jnp.swapaxes(out_c, 0, 1).reshape(ts * bs, GROUP_WIDTH)
    c_k_new = c_k_new.reshape(bs, WINDOW, C_KV_HEADS, HEAD_DIM)
    c_v_new = c_v_new.reshape(bs, WINDOW, C_KV_HEADS, HEAD_DIM)

    x2 = _sample_post_call(x_tm, o, z, out_b, out_c, out_d, lw)
    new = {'a_state': a_state_new, 'a_conv': from_tm(a_tail, A_CONV - 1), 'b_conv': from_tm(b_tail, B_CONV - 1),
           'c_k': c_k_new, 'c_v': c_v_new, 'd_v': from_tm(dvn, ts)}
    return x2, new


def kernel(x_prompt, x_sample, state_delta, state_delta_conv, state_shortconv, cache_win_k, cache_win_v,
           state_ffn_conv, norm1_g, w_in, a_conv_w, a_log, a_dt_bias, a_norm_g, b_conv_w, c_sinks, d_ln_g,
           d_ln_b, d_ws, d_bias, w_out, norm2_g, ffn_w_gate, ffn_w_up, ffn_conv_w, ffn_w_down, final_norm_g):
    bp, tp, _ = x_prompt.shape
    bs, ts, _ = x_sample.shape
    depth = w_in.shape[0]
    win_buf = cache_win_k.shape[2]
    pos_p = np.arange(tp, dtype=np.int32)
    pos_s = PAST_LEN + np.arange(ts, dtype=np.int32)
    assert win_buf == WINDOW and ts == 4
    tabs_p = _rope_tables(pos_p)
    tabs_s = tuple(jnp.repeat(t, bs, axis=0) for t in _rope_tables(pos_s))
    fng = final_norm_g[None, :]
    ck_all = cache_win_k.reshape(depth, bs, WINDOW, 128)
    cv_all = cache_win_v.reshape(depth, bs, WINDOW, 128)
    P = CARRY_ROWS

    hp = x_prompt
    hs = jnp.swapaxes(x_sample, 0, 1).reshape(ts * bs, D_MODEL)
    outs = {k: [] for k in ('sp', 'ss', 'acp', 'acs', 'bcp', 'bcs', 'ckp', 'cks', 'cvp', 'cvs', 'fcp', 'fcs', 'dv')}
    for l in range(depth):
        lw = _layer_weights(l, norm1_g, w_in, a_conv_w, a_log, a_dt_bias, a_norm_g, b_conv_w, c_sinks, d_ln_g,
                            d_ln_b, d_ws, d_bias, w_out, norm2_g, ffn_w_gate, ffn_w_up, ffn_conv_w, ffn_w_down)
        last = l == depth - 1
        hp, s_bd, acv, bcv, ckn, cvn = _mixer_call(
            hp, lw, tabs_p, jnp.zeros((bp, P, 768), F32), jnp.zeros((bp, P, 256), F32),
            jnp.zeros((bp, 256, 256), F32))
        hp, fcv = _ffn_call(hp, lw, jnp.zeros((bp, P, D_FF), F32), fng, tile=TILE_ROWS, stride=1, final_norm=last)
        outs['sp'].append(_diag_blocks(s_bd))
        outs['acp'].append(acv[:, P - (A_CONV - 1):])
        outs['bcp'].append(bcv[:, P - (B_CONV - 1):])
        outs['ckp'].append(ckn.reshape(bp, WINDOW, C_KV_HEADS, HEAD_DIM))
        outs['cvp'].append(cvn.reshape(bp, WINDOW, C_KV_HEADS, HEAD_DIM))
        outs['fcp'].append(fcv[:, P - (FFN_CONV - 1):])
        hs, ns = _sample_mixer(hs, lw, tabs_s, d_ws[l], d_bias[l], state_delta[l], state_delta_conv[l],
                               state_shortconv[l], ck_all, cv_all, l, bs, ts)
        f0 = jnp.swapaxes(state_ffn_conv[l], 0, 1).reshape(1, (FFN_CONV - 1) * bs, D_FF)
        ys_tm, fcs = _ffn_call(hs[None], lw, f0, fng, tile=ts * bs, stride=bs, final_norm=last)
        hs = ys_tm[0]
        outs['ss'].append(ns['a_state'])
        outs['acs'].append(ns['a_conv'])
        outs['bcs'].append(ns['b_conv'])
        outs['cks'].append(ns['c_k'])
        outs['cvs'].append(ns['c_v'])
        outs['fcs'].append(jnp.swapaxes(fcs.reshape(FFN_CONV - 1, bs, D_FF), 0, 1))
        outs['dv'].append(ns['d_v'])
    st = {k: jnp.stack(v) for k, v in outs.items()}
    hs = jnp.swapaxes(hs.reshape(ts, bs, D_MODEL), 0, 1)
    return (hp, hs, st['sp'], st['ss'], st['acp'], st['acs'], st['bcp'], st['bcs'], st['ckp'], st['cks'],
            st['cvp'], st['cvs'], st['fcp'], st['fcs'], st['dv'])
```

```python
import functools

import jax
import jax.numpy as jnp
import numpy as np
from jax import lax
from jax.experimental import pallas as pl
from jax.experimental.pallas import tpu as pltpu

F32 = jnp.float32
BF16 = jnp.bfloat16

D_MODEL = 1024
GROUP_WIDTH = 256
HEAD_DIM = 64
A_HEADS = 4
A_CONV = 4
A_CHUNK = 64
B_CONV = 3
C_HEADS = 4
C_KV_HEADS = 2
WINDOW = 128
ROPE_DIM = 16
ROPE_THETA = 500000.0
D_GROUPS = 4
D_CHUNK = 128
D_FF = 2816
FFN_CONV = 3
EPS = 1e-6
NEG_INF = -1e30
PAST_LEN = 16384

COL_A = 0
W_A = 1152
COL_B = 1152
W_B = 768
COL_C = 1920
W_C = 512
COL_D = 2432
W_D = 512
P_PACKED = 2944

TILE_ROWS = 512
CARRY_ROWS = 8
FFN_COL_CHUNK = 1408
VMEM_LIMIT_BYTES = 56 * 1024 * 1024
MIXER_VMEM_LIMIT_BYTES = 60 * 1024 * 1024


def _dot(a, b):
    return jnp.dot(a, b, preferred_element_type=F32)


def _dot_nt(a, b):
    return lax.dot_general(a, b, (((1,), (1,)), ((), ())), preferred_element_type=F32)


def _dot_tn(a, b):
    return lax.dot_general(a, b, (((0,), (0,)), ((), ())), preferred_element_type=F32)


def _split3(x):
    hi = x.astype(BF16)
    r1 = x - hi.astype(F32)
    mid = r1.astype(BF16)
    lo = (r1 - mid.astype(F32)).astype(BF16)
    return hi, mid, lo


def _dot_lhs3(x, w01):
    hi, mid, lo = _split3(x)
    return _dot(hi, w01) + _dot(mid, w01) + _dot(lo, w01)


def _head_sumsq(x, ones_bd):
    return _dot((x * x).astype(BF16), ones_bd)


def _sigmoid(x):
    return 1.0 / (1.0 + jnp.exp(-x))


def _silu(x):
    return x * _sigmoid(x)


def _softplus(x):
    return jnp.maximum(x, 0.0) + jnp.log(1.0 + jnp.exp(-jnp.abs(x)))


def _gelu_tanh(x):
    return 0.5 * x * (1.0 + jnp.tanh(np.sqrt(2.0 / np.pi).astype(np.float32) * (x + 0.044715 * (x * x * x))))


def _rms(x, g):
    return x * lax.rsqrt(jnp.mean(x * x, axis=-1, keepdims=True) + EPS) * g


def _iota(shape, dim):
    return lax.broadcasted_iota(jnp.int32, shape, dim)


MIXER_SEQS = 2


def _mixer_kernel(sinks_ref, x_ref, n1g_ref, w_in_ref, aconv_w_ref, alog_ref, dtb_ref, anorm_ref,
                  bconv_w_ref, cos_ref, sina_ref, sinb_ref, lng_ref, lnb_ref, ws_ref, dbias_ref,
                  w_out_ref, aconv0_ref, bconv0_ref, s0_ref,
                  x_out_ref, s_out_ref, aconv_out_ref, bconv_out_ref, ck_out_ref, cv_out_ref,
                  *scratch, tile):
    shared = (sinks_ref, n1g_ref, w_in_ref, aconv_w_ref, alog_ref, dtb_ref, anorm_ref, bconv_w_ref, cos_ref,
              sina_ref, sinb_ref, lng_ref, lnb_ref, ws_ref, dbias_ref, w_out_ref)
    per_seq = (x_ref, aconv0_ref, bconv0_ref, s0_ref, x_out_ref, s_out_ref, aconv_out_ref, bconv_out_ref,
               ck_out_ref, cv_out_ref)
    n_scr = len(scratch) // MIXER_SEQS
    gens = [_mixer_seq(*shared, *(r.at[b] for r in per_seq), *scratch[b * n_scr:(b + 1) * n_scr], tile=tile)
            for b in range(MIXER_SEQS)]
    started, live = 1, list(gens[:1])
    while live:
        for gen in list(live):
            if next(gen, 'done') == 'done':
                live.remove(gen)
        if started < len(gens):
            live.append(gens[started])
            started += 1


def _mixer_seq(sinks_ref, n1g_ref, w_in_ref, aconv_w_ref, alog_ref, dtb_ref, anorm_ref, bconv_w_ref, cos_ref,
               sina_ref, sinb_ref, lng_ref, lnb_ref, ws_ref, dbias_ref, w_out_ref,
               x_ref, aconv0_ref, bconv0_ref, s0_ref, x_out_ref, s_out_ref, aconv_out_ref, bconv_out_ref,
               ck_out_ref, cv_out_ref,
               abuf, bbuf, kbuf, vbuf, q_s, k_s, v_s, gcb_s, bb_s, o_s, s_scr, proj_s, *, tile):
    T = tile
    si = pl.program_id(1)
    P = CARRY_ROWS

    @pl.when(si == 0)
    def _():
        abuf[0:P, :] = aconv0_ref[...]
        bbuf[0:P, :] = bconv0_ref[...]
        kbuf[0:WINDOW, :] = jnp.zeros((WINDOW, 128), F32)
        vbuf[0:WINDOW, :] = jnp.zeros((WINDOW, 128), F32)
        s_scr[...] = s0_ref[...]

    res = {}
    h = _rms(x_ref[...], n1g_ref[...]).astype(BF16)
    for lo, width in ((COL_A, W_A), (COL_B, W_B), (COL_C, W_C), (COL_D, W_D)):
        proj_s[:, lo:lo + width] = _dot(h, w_in_ref[:, lo:lo + width])
    yield

    small = proj_s[:, 1024:1152]
    abuf[P:P + T, :] = proj_s[:, 0:768]

    def group_b():
        bbuf[P:P + T, :] = proj_s[:, COL_B + 256:COL_B + 512] * proj_s[:, COL_B + 512:COL_B + 768]
        bx = (bbuf[P:P + T, :] * bconv_w_ref[2:3, :] + bbuf[P - 1:P - 1 + T, :] * bconv_w_ref[1:2, :]
              + bbuf[P - 2:P - 2 + T, :] * bconv_w_ref[0:1, :])
        tail_b = bbuf[T:T + P, :]
        bbuf[0:P, :] = tail_b
        bconv_out_ref[...] = tail_b
        res['out_b'] = proj_s[:, COL_B:COL_B + 256] * bx

    def group_c_rope():
        cos = cos_ref[...]
        sina = sina_ref[...]
        sinb = sinb_ref[...]
        cq = proj_s[:, COL_C:COL_C + 256]
        res['cq'] = (cq * jnp.concatenate([cos, cos], axis=1)
                     + pltpu.roll(cq, 256 - ROPE_DIM // 2, 1) * jnp.concatenate([sina, sina], axis=1)
                     + pltpu.roll(cq, ROPE_DIM // 2, 1) * jnp.concatenate([sinb, sinb], axis=1))
        ck = proj_s[:, COL_C + 256:COL_C + 384]
        ck = (ck * cos + pltpu.roll(ck, 128 - ROPE_DIM // 2, 1) * sina
              + pltpu.roll(ck, ROPE_DIM // 2, 1) * sinb)
        kbuf[WINDOW:WINDOW + T, :] = ck
        vbuf[WINDOW:WINDOW + T, :] = proj_s[:, COL_C + 384:COL_C + 512]
        ck_out_ref[...] = kbuf[T:T + WINDOW, :]
        cv_out_ref[...] = vbuf[T:T + WINDOW, :]

    def group_d_norm():
        res['du'] = _gelu_tanh(proj_s[:, COL_D:COL_D + 256])
        gv = _gelu_tanh(proj_s[:, COL_D + 256:COL_D + 512])
        mu = jnp.mean(gv, axis=-1, keepdims=True)
        xc = gv - mu
        res['dvn'] = (xc * lax.rsqrt(jnp.mean(xc * xc, axis=-1, keepdims=True) + EPS) * lng_ref[...]
                      + lnb_ref[...])
        wr = _iota((D_GROUPS * D_CHUNK, D_CHUNK), 0) & (D_CHUNK - 1)
        wc = _iota((D_GROUPS * D_CHUNK, D_CHUNK), 1)
        res['wm'] = jnp.where(wr >= wc, ws_ref[...], 0.0).astype(BF16)

    y = (abuf[P:P + T, :] * aconv_w_ref[3:4, :] + abuf[P - 1:P - 1 + T, :] * aconv_w_ref[2:3, :]
         + abuf[P - 2:P - 2 + T, :] * aconv_w_ref[1:2, :] + abuf[P - 3:P - 3 + T, :] * aconv_w_ref[0:1, :])
    tail_a = abuf[T:T + P, :]
    abuf[0:P, :] = tail_a
    aconv_out_ref[...] = tail_a
    qkv = _silu(y)
    q_raw = qkv[:, 0:256]
    k_raw = qkv[:, 256:512]
    v_s[...] = qkv[:, 512:768]

    r256 = _iota((256, 256), 0) >> 6
    c256 = _iota((256, 256), 1) >> 6
    mask_bd = r256 == c256
    ones_bd = jnp.where(mask_bd, 1.0, 0.0).astype(BF16)
    q_s[...] = q_raw * lax.rsqrt(_head_sumsq(q_raw, ones_bd) + EPS) * (HEAD_DIM ** -0.5)
    k_s[...] = k_raw * lax.rsqrt(_head_sumsq(k_raw, ones_bd) + EPS)
    yield

    g_log = -jnp.exp(alog_ref[...]) * _softplus(small + dtb_ref[...])
    beta = _sigmoid(small)
    gbeta = jnp.where(_iota((T, 128), 1) < A_HEADS, g_log, beta)
    expand = jnp.where(_iota((128, 512), 0) == (_iota((128, 512), 1) >> 6), 1.0, 0.0).astype(BF16)
    gbb = _dot_lhs3(gbeta, expand)
    bb_s[...] = gbb[:, 256:512]

    def chunk_cumsum(xv):
        row_in_chunk = _iota(xv.shape, 0) & (A_CHUNK - 1)
        step = 1
        while step < A_CHUNK:
            xv = xv + jnp.where(row_in_chunk >= step, pltpu.roll(xv, step, 0), 0.0)
            step *= 2
        return xv

    gcb_s[...] = chunk_cumsum(gbb[:, 0:256])
    gct = chunk_cumsum(g_log).T[0:8, :]
    low_half = (_iota((1, T), 1) & A_CHUNK) == 0
    gct_r = pltpu.roll(gct, A_CHUNK, 1)
    gct_l = pltpu.roll(gct, T - A_CHUNK, 1)
    even_rows = [jnp.where(low_half, gct[a:a + 1, :], gct_r[a + 1:a + 2, :]) for a in (0, 2)]
    odd_rows = [jnp.where(low_half, gct_l[a:a + 1, :], gct[a + 1:a + 2, :]) for a in (0, 2)]
    yield

    ri = _iota((A_CHUNK, 256), 0)
    ci = _iota((A_CHUNK, 256), 1) & (A_CHUNK - 1)
    causal_t = ri >= ci
    strict_t = ri > ci
    eye_t = jnp.where(ri == ci, 1.0, 0.0)

    def bd16(x16):
        return jnp.concatenate([x16, x16, x16, x16], axis=0) * ones_bd

    n_chunks = T // A_CHUNK
    t_inv, pw, qk, qdec, e_tail, vb16, kbe16 = [], [], [], [], [], [], []
    for c in range(n_chunks):
        rows = slice(c * A_CHUNK, (c + 1) * A_CHUNK)
        blk = slice((c // 2) * 128, (c // 2) * 128 + 128)
        src = even_rows if c % 2 == 0 else odd_rows
        gc_row = jnp.concatenate([src[0][:, blk], src[1][:, blk]], axis=1)
        qc = q_s[rows, :]
        kc = k_s[rows, :]
        bbc = bb_s[rows, :]
        gcb = gcb_s[rows, :]
        decay = jnp.where(causal_t, jnp.exp(jnp.where(causal_t, gcb - gc_row, 0.0)), 0.0)
        eg = jnp.exp(gcb)
        kb = kc * bbc
        aq = _dot_nt(jnp.concatenate([kb, qc], axis=0).astype(BF16), bd16(kc.astype(BF16)))
        a_mat = jnp.where(strict_t, aq[0:A_CHUNK] * decay, 0.0)
        qk.append(aq[A_CHUNK:2 * A_CHUNK] * decay)
        t_inv.append(eye_t - a_mat)
        pw.append(a_mat)
        qdec.append(qc * eg)
        e_tail.append(jnp.exp(gcb[A_CHUNK - 1:A_CHUNK, :] - gcb))
        vb16.append((v_s[rows, :] * bbc).astype(BF16))
        kbe16.append((kb * eg).astype(BF16))
    yield
    u, w = [], []
    state = {'s': s_scr[...]}

    def level_first():
        for c in range(n_chunks):
            p16 = pw[c].astype(BF16)
            pw[c] = _dot(p16, bd16(p16))

    def level_mid():
        for c in range(n_chunks):
            p16 = pw[c].astype(BF16)
            res = _dot(jnp.concatenate([p16, t_inv[c].astype(BF16)], axis=0), bd16(p16))
            pw[c] = res[0:A_CHUNK]
            t_inv[c] = t_inv[c] + res[A_CHUNK:2 * A_CHUNK]

    def level_last():
        for c in range(n_chunks):
            t_c = t_inv[c] + _dot(t_inv[c].astype(BF16), bd16(pw[c].astype(BF16)))
            uw = _dot(t_c.astype(BF16), jnp.concatenate([bd16(vb16[c]), bd16(kbe16[c])], axis=1))
            u.append(uw[:, 0:256])
            w.append(uw[:, 256:512])

    def scan_step(c):
        rows = slice(c * A_CHUNK, (c + 1) * A_CHUNK)
        s_bd = state['s']
        wq = _dot(jnp.concatenate([w[c], qdec[c]], axis=0).astype(BF16), s_bd.astype(BF16))
        v_new = u[c] - wq[0:A_CHUNK]
        o_s[rows, :] = wq[A_CHUNK:2 * A_CHUNK] + _dot(qk[c].astype(BF16), bd16(v_new.astype(BF16)))
        kv = _dot_tn(k_s[rows, :].astype(BF16), (v_new * e_tail[c]).astype(BF16))
        g_last = gcb_s[(c + 1) * A_CHUNK - 1:(c + 1) * A_CHUNK, :]
        state['s'] = s_bd * jnp.exp(g_last) + jnp.where(mask_bd, kv, 0.0)

    chain = [level_first] + [level_mid] * 4 + [level_last] + [functools.partial(scan_step, c) for c in range(n_chunks)]

    lane128 = _iota((2 * WINDOW, 128), 1)
    low = _iota((WINDOW, 128), 1) < HEAD_DIM
    qrow = _iota((2 * WINDOW, 2 * WINDOW), 0) & (WINDOW - 1)
    kcol = _iota((2 * WINDOW, 2 * WINDOW), 1)
    band = (kcol > qrow) & (kcol <= qrow + WINDOW)
    top_half = _iota((2 * WINDOW, 1), 0) < WINDOW
    out_c_blocks = []

    def group_c_block(n):
        first_key = si * T + (n - 1) * WINDOW
        valid = band & (kcol + first_key >= 0)
        kwin = kbuf[n * WINDOW:(n + 2) * WINDOW, :]
        vwin = vbuf[n * WINDOW:(n + 2) * WINDOW, :]
        k_sw = pltpu.roll(kwin, HEAD_DIM, 1)
        v_sw = pltpu.roll(vwin, HEAD_DIM, 1)
        pair_out = []
        for g in range(C_KV_HEADS):
            own = (lane128 < HEAD_DIM) if g == 0 else (lane128 >= HEAD_DIM)
            k_dup = jnp.where(own, kwin, k_sw).astype(BF16)
            v_dup = jnp.where(own, vwin, v_sw).astype(BF16)
            qp = res['cq'][n * WINDOW:(n + 1) * WINDOW, g * 128:(g + 1) * 128]
            q_st = jnp.concatenate([jnp.where(low, qp, 0.0), jnp.where(low, 0.0, qp)], axis=0).astype(BF16)
            s = _dot_nt(q_st, k_dup) * (HEAD_DIM ** -0.5)
            s = jnp.where(valid, s, NEG_INF)
            sink = jnp.where(top_half, sinks_ref[2 * g], sinks_ref[2 * g + 1])
            m = jnp.maximum(jnp.max(s, axis=-1, keepdims=True), sink)
            p = jnp.exp(s - m)
            denom = jnp.sum(p, axis=-1, keepdims=True) + jnp.exp(sink - m)
            o2 = _dot((p / denom).astype(BF16), v_dup)
            pair_out.append(jnp.where(low, o2[0:WINDOW], o2[WINDOW:2 * WINDOW]))
        out_c_blocks.append(jnp.concatenate(pair_out, axis=1))
        if n == T // WINDOW - 1:
            kbuf[0:WINDOW, :] = kbuf[T:T + WINDOW, :]
            vbuf[0:WINDOW, :] = vbuf[T:T + WINDOW, :]

    lane_grp = _iota((D_CHUNK, 256), 1) >> 6
    out_d_blocks = []

    def group_d_block(n):
        mx = _dot(res['wm'], res['dvn'][n * D_CHUNK:(n + 1) * D_CHUNK, :].astype(BF16))
        mixed = dbias_ref[...]
        for grp in range(D_GROUPS):
            mixed = mixed + jnp.where(lane_grp == grp, mx[grp * D_CHUNK:(grp + 1) * D_CHUNK, :], 0.0)
        out_d_blocks.append(res['du'][n * D_CHUNK:(n + 1) * D_CHUNK, :] * mixed)

    c_blocks = [functools.partial(group_c_block, n) for n in range(T // WINDOW)]
    d_blocks = [functools.partial(group_d_block, n) for n in range(T // D_CHUNK)]
    fill = [group_b, group_c_rope, group_d_norm] + c_blocks + d_blocks
    for i in range(max(len(chain), len(fill))):
        if i < len(chain):
            chain[i]()
        if i < len(fill):
            fill[i]()
        yield
    out_b = res['out_b']
    out_c = jnp.concatenate(out_c_blocks, axis=0)
    out_d = jnp.concatenate(out_d_blocks, axis=0)

    s_scr[...] = state['s']
    s_out_ref[...] = state['s']
    o = o_s[...]
    out_a = (o * lax.rsqrt(_head_sumsq(o, ones_bd) * (1.0 / HEAD_DIM) + EPS) * anorm_ref[...]
             * _silu(proj_s[:, 768:1024]))

    cat = jnp.concatenate([out_a, out_b, out_c, out_d], axis=1).astype(BF16)
    x_out_ref[...] = x_ref[...] + _dot(cat, w_out_ref[...])


def _mixer_call(x, lw, tabs, aconv0, bconv0, s0):
    bsz, seq, _ = x.shape
    T = TILE_ROWS
    ns = seq // T
    NB = MIXER_SEQS
    P = CARRY_ROWS
    full = lambda shape: pl.BlockSpec(shape, lambda b, s, *_: (0,) * len(shape), pipeline_mode=pl.Buffered(1))
    per_b = lambda shape: pl.BlockSpec((NB,) + shape, lambda b, s, *_: (b,) + (0,) * len(shape))
    tab = pl.BlockSpec((T, 128), lambda b, s, *_: (s, 0))
    seq_scratch = [pltpu.VMEM(shape, F32) for shape in (
        (T + P, 768), (T + P, 256), (T + WINDOW, 128), (T + WINDOW, 128), (T, 256), (T, 256), (T, 256),
        (T, 256), (T, 256), (T, 256), (256, 256), (T, P_PACKED))]
    grid_spec = pltpu.PrefetchScalarGridSpec(
        num_scalar_prefetch=1,
        grid=(bsz // NB, ns),
        in_specs=[
            pl.BlockSpec((NB, T, D_MODEL), lambda b, s, *_: (b, s, 0)),
            full((1, D_MODEL)), full((D_MODEL, P_PACKED)), full((A_CONV, 768)), full((1, 128)), full((1, 128)),
            full((1, 256)), full((B_CONV, 256)), tab, tab, tab, full((1, 256)), full((1, 256)),
            full((D_GROUPS * D_CHUNK, D_CHUNK)), full((D_CHUNK, 256)), full((D_MODEL, D_MODEL)),
            per_b((P, 768)), per_b((P, 256)), per_b((256, 256)),
        ],
        out_specs=[
            pl.BlockSpec((NB, T, D_MODEL), lambda b, s, *_: (b, s, 0)),
            per_b((256, 256)), per_b((P, 768)), per_b((P, 256)), per_b((WINDOW, 128)), per_b((WINDOW, 128)),
        ],
        scratch_shapes=seq_scratch * NB,
    )
    out_shape = [
        jax.ShapeDtypeStruct((bsz, seq, D_MODEL), F32),
        jax.ShapeDtypeStruct((bsz, 256, 256), F32),
        jax.ShapeDtypeStruct((bsz, P, 768), F32),
        jax.ShapeDtypeStruct((bsz, P, 256), F32),
        jax.ShapeDtypeStruct((bsz, WINDOW, 128), F32),
        jax.ShapeDtypeStruct((bsz, WINDOW, 128), F32),
    ]
    return pl.pallas_call(
        functools.partial(_mixer_kernel, tile=T),
        grid_spec=grid_spec,
        out_shape=out_shape,
        compiler_params=pltpu.CompilerParams(
            dimension_semantics=("arbitrary", "arbitrary"), vmem_limit_bytes=MIXER_VMEM_LIMIT_BYTES),
        name="mixer",
    )(lw['c_sinks'], x, lw['norm1_g'], lw['w_in'], lw['a_conv_w'], lw['a_log'], lw['a_dt_bias'],
      lw['a_norm_g'], lw['b_conv_w'], tabs[0], tabs[1], tabs[2], lw['d_ln_g'], lw['d_ln_b'],
      lw['d_ws'], lw['d_bias'], lw['w_out'], aconv0, bconv0, s0)


def _ffn_kernel(x_ref, n2g_ref, wg_ref, wu_ref, cw_ref, wd_ref, fconv0_ref, fng_ref,
                x_out_ref, fconv_out_ref, gbuf, *, tile, stride, final_norm):
    T = tile
    P = (FFN_CONV - 1) * stride if stride > 1 else CARRY_ROWS
    si = pl.program_id(1)

    @pl.when(si == 0)
    def _():
        gbuf[0:P, :] = fconv0_ref[0]

    x = x_ref[0]
    h = _rms(x, n2g_ref[...]).astype(BF16)
    acc = x
    for c0 in range(0, D_FF, FFN_COL_CHUNK):
        cols = slice(c0, c0 + FFN_COL_CHUNK)
        gbuf[P:P + T, cols] = _dot(h, wg_ref[:, cols])
        up = _dot(h, wu_ref[:, cols])
        gate = (gbuf[P:P + T, cols] * cw_ref[2:3, cols] + gbuf[P - stride:P - stride + T, cols] * cw_ref[1:2, cols]
                + gbuf[P - 2 * stride:P - 2 * stride + T, cols] * cw_ref[0:1, cols])
        acc = acc + _dot((_silu(gate) * up).astype(BF16), wd_ref[cols, :])
    tail = gbuf[T:T + P, :]
    gbuf[0:P, :] = tail
    fconv_out_ref[0] = tail
    if final_norm:
        acc = _rms(acc, fng_ref[...])
    x_out_ref[0] = acc


def _ffn_call(x, lw, fconv0, final_g, *, tile, stride, final_norm):
    bsz, seq, _ = x.shape
    T = tile
    P = (FFN_CONV - 1) * stride if stride > 1 else CARRY_ROWS
    full = lambda shape: pl.BlockSpec(shape, lambda b, s: (0,) * len(shape), pipeline_mode=pl.Buffered(1))
    per_b = lambda shape: pl.BlockSpec((1,) + shape, lambda b, s: (b,) + (0,) * len(shape))
    return pl.pallas_call(
        functools.partial(_ffn_kernel, tile=T, stride=stride, final_norm=final_norm),
        grid=(bsz, seq // T),
        in_specs=[
            pl.BlockSpec((1, T, D_MODEL), lambda b, s: (b, s, 0)),
            full((1, D_MODEL)), full((D_MODEL, D_FF)), full((D_MODEL, D_FF)), full((FFN_CONV, D_FF)),
            full((D_FF, D_MODEL)), per_b((P, D_FF)), full((1, D_MODEL)),
        ],
        out_specs=[pl.BlockSpec((1, T, D_MODEL), lambda b, s: (b, s, 0)), per_b((P, D_FF))],
        out_shape=[jax.ShapeDtypeStruct((bsz, seq, D_MODEL), F32), jax.ShapeDtypeStruct((bsz, P, D_FF), F32)],
        scratch_shapes=[pltpu.VMEM((T + P, D_FF), F32)],
        compiler_params=pltpu.CompilerParams(
            dimension_semantics=("arbitrary", "arbitrary"), vmem_limit_bytes=VMEM_LIMIT_BYTES),
        name="ffn",
    )(x, lw['norm2_g'], lw['ffn_w_gate'], lw['ffn_w_up'], lw['ffn_conv_w'], lw['ffn_w_down'], fconv0, final_g)


CAST_ROWS = 256


def _cast_kernel(w_ref, o_ref):
    o_ref[...] = w_ref[0].astype(BF16)


def _layer_bf16(w_all, layer):
    _, rows, cols = w_all.shape
    return pl.pallas_call(
        _cast_kernel,
        grid=(rows // CAST_ROWS,),
        in_specs=[pl.BlockSpec((1, CAST_ROWS, cols), lambda i: (layer, i, 0))],
        out_specs=pl.BlockSpec((CAST_ROWS, cols), lambda i: (i, 0)),
        out_shape=jax.ShapeDtypeStruct((rows, cols), BF16),
        name="cast_bf16",
    )(w_all)


def _pack_w_in(w):
    pad = jnp.zeros((D_MODEL, 128 - 2 * A_HEADS), w.dtype)
    small = jnp.concatenate([w[:, 1028:1032], w[:, 1024:1028], pad], axis=1)
    return jnp.concatenate([w[:, 0:1024], small, w[:, 1032:]], axis=1).astype(BF16)


def _pad_lanes(v, width=128):
    return jnp.concatenate([v, jnp.zeros((width - v.shape[0],), v.dtype)])[None, :]


def _layer_weights(l, norm1_g, w_in, a_conv_w, a_log, a_dt_bias, a_norm_g, b_conv_w, c_sinks, d_ln_g, d_ln_b,
                   d_ws, d_bias, w_out, norm2_g, ffn_w_gate, ffn_w_up, ffn_conv_w, ffn_w_down):
    bias_tab = jnp.broadcast_to(d_bias[l].T[:, :, None], (D_CHUNK, D_GROUPS, HEAD_DIM)).reshape(D_CHUNK, 256)
    return {
        'norm1_g': norm1_g[l][None, :], 'w_in': _pack_w_in(w_in[l]), 'a_conv_w': a_conv_w[l],
        'a_log': _pad_lanes(a_log[l]), 'a_dt_bias': _pad_lanes(a_dt_bias[l]),
        'a_norm_g': jnp.tile(a_norm_g[l], A_HEADS)[None, :], 'b_conv_w': b_conv_w[l], 'c_sinks': c_sinks[l],
        'd_ln_g': d_ln_g[l][None, :], 'd_ln_b': d_ln_b[l][None, :],
        'd_ws': d_ws[l].reshape(D_GROUPS * D_CHUNK, D_CHUNK), 'd_bias': bias_tab,
        'w_out': _layer_bf16(w_out, l), 'norm2_g': norm2_g[l][None, :],
        'ffn_w_gate': _layer_bf16(ffn_w_gate, l), 'ffn_w_up': _layer_bf16(ffn_w_up, l),
        'ffn_conv_w': ffn_conv_w[l], 'ffn_w_down': _layer_bf16(ffn_w_down, l),
    }


def _rope_tables(pos):
    half = ROPE_DIM // 2
    inv = np.power(ROPE_THETA, -np.arange(half, dtype=np.float64) * (2.0 / ROPE_DIM))
    ang = pos.astype(np.float64)[:, None] * inv[None, :]
    cos, sin = jnp.asarray(np.cos(ang), F32), jnp.asarray(np.sin(ang), F32)
    n = pos.shape[0]
    rest = HEAD_DIM - ROPE_DIM
    cos_h = jnp.concatenate([cos, cos, jnp.ones((n, rest), F32)], axis=1)
    sina_h = jnp.concatenate([-sin, jnp.zeros((n, half + rest), F32)], axis=1)
    sinb_h = jnp.concatenate([jnp.zeros((n, half), F32), sin, jnp.zeros((n, rest), F32)], axis=1)
    return tuple(jnp.concatenate([t, t], axis=1) for t in (cos_h, sina_h, sinb_h))


def _diag_blocks(s_bd):
    return jnp.stack([s_bd[:, HEAD_DIM * i:HEAD_DIM * (i + 1), HEAD_DIM * i:HEAD_DIM * (i + 1)]
                      for i in range(A_HEADS)], axis=1)


def _sample_pre_kernel(x_ref, n1g_ref, w_in_ref, aconv_w_ref, alog_ref, dtb_ref, bconv_w_ref, cos_ref, sina_ref,
                       sinb_ref, lng_ref, lnb_ref, wtab_ref, btab_ref, aconv0_ref, bconv0_ref,
                       q_ref, k_ref, v_ref, g_ref, beta_ref, z_ref, outb_ref, cq_ref, ck_ref, cv_ref, outd_ref,
                       dvn_ref, aconv_out_ref, bconv_out_ref, abuf, bbuf, *, rows, stride):
    R, S = rows, stride
    steps = R // S
    pa_rows = (A_CONV - 1) * S
    pb_rows = (B_CONV - 1) * S
    x = x_ref[...]
    h = _rms(x, n1g_ref[...]).astype(BF16)

    pa = _dot(h, w_in_ref[:, COL_A:COL_A + W_A])
    z_ref[...] = pa[:, 768:1024]
    small = pa[:, 1024:1152]
    abuf[0:pa_rows, :] = aconv0_ref[...]
    abuf[pa_rows:pa_rows + R, :] = pa[:, 0:768]
    y = abuf[0:R, :] * aconv_w_ref[0:1, :]
    for j in range(1, A_CONV):
        y = y + abuf[j * S:j * S + R, :] * aconv_w_ref[j:j + 1, :]
    aconv_out_ref[...] = abuf[R:R + pa_rows, :]
    qkv = _silu(y)
    q_raw = qkv[:, 0:256]
    k_raw = qkv[:, 256:512]
    v_ref[...] = qkv[:, 512:768]
    mask_bd = (_iota((256, 256), 0) >> 6) == (_iota((256, 256), 1) >> 6)
    ones_bd = jnp.where(mask_bd, 1.0, 0.0).astype(BF16)
    q_ref[...] = q_raw * lax.rsqrt(_dot_lhs3(q_raw * q_raw, ones_bd) + EPS) * (HEAD_DIM ** -0.5)
    k_ref[...] = k_raw * lax.rsqrt(_dot_lhs3(k_raw * k_raw, ones_bd) + EPS)
    g_ref[...] = -jnp.exp(alog_ref[...]) * _softplus(small + dtb_ref[...])
    beta_ref[...] = _sigmoid(small)

    pb = _dot(h, w_in_ref[:, COL_B:COL_B + W_B])
    bbuf[0:pb_rows, :] = bconv0_ref[...]
    bbuf[pb_rows:pb_rows + R, :] = pb[:, 256:512] * pb[:, 512:768]
    bx = bbuf[0:R, :] * bconv_w_ref[0:1, :]
    for j in range(1, B_CONV):
        bx = bx + bbuf[j * S:j * S + R, :] * bconv_w_ref[j:j + 1, :]
    bconv_out_ref[...] = bbuf[R:R + pb_rows, :]
    outb_ref[...] = pb[:, 0:256] * bx

    pc = _dot(h, w_in_ref[:, COL_C:COL_C + W_C])
    cos = cos_ref[...]
    sina = sina_ref[...]
    sinb = sinb_ref[...]
    cq = pc[:, 0:256]
    cq_ref[...] = (cq * jnp.concatenate([cos, cos], axis=1)
                   + pltpu.roll(cq, 256 - ROPE_DIM // 2, 1) * jnp.concatenate([sina, sina], axis=1)
                   + pltpu.roll(cq, ROPE_DIM // 2, 1) * jnp.concatenate([sinb, sinb], axis=1))
    ck = pc[:, 256:384]
    ck_ref[...] = (ck * cos + pltpu.roll(ck, 128 - ROPE_DIM // 2, 1) * sina
                   + pltpu.roll(ck, ROPE_DIM // 2, 1) * sinb)
    cv_ref[...] = pc[:, 384:512]

    pd = _dot(h, w_in_ref[:, COL_D:COL_D + W_D])
    du = _gelu_tanh(pd[:, 0:256])
    gv = _gelu_tanh(pd[:, 256:512])
    mu = jnp.mean(gv, axis=-1, keepdims=True)
    xc = gv - mu
    dvn = xc * lax.rsqrt(jnp.mean(xc * xc, axis=-1, keepdims=True) + EPS) * lng_ref[...] + lnb_ref[...]
    dvn_ref[...] = dvn
    for t in range(steps):
        mixed = btab_ref[t:t + 1, :]
        for s in range(t + 1):
            mixed = mixed + wtab_ref[t * steps + s:t * steps + s + 1, :] * dvn[s * S:(s + 1) * S, :]
        outd_ref[t * S:(t + 1) * S, :] = du[t * S:(t + 1) * S, :] * mixed


def _sample_pre_call(x, lw, tabs, wtab, btab, aconv0, bconv0, stride):
    R = x.shape[0]
    f = lambda *shape: jax.ShapeDtypeStruct(shape, F32)
    out_shape = [f(R, 256), f(R, 256), f(R, 256), f(R, 128), f(R, 128), f(R, 256), f(R, 256), f(R, 256),
                 f(R, 128), f(R, 128), f(R, 256), f(R, 256), f((A_CONV - 1) * stride, 768),
                 f((B_CONV - 1) * stride, 256)]
    return pl.pallas_call(
        functools.partial(_sample_pre_kernel, rows=R, stride=stride),
        out_shape=out_shape,
        scratch_shapes=[pltpu.VMEM((R + (A_CONV - 1) * stride, 768), F32),
                        pltpu.VMEM((R + (B_CONV - 1) * stride, 256), F32)],
        compiler_params=pltpu.CompilerParams(vmem_limit_bytes=VMEM_LIMIT_BYTES),
        name="sample_pre",
    )(x, lw['norm1_g'], lw['w_in'], lw['a_conv_w'], lw['a_log'], lw['a_dt_bias'], lw['b_conv_w'],
      tabs[0], tabs[1], tabs[2], lw['d_ln_g'], lw['d_ln_b'], wtab, btab, aconv0, bconv0)


def _sample_delta_kernel(g_ref, beta_ref, q_ref, k_ref, v_ref, s_ref, o_ref, snew_ref,
                         qt_s, kt_s, vt_s, gt_s, bt_s, ot_s, st_s, *, steps, nb):
    h = pl.program_id(0)
    n_i = HEAD_DIM
    zeros = jnp.zeros((HEAD_DIM, nb), F32)

    @pl.when(h == 0)
    def _():
        for t in range(steps):
            rows = slice(t * nb, (t + 1) * nb)
            qt_s[t] = q_ref[rows, :].T
            kt_s[t] = k_ref[rows, :].T
            vt_s[t] = v_ref[rows, :].T
            gt_s[t] = g_ref[rows, :].T
            bt_s[t] = beta_ref[rows, :].T

    st_s[...] = s_ref[0].T
    base = pl.multiple_of(h * HEAD_DIM, HEAD_DIM)
    head_rows = pl.ds(base, HEAD_DIM)

    def rows_of(i):
        return pl.ds(pl.multiple_of(i * HEAD_DIM, HEAD_DIM), HEAD_DIM)

    def decay(t):
        return jnp.exp(gt_s[t, pl.ds(h, 1), :])

    dec0 = decay(0)

    def first_pass(i, acc):
        return acc + kt_s[0, pl.ds(base + i, 1), :] * (st_s[rows_of(i), :] * dec0)

    ks = lax.fori_loop(0, n_i, first_pass, zeros)
    for t in range(steps):
        dec = decay(t)
        v_new = bt_s[t, pl.ds(A_HEADS + h, 1), :] * (vt_s[t, head_rows, :] - ks)
        dec_next = decay(t + 1) if t + 1 < steps else None

        def update(i, carry, t=t, dec=dec, v_new=v_new, dec_next=dec_next):
            o_acc, ks_acc = carry
            blk = st_s[rows_of(i), :] * dec + kt_s[t, pl.ds(base + i, 1), :] * v_new
            st_s[rows_of(i), :] = blk
            o_acc = o_acc + qt_s[t, pl.ds(base + i, 1), :] * blk
            if dec_next is not None:
                ks_acc = ks_acc + kt_s[t + 1, pl.ds(base + i, 1), :] * (blk * dec_next)
            return o_acc, ks_acc

        o_acc, ks = lax.fori_loop(0, n_i, update, (zeros, zeros))
        ot_s[t, head_rows, :] = o_acc
    snew_ref[...] = st_s[...].T

    @pl.when(h == A_HEADS - 1)
    def _():
        for t in range(steps):
            o_ref[t * nb:(t + 1) * nb, :] = ot_s[t].T


def _sample_delta_call(g, beta, q, k, v, s_all, layer, steps):
    _, nb, width = s_all.shape
    nh = A_HEADS
    rows = steps * nb
    whole = lambda width: pl.BlockSpec((rows, width), lambda h: (0, 0))
    head_state = HEAD_DIM * HEAD_DIM
    wide = pltpu.VMEM((steps, GROUP_WIDTH, nb), F32)
    narrow = pltpu.VMEM((steps, 128, nb), F32)
    return pl.pallas_call(
        functools.partial(_sample_delta_kernel, steps=steps, nb=nb),
        grid=(nh,),
        in_specs=[whole(128), whole(128), whole(GROUP_WIDTH), whole(GROUP_WIDTH), whole(GROUP_WIDTH),
                  pl.BlockSpec((1, nb, head_state), lambda h: (layer, 0, h))],
        out_specs=[whole(GROUP_WIDTH), pl.BlockSpec((nb, head_state), lambda h: (0, h))],
        out_shape=[jax.ShapeDtypeStruct((rows, GROUP_WIDTH), F32), jax.ShapeDtypeStruct((nb, width), F32)],
        scratch_shapes=[wide, wide, wide, narrow, narrow, wide, pltpu.VMEM((head_state, nb), F32)],
        compiler_params=pltpu.CompilerParams(dimension_semantics=("arbitrary",),
                                             vmem_limit_bytes=VMEM_LIMIT_BYTES),
        name="sample_delta",
    )(g, beta, q, k, v, s_all)


SAMPLE_ATTN_BLOCK = 8
NEW_KEY_ROWS = 8


def _sample_attn_kernel(sinks_ref, qm_ref, kc_ref, kn_ref, vc_ref, vn_ref, o_ref, k_out_ref, v_out_ref, *, steps):
    nq = C_HEADS * steps
    nk = WINDOW + NEW_KEY_ROWS
    BB = SAMPLE_ATTN_BLOCK
    row = _iota((BB * nq, nk), 0)
    col = _iota((BB * nq, nk), 1)
    t_q = row & (steps - 1)
    valid = ((col < WINDOW) & (col > t_q)) | ((col >= WINDOW) & (col - WINDOW <= t_q))
    head = (_iota((BB * nq, 1), 0) >> 2) & (C_HEADS - 1)
    sink = jnp.where(head == 0, sinks_ref[0],
                     jnp.where(head == 1, sinks_ref[1], jnp.where(head == 2, sinks_ref[2], sinks_ref[3])))
    scores = [_dot_nt(qm_ref[b].astype(BF16), jnp.concatenate([kc_ref[0, b], kn_ref[b]], axis=0).astype(BF16))
              for b in range(BB)]
    s = jnp.where(valid, jnp.concatenate(scores, axis=0) * (HEAD_DIM ** -0.5), NEG_INF)
    m = jnp.maximum(jnp.max(s, axis=-1, keepdims=True), sink)
    p = jnp.exp(s - m)
    denom = jnp.sum(p, axis=-1, keepdims=True) + jnp.exp(sink - m)
    p16 = (p / denom).astype(BF16)
    is_new_row = _iota((WINDOW, 128), 0) >= WINDOW - steps
    pad_rows = jnp.zeros((WINDOW - NEW_KEY_ROWS, 128), F32)

    def slide(cache, new_rows):
        kept = pltpu.roll(cache, WINDOW - steps, 0)
        tail = jnp.concatenate([pad_rows, pltpu.roll(new_rows, NEW_KEY_ROWS - steps, 0)], axis=0)
        return jnp.where(is_new_row, tail, kept)

    for b in range(BB):
        v_all = jnp.concatenate([vc_ref[0, b], vn_ref[b]], axis=0).astype(BF16)
        o_ref[b] = _dot(p16[b * nq:(b + 1) * nq, :], v_all)
        k_out_ref[b] = slide(kc_ref[0, b], kn_ref[b])
        v_out_ref[b] = slide(vc_ref[0, b], vn_ref[b])


def _sample_attn_call(sinks, qm, kc_all, kn, vc_all, vn, layer, steps):
    bs, nq, _ = qm.shape
    BB = SAMPLE_ATTN_BLOCK
    blk = lambda r: pl.BlockSpec((BB, r, 128), lambda i, *_: (i, 0, 0))
    cache = pl.BlockSpec((1, BB, WINDOW, 128), lambda i, *_: (layer, i, 0, 0))
    grid_spec = pltpu.PrefetchScalarGridSpec(
        num_scalar_prefetch=1, grid=(bs // BB,),
        in_specs=[blk(nq), cache, blk(NEW_KEY_ROWS), cache, blk(NEW_KEY_ROWS)],
        out_specs=[blk(nq), blk(WINDOW), blk(WINDOW)])
    return pl.pallas_call(
        functools.partial(_sample_attn_kernel, steps=steps),
        grid_spec=grid_spec,
        out_shape=[jax.ShapeDtypeStruct((bs, nq, 128), F32), jax.ShapeDtypeStruct((bs, WINDOW, 128), F32),
                   jax.ShapeDtypeStruct((bs, WINDOW, 128), F32)],
        compiler_params=pltpu.CompilerParams(dimension_semantics=("arbitrary",)),
        name="sample_attn",
    )(sinks, qm, kc_all, kn, vc_all, vn)


def _sample_post_kernel(x_ref, o_ref, z_ref, outb_ref, outc_ref, outd_ref, anorm_ref, w_out_ref, x_out_ref):
    mask_bd = (_iota((256, 256), 0) >> 6) == (_iota((256, 256), 1) >> 6)
    ones_bd = jnp.where(mask_bd, 1.0, 0.0).astype(BF16)
    o = o_ref[...]
    out_a = (o * lax.rsqrt(_dot_lhs3(o * o, ones_bd) * (1.0 / HEAD_DIM) + EPS) * anorm_ref[...]
             * _silu(z_ref[...]))
    cat = jnp.concatenate([out_a, outb_ref[...], outc_ref[...], outd_ref[...]], axis=1).astype(BF16)
    x_out_ref[...] = x_ref[...] + _dot(cat, w_out_ref[...])


def _sample_post_call(x, o, z, out_b, out_c, out_d, lw):
    return pl.pallas_call(
        _sample_post_kernel,
        out_shape=jax.ShapeDtypeStruct(x.shape, F32),
        compiler_params=pltpu.CompilerParams(vmem_limit_bytes=VMEM_LIMIT_BYTES),
        name="sample_post",
    )(x, o, z, out_b, out_c, out_d, lw['a_norm_g'], lw['w_out'])


def _sample_mixer(x_tm, lw, tabs, d_ws_l, d_bias_l, a_state_all, a_conv, b_conv, c_k_all, c_v_all, layer, bs, ts):
    to_tm = lambda a: jnp.swapaxes(a, 0, 1).reshape(a.shape[1] * bs, a.shape[2])
    from_tm = lambda a, n: jnp.swapaxes(a.reshape(n, bs, a.shape[-1]), 0, 1)
    wtab = jnp.repeat(d_ws_l[:, :ts, :ts].transpose(1, 2, 0).reshape(ts * ts, D_GROUPS), HEAD_DIM, axis=1)
    btab = jnp.repeat(d_bias_l[:, :ts].T, HEAD_DIM, axis=1)
    (q, k, v, g, beta, z, out_b, cq, ck, cv, out_d, dvn, a_tail, b_tail) = _sample_pre_call(
        x_tm, lw, tabs, wtab, btab, to_tm(a_conv), to_tm(b_conv), bs)

    o, s_new = _sample_delta_call(g, beta, q, k, v, a_state_all, layer, ts)
    a_state_new = s_new.reshape(bs, A_HEADS, HEAD_DIM, HEAD_DIM)

    cq4 = cq.reshape(ts, bs, C_HEADS, HEAD_DIM).transpose(1, 2, 0, 3)
    zq = jnp.zeros_like(cq4[:, 0])
    qm = jnp.concatenate(
        [jnp.concatenate([cq4[:, hh], zq] if hh // 2 == 0 else [zq, cq4[:, hh]], axis=-1) for hh in range(C_HEADS)],
        axis=1)
    pad_new = lambda a: jnp.concatenate([from_tm(a, ts), jnp.zeros((bs, NEW_KEY_ROWS - ts, 128), F32)], axis=1)
    o_att, c_k_new, c_v_new = _sample_attn_call(lw['c_sinks'], qm, c_k_all, pad_new(ck), c_v_all, pad_new(cv),
                                                layer, ts)
    out_c = jnp.concatenate(
        [o_att[:, hh * ts:(hh + 1) * ts, (hh // 2) * HEAD_DIM:(hh // 2 + 1) * HEAD_DIM] for hh in range(C_HEADS)],
        axis=-1)
    out_c = jnp.swapaxes(out_c, 0, 1).reshape(ts * bs, GROUP_WIDTH)
    c_k_new = c_k_new.reshape(bs, WINDOW, C_KV_HEADS, HEAD_DIM)
    c_v_new = c_v_new.reshape(bs, WINDOW, C_KV_HEADS, HEAD_DIM)

    x2 = _sample_post_call(x_tm, o, z, out_b, out_c, out_d, lw)
    new = {'a_state': a_state_new, 'a_conv': from_tm(a_tail, A_CONV - 1), 'b_conv': from_tm(b_tail, B_CONV - 1),
           'c_k': c_k_new, 'c_v': c_v_new, 'd_v': from_tm(dvn, ts)}
    return x2, new


def kernel(x_prompt, x_sample, state_delta, state_delta_conv, state_shortconv, cache_win_k, cache_win_v,
           state_ffn_conv, norm1_g, w_in, a_conv_w, a_log, a_dt_bias, a_norm_g, b_conv_w, c_sinks, d_ln_g,
           d_ln_b, d_ws, d_bias, w_out, norm2_g, ffn_w_gate, ffn_w_up, ffn_conv_w, ffn_w_down, final_norm_g):
    bp, tp, _ = x_prompt.shape
    bs, ts, _ = x_sample.shape
    depth = w_in.shape[0]
    win_buf = cache_win_k.shape[2]
    pos_p = np.arange(tp, dtype=np.int32)
    pos_s = PAST_LEN + np.arange(ts, dtype=np.int32)
    assert win_buf == WINDOW and ts == 4
    tabs_p = _rope_tables(pos_p)
    tabs_s = tuple(jnp.repeat(t, bs, axis=0) for t in _rope_tables(pos_s))
    fng = final_norm_g[None, :]
    ck_all = cache_win_k.reshape(depth, bs, WINDOW, 128)
    sd_all = state_delta.reshape(depth, bs, A_HEADS * HEAD_DIM * HEAD_DIM)
    cv_all = cache_win_v.reshape(depth, bs, WINDOW, 128)
    P = CARRY_ROWS

    hp = x_prompt
    hs = jnp.swapaxes(x_sample, 0, 1).reshape(ts * bs, D_MODEL)
    outs = {k: [] for k in ('sp', 'ss', 'acp', 'acs', 'bcp', 'bcs', 'ckp', 'cks', 'cvp', 'cvs', 'fcp', 'fcs', 'dv')}
    for l in range(depth):
        lw = _layer_weights(l, norm1_g, w_in, a_conv_w, a_log, a_dt_bias, a_norm_g, b_conv_w, c_sinks, d_ln_g,
                            d_ln_b, d_ws, d_bias, w_out, norm2_g, ffn_w_gate, ffn_w_up, ffn_conv_w, ffn_w_down)
        last = l == depth - 1
        hp, s_bd, acv, bcv, ckn, cvn = _mixer_call(
            hp, lw, tabs_p, jnp.zeros((bp, P, 768), F32), jnp.zeros((bp, P, 256), F32),
            jnp.zeros((bp, 256, 256), F32))
        hp, fcv = _ffn_call(hp, lw, jnp.zeros((bp, P, D_FF), F32), fng, tile=TILE_ROWS, stride=1, final_norm=last)
        outs['sp'].append(_diag_blocks(s_bd))
        outs['acp'].append(acv[:, P - (A_CONV - 1):])
        outs['bcp'].append(bcv[:, P - (B_CONV - 1):])
        outs['ckp'].append(ckn.reshape(bp, WINDOW, C_KV_HEADS, HEAD_DIM))
        outs['cvp'].append(cvn.reshape(bp, WINDOW, C_KV_HEADS, HEAD_DIM))
        outs['fcp'].append(fcv[:, P - (FFN_CONV - 1):])
        hs, ns = _sample_mixer(hs, lw, tabs_s, d_ws[l], d_bias[l], sd_all, state_delta_conv[l],
                               state_shortconv[l], ck_all, cv_all, l, bs, ts)
        f0 = jnp.swapaxes(state_ffn_conv[l], 0, 1).reshape(1, (FFN_CONV - 1) * bs, D_FF)
        ys_tm, fcs = _ffn_call(hs[None], lw, f0, fng, tile=ts * bs, stride=bs, final_norm=last)
        hs = ys_tm[0]
        outs['ss'].append(ns['a_state'])
        outs['acs'].append(ns['a_conv'])
        outs['bcs'].append(ns['b_conv'])
        outs['cks'].append(ns['c_k'])
        outs['cvs'].append(ns['c_v'])
        outs['fcs'].append(jnp.swapaxes(fcs.reshape(FFN_CONV - 1, bs, D_FF), 0, 1))
        outs['dv'].append(ns['d_v'])
    st = {k: jnp.stack(v) for k, v in outs.items()}
    hs = jnp.swapaxes(hs.reshape(ts, bs, D_MODEL), 0, 1)
    return (hp, hs, st['sp'], st['ss'], st['acp'], st['acs'], st['bcp'], st['bcs'], st['ckp'], st['cks'],
            st['cvp'], st['cvs'], st['fcp'], st['fcs'], st['dv'])
```

```python
import functools

import jax
import jax.numpy as jnp
import numpy as np
from jax import lax
from jax.experimental import pallas as pl
from jax.experimental.pallas import tpu as pltpu

F32 = jnp.float32
BF16 = jnp.bfloat16

D_MODEL = 1024
GROUP_WIDTH = 256
HEAD_DIM = 64
A_HEADS = 4
A_CONV = 4
A_CHUNK = 64
B_CONV = 3
C_HEADS = 4
C_KV_HEADS = 2
WINDOW = 128
ROPE_DIM = 16
ROPE_THETA = 500000.0
D_GROUPS = 4
D_CHUNK = 128
D_FF = 2816
FFN_CONV = 3
EPS = 1e-6
NEG_INF = -1e30
PAST_LEN = 16384

HEAD_SHIFT = 6
LANES = 128

COL_A = 0
A_QKV_W = 3 * GROUP_WIDTH
COL_Z = A_QKV_W
COL_SMALL = COL_Z + GROUP_WIDTH
W_A = COL_SMALL + LANES
COL_B = 1152
W_B = 768
COL_C = 1920
W_C = 512
COL_D = 2432
W_D = 512
P_PACKED = 2944

TILE_ROWS = 512
MIXER_TILE_ROWS = 256
CARRY_ROWS = 8
FFN_COL_CHUNK = 1408
VMEM_LIMIT_BYTES = 56 * 1024 * 1024
MIXER_VMEM_LIMIT_BYTES = 60 * 1024 * 1024


def _dot(a, b):
    return jnp.dot(a, b, preferred_element_type=F32)


def _dot_nt(a, b):
    return lax.dot_general(a, b, (((1,), (1,)), ((), ())), preferred_element_type=F32)


def _dot_tn(a, b):
    return lax.dot_general(a, b, (((0,), (0,)), ((), ())), preferred_element_type=F32)


def _split3(x):
    hi = x.astype(BF16)
    r1 = x - hi.astype(F32)
    mid = r1.astype(BF16)
    lo = (r1 - mid.astype(F32)).astype(BF16)
    return hi, mid, lo


def _dot_lhs3(x, w01):
    hi, mid, lo = _split3(x)
    return _dot(hi, w01) + _dot(mid, w01) + _dot(lo, w01)


def _head_sumsq(x, ones_bd):
    return _dot((x * x).astype(BF16), ones_bd)


def _sigmoid(x):
    return 1.0 / (1.0 + jnp.exp(-x))


def _silu(x):
    return x * _sigmoid(x)


def _softplus(x):
    return jnp.maximum(x, 0.0) + jnp.log(1.0 + jnp.exp(-jnp.abs(x)))


def _gelu_tanh(x):
    return 0.5 * x * (1.0 + jnp.tanh(np.sqrt(2.0 / np.pi).astype(np.float32) * (x + 0.044715 * (x * x * x))))


def _rms(x, g):
    return x * lax.rsqrt(jnp.mean(x * x, axis=-1, keepdims=True) + EPS) * g


def _iota(shape, dim):
    return lax.broadcasted_iota(jnp.int32, shape, dim)


MIXER_SEQS = 4


def _mixer_kernel(sinks_ref, x_ref, n1g_ref, w_in_ref, aconv_w_ref, alog_ref, dtb_ref, anorm_ref,
                  bconv_w_ref, cos_ref, sina_ref, sinb_ref, lng_ref, lnb_ref, ws_ref, dbias_ref,
                  w_out_ref, aconv0_ref, bconv0_ref, s0_ref,
                  x_out_ref, s_out_ref, aconv_out_ref, bconv_out_ref, ck_out_ref, cv_out_ref,
                  *scratch, tile):
    shared = (sinks_ref, n1g_ref, w_in_ref, aconv_w_ref, alog_ref, dtb_ref, anorm_ref, bconv_w_ref, cos_ref,
              sina_ref, sinb_ref, lng_ref, lnb_ref, ws_ref, dbias_ref, w_out_ref)
    per_seq = (x_ref, aconv0_ref, bconv0_ref, s0_ref, x_out_ref, s_out_ref, aconv_out_ref, bconv_out_ref,
               ck_out_ref, cv_out_ref)
    n_scr = len(scratch) // MIXER_SEQS
    gens = [_mixer_seq(*shared, *(r.at[b] for r in per_seq), *scratch[b * n_scr:(b + 1) * n_scr], tile=tile)
            for b in range(MIXER_SEQS)]
    started, live = 1, list(gens[:1])
    while live:
        for gen in list(live):
            if next(gen, 'done') == 'done':
                live.remove(gen)
        if started < len(gens):
            live.append(gens[started])
            started += 1


def _mixer_seq(sinks_ref, n1g_ref, w_in_ref, aconv_w_ref, alog_ref, dtb_ref, anorm_ref, bconv_w_ref, cos_ref,
               sina_ref, sinb_ref, lng_ref, lnb_ref, ws_ref, dbias_ref, w_out_ref,
               x_ref, aconv0_ref, bconv0_ref, s0_ref, x_out_ref, s_out_ref, aconv_out_ref, bconv_out_ref,
               ck_out_ref, cv_out_ref,
               abuf, bbuf, kbuf, vbuf, q_s, k_s, v_s, gcb_s, bb_s, o_s, s_scr, proj_s, *, tile):
    T = tile
    si = pl.program_id(1)
    P = CARRY_ROWS

    @pl.when(si == 0)
    def _():
        abuf[0:P, :] = aconv0_ref[...]
        bbuf[0:P, :] = bconv0_ref[...]
        kbuf[0:WINDOW, :] = jnp.zeros((WINDOW, 128), F32)
        vbuf[0:WINDOW, :] = jnp.zeros((WINDOW, 128), F32)
        s_scr[...] = s0_ref[...]

    res = {}
    h = _rms(x_ref[...], n1g_ref[...]).astype(BF16)
    for lo, width in ((COL_A, W_A), (COL_B, W_B), (COL_C, W_C), (COL_D, W_D)):
        proj_s[:, lo:lo + width] = _dot(h, w_in_ref[:, lo:lo + width])
    yield

    small = proj_s[:, COL_SMALL:W_A]
    abuf[P:P + T, :] = proj_s[:, 0:A_QKV_W]

    def group_b():
        bbuf[P:P + T, :] = proj_s[:, COL_B + 256:COL_B + 512] * proj_s[:, COL_B + 512:COL_B + 768]
        bx = (bbuf[P:P + T, :] * bconv_w_ref[2:3, :] + bbuf[P - 1:P - 1 + T, :] * bconv_w_ref[1:2, :]
              + bbuf[P - 2:P - 2 + T, :] * bconv_w_ref[0:1, :])
        tail_b = bbuf[T:T + P, :]
        bbuf[0:P, :] = tail_b
        bconv_out_ref[...] = tail_b
        res['out_b'] = proj_s[:, COL_B:COL_B + 256] * bx

    def group_c_rope():
        cos = cos_ref[...]
        sina = sina_ref[...]
        sinb = sinb_ref[...]
        cq = proj_s[:, COL_C:COL_C + 256]
        res['cq'] = (cq * jnp.concatenate([cos, cos], axis=1)
                     + pltpu.roll(cq, 256 - ROPE_DIM // 2, 1) * jnp.concatenate([sina, sina], axis=1)
                     + pltpu.roll(cq, ROPE_DIM // 2, 1) * jnp.concatenate([sinb, sinb], axis=1))
        ck = proj_s[:, COL_C + 256:COL_C + 384]
        ck = (ck * cos + pltpu.roll(ck, 128 - ROPE_DIM // 2, 1) * sina
              + pltpu.roll(ck, ROPE_DIM // 2, 1) * sinb)
        kbuf[WINDOW:WINDOW + T, :] = ck
        vbuf[WINDOW:WINDOW + T, :] = proj_s[:, COL_C + 384:COL_C + 512]
        ck_out_ref[...] = kbuf[T:T + WINDOW, :]
        cv_out_ref[...] = vbuf[T:T + WINDOW, :]

    def group_d_norm():
        res['du'] = _gelu_tanh(proj_s[:, COL_D:COL_D + 256])
        gv = _gelu_tanh(proj_s[:, COL_D + 256:COL_D + 512])
        mu = jnp.mean(gv, axis=-1, keepdims=True)
        xc = gv - mu
        res['dvn'] = (xc * lax.rsqrt(jnp.mean(xc * xc, axis=-1, keepdims=True) + EPS) * lng_ref[...]
                      + lnb_ref[...])
        wr = _iota((D_GROUPS * D_CHUNK, D_CHUNK), 0) & (D_CHUNK - 1)
        wc = _iota((D_GROUPS * D_CHUNK, D_CHUNK), 1)
        res['wm'] = jnp.where(wr >= wc, ws_ref[...], 0.0).astype(BF16)

    y = (abuf[P:P + T, :] * aconv_w_ref[3:4, :] + abuf[P - 1:P - 1 + T, :] * aconv_w_ref[2:3, :]
         + abuf[P - 2:P - 2 + T, :] * aconv_w_ref[1:2, :] + abuf[P - 3:P - 3 + T, :] * aconv_w_ref[0:1, :])
    tail_a = abuf[T:T + P, :]
    abuf[0:P, :] = tail_a
    aconv_out_ref[...] = tail_a
    qkv = _silu(y)
    q_raw = qkv[:, 0:256]
    k_raw = qkv[:, 256:512]
    v_s[...] = qkv[:, 512:768]

    r256 = _iota((256, 256), 0) >> HEAD_SHIFT
    c256 = _iota((256, 256), 1) >> HEAD_SHIFT
    mask_bd = r256 == c256
    ones_bd = jnp.where(mask_bd, 1.0, 0.0).astype(BF16)
    q_s[...] = q_raw * lax.rsqrt(_head_sumsq(q_raw, ones_bd) + EPS) * (HEAD_DIM ** -0.5)
    k_s[...] = k_raw * lax.rsqrt(_head_sumsq(k_raw, ones_bd) + EPS)
    yield

    g_log = -jnp.exp(alog_ref[...]) * _softplus(small + dtb_ref[...])
    beta = _sigmoid(small)
    gbeta = jnp.where(_iota((T, 128), 1) < A_HEADS, g_log, beta)
    expand = jnp.where(_iota((128, 512), 0) == (_iota((128, 512), 1) >> HEAD_SHIFT), 1.0, 0.0).astype(BF16)
    gbb = _dot_lhs3(gbeta, expand)
    bb_s[...] = gbb[:, 256:512]

    def chunk_cumsum(xv):
        row_in_chunk = _iota(xv.shape, 0) & (A_CHUNK - 1)
        step = 1
        while step < A_CHUNK:
            xv = xv + jnp.where(row_in_chunk >= step, pltpu.roll(xv, step, 0), 0.0)
            step *= 2
        return xv

    gcb_s[...] = chunk_cumsum(gbb[:, 0:256])
    gct = chunk_cumsum(g_log).T[0:8, :]
    low_half = (_iota((1, T), 1) & A_CHUNK) == 0
    gct_r = pltpu.roll(gct, A_CHUNK, 1)
    gct_l = pltpu.roll(gct, T - A_CHUNK, 1)
    even_rows = [jnp.where(low_half, gct[a:a + 1, :], gct_r[a + 1:a + 2, :]) for a in (0, 2)]
    odd_rows = [jnp.where(low_half, gct_l[a:a + 1, :], gct[a + 1:a + 2, :]) for a in (0, 2)]
    yield

    ri = _iota((A_CHUNK, 256), 0)
    ci = _iota((A_CHUNK, 256), 1) & (A_CHUNK - 1)
    causal_t = ri >= ci
    strict_t = ri > ci
    eye_t = jnp.where(ri == ci, 1.0, 0.0)

    def bd16(x16):
        return jnp.concatenate([x16, x16, x16, x16], axis=0) * ones_bd

    n_chunks = T // A_CHUNK
    t_inv, pw, qk, qdec, e_tail, vb16, kbe16 = [], [], [], [], [], [], []
    for c in range(n_chunks):
        rows = slice(c * A_CHUNK, (c + 1) * A_CHUNK)
        blk = slice((c // 2) * 128, (c // 2) * 128 + 128)
        src = even_rows if c % 2 == 0 else odd_rows
        gc_row = jnp.concatenate([src[0][:, blk], src[1][:, blk]], axis=1)
        qc = q_s[rows, :]
        kc = k_s[rows, :]
        bbc = bb_s[rows, :]
        gcb = gcb_s[rows, :]
        decay = jnp.where(causal_t, jnp.exp(jnp.where(causal_t, gcb - gc_row, 0.0)), 0.0)
        eg = jnp.exp(gcb)
        kb = kc * bbc
        aq = _dot_nt(jnp.concatenate([kb, qc], axis=0).astype(BF16), bd16(kc.astype(BF16)))
        a_mat = jnp.where(strict_t, aq[0:A_CHUNK] * decay, 0.0)
        qk.append(aq[A_CHUNK:2 * A_CHUNK] * decay)
        t_inv.append(eye_t - a_mat)
        pw.append(a_mat)
        qdec.append(qc * eg)
        e_tail.append(jnp.exp(gcb[A_CHUNK - 1:A_CHUNK, :] - gcb))
        vb16.append((v_s[rows, :] * bbc).astype(BF16))
        kbe16.append((kb * eg).astype(BF16))
    yield
    u, w = [], []
    state = {'s': s_scr[...]}

    def level_first():
        for c in range(n_chunks):
            p16 = pw[c].astype(BF16)
            pw[c] = _dot(p16, bd16(p16))

    def level_mid():
        for c in range(n_chunks):
            p16 = pw[c].astype(BF16)
            res = _dot(jnp.concatenate([p16, t_inv[c].astype(BF16)], axis=0), bd16(p16))
            pw[c] = res[0:A_CHUNK]
            t_inv[c] = t_inv[c] + res[A_CHUNK:2 * A_CHUNK]

    def level_last():
        for c in range(n_chunks):
            t_c = t_inv[c] + _dot(t_inv[c].astype(BF16), bd16(pw[c].astype(BF16)))
            uw = _dot(t_c.astype(BF16), jnp.concatenate([bd16(vb16[c]), bd16(kbe16[c])], axis=1))
            u.append(uw[:, 0:256])
            w.append(uw[:, 256:512])

    def scan_step(c):
        rows = slice(c * A_CHUNK, (c + 1) * A_CHUNK)
        s_bd = state['s']
        wq = _dot(jnp.concatenate([w[c], qdec[c]], axis=0).astype(BF16), s_bd.astype(BF16))
        v_new = u[c] - wq[0:A_CHUNK]
        o_s[rows, :] = wq[A_CHUNK:2 * A_CHUNK] + _dot(qk[c].astype(BF16), bd16(v_new.astype(BF16)))
        kv = _dot_tn(k_s[rows, :].astype(BF16), (v_new * e_tail[c]).astype(BF16))
        g_last = gcb_s[(c + 1) * A_CHUNK - 1:(c + 1) * A_CHUNK, :]
        state['s'] = s_bd * jnp.exp(g_last) + jnp.where(mask_bd, kv, 0.0)

    chain = [level_first] + [level_mid] * 4 + [level_last] + [functools.partial(scan_step, c) for c in range(n_chunks)]

    lane128 = _iota((2 * WINDOW, 128), 1)
    low = _iota((WINDOW, 128), 1) < HEAD_DIM
    qrow = _iota((2 * WINDOW, 2 * WINDOW), 0) & (WINDOW - 1)
    kcol = _iota((2 * WINDOW, 2 * WINDOW), 1)
    band = (kcol > qrow) & (kcol <= qrow + WINDOW)
    top_half = _iota((2 * WINDOW, 1), 0) < WINDOW
    out_c_blocks = []

    def group_c_block(n):
        first_key = si * T + (n - 1) * WINDOW
        valid = band & (kcol + first_key >= 0)
        kwin = kbuf[n * WINDOW:(n + 2) * WINDOW, :]
        vwin = vbuf[n * WINDOW:(n + 2) * WINDOW, :]
        k_sw = pltpu.roll(kwin, HEAD_DIM, 1)
        v_sw = pltpu.roll(vwin, HEAD_DIM, 1)
        pair_out = []
        for g in range(C_KV_HEADS):
            own = (lane128 < HEAD_DIM) if g == 0 else (lane128 >= HEAD_DIM)
            k_dup = jnp.where(own, kwin, k_sw).astype(BF16)
            v_dup = jnp.where(own, vwin, v_sw).astype(BF16)
            qp = res['cq'][n * WINDOW:(n + 1) * WINDOW, g * 128:(g + 1) * 128]
            q_st = jnp.concatenate([jnp.where(low, qp, 0.0), jnp.where(low, 0.0, qp)], axis=0).astype(BF16)
            s = _dot_nt(q_st, k_dup) * (HEAD_DIM ** -0.5)
            s = jnp.where(valid, s, NEG_INF)
            sink = jnp.where(top_half, sinks_ref[2 * g], sinks_ref[2 * g + 1])
            m = jnp.maximum(jnp.max(s, axis=-1, keepdims=True), sink)
            p = jnp.exp(s - m)
            denom = jnp.sum(p, axis=-1, keepdims=True) + jnp.exp(sink - m)
            o2 = _dot((p / denom).astype(BF16), v_dup)
            pair_out.append(jnp.where(low, o2[0:WINDOW], o2[WINDOW:2 * WINDOW]))
        out_c_blocks.append(jnp.concatenate(pair_out, axis=1))
        if n == T // WINDOW - 1:
            kbuf[0:WINDOW, :] = kbuf[T:T + WINDOW, :]
            vbuf[0:WINDOW, :] = vbuf[T:T + WINDOW, :]

    lane_grp = _iota((D_CHUNK, 256), 1) >> HEAD_SHIFT
    out_d_blocks = []

    def group_d_block(n):
        mx = _dot(res['wm'], res['dvn'][n * D_CHUNK:(n + 1) * D_CHUNK, :].astype(BF16))
        mixed = dbias_ref[...]
        for grp in range(D_GROUPS):
            mixed = mixed + jnp.where(lane_grp == grp, mx[grp * D_CHUNK:(grp + 1) * D_CHUNK, :], 0.0)
        out_d_blocks.append(res['du'][n * D_CHUNK:(n + 1) * D_CHUNK, :] * mixed)

    c_blocks = [functools.partial(group_c_block, n) for n in range(T // WINDOW)]
    d_blocks = [functools.partial(group_d_block, n) for n in range(T // D_CHUNK)]
    fill = [group_b, group_c_rope, group_d_norm] + c_blocks + d_blocks
    for i in range(max(len(chain), len(fill))):
        if i < len(chain):
            chain[i]()
        if i < len(fill):
            fill[i]()
        yield
    out_b = res['out_b']
    out_c = jnp.concatenate(out_c_blocks, axis=0)
    out_d = jnp.concatenate(out_d_blocks, axis=0)

    s_scr[...] = state['s']
    s_out_ref[...] = state['s']
    o = o_s[...]
    out_a = (o * lax.rsqrt(_head_sumsq(o, ones_bd) * (1.0 / HEAD_DIM) + EPS) * anorm_ref[...]
             * _silu(proj_s[:, COL_Z:COL_SMALL]))

    cat = jnp.concatenate([out_a, out_b, out_c, out_d], axis=1).astype(BF16)
    x_out_ref[...] = x_ref[...] + _dot(cat, w_out_ref[...])


def _mixer_call(x, lw, tabs, aconv0, bconv0, s0):
    bsz, seq, _ = x.shape
    T = MIXER_TILE_ROWS
    ns = seq // T
    NB = MIXER_SEQS
    P = CARRY_ROWS
    full = lambda shape: pl.BlockSpec(shape, lambda b, s, *_: (0,) * len(shape), pipeline_mode=pl.Buffered(1))
    per_b = lambda shape: pl.BlockSpec((NB,) + shape, lambda b, s, *_: (b,) + (0,) * len(shape))
    tab = pl.BlockSpec((T, 128), lambda b, s, *_: (s, 0))
    seq_scratch = [pltpu.VMEM(shape, F32) for shape in (
        (T + P, A_QKV_W), (T + P, 256), (T + WINDOW, 128), (T + WINDOW, 128), (T, 256), (T, 256), (T, 256),
        (T, 256), (T, 256), (T, 256), (256, 256), (T, P_PACKED))]
    grid_spec = pltpu.PrefetchScalarGridSpec(
        num_scalar_prefetch=1,
        grid=(bsz // NB, ns),
        in_specs=[
            pl.BlockSpec((NB, T, D_MODEL), lambda b, s, *_: (b, s, 0)),
            full((1, D_MODEL)), full((D_MODEL, P_PACKED)), full((A_CONV, A_QKV_W)), full((1, 128)), full((1, 128)),
            full((1, 256)), full((B_CONV, 256)), tab, tab, tab, full((1, 256)), full((1, 256)),
            full((D_GROUPS * D_CHUNK, D_CHUNK)), full((D_CHUNK, 256)), full((D_MODEL, D_MODEL)),
            per_b((P, A_QKV_W)), per_b((P, 256)), per_b((256, 256)),
        ],
        out_specs=[
            pl.BlockSpec((NB, T, D_MODEL), lambda b, s, *_: (b, s, 0)),
            per_b((256, 256)), per_b((P, A_QKV_W)), per_b((P, 256)), per_b((WINDOW, 128)), per_b((WINDOW, 128)),
        ],
        scratch_shapes=seq_scratch * NB,
    )
    out_shape = [
        jax.ShapeDtypeStruct((bsz, seq, D_MODEL), F32),
        jax.ShapeDtypeStruct((bsz, 256, 256), F32),
        jax.ShapeDtypeStruct((bsz, P, A_QKV_W), F32),
        jax.ShapeDtypeStruct((bsz, P, 256), F32),
        jax.ShapeDtypeStruct((bsz, WINDOW, 128), F32),
        jax.ShapeDtypeStruct((bsz, WINDOW, 128), F32),
    ]
    return pl.pallas_call(
        functools.partial(_mixer_kernel, tile=T),
        grid_spec=grid_spec,
        out_shape=out_shape,
        compiler_params=pltpu.CompilerParams(
            dimension_semantics=("arbitrary", "arbitrary"), vmem_limit_bytes=MIXER_VMEM_LIMIT_BYTES),
        name="mixer",
    )(lw['c_sinks'], x, lw['norm1_g'], lw['w_in'], lw['a_conv_w'], lw['a_log'], lw['a_dt_bias'],
      lw['a_norm_g'], lw['b_conv_w'], tabs[0], tabs[1], tabs[2], lw['d_ln_g'], lw['d_ln_b'],
      lw['d_ws'], lw['d_bias'], lw['w_out'], aconv0, bconv0, s0)


def _ffn_kernel(x_ref, n2g_ref, wg_ref, wu_ref, cw_ref, wd_ref, fconv0_ref, fng_ref,
                x_out_ref, fconv_out_ref, gbuf, *, tile, stride, final_norm):
    T = tile
    P = (FFN_CONV - 1) * stride if stride > 1 else CARRY_ROWS
    si = pl.program_id(1)

    @pl.when(si == 0)
    def _():
        gbuf[0:P, :] = fconv0_ref[0]

    x = x_ref[0]
    h = _rms(x, n2g_ref[...]).astype(BF16)
    acc = x
    for c0 in range(0, D_FF, FFN_COL_CHUNK):
        cols = slice(c0, c0 + FFN_COL_CHUNK)
        gbuf[P:P + T, cols] = _dot(h, wg_ref[:, cols])
        up = _dot(h, wu_ref[:, cols])
        gate = (gbuf[P:P + T, cols] * cw_ref[2:3, cols] + gbuf[P - stride:P - stride + T, cols] * cw_ref[1:2, cols]
                + gbuf[P - 2 * stride:P - 2 * stride + T, cols] * cw_ref[0:1, cols])
        acc = acc + _dot((_silu(gate) * up).astype(BF16), wd_ref[cols, :])
    tail = gbuf[T:T + P, :]
    gbuf[0:P, :] = tail
    fconv_out_ref[0] = tail
    if final_norm:
        acc = _rms(acc, fng_ref[...])
    x_out_ref[0] = acc


def _ffn_call(x, lw, fconv0, final_g, *, tile, stride, final_norm):
    bsz, seq, _ = x.shape
    T = tile
    P = (FFN_CONV - 1) * stride if stride > 1 else CARRY_ROWS
    full = lambda shape: pl.BlockSpec(shape, lambda b, s: (0,) * len(shape), pipeline_mode=pl.Buffered(1))
    per_b = lambda shape: pl.BlockSpec((1,) + shape, lambda b, s: (b,) + (0,) * len(shape))
    return pl.pallas_call(
        functools.partial(_ffn_kernel, tile=T, stride=stride, final_norm=final_norm),
        grid=(bsz, seq // T),
        in_specs=[
            pl.BlockSpec((1, T, D_MODEL), lambda b, s: (b, s, 0)),
            full((1, D_MODEL)), full((D_MODEL, D_FF)), full((D_MODEL, D_FF)), full((FFN_CONV, D_FF)),
            full((D_FF, D_MODEL)), per_b((P, D_FF)), full((1, D_MODEL)),
        ],
        out_specs=[pl.BlockSpec((1, T, D_MODEL), lambda b, s: (b, s, 0)), per_b((P, D_FF))],
        out_shape=[jax.ShapeDtypeStruct((bsz, seq, D_MODEL), F32), jax.ShapeDtypeStruct((bsz, P, D_FF), F32)],
        scratch_shapes=[pltpu.VMEM((T + P, D_FF), F32)],
        compiler_params=pltpu.CompilerParams(
            dimension_semantics=("arbitrary", "arbitrary"), vmem_limit_bytes=VMEM_LIMIT_BYTES),
        name="ffn",
    )(x, lw['norm2_g'], lw['ffn_w_gate'], lw['ffn_w_up'], lw['ffn_conv_w'], lw['ffn_w_down'], fconv0, final_g)


def _pack_w_in(w):
    pad = jnp.zeros((D_MODEL, 128 - 2 * A_HEADS), w.dtype)
    small = jnp.concatenate([w[:, 1028:1032], w[:, 1024:1028], pad], axis=1)
    return jnp.concatenate([w[:, 0:1024], small, w[:, 1032:]], axis=1).astype(BF16)


def _pad_lanes(v, width=128):
    return jnp.concatenate([v, jnp.zeros((width - v.shape[0],), v.dtype)])[None, :]


def _layer_weights(l, norm1_g, w_in, a_conv_w, a_log, a_dt_bias, a_norm_g, b_conv_w, c_sinks, d_ln_g, d_ln_b,
                   d_ws, d_bias, w_out, norm2_g, ffn_w_gate, ffn_w_up, ffn_conv_w, ffn_w_down):
    bias_tab = jnp.broadcast_to(d_bias[l].T[:, :, None], (D_CHUNK, D_GROUPS, HEAD_DIM)).reshape(D_CHUNK, 256)
    return {
        'norm1_g': norm1_g[l][None, :], 'w_in': _pack_w_in(w_in[l]), 'a_conv_w': a_conv_w[l],
        'a_log': _pad_lanes(a_log[l]), 'a_dt_bias': _pad_lanes(a_dt_bias[l]),
        'a_norm_g': jnp.tile(a_norm_g[l], A_HEADS)[None, :], 'b_conv_w': b_conv_w[l], 'c_sinks': c_sinks[l],
        'd_ln_g': d_ln_g[l][None, :], 'd_ln_b': d_ln_b[l][None, :],
        'd_ws': d_ws[l].reshape(D_GROUPS * D_CHUNK, D_CHUNK), 'd_bias': bias_tab,
        'w_out': w_out[l].astype(BF16), 'norm2_g': norm2_g[l][None, :],
        'ffn_w_gate': ffn_w_gate[l].astype(BF16), 'ffn_w_up': ffn_w_up[l].astype(BF16),
        'ffn_conv_w': ffn_conv_w[l], 'ffn_w_down': ffn_w_down[l].astype(BF16),
    }


def _rope_tables(pos):
    half = ROPE_DIM // 2
    inv = np.power(ROPE_THETA, -np.arange(half, dtype=np.float64) * (2.0 / ROPE_DIM))
    ang = pos.astype(np.float64)[:, None] * inv[None, :]
    cos, sin = jnp.asarray(np.cos(ang), F32), jnp.asarray(np.sin(ang), F32)
    n = pos.shape[0]
    rest = HEAD_DIM - ROPE_DIM
    cos_h = jnp.concatenate([cos, cos, jnp.ones((n, rest), F32)], axis=1)
    sina_h = jnp.concatenate([-sin, jnp.zeros((n, half + rest), F32)], axis=1)
    sinb_h = jnp.concatenate([jnp.zeros((n, half), F32), sin, jnp.zeros((n, rest), F32)], axis=1)
    return tuple(jnp.concatenate([t, t], axis=1) for t in (cos_h, sina_h, sinb_h))


def _diag_blocks(s_bd):
    return jnp.stack([s_bd[:, HEAD_DIM * i:HEAD_DIM * (i + 1), HEAD_DIM * i:HEAD_DIM * (i + 1)]
                      for i in range(A_HEADS)], axis=1)


def _sample_pre_kernel(x_ref, n1g_ref, w_in_ref, aconv_w_ref, alog_ref, dtb_ref, bconv_w_ref, cos_ref, sina_ref,
                       sinb_ref, lng_ref, lnb_ref, wtab_ref, btab_ref, aconv0_ref, bconv0_ref,
                       q_ref, k_ref, v_ref, g_ref, beta_ref, z_ref, outb_ref, cq_ref, ck_ref, cv_ref, outd_ref,
                       dvn_ref, aconv_out_ref, bconv_out_ref, abuf, bbuf, *, rows, stride):
    R, S = rows, stride
    steps = R // S
    pa_rows = (A_CONV - 1) * S
    pb_rows = (B_CONV - 1) * S
    x = x_ref[...]
    h = _rms(x, n1g_ref[...]).astype(BF16)

    pa = _dot(h, w_in_ref[:, COL_A:COL_A + W_A])
    z_ref[...] = pa[:, COL_Z:COL_SMALL]
    small = pa[:, COL_SMALL:W_A]
    abuf[0:pa_rows, :] = aconv0_ref[...]
    abuf[pa_rows:pa_rows + R, :] = pa[:, 0:A_QKV_W]
    y = abuf[0:R, :] * aconv_w_ref[0:1, :]
    for j in range(1, A_CONV):
        y = y + abuf[j * S:j * S + R, :] * aconv_w_ref[j:j + 1, :]
    aconv_out_ref[...] = abuf[R:R + pa_rows, :]
    qkv = _silu(y)
    q_raw = qkv[:, 0:256]
    k_raw = qkv[:, 256:512]
    v_ref[...] = qkv[:, 512:768]
    mask_bd = (_iota((256, 256), 0) >> HEAD_SHIFT) == (_iota((256, 256), 1) >> HEAD_SHIFT)
    ones_bd = jnp.where(mask_bd, 1.0, 0.0).astype(BF16)
    q_ref[...] = q_raw * lax.rsqrt(_dot_lhs3(q_raw * q_raw, ones_bd) + EPS) * (HEAD_DIM ** -0.5)
    k_ref[...] = k_raw * lax.rsqrt(_dot_lhs3(k_raw * k_raw, ones_bd) + EPS)
    g_ref[...] = -jnp.exp(alog_ref[...]) * _softplus(small + dtb_ref[...])
    beta_ref[...] = _sigmoid(small)

    pb = _dot(h, w_in_ref[:, COL_B:COL_B + W_B])
    bbuf[0:pb_rows, :] = bconv0_ref[...]
    bbuf[pb_rows:pb_rows + R, :] = pb[:, 256:512] * pb[:, 512:768]
    bx = bbuf[0:R, :] * bconv_w_ref[0:1, :]
    for j in range(1, B_CONV):
        bx = bx + bbuf[j * S:j * S + R, :] * bconv_w_ref[j:j + 1, :]
    bconv_out_ref[...] = bbuf[R:R + pb_rows, :]
    outb_ref[...] = pb[:, 0:256] * bx

    pc = _dot(h, w_in_ref[:, COL_C:COL_C + W_C])
    cos = cos_ref[...]
    sina = sina_ref[...]
    sinb = sinb_ref[...]
    cq = pc[:, 0:256]
    cq_ref[...] = (cq * jnp.concatenate([cos, cos], axis=1)
                   + pltpu.roll(cq, 256 - ROPE_DIM // 2, 1) * jnp.concatenate([sina, sina], axis=1)
                   + pltpu.roll(cq, ROPE_DIM // 2, 1) * jnp.concatenate([sinb, sinb], axis=1))
    ck = pc[:, 256:384]
    ck_ref[...] = (ck * cos + pltpu.roll(ck, 128 - ROPE_DIM // 2, 1) * sina
                   + pltpu.roll(ck, ROPE_DIM // 2, 1) * sinb)
    cv_ref[...] = pc[:, 384:512]

    pd = _dot(h, w_in_ref[:, COL_D:COL_D + W_D])
    du = _gelu_tanh(pd[:, 0:256])
    gv = _gelu_tanh(pd[:, 256:512])
    mu = jnp.mean(gv, axis=-1, keepdims=True)
    xc = gv - mu
    dvn = xc * lax.rsqrt(jnp.mean(xc * xc, axis=-1, keepdims=True) + EPS) * lng_ref[...] + lnb_ref[...]
    dvn_ref[...] = dvn
    for t in range(steps):
        mixed = btab_ref[t:t + 1, :]
        for s in range(t + 1):
            mixed = mixed + wtab_ref[t * steps + s:t * steps + s + 1, :] * dvn[s * S:(s + 1) * S, :]
        outd_ref[t * S:(t + 1) * S, :] = du[t * S:(t + 1) * S, :] * mixed


def _sample_pre_call(x, lw, tabs, wtab, btab, aconv0, bconv0, stride):
    R = x.shape[0]
    f = lambda *shape: jax.ShapeDtypeStruct(shape, F32)
    out_shape = [f(R, 256), f(R, 256), f(R, 256), f(R, 128), f(R, 128), f(R, 256), f(R, 256), f(R, 256),
                 f(R, 128), f(R, 128), f(R, 256), f(R, 256), f((A_CONV - 1) * stride, A_QKV_W),
                 f((B_CONV - 1) * stride, 256)]
    return pl.pallas_call(
        functools.partial(_sample_pre_kernel, rows=R, stride=stride),
        out_shape=out_shape,
        scratch_shapes=[pltpu.VMEM((R + (A_CONV - 1) * stride, A_QKV_W), F32),
                        pltpu.VMEM((R + (B_CONV - 1) * stride, 256), F32)],
        compiler_params=pltpu.CompilerParams(vmem_limit_bytes=VMEM_LIMIT_BYTES),
        name="sample_pre",
    )(x, lw['norm1_g'], lw['w_in'], lw['a_conv_w'], lw['a_log'], lw['a_dt_bias'], lw['b_conv_w'],
      tabs[0], tabs[1], tabs[2], lw['d_ln_g'], lw['d_ln_b'], wtab, btab, aconv0, bconv0)


def _sample_delta_kernel(g_ref, beta_ref, q_ref, k_ref, v_ref, s_ref, o_ref, snew_ref,
                         qt_s, kt_s, vt_s, gt_s, bt_s, ot_s, *, steps, nb):
    h = pl.program_id(0)
    n_i = HEAD_DIM
    zeros = jnp.zeros((HEAD_DIM, nb), F32)

    @pl.when(h == 0)
    def _():
        for t in range(steps):
            rows = slice(t * nb, (t + 1) * nb)
            qt_s[t] = q_ref[rows, :].T
            kt_s[t] = k_ref[rows, :].T
            vt_s[t] = v_ref[rows, :].T
            gt_s[t] = g_ref[rows, :].T
            bt_s[t] = beta_ref[rows, :].T

    base = pl.multiple_of(h * HEAD_DIM, HEAD_DIM)
    head_rows = pl.ds(base, HEAD_DIM)

    def rows_of(i):
        return pl.ds(pl.multiple_of(i * HEAD_DIM, HEAD_DIM), HEAD_DIM)

    def decay(t):
        return jnp.exp(gt_s[t, pl.ds(h, 1), :])

    dec0 = decay(0)

    def first_pass(i, acc):
        return acc + kt_s[0, pl.ds(base + i, 1), :] * (s_ref[0, rows_of(i), :] * dec0)

    ks = lax.fori_loop(0, n_i, first_pass, zeros)
    for t in range(steps):
        dec = decay(t)
        v_new = bt_s[t, pl.ds(A_HEADS + h, 1), :] * (vt_s[t, head_rows, :] - ks)
        src = s_ref if t == 0 else snew_ref
        dec_next = decay(t + 1) if t + 1 < steps else None

        def update(i, carry, t=t, dec=dec, v_new=v_new, src=src, dec_next=dec_next):
            o_acc, ks_acc = carry
            blk = src[0, rows_of(i), :] * dec + kt_s[t, pl.ds(base + i, 1), :] * v_new
            snew_ref[0, rows_of(i), :] = blk
            o_acc = o_acc + qt_s[t, pl.ds(base + i, 1), :] * blk
            if dec_next is not None:
                ks_acc = ks_acc + kt_s[t + 1, pl.ds(base + i, 1), :] * (blk * dec_next)
            return o_acc, ks_acc

        o_acc, ks = lax.fori_loop(0, n_i, update, (zeros, zeros))
        ot_s[t, head_rows, :] = o_acc

    @pl.when(h == A_HEADS - 1)
    def _():
        for t in range(steps):
            o_ref[t * nb:(t + 1) * nb, :] = ot_s[t].T


def _sample_delta_call(g, beta, q, k, v, s, steps):
    nh, _, nb = s.shape
    rows = steps * nb
    whole = lambda width: pl.BlockSpec((rows, width), lambda h: (0, 0))
    st = pl.BlockSpec((1, HEAD_DIM * HEAD_DIM, nb), lambda h: (h, 0, 0))
    wide = pltpu.VMEM((steps, GROUP_WIDTH, nb), F32)
    narrow = pltpu.VMEM((steps, 128, nb), F32)
    return pl.pallas_call(
        functools.partial(_sample_delta_kernel, steps=steps, nb=nb),
        grid=(nh,),
        in_specs=[whole(128), whole(128), whole(GROUP_WIDTH), whole(GROUP_WIDTH), whole(GROUP_WIDTH), st],
        out_specs=[whole(GROUP_WIDTH), st],
        out_shape=[jax.ShapeDtypeStruct((rows, GROUP_WIDTH), F32),
                   jax.ShapeDtypeStruct((nh, HEAD_DIM * HEAD_DIM, nb), F32)],
        scratch_shapes=[wide, wide, wide, narrow, narrow, wide],
        compiler_params=pltpu.CompilerParams(dimension_semantics=("arbitrary",),
                                             vmem_limit_bytes=VMEM_LIMIT_BYTES),
        name="sample_delta",
    )(g, beta, q, k, v, s)


SAMPLE_ATTN_BLOCK = 8
NEW_KEY_ROWS = 8


def _sample_attn_kernel(sinks_ref, qm_ref, kc_ref, kn_ref, vc_ref, vn_ref, o_ref, k_out_ref, v_out_ref, *, steps):
    nq = C_HEADS * steps
    nk = WINDOW + NEW_KEY_ROWS
    BB = SAMPLE_ATTN_BLOCK
    row = _iota((BB * nq, nk), 0)
    col = _iota((BB * nq, nk), 1)
    t_q = row & (steps - 1)
    valid = ((col < WINDOW) & (col > t_q)) | ((col >= WINDOW) & (col - WINDOW <= t_q))
    head = (_iota((BB * nq, 1), 0) >> 2) & (C_HEADS - 1)
    sink = jnp.where(head == 0, sinks_ref[0],
                     jnp.where(head == 1, sinks_ref[1], jnp.where(head == 2, sinks_ref[2], sinks_ref[3])))
    scores = [_dot_nt(qm_ref[b].astype(BF16), jnp.concatenate([kc_ref[0, b], kn_ref[b]], axis=0).astype(BF16))
              for b in range(BB)]
    s = jnp.where(valid, jnp.concatenate(scores, axis=0) * (HEAD_DIM ** -0.5), NEG_INF)
    m = jnp.maximum(jnp.max(s, axis=-1, keepdims=True), sink)
    p = jnp.exp(s - m)
    denom = jnp.sum(p, axis=-1, keepdims=True) + jnp.exp(sink - m)
    p16 = (p / denom).astype(BF16)
    is_new_row = _iota((WINDOW, 128), 0) >= WINDOW - steps
    pad_rows = jnp.zeros((WINDOW - NEW_KEY_ROWS, 128), F32)

    def slide(cache, new_rows):
        kept = pltpu.roll(cache, WINDOW - steps, 0)
        tail = jnp.concatenate([pad_rows, pltpu.roll(new_rows, NEW_KEY_ROWS - steps, 0)], axis=0)
        return jnp.where(is_new_row, tail, kept)

    for b in range(BB):
        v_all = jnp.concatenate([vc_ref[0, b], vn_ref[b]], axis=0).astype(BF16)
        o_ref[b] = _dot(p16[b * nq:(b + 1) * nq, :], v_all)
        k_out_ref[b] = slide(kc_ref[0, b], kn_ref[b])
        v_out_ref[b] = slide(vc_ref[0, b], vn_ref[b])


def _sample_attn_call(sinks, qm, kc_all, kn, vc_all, vn, layer, steps):
    bs, nq, _ = qm.shape
    BB = SAMPLE_ATTN_BLOCK
    blk = lambda r: pl.BlockSpec((BB, r, 128), lambda i, *_: (i, 0, 0))
    cache = pl.BlockSpec((1, BB, WINDOW, 128), lambda i, *_: (layer, i, 0, 0))
    grid_spec = pltpu.PrefetchScalarGridSpec(
        num_scalar_prefetch=1, grid=(bs // BB,),
        in_specs=[blk(nq), cache, blk(NEW_KEY_ROWS), cache, blk(NEW_KEY_ROWS)],
        out_specs=[blk(nq), blk(WINDOW), blk(WINDOW)])
    return pl.pallas_call(
        functools.partial(_sample_attn_kernel, steps=steps),
        grid_spec=grid_spec,
        out_shape=[jax.ShapeDtypeStruct((bs, nq, 128), F32), jax.ShapeDtypeStruct((bs, WINDOW, 128), F32),
                   jax.ShapeDtypeStruct((bs, WINDOW, 128), F32)],
        compiler_params=pltpu.CompilerParams(dimension_semantics=("arbitrary",)),
        name="sample_attn",
    )(sinks, qm, kc_all, kn, vc_all, vn)


def _sample_post_kernel(x_ref, o_ref, z_ref, outb_ref, outc_ref, outd_ref, anorm_ref, w_out_ref, x_out_ref):
    mask_bd = (_iota((256, 256), 0) >> HEAD_SHIFT) == (_iota((256, 256), 1) >> HEAD_SHIFT)
    ones_bd = jnp.where(mask_bd, 1.0, 0.0).astype(BF16)
    o = o_ref[...]
    out_a = (o * lax.rsqrt(_dot_lhs3(o * o, ones_bd) * (1.0 / HEAD_DIM) + EPS) * anorm_ref[...]
             * _silu(z_ref[...]))
    cat = jnp.concatenate([out_a, outb_ref[...], outc_ref[...], outd_ref[...]], axis=1).astype(BF16)
    x_out_ref[...] = x_ref[...] + _dot(cat, w_out_ref[...])


def _sample_post_call(x, o, z, out_b, out_c, out_d, lw):
    return pl.pallas_call(
        _sample_post_kernel,
        out_shape=jax.ShapeDtypeStruct(x.shape, F32),
        compiler_params=pltpu.CompilerParams(vmem_limit_bytes=VMEM_LIMIT_BYTES),
        name="sample_post",
    )(x, o, z, out_b, out_c, out_d, lw['a_norm_g'], lw['w_out'])


def _sample_mixer(x_tm, lw, tabs, d_ws_l, d_bias_l, a_state, a_conv, b_conv, c_k_all, c_v_all, layer, bs, ts):
    to_tm = lambda a: jnp.swapaxes(a, 0, 1).reshape(a.shape[1] * bs, a.shape[2])
    from_tm = lambda a, n: jnp.swapaxes(a.reshape(n, bs, a.shape[-1]), 0, 1)
    wtab = jnp.repeat(d_ws_l[:, :ts, :ts].transpose(1, 2, 0).reshape(ts * ts, D_GROUPS), HEAD_DIM, axis=1)
    btab = jnp.repeat(d_bias_l[:, :ts].T, HEAD_DIM, axis=1)
    (q, k, v, g, beta, z, out_b, cq, ck, cv, out_d, dvn, a_tail, b_tail) = _sample_pre_call(
        x_tm, lw, tabs, wtab, btab, to_tm(a_conv), to_tm(b_conv), bs)

    s_t = a_state.reshape(bs, A_HEADS, HEAD_DIM * HEAD_DIM).transpose(1, 2, 0)
    o, s_new_t = _sample_delta_call(g, beta, q, k, v, s_t, ts)
    a_state_new = s_new_t.transpose(2, 0, 1).reshape(bs, A_HEADS, HEAD_DIM, HEAD_DIM)

    cq4 = cq.reshape(ts, bs, C_HEADS, HEAD_DIM).transpose(1, 2, 0, 3)
    zq = jnp.zeros_like(cq4[:, 0])
    qm = jnp.concatenate(
        [jnp.concatenate([cq4[:, hh], zq] if hh // 2 == 0 else [zq, cq4[:, hh]], axis=-1) for hh in range(C_HEADS)],
        axis=1)
    pad_new = lambda a: jnp.concatenate([from_tm(a, ts), jnp.zeros((bs, NEW_KEY_ROWS - ts, 128), F32)], axis=1)
    o_att, c_k_new, c_v_new = _sample_attn_call(lw['c_sinks'], qm, c_k_all, pad_new(ck), c_v_all, pad_new(cv),
                                                layer, ts)
    out_c = jnp.concatenate(
        [o_att[:, hh * ts:(hh + 1) * ts, (hh // 2) * HEAD_DIM:(hh // 2 + 1) * HEAD_DIM] for hh in range(C_HEADS)],
        axis=-1)
    out_c = jnp.swapaxes(out_c, 0, 1).reshape(ts * bs, GROUP_WIDTH)
    c_k_new = c_k_new.reshape(bs, WINDOW, C_KV_HEADS, HEAD_DIM)
    c_v_new = c_v_new.reshape(bs, WINDOW, C_KV_HEADS, HEAD_DIM)

    x2 = _sample_post_call(x_tm, o, z, out_b, out_c, out_d, lw)
    new = {'a_state': a_state_new, 'a_conv': from_tm(a_tail, A_CONV - 1), 'b_conv': from_tm(b_tail, B_CONV - 1),
           'c_k': c_k_new, 'c_v': c_v_new, 'd_v': from_tm(dvn, ts)}
    return x2, new


def kernel(x_prompt, x_sample, state_delta, state_delta_conv, state_shortconv, cache_win_k, cache_win_v,
           state_ffn_conv, norm1_g, w_in, a_conv_w, a_log, a_dt_bias, a_norm_g, b_conv_w, c_sinks, d_ln_g,
           d_ln_b, d_ws, d_bias, w_out, norm2_g, ffn_w_gate, ffn_w_up, ffn_conv_w, ffn_w_down, final_norm_g):
    bp, tp, _ = x_prompt.shape
    bs, ts, _ = x_sample.shape
    depth = w_in.shape[0]
    win_buf = cache_win_k.shape[2]
    pos_p = np.arange(tp, dtype=np.int32)
    pos_s = PAST_LEN + np.arange(ts, dtype=np.int32)
    assert win_buf == WINDOW and ts == 4
    tabs_p = _rope_tables(pos_p)
    tabs_s = tuple(jnp.repeat(t, bs, axis=0) for t in _rope_tables(pos_s))
    fng = final_norm_g[None, :]
    ck_all = cache_win_k.reshape(depth, bs, WINDOW, 128)
    cv_all = cache_win_v.reshape(depth, bs, WINDOW, 128)
    P = CARRY_ROWS

    hp = x_prompt
    hs = jnp.swapaxes(x_sample, 0, 1).reshape(ts * bs, D_MODEL)
    outs = {k: [] for k in ('sp', 'ss', 'acp', 'acs', 'bcp', 'bcs', 'ckp', 'cks', 'cvp', 'cvs', 'fcp', 'fcs', 'dv')}
    for l in range(depth):
        lw = _layer_weights(l, norm1_g, w_in, a_conv_w, a_log, a_dt_bias, a_norm_g, b_conv_w, c_sinks, d_ln_g,
                            d_ln_b, d_ws, d_bias, w_out, norm2_g, ffn_w_gate, ffn_w_up, ffn_conv_w, ffn_w_down)
        last = l == depth - 1
        hp, s_bd, acv, bcv, ckn, cvn = _mixer_call(
            hp, lw, tabs_p, jnp.zeros((bp, P, A_QKV_W), F32), jnp.zeros((bp, P, 256), F32),
            jnp.zeros((bp, 256, 256), F32))
        hp, fcv = _ffn_call(hp, lw, jnp.zeros((bp, P, D_FF), F32), fng, tile=TILE_ROWS, stride=1, final_norm=last)
        outs['sp'].append(_diag_blocks(s_bd))
        outs['acp'].append(acv[:, P - (A_CONV - 1):])
        outs['bcp'].append(bcv[:, P - (B_CONV - 1):])
        outs['ckp'].append(ckn.reshape(bp, WINDOW, C_KV_HEADS, HEAD_DIM))
        outs['cvp'].append(cvn.reshape(bp, WINDOW, C_KV_HEADS, HEAD_DIM))
        outs['fcp'].append(fcv[:, P - (FFN_CONV - 1):])
        hs, ns = _sample_mixer(hs, lw, tabs_s, d_ws[l], d_bias[l], state_delta[l], state_delta_conv[l],
                               state_shortconv[l], ck_all, cv_all, l, bs, ts)
        f0 = jnp.swapaxes(state_ffn_conv[l], 0, 1).reshape(1, (FFN_CONV - 1) * bs, D_FF)
        ys_tm, fcs = _ffn_call(hs[None], lw, f0, fng, tile=ts * bs, stride=bs, final_norm=last)
        hs = ys_tm[0]
        outs['ss'].append(ns['a_state'])
        outs['acs'].append(ns['a_conv'])
        outs['bcs'].append(ns['b_conv'])
        outs['cks'].append(ns['c_k'])
        outs['cvs'].append(ns['c_v'])
        outs['fcs'].append(jnp.swapaxes(fcs.reshape(FFN_CONV - 1, bs, D_FF), 0, 1))
        outs['dv'].append(ns['d_v'])
    st = {k: jnp.stack(v) for k, v in outs.items()}
    hs = jnp.swapaxes(hs.reshape(ts, bs, D_MODEL), 0, 1)
    return (hp, hs, st['sp'], st['ss'], st['acp'], st['acs'], st['bcp'], st['bcs'], st['ckp'], st['cks'],
            st['cvp'], st['cvs'], st['fcp'], st['fcs'], st['dv'])
```

```python
import functools

import jax
import jax.numpy as jnp
import numpy as np
from jax import lax
from jax.experimental import pallas as pl
from jax.experimental.pallas import tpu as pltpu

F32 = jnp.float32
BF16 = jnp.bfloat16

D_MODEL = 1024
GROUP_WIDTH = 256
HEAD_DIM = 64
A_HEADS = 4
A_CONV = 4
A_CHUNK = 64
B_CONV = 3
C_HEADS = 4
C_KV_HEADS = 2
WINDOW = 128
ROPE_DIM = 16
ROPE_THETA = 500000.0
D_GROUPS = 4
D_CHUNK = 128
D_FF = 2816
FFN_CONV = 3
EPS = 1e-6
NEG_INF = -1e30
PAST_LEN = 16384

HEAD_SHIFT = 6
LANES = 128

COL_A = 0
A_QKV_W = 3 * GROUP_WIDTH
COL_Z = A_QKV_W
COL_SMALL = COL_Z + GROUP_WIDTH
W_A = COL_SMALL + LANES
COL_B = 1152
W_B = 768
COL_C = 1920
W_C = 512
COL_D = 2432
W_D = 512
P_PACKED = 2944

TILE_ROWS = 512
MIXER_TILE_ROWS = 512
CARRY_ROWS = 8
FFN_COL_CHUNK = 1408
VMEM_LIMIT_BYTES = 56 * 1024 * 1024
MIXER_VMEM_LIMIT_BYTES = 60 * 1024 * 1024


def _dot(a, b):
    return jnp.dot(a, b, preferred_element_type=F32)


def _dot_nt(a, b):
    return lax.dot_general(a, b, (((1,), (1,)), ((), ())), preferred_element_type=F32)


def _dot_tn(a, b):
    return lax.dot_general(a, b, (((0,), (0,)), ((), ())), preferred_element_type=F32)


def _split3(x):
    hi = x.astype(BF16)
    r1 = x - hi.astype(F32)
    mid = r1.astype(BF16)
    lo = (r1 - mid.astype(F32)).astype(BF16)
    return hi, mid, lo


def _dot_lhs3(x, w01):
    hi, mid, lo = _split3(x)
    return _dot(hi, w01) + _dot(mid, w01) + _dot(lo, w01)


def _head_sumsq(x, ones_bd):
    return _dot((x * x).astype(BF16), ones_bd)


def _sigmoid(x):
    return 1.0 / (1.0 + jnp.exp(-x))


def _silu(x):
    return x * _sigmoid(x)


def _softplus(x):
    return jnp.maximum(x, 0.0) + jnp.log(1.0 + jnp.exp(-jnp.abs(x)))


def _gelu_tanh(x):
    return 0.5 * x * (1.0 + jnp.tanh(np.sqrt(2.0 / np.pi).astype(np.float32) * (x + 0.044715 * (x * x * x))))


def _rms(x, g):
    return x * lax.rsqrt(jnp.mean(x * x, axis=-1, keepdims=True) + EPS) * g


def _iota(shape, dim):
    return lax.broadcasted_iota(jnp.int32, shape, dim)


MIXER_SEQS = 2


def _mixer_kernel(sinks_ref, x_ref, n1g_ref, w_in_ref, aconv_w_ref, alog_ref, dtb_ref, anorm_ref,
                  bconv_w_ref, cos_ref, sina_ref, sinb_ref, lng_ref, lnb_ref, ws_ref, dbias_ref,
                  w_out_ref, aconv0_ref, bconv0_ref, s0_ref,
                  x_out_ref, s_out_ref, aconv_out_ref, bconv_out_ref, ck_out_ref, cv_out_ref,
                  *scratch, tile):
    shared = (sinks_ref, n1g_ref, w_in_ref, aconv_w_ref, alog_ref, dtb_ref, anorm_ref, bconv_w_ref, cos_ref,
              sina_ref, sinb_ref, lng_ref, lnb_ref, ws_ref, dbias_ref, w_out_ref)
    per_seq = (x_ref, aconv0_ref, bconv0_ref, s0_ref, x_out_ref, s_out_ref, aconv_out_ref, bconv_out_ref,
               ck_out_ref, cv_out_ref)
    n_scr = len(scratch) // MIXER_SEQS
    gens = [_mixer_seq(*shared, *(r.at[b] for r in per_seq), *scratch[b * n_scr:(b + 1) * n_scr], tile=tile)
            for b in range(MIXER_SEQS)]
    started, live = 1, list(gens[:1])
    while live:
        for gen in list(live):
            if next(gen, 'done') == 'done':
                live.remove(gen)
        if started < len(gens):
            live.append(gens[started])
            started += 1


def _mixer_seq(sinks_ref, n1g_ref, w_in_ref, aconv_w_ref, alog_ref, dtb_ref, anorm_ref, bconv_w_ref, cos_ref,
               sina_ref, sinb_ref, lng_ref, lnb_ref, ws_ref, dbias_ref, w_out_ref,
               x_ref, aconv0_ref, bconv0_ref, s0_ref, x_out_ref, s_out_ref, aconv_out_ref, bconv_out_ref,
               ck_out_ref, cv_out_ref,
               abuf, bbuf, kbuf, vbuf, q_s, k_s, v_s, gcb_s, bb_s, o_s, s_scr, proj_s, *, tile):
    T = tile
    si = pl.program_id(1)
    P = CARRY_ROWS

    @pl.when(si == 0)
    def _():
        abuf[0:P, :] = aconv0_ref[...]
        bbuf[0:P, :] = bconv0_ref[...]
        kbuf[0:WINDOW, :] = jnp.zeros((WINDOW, 128), F32)
        vbuf[0:WINDOW, :] = jnp.zeros((WINDOW, 128), F32)
        s_scr[...] = s0_ref[...]

    res = {}
    h = _rms(x_ref[...], n1g_ref[...]).astype(BF16)
    for lo, width in ((COL_A, W_A), (COL_B, W_B), (COL_C, W_C), (COL_D, W_D)):
        proj_s[:, lo:lo + width] = _dot(h, w_in_ref[:, lo:lo + width])
    yield

    small = proj_s[:, COL_SMALL:W_A]
    abuf[P:P + T, :] = proj_s[:, 0:A_QKV_W]

    def group_b():
        bbuf[P:P + T, :] = proj_s[:, COL_B + 256:COL_B + 512] * proj_s[:, COL_B + 512:COL_B + 768]
        bx = (bbuf[P:P + T, :] * bconv_w_ref[2:3, :] + bbuf[P - 1:P - 1 + T, :] * bconv_w_ref[1:2, :]
              + bbuf[P - 2:P - 2 + T, :] * bconv_w_ref[0:1, :])
        tail_b = bbuf[T:T + P, :]
        bbuf[0:P, :] = tail_b
        bconv_out_ref[...] = tail_b
        res['out_b'] = proj_s[:, COL_B:COL_B + 256] * bx

    def group_c_rope():
        cos = cos_ref[...]
        sina = sina_ref[...]
        sinb = sinb_ref[...]
        cq = proj_s[:, COL_C:COL_C + 256]
        res['cq'] = (cq * jnp.concatenate([cos, cos], axis=1)
                     + pltpu.roll(cq, 256 - ROPE_DIM // 2, 1) * jnp.concatenate([sina, sina], axis=1)
                     + pltpu.roll(cq, ROPE_DIM // 2, 1) * jnp.concatenate([sinb, sinb], axis=1))
        ck = proj_s[:, COL_C + 256:COL_C + 384]
        ck = (ck * cos + pltpu.roll(ck, 128 - ROPE_DIM // 2, 1) * sina
              + pltpu.roll(ck, ROPE_DIM // 2, 1) * sinb)
        kbuf[WINDOW:WINDOW + T, :] = ck
        vbuf[WINDOW:WINDOW + T, :] = proj_s[:, COL_C + 384:COL_C + 512]
        ck_out_ref[...] = kbuf[T:T + WINDOW, :]
        cv_out_ref[...] = vbuf[T:T + WINDOW, :]

    def group_d_norm():
        res['du'] = _gelu_tanh(proj_s[:, COL_D:COL_D + 256])
        gv = _gelu_tanh(proj_s[:, COL_D + 256:COL_D + 512])
        mu = jnp.mean(gv, axis=-1, keepdims=True)
        xc = gv - mu
        res['dvn'] = (xc * lax.rsqrt(jnp.mean(xc * xc, axis=-1, keepdims=True) + EPS) * lng_ref[...]
                      + lnb_ref[...])
        wr = _iota((D_GROUPS * D_CHUNK, D_CHUNK), 0) & (D_CHUNK - 1)
        wc = _iota((D_GROUPS * D_CHUNK, D_CHUNK), 1)
        res['wm'] = jnp.where(wr >= wc, ws_ref[...], 0.0).astype(BF16)

    y = (abuf[P:P + T, :] * aconv_w_ref[3:4, :] + abuf[P - 1:P - 1 + T, :] * aconv_w_ref[2:3, :]
         + abuf[P - 2:P - 2 + T, :] * aconv_w_ref[1:2, :] + abuf[P - 3:P - 3 + T, :] * aconv_w_ref[0:1, :])
    tail_a = abuf[T:T + P, :]
    abuf[0:P, :] = tail_a
    aconv_out_ref[...] = tail_a
    qkv = _silu(y)
    q_raw = qkv[:, 0:256]
    k_raw = qkv[:, 256:512]
    v_s[...] = qkv[:, 512:768]

    r256 = _iota((256, 256), 0) >> HEAD_SHIFT
    c256 = _iota((256, 256), 1) >> HEAD_SHIFT
    mask_bd = r256 == c256
    ones_bd = jnp.where(mask_bd, 1.0, 0.0).astype(BF16)
    q_s[...] = q_raw * lax.rsqrt(_head_sumsq(q_raw, ones_bd) + EPS) * (HEAD_DIM ** -0.5)
    k_s[...] = k_raw * lax.rsqrt(_head_sumsq(k_raw, ones_bd) + EPS)
    yield

    g_log = -jnp.exp(alog_ref[...]) * _softplus(small + dtb_ref[...])
    beta = _sigmoid(small)
    gbeta = jnp.where(_iota((T, 128), 1) < A_HEADS, g_log, beta)
    expand = jnp.where(_iota((128, 512), 0) == (_iota((128, 512), 1) >> HEAD_SHIFT), 1.0, 0.0).astype(BF16)
    gbb = _dot_lhs3(gbeta, expand)
    bb_s[...] = gbb[:, 256:512]

    def chunk_cumsum(xv):
        row_in_chunk = _iota(xv.shape, 0) & (A_CHUNK - 1)
        step = 1
        while step < A_CHUNK:
            xv = xv + jnp.where(row_in_chunk >= step, pltpu.roll(xv, step, 0), 0.0)
            step *= 2
        return xv

    gcb_s[...] = chunk_cumsum(gbb[:, 0:256])
    gct = chunk_cumsum(g_log).T[0:8, :]
    low_half = (_iota((1, T), 1) & A_CHUNK) == 0
    gct_r = pltpu.roll(gct, A_CHUNK, 1)
    gct_l = pltpu.roll(gct, T - A_CHUNK, 1)
    even_rows = [jnp.where(low_half, gct[a:a + 1, :], gct_r[a + 1:a + 2, :]) for a in (0, 2)]
    odd_rows = [jnp.where(low_half, gct_l[a:a + 1, :], gct[a + 1:a + 2, :]) for a in (0, 2)]
    yield

    ri = _iota((A_CHUNK, 256), 0)
    ci = _iota((A_CHUNK, 256), 1) & (A_CHUNK - 1)
    causal_t = ri >= ci
    strict_t = ri > ci
    eye_t = jnp.where(ri == ci, 1.0, 0.0)

    def bd16(x16):
        return jnp.concatenate([x16, x16, x16, x16], axis=0) * ones_bd

    n_chunks = T // A_CHUNK
    t_inv, pw, qk, qdec, e_tail, vb16, kbe16 = [], [], [], [], [], [], []
    for c in range(n_chunks):
        rows = slice(c * A_CHUNK, (c + 1) * A_CHUNK)
        blk = slice((c // 2) * 128, (c // 2) * 128 + 128)
        src = even_rows if c % 2 == 0 else odd_rows
        gc_row = jnp.concatenate([src[0][:, blk], src[1][:, blk]], axis=1)
        qc = q_s[rows, :]
        kc = k_s[rows, :]
        bbc = bb_s[rows, :]
        gcb = gcb_s[rows, :]
        decay = jnp.where(causal_t, jnp.exp(jnp.where(causal_t, gcb - gc_row, 0.0)), 0.0)
        eg = jnp.exp(gcb)
        kb = kc * bbc
        aq = _dot_nt(jnp.concatenate([kb, qc], axis=0).astype(BF16), bd16(kc.astype(BF16)))
        a_mat = jnp.where(strict_t, aq[0:A_CHUNK] * decay, 0.0)
        qk.append(aq[A_CHUNK:2 * A_CHUNK] * decay)
        t_inv.append(eye_t - a_mat)
        pw.append(a_mat)
        qdec.append(qc * eg)
        e_tail.append(jnp.exp(gcb[A_CHUNK - 1:A_CHUNK, :] - gcb))
        vb16.append((v_s[rows, :] * bbc).astype(BF16))
        kbe16.append((kb * eg).astype(BF16))
    yield
    u, w = [], []
    state = {'s': s_scr[...]}

    def level_first():
        for c in range(n_chunks):
            p16 = pw[c].astype(BF16)
            pw[c] = _dot(p16, bd16(p16))

    def level_mid():
        for c in range(n_chunks):
            p16 = pw[c].astype(BF16)
            res = _dot(jnp.concatenate([p16, t_inv[c].astype(BF16)], axis=0), bd16(p16))
            pw[c] = res[0:A_CHUNK]
            t_inv[c] = t_inv[c] + res[A_CHUNK:2 * A_CHUNK]

    def level_last():
        for c in range(n_chunks):
            t_c = t_inv[c] + _dot(t_inv[c].astype(BF16), bd16(pw[c].astype(BF16)))
            uw = _dot(t_c.astype(BF16), jnp.concatenate([bd16(vb16[c]), bd16(kbe16[c])], axis=1))
            u.append(uw[:, 0:256])
            w.append(uw[:, 256:512])

    def scan_step(c):
        rows = slice(c * A_CHUNK, (c + 1) * A_CHUNK)
        s_bd = state['s']
        wq = _dot(jnp.concatenate([w[c], qdec[c]], axis=0).astype(BF16), s_bd.astype(BF16))
        v_new = u[c] - wq[0:A_CHUNK]
        o_s[rows, :] = wq[A_CHUNK:2 * A_CHUNK] + _dot(qk[c].astype(BF16), bd16(v_new.astype(BF16)))
        kv = _dot_tn(k_s[rows, :].astype(BF16), (v_new * e_tail[c]).astype(BF16))
        g_last = gcb_s[(c + 1) * A_CHUNK - 1:(c + 1) * A_CHUNK, :]
        state['s'] = s_bd * jnp.exp(g_last) + jnp.where(mask_bd, kv, 0.0)

    chain = [level_first] + [level_mid] * 4 + [level_last] + [functools.partial(scan_step, c) for c in range(n_chunks)]

    lane128 = _iota((2 * WINDOW, 128), 1)
    low = _iota((WINDOW, 128), 1) < HEAD_DIM
    qrow = _iota((2 * WINDOW, 2 * WINDOW), 0) & (WINDOW - 1)
    kcol = _iota((2 * WINDOW, 2 * WINDOW), 1)
    band = (kcol > qrow) & (kcol <= qrow + WINDOW)
    top_half = _iota((2 * WINDOW, 1), 0) < WINDOW
    out_c_blocks = []

    def group_c_block(n):
        first_key = si * T + (n - 1) * WINDOW
        valid = band & (kcol + first_key >= 0)
        kwin = kbuf[n * WINDOW:(n + 2) * WINDOW, :]
        vwin = vbuf[n * WINDOW:(n + 2) * WINDOW, :]
        k_sw = pltpu.roll(kwin, HEAD_DIM, 1)
        v_sw = pltpu.roll(vwin, HEAD_DIM, 1)
        pair_out = []
        for g in range(C_KV_HEADS):
            own = (lane128 < HEAD_DIM) if g == 0 else (lane128 >= HEAD_DIM)
            k_dup = jnp.where(own, kwin, k_sw).astype(BF16)
            v_dup = jnp.where(own, vwin, v_sw).astype(BF16)
            qp = res['cq'][n * WINDOW:(n + 1) * WINDOW, g * 128:(g + 1) * 128]
            q_st = jnp.concatenate([jnp.where(low, qp, 0.0), jnp.where(low, 0.0, qp)], axis=0).astype(BF16)
            s = _dot_nt(q_st, k_dup) * (HEAD_DIM ** -0.5)
            s = jnp.where(valid, s, NEG_INF)
            sink = jnp.where(top_half, sinks_ref[2 * g], sinks_ref[2 * g + 1])
            m = jnp.maximum(jnp.max(s, axis=-1, keepdims=True), sink)
            p = jnp.exp(s - m)
            denom = jnp.sum(p, axis=-1, keepdims=True) + jnp.exp(sink - m)
            o2 = _dot((p / denom).astype(BF16), v_dup)
            pair_out.append(jnp.where(low, o2[0:WINDOW], o2[WINDOW:2 * WINDOW]))
        out_c_blocks.append(jnp.concatenate(pair_out, axis=1))
        if n == T // WINDOW - 1:
            kbuf[0:WINDOW, :] = kbuf[T:T + WINDOW, :]
            vbuf[0:WINDOW, :] = vbuf[T:T + WINDOW, :]

    lane_grp = _iota((D_CHUNK, 256), 1) >> HEAD_SHIFT
    out_d_blocks = []

    def group_d_block(n):
        mx = _dot(res['wm'], res['dvn'][n * D_CHUNK:(n + 1) * D_CHUNK, :].astype(BF16))
        mixed = dbias_ref[...]
        for grp in range(D_GROUPS):
            mixed = mixed + jnp.where(lane_grp == grp, mx[grp * D_CHUNK:(grp + 1) * D_CHUNK, :], 0.0)
        out_d_blocks.append(res['du'][n * D_CHUNK:(n + 1) * D_CHUNK, :] * mixed)

    c_blocks = [functools.partial(group_c_block, n) for n in range(T // WINDOW)]
    d_blocks = [functools.partial(group_d_block, n) for n in range(T // D_CHUNK)]
    fill = [group_b, group_c_rope, group_d_norm] + c_blocks + d_blocks
    for i in range(max(len(chain), len(fill))):
        if i < len(chain):
            chain[i]()
        if i < len(fill):
            fill[i]()
        yield
    out_b = res['out_b']
    out_c = jnp.concatenate(out_c_blocks, axis=0)
    out_d = jnp.concatenate(out_d_blocks, axis=0)

    s_scr[...] = state['s']
    s_out_ref[...] = state['s']
    o = o_s[...]
    out_a = (o * lax.rsqrt(_head_sumsq(o, ones_bd) * (1.0 / HEAD_DIM) + EPS) * anorm_ref[...]
             * _silu(proj_s[:, COL_Z:COL_SMALL]))

    cat = jnp.concatenate([out_a, out_b, out_c, out_d], axis=1).astype(BF16)
    x_out_ref[...] = x_ref[...] + _dot(cat, w_out_ref[...])


def _mixer_call(x, lw, tabs, aconv0, bconv0, s0):
    bsz, seq, _ = x.shape
    T = MIXER_TILE_ROWS
    ns = seq // T
    NB = MIXER_SEQS
    P = CARRY_ROWS
    full = lambda shape: pl.BlockSpec(shape, lambda b, s, *_: (0,) * len(shape), pipeline_mode=pl.Buffered(1))
    per_b = lambda shape: pl.BlockSpec((NB,) + shape, lambda b, s, *_: (b,) + (0,) * len(shape))
    tab = pl.BlockSpec((T, 128), lambda b, s, *_: (s, 0))
    seq_scratch = [pltpu.VMEM(shape, F32) for shape in (
        (T + P, A_QKV_W), (T + P, 256), (T + WINDOW, 128), (T + WINDOW, 128), (T, 256), (T, 256), (T, 256),
        (T, 256), (T, 256), (T, 256), (256, 256), (T, P_PACKED))]
    grid_spec = pltpu.PrefetchScalarGridSpec(
        num_scalar_prefetch=1,
        grid=(bsz // NB, ns),
        in_specs=[
            pl.BlockSpec((NB, T, D_MODEL), lambda b, s, *_: (b, s, 0)),
            full((1, D_MODEL)), full((D_MODEL, P_PACKED)), full((A_CONV, A_QKV_W)), full((1, 128)), full((1, 128)),
            full((1, 256)), full((B_CONV, 256)), tab, tab, tab, full((1, 256)), full((1, 256)),
            full((D_GROUPS * D_CHUNK, D_CHUNK)), full((D_CHUNK, 256)), full((D_MODEL, D_MODEL)),
            per_b((P, A_QKV_W)), per_b((P, 256)), per_b((256, 256)),
        ],
        out_specs=[
            pl.BlockSpec((NB, T, D_MODEL), lambda b, s, *_: (b, s, 0)),
            per_b((256, 256)), per_b((P, A_QKV_W)), per_b((P, 256)), per_b((WINDOW, 128)), per_b((WINDOW, 128)),
        ],
        scratch_shapes=seq_scratch * NB,
    )
    out_shape = [
        jax.ShapeDtypeStruct((bsz, seq, D_MODEL), F32),
        jax.ShapeDtypeStruct((bsz, 256, 256), F32),
        jax.ShapeDtypeStruct((bsz, P, A_QKV_W), F32),
        jax.ShapeDtypeStruct((bsz, P, 256), F32),
        jax.ShapeDtypeStruct((bsz, WINDOW, 128), F32),
        jax.ShapeDtypeStruct((bsz, WINDOW, 128), F32),
    ]
    return pl.pallas_call(
        functools.partial(_mixer_kernel, tile=T),
        grid_spec=grid_spec,
        out_shape=out_shape,
        compiler_params=pltpu.CompilerParams(
            dimension_semantics=("arbitrary", "arbitrary"), vmem_limit_bytes=MIXER_VMEM_LIMIT_BYTES),
        name="mixer",
    )(lw['c_sinks'], x, lw['norm1_g'], lw['w_in'], lw['a_conv_w'], lw['a_log'], lw['a_dt_bias'],
      lw['a_norm_g'], lw['b_conv_w'], tabs[0], tabs[1], tabs[2], lw['d_ln_g'], lw['d_ln_b'],
      lw['d_ws'], lw['d_bias'], lw['w_out'], aconv0, bconv0, s0)


def _ffn_kernel(x_ref, n2g_ref, wg_ref, wu_ref, cw_ref, wd_ref, fconv0_ref, fng_ref,
                x_out_ref, fconv_out_ref, gbuf, *, tile, stride, final_norm):
    T = tile
    P = (FFN_CONV - 1) * stride if stride > 1 else CARRY_ROWS
    si = pl.program_id(1)

    @pl.when(si == 0)
    def _():
        gbuf[0:P, :] = fconv0_ref[0]

    x = x_ref[0]
    h = _rms(x, n2g_ref[...]).astype(BF16)
    acc = x
    for c0 in range(0, D_FF, FFN_COL_CHUNK):
        cols = slice(c0, c0 + FFN_COL_CHUNK)
        gbuf[P:P + T, cols] = _dot(h, wg_ref[:, cols])
        up = _dot(h, wu_ref[:, cols])
        gate = (gbuf[P:P + T, cols] * cw_ref[2:3, cols] + gbuf[P - stride:P - stride + T, cols] * cw_ref[1:2, cols]
                + gbuf[P - 2 * stride:P - 2 * stride + T, cols] * cw_ref[0:1, cols])
        acc = acc + _dot((_silu(gate) * up).astype(BF16), wd_ref[cols, :])
    tail = gbuf[T:T + P, :]
    gbuf[0:P, :] = tail
    fconv_out_ref[0] = tail
    if final_norm:
        acc = _rms(acc, fng_ref[...])
    x_out_ref[0] = acc


def _ffn_call(x, lw, fconv0, final_g, *, tile, stride, final_norm):
    bsz, seq, _ = x.shape
    T = tile
    P = (FFN_CONV - 1) * stride if stride > 1 else CARRY_ROWS
    full = lambda shape: pl.BlockSpec(shape, lambda b, s: (0,) * len(shape), pipeline_mode=pl.Buffered(1))
    per_b = lambda shape: pl.BlockSpec((1,) + shape, lambda b, s: (b,) + (0,) * len(shape))
    return pl.pallas_call(
        functools.partial(_ffn_kernel, tile=T, stride=stride, final_norm=final_norm),
        grid=(bsz, seq // T),
        in_specs=[
            pl.BlockSpec((1, T, D_MODEL), lambda b, s: (b, s, 0)),
            full((1, D_MODEL)), full((D_MODEL, D_FF)), full((D_MODEL, D_FF)), full((FFN_CONV, D_FF)),
            full((D_FF, D_MODEL)), per_b((P, D_FF)), full((1, D_MODEL)),
        ],
        out_specs=[pl.BlockSpec((1, T, D_MODEL), lambda b, s: (b, s, 0)), per_b((P, D_FF))],
        out_shape=[jax.ShapeDtypeStruct((bsz, seq, D_MODEL), F32), jax.ShapeDtypeStruct((bsz, P, D_FF), F32)],
        scratch_shapes=[pltpu.VMEM((T + P, D_FF), F32)],
        compiler_params=pltpu.CompilerParams(
            dimension_semantics=("arbitrary", "arbitrary"), vmem_limit_bytes=VMEM_LIMIT_BYTES),
        name="ffn",
    )(x, lw['norm2_g'], lw['ffn_w_gate'], lw['ffn_w_up'], lw['ffn_conv_w'], lw['ffn_w_down'], fconv0, final_g)


def _pack_w_in(w):
    pad = jnp.zeros((D_MODEL, 128 - 2 * A_HEADS), w.dtype)
    small = jnp.concatenate([w[:, 1028:1032], w[:, 1024:1028], pad], axis=1)
    return jnp.concatenate([w[:, 0:1024], small, w[:, 1032:]], axis=1).astype(BF16)


def _pad_lanes(v, width=128):
    return jnp.concatenate([v, jnp.zeros((width - v.shape[0],), v.dtype)])[None, :]


def _layer_weights(l, norm1_g, w_in, a_conv_w, a_log, a_dt_bias, a_norm_g, b_conv_w, c_sinks, d_ln_g, d_ln_b,
                   d_ws, d_bias, w_out, norm2_g, ffn_w_gate, ffn_w_up, ffn_conv_w, ffn_w_down):
    bias_tab = jnp.broadcast_to(d_bias[l].T[:, :, None], (D_CHUNK, D_GROUPS, HEAD_DIM)).reshape(D_CHUNK, 256)
    return {
        'norm1_g': norm1_g[l][None, :], 'w_in': _pack_w_in(w_in[l]), 'a_conv_w': a_conv_w[l],
        'a_log': _pad_lanes(a_log[l]), 'a_dt_bias': _pad_lanes(a_dt_bias[l]),
        'a_norm_g': jnp.tile(a_norm_g[l], A_HEADS)[None, :], 'b_conv_w': b_conv_w[l], 'c_sinks': c_sinks[l],
        'd_ln_g': d_ln_g[l][None, :], 'd_ln_b': d_ln_b[l][None, :],
        'd_ws': d_ws[l].reshape(D_GROUPS * D_CHUNK, D_CHUNK), 'd_bias': bias_tab,
        'w_out': w_out[l].astype(BF16), 'norm2_g': norm2_g[l][None, :],
        'ffn_w_gate': ffn_w_gate[l].astype(BF16), 'ffn_w_up': ffn_w_up[l].astype(BF16),
        'ffn_conv_w': ffn_conv_w[l], 'ffn_w_down': ffn_w_down[l].astype(BF16),
    }


def _rope_tables(pos):
    half = ROPE_DIM // 2
    inv = np.power(ROPE_THETA, -np.arange(half, dtype=np.float64) * (2.0 / ROPE_DIM))
    ang = pos.astype(np.float64)[:, None] * inv[None, :]
    cos, sin = jnp.asarray(np.cos(ang), F32), jnp.asarray(np.sin(ang), F32)
    n = pos.shape[0]
    rest = HEAD_DIM - ROPE_DIM
    cos_h = jnp.concatenate([cos, cos, jnp.ones((n, rest), F32)], axis=1)
    sina_h = jnp.concatenate([-sin, jnp.zeros((n, half + rest), F32)], axis=1)
    sinb_h = jnp.concatenate([jnp.zeros((n, half), F32), sin, jnp.zeros((n, rest), F32)], axis=1)
    return tuple(jnp.concatenate([t, t], axis=1) for t in (cos_h, sina_h, sinb_h))


def _diag_blocks(s_bd):
    return jnp.stack([s_bd[:, HEAD_DIM * i:HEAD_DIM * (i + 1), HEAD_DIM * i:HEAD_DIM * (i + 1)]
                      for i in range(A_HEADS)], axis=1)


def _sample_pre_kernel(x_ref, n1g_ref, w_in_ref, aconv_w_ref, alog_ref, dtb_ref, bconv_w_ref, cos_ref, sina_ref,
                       sinb_ref, lng_ref, lnb_ref, wtab_ref, btab_ref, aconv0_ref, bconv0_ref,
                       q_ref, k_ref, v_ref, g_ref, beta_ref, z_ref, outb_ref, cq_ref, ck_ref, cv_ref, outd_ref,
                       dvn_ref, aconv_out_ref, bconv_out_ref, abuf, bbuf, *, rows, stride):
    R, S = rows, stride
    steps = R // S
    pa_rows = (A_CONV - 1) * S
    pb_rows = (B_CONV - 1) * S
    x = x_ref[...]
    h = _rms(x, n1g_ref[...]).astype(BF16)

    pa = _dot(h, w_in_ref[:, COL_A:COL_A + W_A])
    z_ref[...] = pa[:, COL_Z:COL_SMALL]
    small = pa[:, COL_SMALL:W_A]
    abuf[0:pa_rows, :] = aconv0_ref[...]
    abuf[pa_rows:pa_rows + R, :] = pa[:, 0:A_QKV_W]
    y = abuf[0:R, :] * aconv_w_ref[0:1, :]
    for j in range(1, A_CONV):
        y = y + abuf[j * S:j * S + R, :] * aconv_w_ref[j:j + 1, :]
    aconv_out_ref[...] = abuf[R:R + pa_rows, :]
    qkv = _silu(y)
    q_raw = qkv[:, 0:256]
    k_raw = qkv[:, 256:512]
    v_ref[...] = qkv[:, 512:768]
    mask_bd = (_iota((256, 256), 0) >> HEAD_SHIFT) == (_iota((256, 256), 1) >> HEAD_SHIFT)
    ones_bd = jnp.where(mask_bd, 1.0, 0.0).astype(BF16)
    q_ref[...] = q_raw * lax.rsqrt(_dot_lhs3(q_raw * q_raw, ones_bd) + EPS) * (HEAD_DIM ** -0.5)
    k_ref[...] = k_raw * lax.rsqrt(_dot_lhs3(k_raw * k_raw, ones_bd) + EPS)
    g_ref[...] = -jnp.exp(alog_ref[...]) * _softplus(small + dtb_ref[...])
    beta_ref[...] = _sigmoid(small)

    pb = _dot(h, w_in_ref[:, COL_B:COL_B + W_B])
    bbuf[0:pb_rows, :] = bconv0_ref[...]
    bbuf[pb_rows:pb_rows + R, :] = pb[:, 256:512] * pb[:, 512:768]
    bx = bbuf[0:R, :] * bconv_w_ref[0:1, :]
    for j in range(1, B_CONV):
        bx = bx + bbuf[j * S:j * S + R, :] * bconv_w_ref[j:j + 1, :]
    bconv_out_ref[...] = bbuf[R:R + pb_rows, :]
    outb_ref[...] = pb[:, 0:256] * bx

    pc = _dot(h, w_in_ref[:, COL_C:COL_C + W_C])
    cos = cos_ref[...]
    sina = sina_ref[...]
    sinb = sinb_ref[...]
    cq = pc[:, 0:256]
    cq_ref[...] = (cq * jnp.concatenate([cos, cos], axis=1)
                   + pltpu.roll(cq, 256 - ROPE_DIM // 2, 1) * jnp.concatenate([sina, sina], axis=1)
                   + pltpu.roll(cq, ROPE_DIM // 2, 1) * jnp.concatenate([sinb, sinb], axis=1))
    ck = pc[:, 256:384]
    ck_ref[...] = (ck * cos + pltpu.roll(ck, 128 - ROPE_DIM // 2, 1) * sina
                   + pltpu.roll(ck, ROPE_DIM // 2, 1) * sinb)
    cv_ref[...] = pc[:, 384:512]

    pd = _dot(h, w_in_ref[:, COL_D:COL_D + W_D])
    du = _gelu_tanh(pd[:, 0:256])
    gv = _gelu_tanh(pd[:, 256:512])
    mu = jnp.mean(gv, axis=-1, keepdims=True)
    xc = gv - mu
    dvn = xc * lax.rsqrt(jnp.mean(xc * xc, axis=-1, keepdims=True) + EPS) * lng_ref[...] + lnb_ref[...]
    dvn_ref[...] = dvn
    for t in range(steps):
        mixed = btab_ref[t:t + 1, :]
        for s in range(t + 1):
            mixed = mixed + wtab_ref[t * steps + s:t * steps + s + 1, :] * dvn[s * S:(s + 1) * S, :]
        outd_ref[t * S:(t + 1) * S, :] = du[t * S:(t + 1) * S, :] * mixed


def _sample_pre_call(x, lw, tabs, wtab, btab, aconv0, bconv0, stride):
    R = x.shape[0]
    f = lambda *shape: jax.ShapeDtypeStruct(shape, F32)
    out_shape = [f(R, 256), f(R, 256), f(R, 256), f(R, 128), f(R, 128), f(R, 256), f(R, 256), f(R, 256),
                 f(R, 128), f(R, 128), f(R, 256), f(R, 256), f((A_CONV - 1) * stride, A_QKV_W),
                 f((B_CONV - 1) * stride, 256)]
    return pl.pallas_call(
        functools.partial(_sample_pre_kernel, rows=R, stride=stride),
        out_shape=out_shape,
        scratch_shapes=[pltpu.VMEM((R + (A_CONV - 1) * stride, A_QKV_W), F32),
                        pltpu.VMEM((R + (B_CONV - 1) * stride, 256), F32)],
        compiler_params=pltpu.CompilerParams(vmem_limit_bytes=VMEM_LIMIT_BYTES),
        name="sample_pre",
    )(x, lw['norm1_g'], lw['w_in'], lw['a_conv_w'], lw['a_log'], lw['a_dt_bias'], lw['b_conv_w'],
      tabs[0], tabs[1], tabs[2], lw['d_ln_g'], lw['d_ln_b'], wtab, btab, aconv0, bconv0)


DELTA_UNROLL = 4


def _sample_delta_kernel(g_ref, beta_ref, q_ref, k_ref, v_ref, s_ref, o_ref, snew_ref,
                         qt_s, kt_s, vt_s, gt_s, bt_s, ot_s, *, steps, nb):
    h = pl.program_id(0)
    n_i = HEAD_DIM
    zeros = jnp.zeros((HEAD_DIM, nb), F32)

    @pl.when(h == 0)
    def _():
        for t in range(steps):
            rows = slice(t * nb, (t + 1) * nb)
            qt_s[t] = q_ref[rows, :].T
            kt_s[t] = k_ref[rows, :].T
            vt_s[t] = v_ref[rows, :].T
            gt_s[t] = g_ref[rows, :].T
            bt_s[t] = beta_ref[rows, :].T

    base = pl.multiple_of(h * HEAD_DIM, HEAD_DIM)
    head_rows = pl.ds(base, HEAD_DIM)

    def rows_of(i):
        return pl.ds(pl.multiple_of(i * HEAD_DIM, HEAD_DIM), HEAD_DIM)

    def decay(t):
        return jnp.exp(gt_s[t, pl.ds(h, 1), :])

    dec0 = decay(0)

    def first_pass(i, acc):
        return acc + kt_s[0, pl.ds(base + i, 1), :] * (s_ref[0, rows_of(i), :] * dec0)

    ks = lax.fori_loop(0, n_i, first_pass, zeros, unroll=DELTA_UNROLL)
    for t in range(steps):
        dec = decay(t)
        v_new = bt_s[t, pl.ds(A_HEADS + h, 1), :] * (vt_s[t, head_rows, :] - ks)
        src = s_ref if t == 0 else snew_ref
        dec_next = decay(t + 1) if t + 1 < steps else None

        def update(i, carry, t=t, dec=dec, v_new=v_new, src=src, dec_next=dec_next):
            o_acc, ks_acc = carry
            blk = src[0, rows_of(i), :] * dec + kt_s[t, pl.ds(base + i, 1), :] * v_new
            snew_ref[0, rows_of(i), :] = blk
            o_acc = o_acc + qt_s[t, pl.ds(base + i, 1), :] * blk
            if dec_next is not None:
                ks_acc = ks_acc + kt_s[t + 1, pl.ds(base + i, 1), :] * (blk * dec_next)
            return o_acc, ks_acc

        o_acc, ks = lax.fori_loop(0, n_i, update, (zeros, zeros), unroll=DELTA_UNROLL)
        ot_s[t, head_rows, :] = o_acc

    @pl.when(h == A_HEADS - 1)
    def _():
        for t in range(steps):
            o_ref[t * nb:(t + 1) * nb, :] = ot_s[t].T


def _sample_delta_call(g, beta, q, k, v, s, steps):
    nh, _, nb = s.shape
    rows = steps * nb
    whole = lambda width: pl.BlockSpec((rows, width), lambda h: (0, 0))
    st = pl.BlockSpec((1, HEAD_DIM * HEAD_DIM, nb), lambda h: (h, 0, 0))
    wide = pltpu.VMEM((steps, GROUP_WIDTH, nb), F32)
    narrow = pltpu.VMEM((steps, 128, nb), F32)
    return pl.pallas_call(
        functools.partial(_sample_delta_kernel, steps=steps, nb=nb),
        grid=(nh,),
        in_specs=[whole(128), whole(128), whole(GROUP_WIDTH), whole(GROUP_WIDTH), whole(GROUP_WIDTH), st],
        out_specs=[whole(GROUP_WIDTH), st],
        out_shape=[jax.ShapeDtypeStruct((rows, GROUP_WIDTH), F32),
                   jax.ShapeDtypeStruct((nh, HEAD_DIM * HEAD_DIM, nb), F32)],
        scratch_shapes=[wide, wide, wide, narrow, narrow, wide],
        compiler_params=pltpu.CompilerParams(dimension_semantics=("arbitrary",),
                                             vmem_limit_bytes=VMEM_LIMIT_BYTES),
        name="sample_delta",
    )(g, beta, q, k, v, s)


SAMPLE_ATTN_BLOCK = 8
NEW_KEY_ROWS = 8


def _sample_attn_kernel(sinks_ref, qm_ref, kc_ref, kn_ref, vc_ref, vn_ref, o_ref, k_out_ref, v_out_ref, *, steps):
    nq = C_HEADS * steps
    nk = WINDOW + NEW_KEY_ROWS
    BB = SAMPLE_ATTN_BLOCK
    row = _iota((BB * nq, nk), 0)
    col = _iota((BB * nq, nk), 1)
    t_q = row & (steps - 1)
    valid = ((col < WINDOW) & (col > t_q)) | ((col >= WINDOW) & (col - WINDOW <= t_q))
    head = (_iota((BB * nq, 1), 0) >> 2) & (C_HEADS - 1)
    sink = jnp.where(head == 0, sinks_ref[0],
                     jnp.where(head == 1, sinks_ref[1], jnp.where(head == 2, sinks_ref[2], sinks_ref[3])))
    scores = [_dot_nt(qm_ref[b].astype(BF16), jnp.concatenate([kc_ref[0, b], kn_ref[b]], axis=0).astype(BF16))
              for b in range(BB)]
    s = jnp.where(valid, jnp.concatenate(scores, axis=0) * (HEAD_DIM ** -0.5), NEG_INF)
    m = jnp.maximum(jnp.max(s, axis=-1, keepdims=True), sink)
    p = jnp.exp(s - m)
    denom = jnp.sum(p, axis=-1, keepdims=True) + jnp.exp(sink - m)
    p16 = (p / denom).astype(BF16)
    is_new_row = _iota((WINDOW, 128), 0) >= WINDOW - steps
    pad_rows = jnp.zeros((WINDOW - NEW_KEY_ROWS, 128), F32)

    def slide(cache, new_rows):
        kept = pltpu.roll(cache, WINDOW - steps, 0)
        tail = jnp.concatenate([pad_rows, pltpu.roll(new_rows, NEW_KEY_ROWS - steps, 0)], axis=0)
        return jnp.where(is_new_row, tail, kept)

    for b in range(BB):
        v_all = jnp.concatenate([vc_ref[0, b], vn_ref[b]], axis=0).astype(BF16)
        o_ref[b] = _dot(p16[b * nq:(b + 1) * nq, :], v_all)
        k_out_ref[b] = slide(kc_ref[0, b], kn_ref[b])
        v_out_ref[b] = slide(vc_ref[0, b], vn_ref[b])


def _sample_attn_call(sinks, qm, kc_all, kn, vc_all, vn, layer, steps):
    bs, nq, _ = qm.shape
    BB = SAMPLE_ATTN_BLOCK
    blk = lambda r: pl.BlockSpec((BB, r, 128), lambda i, *_: (i, 0, 0))
    cache = pl.BlockSpec((1, BB, WINDOW, 128), lambda i, *_: (layer, i, 0, 0))
    grid_spec = pltpu.PrefetchScalarGridSpec(
        num_scalar_prefetch=1, grid=(bs // BB,),
        in_specs=[blk(nq), cache, blk(NEW_KEY_ROWS), cache, blk(NEW_KEY_ROWS)],
        out_specs=[blk(nq), blk(WINDOW), blk(WINDOW)])
    return pl.pallas_call(
        functools.partial(_sample_attn_kernel, steps=steps),
        grid_spec=grid_spec,
        out_shape=[jax.ShapeDtypeStruct((bs, nq, 128), F32), jax.ShapeDtypeStruct((bs, WINDOW, 128), F32),
                   jax.ShapeDtypeStruct((bs, WINDOW, 128), F32)],
        compiler_params=pltpu.CompilerParams(dimension_semantics=("arbitrary",)),
        name="sample_attn",
    )(sinks, qm, kc_all, kn, vc_all, vn)


def _sample_post_kernel(x_ref, o_ref, z_ref, outb_ref, outc_ref, outd_ref, anorm_ref, w_out_ref, x_out_ref):
    mask_bd = (_iota((256, 256), 0) >> HEAD_SHIFT) == (_iota((256, 256), 1) >> HEAD_SHIFT)
    ones_bd = jnp.where(mask_bd, 1.0, 0.0).astype(BF16)
    o = o_ref[...]
    out_a = (o * lax.rsqrt(_dot_lhs3(o * o, ones_bd) * (1.0 / HEAD_DIM) + EPS) * anorm_ref[...]
             * _silu(z_ref[...]))
    cat = jnp.concatenate([out_a, outb_ref[...], outc_ref[...], outd_ref[...]], axis=1).astype(BF16)
    x_out_ref[...] = x_ref[...] + _dot(cat, w_out_ref[...])


def _sample_post_call(x, o, z, out_b, out_c, out_d, lw):
    return pl.pallas_call(
        _sample_post_kernel,
        out_shape=jax.ShapeDtypeStruct(x.shape, F32),
        compiler_params=pltpu.CompilerParams(vmem_limit_bytes=VMEM_LIMIT_BYTES),
        name="sample_post",
    )(x, o, z, out_b, out_c, out_d, lw['a_norm_g'], lw['w_out'])


def _sample_mixer(x_tm, lw, tabs, d_ws_l, d_bias_l, a_state, a_conv, b_conv, c_k_all, c_v_all, layer, bs, ts):
    to_tm = lambda a: jnp.swapaxes(a, 0, 1).reshape(a.shape[1] * bs, a.shape[2])
    from_tm = lambda a, n: jnp.swapaxes(a.reshape(n, bs, a.shape[-1]), 0, 1)
    wtab = jnp.repeat(d_ws_l[:, :ts, :ts].transpose(1, 2, 0).reshape(ts * ts, D_GROUPS), HEAD_DIM, axis=1)
    btab = jnp.repeat(d_bias_l[:, :ts].T, HEAD_DIM, axis=1)
    (q, k, v, g, beta, z, out_b, cq, ck, cv, out_d, dvn, a_tail, b_tail) = _sample_pre_call(
        x_tm, lw, tabs, wtab, btab, to_tm(a_conv), to_tm(b_conv), bs)

    s_t = a_state.reshape(bs, A_HEADS, HEAD_DIM * HEAD_DIM).transpose(1, 2, 0)
    o, s_new_t = _sample_delta_call(g, beta, q, k, v, s_t, ts)
    a_state_new = s_new_t.transpose(2, 0, 1).reshape(bs, A_HEADS, HEAD_DIM, HEAD_DIM)

    cq4 = cq.reshape(ts, bs, C_HEADS, HEAD_DIM).transpose(1, 2, 0, 3)
    zq = jnp.zeros_like(cq4[:, 0])
    qm = jnp.concatenate(
        [jnp.concatenate([cq4[:, hh], zq] if hh // 2 == 0 else [zq, cq4[:, hh]], axis=-1) for hh in range(C_HEADS)],
        axis=1)
    pad_new = lambda a: jnp.concatenate([from_tm(a, ts), jnp.zeros((bs, NEW_KEY_ROWS - ts, 128), F32)], axis=1)
    o_att, c_k_new, c_v_new = _sample_attn_call(lw['c_sinks'], qm, c_k_all, pad_new(ck), c_v_all, pad_new(cv),
                                                layer, ts)
    out_c = jnp.concatenate(
        [o_att[:, hh * ts:(hh + 1) * ts, (hh // 2) * HEAD_DIM:(hh // 2 + 1) * HEAD_DIM] for hh in range(C_HEADS)],
        axis=-1)
    out_c = jnp.swapaxes(out_c, 0, 1).reshape(ts * bs, GROUP_WIDTH)
    c_k_new = c_k_new.reshape(bs, WINDOW, C_KV_HEADS, HEAD_DIM)
    c_v_new = c_v_new.reshape(bs, WINDOW, C_KV_HEADS, HEAD_DIM)

    x2 = _sample_post_call(x_tm, o, z, out_b, out_c, out_d, lw)
    new = {'a_state': a_state_new, 'a_conv': from_tm(a_tail, A_CONV - 1), 'b_conv': from_tm(b_tail, B_CONV - 1),
           'c_k': c_k_new, 'c_v': c_v_new, 'd_v': from_tm(dvn, ts)}
    return x2, new


def kernel(x_prompt, x_sample, state_delta, state_delta_conv, state_shortconv, cache_win_k, cache_win_v,
           state_ffn_conv, norm1_g, w_in, a_conv_w, a_log, a_dt_bias, a_norm_g, b_conv_w, c_sinks, d_ln_g,
           d_ln_b, d_ws, d_bias, w_out, norm2_g, ffn_w_gate, ffn_w_up, ffn_conv_w, ffn_w_down, final_norm_g):
    bp, tp, _ = x_prompt.shape
    bs, ts, _ = x_sample.shape
    depth = w_in.shape[0]
    win_buf = cache_win_k.shape[2]
    pos_p = np.arange(tp, dtype=np.int32)
    pos_s = PAST_LEN + np.arange(ts, dtype=np.int32)
    assert win_buf == WINDOW and ts == 4
    tabs_p = _rope_tables(pos_p)
    tabs_s = tuple(jnp.repeat(t, bs, axis=0) for t in _rope_tables(pos_s))
    fng = final_norm_g[None, :]
    ck_all = cache_win_k.reshape(depth, bs, WINDOW, 128)
    cv_all = cache_win_v.reshape(depth, bs, WINDOW, 128)
    P = CARRY_ROWS

    hp = x_prompt
    hs = jnp.swapaxes(x_sample, 0, 1).reshape(ts * bs, D_MODEL)
    outs = {k: [] for k in ('sp', 'ss', 'acp', 'acs', 'bcp', 'bcs', 'ckp', 'cks', 'cvp', 'cvs', 'fcp', 'fcs', 'dv')}
    for l in range(depth):
        lw = _layer_weights(l, norm1_g, w_in, a_conv_w, a_log, a_dt_bias, a_norm_g, b_conv_w, c_sinks, d_ln_g,
                            d_ln_b, d_ws, d_bias, w_out, norm2_g, ffn_w_gate, ffn_w_up, ffn_conv_w, ffn_w_down)
        last = l == depth - 1
        hp, s_bd, acv, bcv, ckn, cvn = _mixer_call(
            hp, lw, tabs_p, jnp.zeros((bp, P, A_QKV_W), F32), jnp.zeros((bp, P, 256), F32),
            jnp.zeros((bp, 256, 256), F32))
        hp, fcv = _ffn_call(hp, lw, jnp.zeros((bp, P, D_FF), F32), fng, tile=TILE_ROWS, stride=1, final_norm=last)
        outs['sp'].append(_diag_blocks(s_bd))
        outs['acp'].append(acv[:, P - (A_CONV - 1):])
        outs['bcp'].append(bcv[:, P - (B_CONV - 1):])
        outs['ckp'].append(ckn.reshape(bp, WINDOW, C_KV_HEADS, HEAD_DIM))
        outs['cvp'].append(cvn.reshape(bp, WINDOW, C_KV_HEADS, HEAD_DIM))
        outs['fcp'].append(fcv[:, P - (FFN_CONV - 1):])
        hs, ns = _sample_mixer(hs, lw, tabs_s, d_ws[l], d_bias[l], state_delta[l], state_delta_conv[l],
                               state_shortconv[l], ck_all, cv_all, l, bs, ts)
        f0 = jnp.swapaxes(state_ffn_conv[l], 0, 1).reshape(1, (FFN_CONV - 1) * bs, D_FF)
        ys_tm, fcs = _ffn_call(hs[None], lw, f0, fng, tile=ts * bs, stride=bs, final_norm=last)
        hs = ys_tm[0]
        outs['ss'].append(ns['a_state'])
        outs['acs'].append(ns['a_conv'])
        outs['bcs'].append(ns['b_conv'])
        outs['cks'].append(ns['c_k'])
        outs['cvs'].append(ns['c_v'])
        outs['fcs'].append(jnp.swapaxes(fcs.reshape(FFN_CONV - 1, bs, D_FF), 0, 1))
        outs['dv'].append(ns['d_v'])
    st = {k: jnp.stack(v) for k, v in outs.items()}
    hs = jnp.swapaxes(hs.reshape(ts, bs, D_MODEL), 0, 1)
    return (hp, hs, st['sp'], st['ss'], st['acp'], st['acs'], st['bcp'], st['bcs'], st['ckp'], st['cks'],
            st['cvp'], st['cvs'], st['fcp'], st['fcs'], st['dv'])
```

```python
import functools

import jax
import jax.numpy as jnp
import numpy as np
from jax import lax
from jax.experimental import pallas as pl
from jax.experimental.pallas import tpu as pltpu

F32 = jnp.float32
BF16 = jnp.bfloat16

D_MODEL = 1024
GROUP_WIDTH = 256
HEAD_DIM = 64
A_HEADS = 4
A_CONV = 4
A_CHUNK = 64
B_CONV = 3
C_HEADS = 4
C_KV_HEADS = 2
WINDOW = 128
ROPE_DIM = 16
ROPE_THETA = 500000.0
D_GROUPS = 4
D_CHUNK = 128
D_FF = 2816
FFN_CONV = 3
EPS = 1e-6
NEG_INF = -1e30
PAST_LEN = 16384

HEAD_SHIFT = 6
LANES = 128

COL_A = 0
A_QKV_W = 3 * GROUP_WIDTH
COL_Z = A_QKV_W
COL_SMALL = COL_Z + GROUP_WIDTH
W_A = COL_SMALL + LANES
COL_B = 1152
W_B = 768
COL_C = 1920
W_C = 512
COL_D = 2432
W_D = 512
P_PACKED = 2944

TILE_ROWS = 512
MIXER_TILE_ROWS = 512
CARRY_ROWS = 8
FFN_COL_CHUNK = 1408
VMEM_LIMIT_BYTES = 56 * 1024 * 1024
MIXER_VMEM_LIMIT_BYTES = 60 * 1024 * 1024


def _dot(a, b):
    return jnp.dot(a, b, preferred_element_type=F32)


def _dot_w32(a, w):
    return lax.dot_general(a, w, (((1,), (0,)), ((), ())), preferred_element_type=F32)


def _dot_nt(a, b):
    return lax.dot_general(a, b, (((1,), (1,)), ((), ())), preferred_element_type=F32)


def _dot_tn(a, b):
    return lax.dot_general(a, b, (((0,), (0,)), ((), ())), preferred_element_type=F32)


def _split3(x):
    hi = x.astype(BF16)
    r1 = x - hi.astype(F32)
    mid = r1.astype(BF16)
    lo = (r1 - mid.astype(F32)).astype(BF16)
    return hi, mid, lo


def _dot_lhs3(x, w01):
    hi, mid, lo = _split3(x)
    return _dot(hi, w01) + _dot(mid, w01) + _dot(lo, w01)


def _head_sumsq(x, ones_bd):
    return _dot((x * x).astype(BF16), ones_bd)


def _sigmoid(x):
    return 1.0 / (1.0 + jnp.exp(-x))


def _silu(x):
    return x * _sigmoid(x)


def _softplus(x):
    return jnp.maximum(x, 0.0) + jnp.log(1.0 + jnp.exp(-jnp.abs(x)))


def _gelu_tanh(x):
    return 0.5 * x * (1.0 + jnp.tanh(np.sqrt(2.0 / np.pi).astype(np.float32) * (x + 0.044715 * (x * x * x))))


def _rms(x, g):
    return x * lax.rsqrt(jnp.mean(x * x, axis=-1, keepdims=True) + EPS) * g


def _iota(shape, dim):
    return lax.broadcasted_iota(jnp.int32, shape, dim)


MIXER_SEQS = 2


def _mixer_kernel(sinks_ref, x_ref, n1g_ref, w_in_ref, aconv_w_ref, alog_ref, dtb_ref, anorm_ref,
                  bconv_w_ref, cos_ref, sina_ref, sinb_ref, lng_ref, lnb_ref, ws_ref, dbias_ref,
                  w_out_ref, aconv0_ref, bconv0_ref, s0_ref,
                  x_out_ref, s_out_ref, aconv_out_ref, bconv_out_ref, ck_out_ref, cv_out_ref,
                  *scratch, tile):
    shared = (sinks_ref, n1g_ref, w_in_ref, aconv_w_ref, alog_ref, dtb_ref, anorm_ref, bconv_w_ref, cos_ref,
              sina_ref, sinb_ref, lng_ref, lnb_ref, ws_ref, dbias_ref, w_out_ref)
    per_seq = (x_ref, aconv0_ref, bconv0_ref, s0_ref, x_out_ref, s_out_ref, aconv_out_ref, bconv_out_ref,
               ck_out_ref, cv_out_ref)
    n_scr = len(scratch) // MIXER_SEQS
    gens = [_mixer_seq(*shared, *(r.at[b] for r in per_seq), *scratch[b * n_scr:(b + 1) * n_scr], tile=tile)
            for b in range(MIXER_SEQS)]
    started, live = 1, list(gens[:1])
    while live:
        for gen in list(live):
            if next(gen, 'done') == 'done':
                live.remove(gen)
        if started < len(gens):
            live.append(gens[started])
            started += 1


def _mixer_seq(sinks_ref, n1g_ref, w_in_ref, aconv_w_ref, alog_ref, dtb_ref, anorm_ref, bconv_w_ref, cos_ref,
               sina_ref, sinb_ref, lng_ref, lnb_ref, ws_ref, dbias_ref, w_out_ref,
               x_ref, aconv0_ref, bconv0_ref, s0_ref, x_out_ref, s_out_ref, aconv_out_ref, bconv_out_ref,
               ck_out_ref, cv_out_ref,
               abuf, bbuf, kbuf, vbuf, q_s, k_s, v_s, gcb_s, bb_s, o_s, s_scr, proj_s, *, tile):
    T = tile
    si = pl.program_id(1)
    P = CARRY_ROWS

    @pl.when(si == 0)
    def _():
        abuf[0:P, :] = aconv0_ref[...]
        bbuf[0:P, :] = bconv0_ref[...]
        kbuf[0:WINDOW, :] = jnp.zeros((WINDOW, 128), F32)
        vbuf[0:WINDOW, :] = jnp.zeros((WINDOW, 128), F32)
        s_scr[...] = s0_ref[...]

    res = {}
    h = _rms(x_ref[...], n1g_ref[...]).astype(BF16)
    for lo, width in ((COL_A, W_A), (COL_B, W_B), (COL_C, W_C), (COL_D, W_D)):
        proj_s[:, lo:lo + width] = _dot(h, w_in_ref[:, lo:lo + width])
    yield

    small = proj_s[:, COL_SMALL:W_A]
    abuf[P:P + T, :] = proj_s[:, 0:A_QKV_W]

    def group_b():
        bbuf[P:P + T, :] = proj_s[:, COL_B + 256:COL_B + 512] * proj_s[:, COL_B + 512:COL_B + 768]
        bx = (bbuf[P:P + T, :] * bconv_w_ref[2:3, :] + bbuf[P - 1:P - 1 + T, :] * bconv_w_ref[1:2, :]
              + bbuf[P - 2:P - 2 + T, :] * bconv_w_ref[0:1, :])
        tail_b = bbuf[T:T + P, :]
        bbuf[0:P, :] = tail_b
        bconv_out_ref[...] = tail_b
        res['out_b'] = proj_s[:, COL_B:COL_B + 256] * bx

    def group_c_rope():
        cos = cos_ref[...]
        sina = sina_ref[...]
        sinb = sinb_ref[...]
        cq = proj_s[:, COL_C:COL_C + 256]
        res['cq'] = (cq * jnp.concatenate([cos, cos], axis=1)
                     + pltpu.roll(cq, 256 - ROPE_DIM // 2, 1) * jnp.concatenate([sina, sina], axis=1)
                     + pltpu.roll(cq, ROPE_DIM // 2, 1) * jnp.concatenate([sinb, sinb], axis=1))
        ck = proj_s[:, COL_C + 256:COL_C + 384]
        ck = (ck * cos + pltpu.roll(ck, 128 - ROPE_DIM // 2, 1) * sina
              + pltpu.roll(ck, ROPE_DIM // 2, 1) * sinb)
        kbuf[WINDOW:WINDOW + T, :] = ck
        vbuf[WINDOW:WINDOW + T, :] = proj_s[:, COL_C + 384:COL_C + 512]
        ck_out_ref[...] = kbuf[T:T + WINDOW, :]
        cv_out_ref[...] = vbuf[T:T + WINDOW, :]

    def group_d_norm():
        res['du'] = _gelu_tanh(proj_s[:, COL_D:COL_D + 256])
        gv = _gelu_tanh(proj_s[:, COL_D + 256:COL_D + 512])
        mu = jnp.mean(gv, axis=-1, keepdims=True)
        xc = gv - mu
        res['dvn'] = (xc * lax.rsqrt(jnp.mean(xc * xc, axis=-1, keepdims=True) + EPS) * lng_ref[...]
                      + lnb_ref[...])
        wr = _iota((D_GROUPS * D_CHUNK, D_CHUNK), 0) & (D_CHUNK - 1)
        wc = _iota((D_GROUPS * D_CHUNK, D_CHUNK), 1)
        res['wm'] = jnp.where(wr >= wc, ws_ref[...], 0.0).astype(BF16)

    y = (abuf[P:P + T, :] * aconv_w_ref[3:4, :] + abuf[P - 1:P - 1 + T, :] * aconv_w_ref[2:3, :]
         + abuf[P - 2:P - 2 + T, :] * aconv_w_ref[1:2, :] + abuf[P - 3:P - 3 + T, :] * aconv_w_ref[0:1, :])
    tail_a = abuf[T:T + P, :]
    abuf[0:P, :] = tail_a
    aconv_out_ref[...] = tail_a
    qkv = _silu(y)
    q_raw = qkv[:, 0:256]
    k_raw = qkv[:, 256:512]
    v_s[...] = qkv[:, 512:768]

    r256 = _iota((256, 256), 0) >> HEAD_SHIFT
    c256 = _iota((256, 256), 1) >> HEAD_SHIFT
    mask_bd = r256 == c256
    ones_bd = jnp.where(mask_bd, 1.0, 0.0).astype(BF16)
    q_s[...] = q_raw * lax.rsqrt(_head_sumsq(q_raw, ones_bd) + EPS) * (HEAD_DIM ** -0.5)
    k_s[...] = k_raw * lax.rsqrt(_head_sumsq(k_raw, ones_bd) + EPS)
    yield

    g_log = -jnp.exp(alog_ref[...]) * _softplus(small + dtb_ref[...])
    beta = _sigmoid(small)
    gbeta = jnp.where(_iota((T, 128), 1) < A_HEADS, g_log, beta)
    expand = jnp.where(_iota((128, 512), 0) == (_iota((128, 512), 1) >> HEAD_SHIFT), 1.0, 0.0).astype(BF16)
    gbb = _dot_lhs3(gbeta, expand)
    bb_s[...] = gbb[:, 256:512]

    def chunk_cumsum(xv):
        row_in_chunk = _iota(xv.shape, 0) & (A_CHUNK - 1)
        step = 1
        while step < A_CHUNK:
            xv = xv + jnp.where(row_in_chunk >= step, pltpu.roll(xv, step, 0), 0.0)
            step *= 2
        return xv

    gcb_s[...] = chunk_cumsum(gbb[:, 0:256])
    gct = chunk_cumsum(g_log).T[0:8, :]
    low_half = (_iota((1, T), 1) & A_CHUNK) == 0
    gct_r = pltpu.roll(gct, A_CHUNK, 1)
    gct_l = pltpu.roll(gct, T - A_CHUNK, 1)
    even_rows = [jnp.where(low_half, gct[a:a + 1, :], gct_r[a + 1:a + 2, :]) for a in (0, 2)]
    odd_rows = [jnp.where(low_half, gct_l[a:a + 1, :], gct[a + 1:a + 2, :]) for a in (0, 2)]
    yield

    ri = _iota((A_CHUNK, 256), 0)
    ci = _iota((A_CHUNK, 256), 1) & (A_CHUNK - 1)
    causal_t = ri >= ci
    strict_t = ri > ci
    eye_t = jnp.where(ri == ci, 1.0, 0.0)

    def bd16(x16):
        return jnp.concatenate([x16, x16, x16, x16], axis=0) * ones_bd

    n_chunks = T // A_CHUNK
    t_inv, pw, qk, qdec, e_tail, vb16, kbe16 = [], [], [], [], [], [], []
    for c in range(n_chunks):
        rows = slice(c * A_CHUNK, (c + 1) * A_CHUNK)
        blk = slice((c // 2) * 128, (c // 2) * 128 + 128)
        src = even_rows if c % 2 == 0 else odd_rows
        gc_row = jnp.concatenate([src[0][:, blk], src[1][:, blk]], axis=1)
        qc = q_s[rows, :]
        kc = k_s[rows, :]
        bbc = bb_s[rows, :]
        gcb = gcb_s[rows, :]
        decay = jnp.where(causal_t, jnp.exp(jnp.where(causal_t, gcb - gc_row, 0.0)), 0.0)
        eg = jnp.exp(gcb)
        kb = kc * bbc
        aq = _dot_nt(jnp.concatenate([kb, qc], axis=0).astype(BF16), bd16(kc.astype(BF16)))
        a_mat = jnp.where(strict_t, aq[0:A_CHUNK] * decay, 0.0)
        qk.append(aq[A_CHUNK:2 * A_CHUNK] * decay)
        t_inv.append(eye_t - a_mat)
        pw.append(a_mat)
        qdec.append(qc * eg)
        e_tail.append(jnp.exp(gcb[A_CHUNK - 1:A_CHUNK, :] - gcb))
        vb16.append((v_s[rows, :] * bbc).astype(BF16))
        kbe16.append((kb * eg).astype(BF16))
    yield
    u, w = [], []
    state = {'s': s_scr[...]}

    def level_first():
        for c in range(n_chunks):
            p16 = pw[c].astype(BF16)
            pw[c] = _dot(p16, bd16(p16))

    def level_mid():
        for c in range(n_chunks):
            p16 = pw[c].astype(BF16)
            res = _dot(jnp.concatenate([p16, t_inv[c].astype(BF16)], axis=0), bd16(p16))
            pw[c] = res[0:A_CHUNK]
            t_inv[c] = t_inv[c] + res[A_CHUNK:2 * A_CHUNK]

    def level_last():
        for c in range(n_chunks):
            t_c = t_inv[c] + _dot(t_inv[c].astype(BF16), bd16(pw[c].astype(BF16)))
            uw = _dot(t_c.astype(BF16), jnp.concatenate([bd16(vb16[c]), bd16(kbe16[c])], axis=1))
            u.append(uw[:, 0:256])
            w.append(uw[:, 256:512])

    def scan_step(c):
        rows = slice(c * A_CHUNK, (c + 1) * A_CHUNK)
        s_bd = state['s']
        wq = _dot(jnp.concatenate([w[c], qdec[c]], axis=0).astype(BF16), s_bd.astype(BF16))
        v_new = u[c] - wq[0:A_CHUNK]
        o_s[rows, :] = wq[A_CHUNK:2 * A_CHUNK] + _dot(qk[c].astype(BF16), bd16(v_new.astype(BF16)))
        kv = _dot_tn(k_s[rows, :].astype(BF16), (v_new * e_tail[c]).astype(BF16))
        g_last = gcb_s[(c + 1) * A_CHUNK - 1:(c + 1) * A_CHUNK, :]
        state['s'] = s_bd * jnp.exp(g_last) + jnp.where(mask_bd, kv, 0.0)

    chain = [level_first] + [level_mid] * 4 + [level_last] + [functools.partial(scan_step, c) for c in range(n_chunks)]

    lane128 = _iota((2 * WINDOW, 128), 1)
    low = _iota((WINDOW, 128), 1) < HEAD_DIM
    qrow = _iota((2 * WINDOW, 2 * WINDOW), 0) & (WINDOW - 1)
    kcol = _iota((2 * WINDOW, 2 * WINDOW), 1)
    band = (kcol > qrow) & (kcol <= qrow + WINDOW)
    top_half = _iota((2 * WINDOW, 1), 0) < WINDOW
    out_c_blocks = []

    def group_c_block(n):
        first_key = si * T + (n - 1) * WINDOW
        valid = band & (kcol + first_key >= 0)
        kwin = kbuf[n * WINDOW:(n + 2) * WINDOW, :]
        vwin = vbuf[n * WINDOW:(n + 2) * WINDOW, :]
        k_sw = pltpu.roll(kwin, HEAD_DIM, 1)
        v_sw = pltpu.roll(vwin, HEAD_DIM, 1)
        pair_out = []
        for g in range(C_KV_HEADS):
            own = (lane128 < HEAD_DIM) if g == 0 else (lane128 >= HEAD_DIM)
            k_dup = jnp.where(own, kwin, k_sw).astype(BF16)
            v_dup = jnp.where(own, vwin, v_sw).astype(BF16)
            qp = res['cq'][n * WINDOW:(n + 1) * WINDOW, g * 128:(g + 1) * 128]
            q_st = jnp.concatenate([jnp.where(low, qp, 0.0), jnp.where(low, 0.0, qp)], axis=0).astype(BF16)
            s = _dot_nt(q_st, k_dup) * (HEAD_DIM ** -0.5)
            s = jnp.where(valid, s, NEG_INF)
            sink = jnp.where(top_half, sinks_ref[2 * g], sinks_ref[2 * g + 1])
            m = jnp.maximum(jnp.max(s, axis=-1, keepdims=True), sink)
            p = jnp.exp(s - m)
            denom = jnp.sum(p, axis=-1, keepdims=True) + jnp.exp(sink - m)
            o2 = _dot((p / denom).astype(BF16), v_dup)
            pair_out.append(jnp.where(low, o2[0:WINDOW], o2[WINDOW:2 * WINDOW]))
        out_c_blocks.append(jnp.concatenate(pair_out, axis=1))
        if n == T // WINDOW - 1:
            kbuf[0:WINDOW, :] = kbuf[T:T + WINDOW, :]
            vbuf[0:WINDOW, :] = vbuf[T:T + WINDOW, :]

    lane_grp = _iota((D_CHUNK, 256), 1) >> HEAD_SHIFT
    out_d_blocks = []

    def group_d_block(n):
        mx = _dot(res['wm'], res['dvn'][n * D_CHUNK:(n + 1) * D_CHUNK, :].astype(BF16))
        mixed = dbias_ref[...]
        for grp in range(D_GROUPS):
            mixed = mixed + jnp.where(lane_grp == grp, mx[grp * D_CHUNK:(grp + 1) * D_CHUNK, :], 0.0)
        out_d_blocks.append(res['du'][n * D_CHUNK:(n + 1) * D_CHUNK, :] * mixed)

    c_blocks = [functools.partial(group_c_block, n) for n in range(T // WINDOW)]
    d_blocks = [functools.partial(group_d_block, n) for n in range(T // D_CHUNK)]
    fill = [group_b, group_c_rope, group_d_norm] + c_blocks + d_blocks
    for i in range(max(len(chain), len(fill))):
        if i < len(chain):
            chain[i]()
        if i < len(fill):
            fill[i]()
        yield
    out_b = res['out_b']
    out_c = jnp.concatenate(out_c_blocks, axis=0)
    out_d = jnp.concatenate(out_d_blocks, axis=0)

    s_scr[...] = state['s']
    s_out_ref[...] = state['s']
    o = o_s[...]
    out_a = (o * lax.rsqrt(_head_sumsq(o, ones_bd) * (1.0 / HEAD_DIM) + EPS) * anorm_ref[...]
             * _silu(proj_s[:, COL_Z:COL_SMALL]))

    cat = jnp.concatenate([out_a, out_b, out_c, out_d], axis=1).astype(BF16)
    x_out_ref[...] = x_ref[...] + _dot(cat, w_out_ref[...])


def _mixer_call(x, lw, tabs, aconv0, bconv0, s0):
    bsz, seq, _ = x.shape
    T = MIXER_TILE_ROWS
    ns = seq // T
    NB = MIXER_SEQS
    P = CARRY_ROWS
    full = lambda shape: pl.BlockSpec(shape, lambda b, s, *_: (0,) * len(shape), pipeline_mode=pl.Buffered(1))
    per_b = lambda shape: pl.BlockSpec((NB,) + shape, lambda b, s, *_: (b,) + (0,) * len(shape))
    tab = pl.BlockSpec((T, 128), lambda b, s, *_: (s, 0))
    seq_scratch = [pltpu.VMEM(shape, F32) for shape in (
        (T + P, A_QKV_W), (T + P, 256), (T + WINDOW, 128), (T + WINDOW, 128), (T, 256), (T, 256), (T, 256),
        (T, 256), (T, 256), (T, 256), (256, 256), (T, P_PACKED))]
    grid_spec = pltpu.PrefetchScalarGridSpec(
        num_scalar_prefetch=1,
        grid=(bsz // NB, ns),
        in_specs=[
            pl.BlockSpec((NB, T, D_MODEL), lambda b, s, *_: (b, s, 0)),
            full((1, D_MODEL)), full((D_MODEL, P_PACKED)), full((A_CONV, A_QKV_W)), full((1, 128)), full((1, 128)),
            full((1, 256)), full((B_CONV, 256)), tab, tab, tab, full((1, 256)), full((1, 256)),
            full((D_GROUPS * D_CHUNK, D_CHUNK)), full((D_CHUNK, 256)), full((D_MODEL, D_MODEL)),
            per_b((P, A_QKV_W)), per_b((P, 256)), per_b((256, 256)),
        ],
        out_specs=[
            pl.BlockSpec((NB, T, D_MODEL), lambda b, s, *_: (b, s, 0)),
            per_b((256, 256)), per_b((P, A_QKV_W)), per_b((P, 256)), per_b((WINDOW, 128)), per_b((WINDOW, 128)),
        ],
        scratch_shapes=seq_scratch * NB,
    )
    out_shape = [
        jax.ShapeDtypeStruct((bsz, seq, D_MODEL), F32),
        jax.ShapeDtypeStruct((bsz, 256, 256), F32),
        jax.ShapeDtypeStruct((bsz, P, A_QKV_W), F32),
        jax.ShapeDtypeStruct((bsz, P, 256), F32),
        jax.ShapeDtypeStruct((bsz, WINDOW, 128), F32),
        jax.ShapeDtypeStruct((bsz, WINDOW, 128), F32),
    ]
    return pl.pallas_call(
        functools.partial(_mixer_kernel, tile=T),
        grid_spec=grid_spec,
        out_shape=out_shape,
        compiler_params=pltpu.CompilerParams(
            dimension_semantics=("arbitrary", "arbitrary"), vmem_limit_bytes=MIXER_VMEM_LIMIT_BYTES),
        name="mixer",
    )(lw['c_sinks'], x, lw['norm1_g'], lw['w_in'], lw['a_conv_w'], lw['a_log'], lw['a_dt_bias'],
      lw['a_norm_g'], lw['b_conv_w'], tabs[0], tabs[1], tabs[2], lw['d_ln_g'], lw['d_ln_b'],
      lw['d_ws'], lw['d_bias'], lw['w_out'], aconv0, bconv0, s0)


def _ffn_kernel(x_ref, n2g_ref, wg_ref, wu_ref, cw_ref, wd_ref, fconv0_ref, fng_ref,
                x_out_ref, fconv_out_ref, gbuf, *, tile, stride, final_norm):
    T = tile
    P = (FFN_CONV - 1) * stride if stride > 1 else CARRY_ROWS
    si = pl.program_id(1)

    @pl.when(si == 0)
    def _():
        gbuf[0:P, :] = fconv0_ref[0]

    x = x_ref[0]
    h = _rms(x, n2g_ref[...]).astype(BF16)
    acc = x
    for c0 in range(0, D_FF, FFN_COL_CHUNK):
        cols = slice(c0, c0 + FFN_COL_CHUNK)
        gbuf[P:P + T, cols] = _dot_w32(h, wg_ref[0, :, cols])
        up = _dot_w32(h, wu_ref[0, :, cols])
        gate = (gbuf[P:P + T, cols] * cw_ref[2:3, cols] + gbuf[P - stride:P - stride + T, cols] * cw_ref[1:2, cols]
                + gbuf[P - 2 * stride:P - 2 * stride + T, cols] * cw_ref[0:1, cols])
        acc = acc + _dot_w32((_silu(gate) * up).astype(BF16), wd_ref[0, cols, :])
    tail = gbuf[T:T + P, :]
    gbuf[0:P, :] = tail
    fconv_out_ref[0] = tail
    if final_norm:
        acc = _rms(acc, fng_ref[...])
    x_out_ref[0] = acc


def _ffn_call(x, lw, fconv0, final_g, *, layer, tile, stride, final_norm):
    bsz, seq, _ = x.shape
    T = tile
    P = (FFN_CONV - 1) * stride if stride > 1 else CARRY_ROWS
    full = lambda shape: pl.BlockSpec(shape, lambda b, s: (0,) * len(shape), pipeline_mode=pl.Buffered(1))
    per_b = lambda shape: pl.BlockSpec((1,) + shape, lambda b, s: (b,) + (0,) * len(shape))
    weight = lambda shape: pl.BlockSpec((1,) + shape, lambda b, s: (layer, 0, 0), pipeline_mode=pl.Buffered(1))
    return pl.pallas_call(
        functools.partial(_ffn_kernel, tile=T, stride=stride, final_norm=final_norm),
        grid=(bsz, seq // T),
        in_specs=[
            pl.BlockSpec((1, T, D_MODEL), lambda b, s: (b, s, 0)),
            full((1, D_MODEL)), weight((D_MODEL, D_FF)), weight((D_MODEL, D_FF)), full((FFN_CONV, D_FF)),
            weight((D_FF, D_MODEL)), per_b((P, D_FF)), full((1, D_MODEL)),
        ],
        out_specs=[pl.BlockSpec((1, T, D_MODEL), lambda b, s: (b, s, 0)), per_b((P, D_FF))],
        out_shape=[jax.ShapeDtypeStruct((bsz, seq, D_MODEL), F32), jax.ShapeDtypeStruct((bsz, P, D_FF), F32)],
        scratch_shapes=[pltpu.VMEM((T + P, D_FF), F32)],
        compiler_params=pltpu.CompilerParams(
            dimension_semantics=("arbitrary", "arbitrary"), vmem_limit_bytes=VMEM_LIMIT_BYTES),
        name="ffn",
    )(x, lw['norm2_g'], lw['ffn_w_gate'], lw['ffn_w_up'], lw['ffn_conv_w'], lw['ffn_w_down'], fconv0, final_g)


def _pack_w_in(w):
    pad = jnp.zeros((D_MODEL, 128 - 2 * A_HEADS), w.dtype)
    small = jnp.concatenate([w[:, 1028:1032], w[:, 1024:1028], pad], axis=1)
    return jnp.concatenate([w[:, 0:1024], small, w[:, 1032:]], axis=1).astype(BF16)


def _pad_lanes(v, width=128):
    return jnp.concatenate([v, jnp.zeros((width - v.shape[0],), v.dtype)])[None, :]


def _layer_weights(l, norm1_g, w_in, a_conv_w, a_log, a_dt_bias, a_norm_g, b_conv_w, c_sinks, d_ln_g, d_ln_b,
                   d_ws, d_bias, w_out, norm2_g, ffn_w_gate, ffn_w_up, ffn_conv_w, ffn_w_down):
    bias_tab = jnp.broadcast_to(d_bias[l].T[:, :, None], (D_CHUNK, D_GROUPS, HEAD_DIM)).reshape(D_CHUNK, 256)
    return {
        'norm1_g': norm1_g[l][None, :], 'w_in': _pack_w_in(w_in[l]), 'a_conv_w': a_conv_w[l],
        'a_log': _pad_lanes(a_log[l]), 'a_dt_bias': _pad_lanes(a_dt_bias[l]),
        'a_norm_g': jnp.tile(a_norm_g[l], A_HEADS)[None, :], 'b_conv_w': b_conv_w[l], 'c_sinks': c_sinks[l],
        'd_ln_g': d_ln_g[l][None, :], 'd_ln_b': d_ln_b[l][None, :],
        'd_ws': d_ws[l].reshape(D_GROUPS * D_CHUNK, D_CHUNK), 'd_bias': bias_tab,
        'w_out': w_out[l].astype(BF16), 'norm2_g': norm2_g[l][None, :],
        'ffn_w_gate': ffn_w_gate, 'ffn_w_up': ffn_w_up, 'ffn_conv_w': ffn_conv_w[l], 'ffn_w_down': ffn_w_down,
    }


def _rope_tables(pos):
    half = ROPE_DIM // 2
    inv = np.power(ROPE_THETA, -np.arange(half, dtype=np.float64) * (2.0 / ROPE_DIM))
    ang = pos.astype(np.float64)[:, None] * inv[None, :]
    cos, sin = jnp.asarray(np.cos(ang), F32), jnp.asarray(np.sin(ang), F32)
    n = pos.shape[0]
    rest = HEAD_DIM - ROPE_DIM
    cos_h = jnp.concatenate([cos, cos, jnp.ones((n, rest), F32)], axis=1)
    sina_h = jnp.concatenate([-sin, jnp.zeros((n, half + rest), F32)], axis=1)
    sinb_h = jnp.concatenate([jnp.zeros((n, half), F32), sin, jnp.zeros((n, rest), F32)], axis=1)
    return tuple(jnp.concatenate([t, t], axis=1) for t in (cos_h, sina_h, sinb_h))


def _diag_blocks(s_bd):
    return jnp.stack([s_bd[:, HEAD_DIM * i:HEAD_DIM * (i + 1), HEAD_DIM * i:HEAD_DIM * (i + 1)]
                      for i in range(A_HEADS)], axis=1)


def _sample_pre_kernel(x_ref, n1g_ref, w_in_ref, aconv_w_ref, alog_ref, dtb_ref, bconv_w_ref, cos_ref, sina_ref,
                       sinb_ref, lng_ref, lnb_ref, wtab_ref, btab_ref, aconv0_ref, bconv0_ref,
                       q_ref, k_ref, v_ref, g_ref, beta_ref, z_ref, outb_ref, cq_ref, ck_ref, cv_ref, outd_ref,
                       dvn_ref, aconv_out_ref, bconv_out_ref, abuf, bbuf, *, rows, stride):
    R, S = rows, stride
    steps = R // S
    pa_rows = (A_CONV - 1) * S
    pb_rows = (B_CONV - 1) * S
    x = x_ref[...]
    h = _rms(x, n1g_ref[...]).astype(BF16)

    pa = _dot(h, w_in_ref[:, COL_A:COL_A + W_A])
    z_ref[...] = pa[:, COL_Z:COL_SMALL]
    small = pa[:, COL_SMALL:W_A]
    abuf[0:pa_rows, :] = aconv0_ref[...]
    abuf[pa_rows:pa_rows + R, :] = pa[:, 0:A_QKV_W]
    y = abuf[0:R, :] * aconv_w_ref[0:1, :]
    for j in range(1, A_CONV):
        y = y + abuf[j * S:j * S + R, :] * aconv_w_ref[j:j + 1, :]
    aconv_out_ref[...] = abuf[R:R + pa_rows, :]
    qkv = _silu(y)
    q_raw = qkv[:, 0:256]
    k_raw = qkv[:, 256:512]
    v_ref[...] = qkv[:, 512:768]
    mask_bd = (_iota((256, 256), 0) >> HEAD_SHIFT) == (_iota((256, 256), 1) >> HEAD_SHIFT)
    ones_bd = jnp.where(mask_bd, 1.0, 0.0).astype(BF16)
    q_ref[...] = q_raw * lax.rsqrt(_dot_lhs3(q_raw * q_raw, ones_bd) + EPS) * (HEAD_DIM ** -0.5)
    k_ref[...] = k_raw * lax.rsqrt(_dot_lhs3(k_raw * k_raw, ones_bd) + EPS)
    g_ref[...] = -jnp.exp(alog_ref[...]) * _softplus(small + dtb_ref[...])
    beta_ref[...] = _sigmoid(small)

    pb = _dot(h, w_in_ref[:, COL_B:COL_B + W_B])
    bbuf[0:pb_rows, :] = bconv0_ref[...]
    bbuf[pb_rows:pb_rows + R, :] = pb[:, 256:512] * pb[:, 512:768]
    bx = bbuf[0:R, :] * bconv_w_ref[0:1, :]
    for j in range(1, B_CONV):
        bx = bx + bbuf[j * S:j * S + R, :] * bconv_w_ref[j:j + 1, :]
    bconv_out_ref[...] = bbuf[R:R + pb_rows, :]
    outb_ref[...] = pb[:, 0:256] * bx

    pc = _dot(h, w_in_ref[:, COL_C:COL_C + W_C])
    cos = cos_ref[...]
    sina = sina_ref[...]
    sinb = sinb_ref[...]
    cq = pc[:, 0:256]
    cq_ref[...] = (cq * jnp.concatenate([cos, cos], axis=1)
                   + pltpu.roll(cq, 256 - ROPE_DIM // 2, 1) * jnp.concatenate([sina, sina], axis=1)
                   + pltpu.roll(cq, ROPE_DIM // 2, 1) * jnp.concatenate([sinb, sinb], axis=1))
    ck = pc[:, 256:384]
    ck_ref[...] = (ck * cos + pltpu.roll(ck, 128 - ROPE_DIM // 2, 1) * sina
                   + pltpu.roll(ck, ROPE_DIM // 2, 1) * sinb)
    cv_ref[...] = pc[:, 384:512]

    pd = _dot(h, w_in_ref[:, COL_D:COL_D + W_D])
    du = _gelu_tanh(pd[:, 0:256])
    gv = _gelu_tanh(pd[:, 256:512])
    mu = jnp.mean(gv, axis=-1, keepdims=True)
    xc = gv - mu
    dvn = xc * lax.rsqrt(jnp.mean(xc * xc, axis=-1, keepdims=True) + EPS) * lng_ref[...] + lnb_ref[...]
    dvn_ref[...] = dvn
    for t in range(steps):
        mixed = btab_ref[t:t + 1, :]
        for s in range(t + 1):
            mixed = mixed + wtab_ref[t * steps + s:t * steps + s + 1, :] * dvn[s * S:(s + 1) * S, :]
        outd_ref[t * S:(t + 1) * S, :] = du[t * S:(t + 1) * S, :] * mixed


def _sample_pre_call(x, lw, tabs, wtab, btab, aconv0, bconv0, stride):
    R = x.shape[0]
    f = lambda *shape: jax.ShapeDtypeStruct(shape, F32)
    out_shape = [f(R, 256), f(R, 256), f(R, 256), f(R, 128), f(R, 128), f(R, 256), f(R, 256), f(R, 256),
                 f(R, 128), f(R, 128), f(R, 256), f(R, 256), f((A_CONV - 1) * stride, A_QKV_W),
                 f((B_CONV - 1) * stride, 256)]
    return pl.pallas_call(
        functools.partial(_sample_pre_kernel, rows=R, stride=stride),
        out_shape=out_shape,
        scratch_shapes=[pltpu.VMEM((R + (A_CONV - 1) * stride, A_QKV_W), F32),
                        pltpu.VMEM((R + (B_CONV - 1) * stride, 256), F32)],
        compiler_params=pltpu.CompilerParams(vmem_limit_bytes=VMEM_LIMIT_BYTES),
        name="sample_pre",
    )(x, lw['norm1_g'], lw['w_in'], lw['a_conv_w'], lw['a_log'], lw['a_dt_bias'], lw['b_conv_w'],
      tabs[0], tabs[1], tabs[2], lw['d_ln_g'], lw['d_ln_b'], wtab, btab, aconv0, bconv0)


DELTA_UNROLL = 4


def _sample_delta_kernel(g_ref, beta_ref, q_ref, k_ref, v_ref, s_ref, o_ref, snew_ref,
                         qt_s, kt_s, vt_s, gt_s, bt_s, ot_s, *, steps, nb):
    h = pl.program_id(0)
    n_i = HEAD_DIM
    zeros = jnp.zeros((HEAD_DIM, nb), F32)

    @pl.when(h == 0)
    def _():
        for t in range(steps):
            rows = slice(t * nb, (t + 1) * nb)
            qt_s[t] = q_ref[rows, :].T
            kt_s[t] = k_ref[rows, :].T
            vt_s[t] = v_ref[rows, :].T
            gt_s[t] = g_ref[rows, :].T
            bt_s[t] = beta_ref[rows, :].T

    base = pl.multiple_of(h * HEAD_DIM, HEAD_DIM)
    head_rows = pl.ds(base, HEAD_DIM)

    def rows_of(i):
        return pl.ds(pl.multiple_of(i * HEAD_DIM, HEAD_DIM), HEAD_DIM)

    def decay(t):
        return jnp.exp(gt_s[t, pl.ds(h, 1), :])

    dec0 = decay(0)

    def first_pass(i, acc):
        return acc + kt_s[0, pl.ds(base + i, 1), :] * (s_ref[0, rows_of(i), :] * dec0)

    ks = lax.fori_loop(0, n_i, first_pass, zeros, unroll=DELTA_UNROLL)
    for t in range(steps):
        dec = decay(t)
        v_new = bt_s[t, pl.ds(A_HEADS + h, 1), :] * (vt_s[t, head_rows, :] - ks)
        src = s_ref if t == 0 else snew_ref
        dec_next = decay(t + 1) if t + 1 < steps else None

        def update(i, carry, t=t, dec=dec, v_new=v_new, src=src, dec_next=dec_next):
            o_acc, ks_acc = carry
            blk = src[0, rows_of(i), :] * dec + kt_s[t, pl.ds(base + i, 1), :] * v_new
            snew_ref[0, rows_of(i), :] = blk
            o_acc = o_acc + qt_s[t, pl.ds(base + i, 1), :] * blk
            if dec_next is not None:
                ks_acc = ks_acc + kt_s[t + 1, pl.ds(base + i, 1), :] * (blk * dec_next)
            return o_acc, ks_acc

        o_acc, ks = lax.fori_loop(0, n_i, update, (zeros, zeros), unroll=DELTA_UNROLL)
        ot_s[t, head_rows, :] = o_acc

    @pl.when(h == A_HEADS - 1)
    def _():
        for t in range(steps):
            o_ref[t * nb:(t + 1) * nb, :] = ot_s[t].T


def _sample_delta_call(g, beta, q, k, v, s, steps):
    nh, _, nb = s.shape
    rows = steps * nb
    whole = lambda width: pl.BlockSpec((rows, width), lambda h: (0, 0))
    st = pl.BlockSpec((1, HEAD_DIM * HEAD_DIM, nb), lambda h: (h, 0, 0))
    wide = pltpu.VMEM((steps, GROUP_WIDTH, nb), F32)
    narrow = pltpu.VMEM((steps, 128, nb), F32)
    return pl.pallas_call(
        functools.partial(_sample_delta_kernel, steps=steps, nb=nb),
        grid=(nh,),
        in_specs=[whole(128), whole(128), whole(GROUP_WIDTH), whole(GROUP_WIDTH), whole(GROUP_WIDTH), st],
        out_specs=[whole(GROUP_WIDTH), st],
        out_shape=[jax.ShapeDtypeStruct((rows, GROUP_WIDTH), F32),
                   jax.ShapeDtypeStruct((nh, HEAD_DIM * HEAD_DIM, nb), F32)],
        scratch_shapes=[wide, wide, wide, narrow, narrow, wide],
        compiler_params=pltpu.CompilerParams(dimension_semantics=("arbitrary",),
                                             vmem_limit_bytes=VMEM_LIMIT_BYTES),
        name="sample_delta",
    )(g, beta, q, k, v, s)


SAMPLE_ATTN_BLOCK = 8
NEW_KEY_ROWS = 8


def _sample_attn_kernel(sinks_ref, qm_ref, kc_ref, kn_ref, vc_ref, vn_ref, o_ref, k_out_ref, v_out_ref, *, steps):
    nq = C_HEADS * steps
    nk = WINDOW + NEW_KEY_ROWS
    BB = SAMPLE_ATTN_BLOCK
    row = _iota((BB * nq, nk), 0)
    col = _iota((BB * nq, nk), 1)
    t_q = row & (steps - 1)
    valid = ((col < WINDOW) & (col > t_q)) | ((col >= WINDOW) & (col - WINDOW <= t_q))
    head = (_iota((BB * nq, 1), 0) >> 2) & (C_HEADS - 1)
    sink = jnp.where(head == 0, sinks_ref[0],
                     jnp.where(head == 1, sinks_ref[1], jnp.where(head == 2, sinks_ref[2], sinks_ref[3])))
    scores = [_dot_nt(qm_ref[b].astype(BF16), jnp.concatenate([kc_ref[0, b], kn_ref[b]], axis=0).astype(BF16))
              for b in range(BB)]
    s = jnp.where(valid, jnp.concatenate(scores, axis=0) * (HEAD_DIM ** -0.5), NEG_INF)
    m = jnp.maximum(jnp.max(s, axis=-1, keepdims=True), sink)
    p = jnp.exp(s - m)
    denom = jnp.sum(p, axis=-1, keepdims=True) + jnp.exp(sink - m)
    p16 = (p / denom).astype(BF16)
    is_new_row = _iota((WINDOW, 128), 0) >= WINDOW - steps
    pad_rows = jnp.zeros((WINDOW - NEW_KEY_ROWS, 128), F32)

    def slide(cache, new_rows):
        kept = pltpu.roll(cache, WINDOW - steps, 0)
        tail = jnp.concatenate([pad_rows, pltpu.roll(new_rows, NEW_KEY_ROWS - steps, 0)], axis=0)
        return jnp.where(is_new_row, tail, kept)

    for b in range(BB):
        v_all = jnp.concatenate([vc_ref[0, b], vn_ref[b]], axis=0).astype(BF16)
        o_ref[b] = _dot(p16[b * nq:(b + 1) * nq, :], v_all)
        k_out_ref[b] = slide(kc_ref[0, b], kn_ref[b])
        v_out_ref[b] = slide(vc_ref[0, b], vn_ref[b])


def _sample_attn_call(sinks, qm, kc_all, kn, vc_all, vn, layer, steps):
    bs, nq, _ = qm.shape
    BB = SAMPLE_ATTN_BLOCK
    blk = lambda r: pl.BlockSpec((BB, r, 128), lambda i, *_: (i, 0, 0))
    cache = pl.BlockSpec((1, BB, WINDOW, 128), lambda i, *_: (layer, i, 0, 0))
    grid_spec = pltpu.PrefetchScalarGridSpec(
        num_scalar_prefetch=1, grid=(bs // BB,),
        in_specs=[blk(nq), cache, blk(NEW_KEY_ROWS), cache, blk(NEW_KEY_ROWS)],
        out_specs=[blk(nq), blk(WINDOW), blk(WINDOW)])
    return pl.pallas_call(
        functools.partial(_sample_attn_kernel, steps=steps),
        grid_spec=grid_spec,
        out_shape=[jax.ShapeDtypeStruct((bs, nq, 128), F32), jax.ShapeDtypeStruct((bs, WINDOW, 128), F32),
                   jax.ShapeDtypeStruct((bs, WINDOW, 128), F32)],
        compiler_params=pltpu.CompilerParams(dimension_semantics=("arbitrary",)),
        name="sample_attn",
    )(sinks, qm, kc_all, kn, vc_all, vn)


def _sample_post_kernel(x_ref, o_ref, z_ref, outb_ref, outc_ref, outd_ref, anorm_ref, w_out_ref, x_out_ref):
    mask_bd = (_iota((256, 256), 0) >> HEAD_SHIFT) == (_iota((256, 256), 1) >> HEAD_SHIFT)
    ones_bd = jnp.where(mask_bd, 1.0, 0.0).astype(BF16)
    o = o_ref[...]
    out_a = (o * lax.rsqrt(_dot_lhs3(o * o, ones_bd) * (1.0 / HEAD_DIM) + EPS) * anorm_ref[...]
             * _silu(z_ref[...]))
    cat = jnp.concatenate([out_a, outb_ref[...], outc_ref[...], outd_ref[...]], axis=1).astype(BF16)
    x_out_ref[...] = x_ref[...] + _dot(cat, w_out_ref[...])


def _sample_post_call(x, o, z, out_b, out_c, out_d, lw):
    return pl.pallas_call(
        _sample_post_kernel,
        out_shape=jax.ShapeDtypeStruct(x.shape, F32),
        compiler_params=pltpu.CompilerParams(vmem_limit_bytes=VMEM_LIMIT_BYTES),
        name="sample_post",
    )(x, o, z, out_b, out_c, out_d, lw['a_norm_g'], lw['w_out'])


def _sample_mixer(x_tm, lw, tabs, d_ws_l, d_bias_l, a_state, a_conv, b_conv, c_k_all, c_v_all, layer, bs, ts):
    to_tm = lambda a: jnp.swapaxes(a, 0, 1).reshape(a.shape[1] * bs, a.shape[2])
    from_tm = lambda a, n: jnp.swapaxes(a.reshape(n, bs, a.shape[-1]), 0, 1)
    wtab = jnp.repeat(d_ws_l[:, :ts, :ts].transpose(1, 2, 0).reshape(ts * ts, D_GROUPS), HEAD_DIM, axis=1)
    btab = jnp.repeat(d_bias_l[:, :ts].T, HEAD_DIM, axis=1)
    (q, k, v, g, beta, z, out_b, cq, ck, cv, out_d, dvn, a_tail, b_tail) = _sample_pre_call(
        x_tm, lw, tabs, wtab, btab, to_tm(a_conv), to_tm(b_conv), bs)

    s_t = a_state.reshape(bs, A_HEADS, HEAD_DIM * HEAD_DIM).transpose(1, 2, 0)
    o, s_new_t = _sample_delta_call(g, beta, q, k, v, s_t, ts)
    a_state_new = s_new_t.transpose(2, 0, 1).reshape(bs, A_HEADS, HEAD_DIM, HEAD_DIM)

    cq4 = cq.reshape(ts, bs, C_HEADS, HEAD_DIM).transpose(1, 2, 0, 3)
    zq = jnp.zeros_like(cq4[:, 0])
    qm = jnp.concatenate(
        [jnp.concatenate([cq4[:, hh], zq] if hh // 2 == 0 else [zq, cq4[:, hh]], axis=-1) for hh in range(C_HEADS)],
        axis=1)
    pad_new = lambda a: jnp.concatenate([from_tm(a, ts), jnp.zeros((bs, NEW_KEY_ROWS - ts, 128), F32)], axis=1)
    o_att, c_k_new, c_v_new = _sample_attn_call(lw['c_sinks'], qm, c_k_all, pad_new(ck), c_v_all, pad_new(cv),
                                                layer, ts)
    out_c = jnp.concatenate(
        [o_att[:, hh * ts:(hh + 1) * ts, (hh // 2) * HEAD_DIM:(hh // 2 + 1) * HEAD_DIM] for hh in range(C_HEADS)],
        axis=-1)
    out_c = jnp.swapaxes(out_c, 0, 1).reshape(ts * bs, GROUP_WIDTH)
    c_k_new = c_k_new.reshape(bs, WINDOW, C_KV_HEADS, HEAD_DIM)
    c_v_new = c_v_new.reshape(bs, WINDOW, C_KV_HEADS, HEAD_DIM)

    x2 = _sample_post_call(x_tm, o, z, out_b, out_c, out_d, lw)
    new = {'a_state': a_state_new, 'a_conv': from_tm(a_tail, A_CONV - 1), 'b_conv': from_tm(b_tail, B_CONV - 1),
           'c_k': c_k_new, 'c_v': c_v_new, 'd_v': from_tm(dvn, ts)}
    return x2, new


def kernel(x_prompt, x_sample, state_delta, state_delta_conv, state_shortconv, cache_win_k, cache_win_v,
           state_ffn_conv, norm1_g, w_in, a_conv_w, a_log, a_dt_bias, a_norm_g, b_conv_w, c_sinks, d_ln_g,
           d_ln_b, d_ws, d_bias, w_out, norm2_g, ffn_w_gate, ffn_w_up, ffn_conv_w, ffn_w_down, final_norm_g):
    bp, tp, _ = x_prompt.shape
    bs, ts, _ = x_sample.shape
    depth = w_in.shape[0]
    win_buf = cache_win_k.shape[2]
    pos_p = np.arange(tp, dtype=np.int32)
    pos_s = PAST_LEN + np.arange(ts, dtype=np.int32)
    assert win_buf == WINDOW and ts == 4
    tabs_p = _rope_tables(pos_p)
    tabs_s = tuple(jnp.repeat(t, bs, axis=0) for t in _rope_tables(pos_s))
    fng = final_norm_g[None, :]
    ck_all = cache_win_k.reshape(depth, bs, WINDOW, 128)
    cv_all = cache_win_v.reshape(depth, bs, WINDOW, 128)
    P = CARRY_ROWS

    hp = x_prompt
    hs = jnp.swapaxes(x_sample, 0, 1).reshape(ts * bs, D_MODEL)
    outs = {k: [] for k in ('sp', 'ss', 'acp', 'acs', 'bcp', 'bcs', 'ckp', 'cks', 'cvp', 'cvs', 'fcp', 'fcs', 'dv')}
    for l in range(depth):
        lw = _layer_weights(l, norm1_g, w_in, a_conv_w, a_log, a_dt_bias, a_norm_g, b_conv_w, c_sinks, d_ln_g,
                            d_ln_b, d_ws, d_bias, w_out, norm2_g, ffn_w_gate, ffn_w_up, ffn_conv_w, ffn_w_down)
        last = l == depth - 1
        hp, s_bd, acv, bcv, ckn, cvn = _mixer_call(
            hp, lw, tabs_p, jnp.zeros((bp, P, A_QKV_W), F32), jnp.zeros((bp, P, 256), F32),
            jnp.zeros((bp, 256, 256), F32))
        hp, fcv = _ffn_call(hp, lw, jnp.zeros((bp, P, D_FF), F32), fng, layer=l, tile=TILE_ROWS, stride=1,
                            final_norm=last)
        outs['sp'].append(_diag_blocks(s_bd))
        outs['acp'].append(acv[:, P - (A_CONV - 1):])
        outs['bcp'].append(bcv[:, P - (B_CONV - 1):])
        outs['ckp'].append(ckn.reshape(bp, WINDOW, C_KV_HEADS, HEAD_DIM))
        outs['cvp'].append(cvn.reshape(bp, WINDOW, C_KV_HEADS, HEAD_DIM))
        outs['fcp'].append(fcv[:, P - (FFN_CONV - 1):])
        hs, ns = _sample_mixer(hs, lw, tabs_s, d_ws[l], d_bias[l], state_delta[l], state_delta_conv[l],
                               state_shortconv[l], ck_all, cv_all, l, bs, ts)
        f0 = jnp.swapaxes(state_ffn_conv[l], 0, 1).reshape(1, (FFN_CONV - 1) * bs, D_FF)
        ys_tm, fcs = _ffn_call(hs[None], lw, f0, fng, layer=l, tile=ts * bs, stride=bs, final_norm=last)
        hs = ys_tm[0]
        outs['ss'].append(ns['a_state'])
        outs['acs'].append(ns['a_conv'])
        outs['bcs'].append(ns['b_conv'])
        outs['cks'].append(ns['c_k'])
        outs['cvs'].append(ns['c_v'])
        outs['fcs'].append(jnp.swapaxes(fcs.reshape(FFN_CONV - 1, bs, D_FF), 0, 1))
        outs['dv'].append(ns['d_v'])
    st = {k: jnp.stack(v) for k, v in outs.items()}
    hs = jnp.swapaxes(hs.reshape(ts, bs, D_MODEL), 0, 1)
    return (hp, hs, st['sp'], st['ss'], st['acp'], st['acs'], st['bcp'], st['bcs'], st['ckp'], st['cks'],
            st['cvp'], st['cvs'], st['fcp'], st['fcs'], st['dv'])
```

```python
import functools

import jax
import jax.numpy as jnp
import numpy as np
from jax import lax
from jax.experimental import pallas as pl
from jax.experimental.pallas import tpu as pltpu

F32 = jnp.float32
BF16 = jnp.bfloat16

D_MODEL = 1024
GROUP_WIDTH = 256
HEAD_DIM = 64
A_HEADS = 4
A_CONV = 4
A_CHUNK = 64
B_CONV = 3
C_HEADS = 4
C_KV_HEADS = 2
WINDOW = 128
ROPE_DIM = 16
ROPE_THETA = 500000.0
D_GROUPS = 4
D_CHUNK = 128
D_FF = 2816
FFN_CONV = 3
EPS = 1e-6
NEG_INF = -1e30
PAST_LEN = 16384

HEAD_SHIFT = 6
LANES = 128

COL_A = 0
A_QKV_W = 3 * GROUP_WIDTH
COL_Z = A_QKV_W
COL_SMALL = COL_Z + GROUP_WIDTH
W_A = COL_SMALL + LANES
COL_B = 1152
W_B = 768
COL_C = 1920
W_C = 512
COL_D = 2432
W_D = 512
P_PACKED = 2944

TILE_ROWS = 512
MIXER_TILE_ROWS = 512
CARRY_ROWS = 8
FFN_COL_CHUNK = 1408
VMEM_LIMIT_BYTES = 56 * 1024 * 1024
MIXER_VMEM_LIMIT_BYTES = 60 * 1024 * 1024


def _dot(a, b):
    return jnp.dot(a, b, preferred_element_type=F32)


def _dot_w32(a, w):
    return lax.dot_general(a, w, (((1,), (0,)), ((), ())), preferred_element_type=F32)


def _dot_nt(a, b):
    return lax.dot_general(a, b, (((1,), (1,)), ((), ())), preferred_element_type=F32)


def _dot_tn(a, b):
    return lax.dot_general(a, b, (((0,), (0,)), ((), ())), preferred_element_type=F32)


def _split3(x):
    hi = x.astype(BF16)
    r1 = x - hi.astype(F32)
    mid = r1.astype(BF16)
    lo = (r1 - mid.astype(F32)).astype(BF16)
    return hi, mid, lo


def _dot_lhs3(x, w01):
    hi, mid, lo = _split3(x)
    return _dot(hi, w01) + _dot(mid, w01) + _dot(lo, w01)


def _head_sumsq(x, ones_bd):
    return _dot((x * x).astype(BF16), ones_bd)


def _sigmoid(x):
    return 1.0 / (1.0 + jnp.exp(-x))


def _silu(x):
    return x * _sigmoid(x)


def _softplus(x):
    return jnp.maximum(x, 0.0) + jnp.log(1.0 + jnp.exp(-jnp.abs(x)))


def _gelu_tanh(x):
    return 0.5 * x * (1.0 + jnp.tanh(np.sqrt(2.0 / np.pi).astype(np.float32) * (x + 0.044715 * (x * x * x))))


def _rms(x, g):
    return x * lax.rsqrt(jnp.mean(x * x, axis=-1, keepdims=True) + EPS) * g


def _iota(shape, dim):
    return lax.broadcasted_iota(jnp.int32, shape, dim)


MIXER_SEQS = 2


def _mixer_kernel(sinks_ref, x_ref, n1g_ref, w_in_ref, aconv_w_ref, alog_ref, dtb_ref, anorm_ref,
                  bconv_w_ref, cos_ref, sina_ref, sinb_ref, lng_ref, lnb_ref, ws_ref, dbias_ref,
                  w_out_ref, aconv0_ref, bconv0_ref, s0_ref,
                  x_out_ref, s_out_ref, aconv_out_ref, bconv_out_ref, ck_out_ref, cv_out_ref,
                  *scratch, tile):
    shared = (sinks_ref, n1g_ref, w_in_ref, aconv_w_ref, alog_ref, dtb_ref, anorm_ref, bconv_w_ref, cos_ref,
              sina_ref, sinb_ref, lng_ref, lnb_ref, ws_ref, dbias_ref, w_out_ref)
    per_seq = (x_ref, aconv0_ref, bconv0_ref, s0_ref, x_out_ref, s_out_ref, aconv_out_ref, bconv_out_ref,
               ck_out_ref, cv_out_ref)
    n_scr = len(scratch) // MIXER_SEQS
    gens = [_mixer_seq(*shared, *(r.at[b] for r in per_seq), *scratch[b * n_scr:(b + 1) * n_scr], tile=tile)
            for b in range(MIXER_SEQS)]
    started, live = 1, list(gens[:1])
    while live:
        for gen in list(live):
            if next(gen, 'done') == 'done':
                live.remove(gen)
        if started < len(gens):
            live.append(gens[started])
            started += 1


def _mixer_seq(sinks_ref, n1g_ref, w_in_ref, aconv_w_ref, alog_ref, dtb_ref, anorm_ref, bconv_w_ref, cos_ref,
               sina_ref, sinb_ref, lng_ref, lnb_ref, ws_ref, dbias_ref, w_out_ref,
               x_ref, aconv0_ref, bconv0_ref, s0_ref, x_out_ref, s_out_ref, aconv_out_ref, bconv_out_ref,
               ck_out_ref, cv_out_ref,
               abuf, bbuf, kbuf, vbuf, q_s, k_s, v_s, gcb_s, bb_s, o_s, s_scr, proj_s, *, tile):
    T = tile
    si = pl.program_id(1)
    P = CARRY_ROWS

    @pl.when(si == 0)
    def _():
        abuf[0:P, :] = aconv0_ref[...]
        bbuf[0:P, :] = bconv0_ref[...]
        kbuf[0:WINDOW, :] = jnp.zeros((WINDOW, 128), F32)
        vbuf[0:WINDOW, :] = jnp.zeros((WINDOW, 128), F32)
        s_scr[...] = s0_ref[...]

    res = {}
    h = _rms(x_ref[...], n1g_ref[...]).astype(BF16)
    for lo, width in ((COL_A, W_A), (COL_B, W_B), (COL_C, W_C), (COL_D, W_D)):
        proj_s[:, lo:lo + width] = _dot(h, w_in_ref[:, lo:lo + width])
    yield

    small = proj_s[:, COL_SMALL:W_A]
    abuf[P:P + T, :] = proj_s[:, 0:A_QKV_W]

    def group_b():
        bbuf[P:P + T, :] = proj_s[:, COL_B + 256:COL_B + 512] * proj_s[:, COL_B + 512:COL_B + 768]
        bx = (bbuf[P:P + T, :] * bconv_w_ref[2:3, :] + bbuf[P - 1:P - 1 + T, :] * bconv_w_ref[1:2, :]
              + bbuf[P - 2:P - 2 + T, :] * bconv_w_ref[0:1, :])
        tail_b = bbuf[T:T + P, :]
        bbuf[0:P, :] = tail_b
        bconv_out_ref[...] = tail_b
        res['out_b'] = proj_s[:, COL_B:COL_B + 256] * bx

    def group_c_rope():
        cos = cos_ref[...]
        sina = sina_ref[...]
        sinb = sinb_ref[...]
        cq = proj_s[:, COL_C:COL_C + 256]
        res['cq'] = (cq * jnp.concatenate([cos, cos], axis=1)
                     + pltpu.roll(cq, 256 - ROPE_DIM // 2, 1) * jnp.concatenate([sina, sina], axis=1)
                     + pltpu.roll(cq, ROPE_DIM // 2, 1) * jnp.concatenate([sinb, sinb], axis=1))
        ck = proj_s[:, COL_C + 256:COL_C + 384]
        ck = (ck * cos + pltpu.roll(ck, 128 - ROPE_DIM // 2, 1) * sina
              + pltpu.roll(ck, ROPE_DIM // 2, 1) * sinb)
        kbuf[WINDOW:WINDOW + T, :] = ck
        vbuf[WINDOW:WINDOW + T, :] = proj_s[:, COL_C + 384:COL_C + 512]
        ck_out_ref[...] = kbuf[T:T + WINDOW, :]
        cv_out_ref[...] = vbuf[T:T + WINDOW, :]

    def group_d_norm():
        res['du'] = _gelu_tanh(proj_s[:, COL_D:COL_D + 256])
        gv = _gelu_tanh(proj_s[:, COL_D + 256:COL_D + 512])
        mu = jnp.mean(gv, axis=-1, keepdims=True)
        xc = gv - mu
        res['dvn'] = (xc * lax.rsqrt(jnp.mean(xc * xc, axis=-1, keepdims=True) + EPS) * lng_ref[...]
                      + lnb_ref[...])
        wr = _iota((D_GROUPS * D_CHUNK, D_CHUNK), 0) & (D_CHUNK - 1)
        wc = _iota((D_GROUPS * D_CHUNK, D_CHUNK), 1)
        res['wm'] = jnp.where(wr >= wc, ws_ref[...], 0.0).astype(BF16)

    y = (abuf[P:P + T, :] * aconv_w_ref[3:4, :] + abuf[P - 1:P - 1 + T, :] * aconv_w_ref[2:3, :]
         + abuf[P - 2:P - 2 + T, :] * aconv_w_ref[1:2, :] + abuf[P - 3:P - 3 + T, :] * aconv_w_ref[0:1, :])
    tail_a = abuf[T:T + P, :]
    abuf[0:P, :] = tail_a
    aconv_out_ref[...] = tail_a
    qkv = _silu(y)
    q_raw = qkv[:, 0:256]
    k_raw = qkv[:, 256:512]
    v_s[...] = qkv[:, 512:768]

    r256 = _iota((256, 256), 0) >> HEAD_SHIFT
    c256 = _iota((256, 256), 1) >> HEAD_SHIFT
    mask_bd = r256 == c256
    ones_bd = jnp.where(mask_bd, 1.0, 0.0).astype(BF16)
    q_s[...] = q_raw * lax.rsqrt(_head_sumsq(q_raw, ones_bd) + EPS) * (HEAD_DIM ** -0.5)
    k_s[...] = k_raw * lax.rsqrt(_head_sumsq(k_raw, ones_bd) + EPS)
    yield

    g_log = -jnp.exp(alog_ref[...]) * _softplus(small + dtb_ref[...])
    beta = _sigmoid(small)
    gbeta = jnp.where(_iota((T, 128), 1) < A_HEADS, g_log, beta)
    expand = jnp.where(_iota((128, 512), 0) == (_iota((128, 512), 1) >> HEAD_SHIFT), 1.0, 0.0).astype(BF16)
    gbb = _dot_lhs3(gbeta, expand)
    bb_s[...] = gbb[:, 256:512]

    def chunk_cumsum(xv):
        row_in_chunk = _iota(xv.shape, 0) & (A_CHUNK - 1)
        step = 1
        while step < A_CHUNK:
            xv = xv + jnp.where(row_in_chunk >= step, pltpu.roll(xv, step, 0), 0.0)
            step *= 2
        return xv

    gcb_s[...] = chunk_cumsum(gbb[:, 0:256])
    gct = chunk_cumsum(g_log).T[0:8, :]
    low_half = (_iota((1, T), 1) & A_CHUNK) == 0
    gct_r = pltpu.roll(gct, A_CHUNK, 1)
    gct_l = pltpu.roll(gct, T - A_CHUNK, 1)
    even_rows = [jnp.where(low_half, gct[a:a + 1, :], gct_r[a + 1:a + 2, :]) for a in (0, 2)]
    odd_rows = [jnp.where(low_half, gct_l[a:a + 1, :], gct[a + 1:a + 2, :]) for a in (0, 2)]
    yield

    ri = _iota((A_CHUNK, 256), 0)
    ci = _iota((A_CHUNK, 256), 1) & (A_CHUNK - 1)
    causal_t = ri >= ci
    strict_t = ri > ci
    eye_t = jnp.where(ri == ci, 1.0, 0.0)

    def bd16(x16):
        return jnp.concatenate([x16, x16, x16, x16], axis=0) * ones_bd

    n_chunks = T // A_CHUNK
    t_inv, pw, qk, qdec, e_tail, vb16, kbe16 = [], [], [], [], [], [], []
    for c in range(n_chunks):
        rows = slice(c * A_CHUNK, (c + 1) * A_CHUNK)
        blk = slice((c // 2) * 128, (c // 2) * 128 + 128)
        src = even_rows if c % 2 == 0 else odd_rows
        gc_row = jnp.concatenate([src[0][:, blk], src[1][:, blk]], axis=1)
        qc = q_s[rows, :]
        kc = k_s[rows, :]
        bbc = bb_s[rows, :]
        gcb = gcb_s[rows, :]
        decay = jnp.where(causal_t, jnp.exp(jnp.where(causal_t, gcb - gc_row, 0.0)), 0.0)
        eg = jnp.exp(gcb)
        kb = kc * bbc
        aq = _dot_nt(jnp.concatenate([kb, qc], axis=0).astype(BF16), bd16(kc.astype(BF16)))
        a_mat = jnp.where(strict_t, aq[0:A_CHUNK] * decay, 0.0)
        qk.append(aq[A_CHUNK:2 * A_CHUNK] * decay)
        t_inv.append(eye_t - a_mat)
        pw.append(a_mat)
        qdec.append(qc * eg)
        e_tail.append(jnp.exp(gcb[A_CHUNK - 1:A_CHUNK, :] - gcb))
        vb16.append((v_s[rows, :] * bbc).astype(BF16))
        kbe16.append((kb * eg).astype(BF16))
    yield
    u, w = [], []
    state = {'s': s_scr[...]}

    def level_first():
        for c in range(n_chunks):
            p16 = pw[c].astype(BF16)
            pw[c] = _dot(p16, bd16(p16))

    def level_mid():
        for c in range(n_chunks):
            p16 = pw[c].astype(BF16)
            res = _dot(jnp.concatenate([p16, t_inv[c].astype(BF16)], axis=0), bd16(p16))
            pw[c] = res[0:A_CHUNK]
            t_inv[c] = t_inv[c] + res[A_CHUNK:2 * A_CHUNK]

    def level_last():
        for c in range(n_chunks):
            t_c = t_inv[c] + _dot(t_inv[c].astype(BF16), bd16(pw[c].astype(BF16)))
            uw = _dot(t_c.astype(BF16), jnp.concatenate([bd16(vb16[c]), bd16(kbe16[c])], axis=1))
            u.append(uw[:, 0:256])
            w.append(uw[:, 256:512])

    def scan_step(c):
        rows = slice(c * A_CHUNK, (c + 1) * A_CHUNK)
        s_bd = state['s']
        wq = _dot(jnp.concatenate([w[c], qdec[c]], axis=0).astype(BF16), s_bd.astype(BF16))
        v_new = u[c] - wq[0:A_CHUNK]
        o_s[rows, :] = wq[A_CHUNK:2 * A_CHUNK] + _dot(qk[c].astype(BF16), bd16(v_new.astype(BF16)))
        kv = _dot_tn(k_s[rows, :].astype(BF16), (v_new * e_tail[c]).astype(BF16))
        g_last = gcb_s[(c + 1) * A_CHUNK - 1:(c + 1) * A_CHUNK, :]
        state['s'] = s_bd * jnp.exp(g_last) + jnp.where(mask_bd, kv, 0.0)

    chain = [level_first] + [level_mid] * 4 + [level_last] + [functools.partial(scan_step, c) for c in range(n_chunks)]

    lane128 = _iota((2 * WINDOW, 128), 1)
    low = _iota((WINDOW, 128), 1) < HEAD_DIM
    qrow = _iota((2 * WINDOW, 2 * WINDOW), 0) & (WINDOW - 1)
    kcol = _iota((2 * WINDOW, 2 * WINDOW), 1)
    band = (kcol > qrow) & (kcol <= qrow + WINDOW)
    top_half = _iota((2 * WINDOW, 1), 0) < WINDOW
    out_c_blocks = []

    def group_c_block(n):
        first_key = si * T + (n - 1) * WINDOW
        valid = band & (kcol + first_key >= 0)
        kwin = kbuf[n * WINDOW:(n + 2) * WINDOW, :]
        vwin = vbuf[n * WINDOW:(n + 2) * WINDOW, :]
        k_sw = pltpu.roll(kwin, HEAD_DIM, 1)
        v_sw = pltpu.roll(vwin, HEAD_DIM, 1)
        pair_out = []
        for g in range(C_KV_HEADS):
            own = (lane128 < HEAD_DIM) if g == 0 else (lane128 >= HEAD_DIM)
            k_dup = jnp.where(own, kwin, k_sw).astype(BF16)
            v_dup = jnp.where(own, vwin, v_sw).astype(BF16)
            qp = res['cq'][n * WINDOW:(n + 1) * WINDOW, g * 128:(g + 1) * 128]
            q_st = jnp.concatenate([jnp.where(low, qp, 0.0), jnp.where(low, 0.0, qp)], axis=0).astype(BF16)
            s = _dot_nt(q_st, k_dup) * (HEAD_DIM ** -0.5)
            s = jnp.where(valid, s, NEG_INF)
            sink = jnp.where(top_half, sinks_ref[2 * g], sinks_ref[2 * g + 1])
            m = jnp.maximum(jnp.max(s, axis=-1, keepdims=True), sink)
            p = jnp.exp(s - m)
            denom = jnp.sum(p, axis=-1, keepdims=True) + jnp.exp(sink - m)
            o2 = _dot((p / denom).astype(BF16), v_dup)
            pair_out.append(jnp.where(low, o2[0:WINDOW], o2[WINDOW:2 * WINDOW]))
        out_c_blocks.append(jnp.concatenate(pair_out, axis=1))
        if n == T // WINDOW - 1:
            kbuf[0:WINDOW, :] = kbuf[T:T + WINDOW, :]
            vbuf[0:WINDOW, :] = vbuf[T:T + WINDOW, :]

    lane_grp = _iota((D_CHUNK, 256), 1) >> HEAD_SHIFT
    out_d_blocks = []

    def group_d_block(n):
        mx = _dot(res['wm'], res['dvn'][n * D_CHUNK:(n + 1) * D_CHUNK, :].astype(BF16))
        mixed = dbias_ref[...]
        for grp in range(D_GROUPS):
            mixed = mixed + jnp.where(lane_grp == grp, mx[grp * D_CHUNK:(grp + 1) * D_CHUNK, :], 0.0)
        out_d_blocks.append(res['du'][n * D_CHUNK:(n + 1) * D_CHUNK, :] * mixed)

    c_blocks = [functools.partial(group_c_block, n) for n in range(T // WINDOW)]
    d_blocks = [functools.partial(group_d_block, n) for n in range(T // D_CHUNK)]
    fill = [group_b, group_c_rope, group_d_norm] + c_blocks + d_blocks
    for i in range(max(len(chain), len(fill))):
        if i < len(chain):
            chain[i]()
        if i < len(fill):
            fill[i]()
        yield
    out_b = res['out_b']
    out_c = jnp.concatenate(out_c_blocks, axis=0)
    out_d = jnp.concatenate(out_d_blocks, axis=0)

    s_scr[...] = state['s']
    s_out_ref[...] = state['s']
    o = o_s[...]
    out_a = (o * lax.rsqrt(_head_sumsq(o, ones_bd) * (1.0 / HEAD_DIM) + EPS) * anorm_ref[...]
             * _silu(proj_s[:, COL_Z:COL_SMALL]))

    cat = jnp.concatenate([out_a, out_b, out_c, out_d], axis=1).astype(BF16)
    x_out_ref[...] = x_ref[...] + _dot_w32(cat, w_out_ref[0])


def _mixer_call(x, lw, tabs, aconv0, bconv0, s0, layer):
    bsz, seq, _ = x.shape
    T = MIXER_TILE_ROWS
    ns = seq // T
    NB = MIXER_SEQS
    P = CARRY_ROWS
    full = lambda shape: pl.BlockSpec(shape, lambda b, s, *_: (0,) * len(shape), pipeline_mode=pl.Buffered(1))
    per_b = lambda shape: pl.BlockSpec((NB,) + shape, lambda b, s, *_: (b,) + (0,) * len(shape))
    tab = pl.BlockSpec((T, 128), lambda b, s, *_: (s, 0))
    seq_scratch = [pltpu.VMEM(shape, F32) for shape in (
        (T + P, A_QKV_W), (T + P, 256), (T + WINDOW, 128), (T + WINDOW, 128), (T, 256), (T, 256), (T, 256),
        (T, 256), (T, 256), (T, 256), (256, 256), (T, P_PACKED))]
    grid_spec = pltpu.PrefetchScalarGridSpec(
        num_scalar_prefetch=1,
        grid=(bsz // NB, ns),
        in_specs=[
            pl.BlockSpec((NB, T, D_MODEL), lambda b, s, *_: (b, s, 0)),
            full((1, D_MODEL)), full((D_MODEL, P_PACKED)), full((A_CONV, A_QKV_W)), full((1, 128)), full((1, 128)),
            full((1, 256)), full((B_CONV, 256)), tab, tab, tab, full((1, 256)), full((1, 256)),
            full((D_GROUPS * D_CHUNK, D_CHUNK)), full((D_CHUNK, 256)),
            pl.BlockSpec((1, D_MODEL, D_MODEL), lambda b, s, *_: (layer, 0, 0), pipeline_mode=pl.Buffered(1)),
            per_b((P, A_QKV_W)), per_b((P, 256)), per_b((256, 256)),
        ],
        out_specs=[
            pl.BlockSpec((NB, T, D_MODEL), lambda b, s, *_: (b, s, 0)),
            per_b((256, 256)), per_b((P, A_QKV_W)), per_b((P, 256)), per_b((WINDOW, 128)), per_b((WINDOW, 128)),
        ],
        scratch_shapes=seq_scratch * NB,
    )
    out_shape = [
        jax.ShapeDtypeStruct((bsz, seq, D_MODEL), F32),
        jax.ShapeDtypeStruct((bsz, 256, 256), F32),
        jax.ShapeDtypeStruct((bsz, P, A_QKV_W), F32),
        jax.ShapeDtypeStruct((bsz, P, 256), F32),
        jax.ShapeDtypeStruct((bsz, WINDOW, 128), F32),
        jax.ShapeDtypeStruct((bsz, WINDOW, 128), F32),
    ]
    return pl.pallas_call(
        functools.partial(_mixer_kernel, tile=T),
        grid_spec=grid_spec,
        out_shape=out_shape,
        compiler_params=pltpu.CompilerParams(
            dimension_semantics=("arbitrary", "arbitrary"), vmem_limit_bytes=MIXER_VMEM_LIMIT_BYTES),
        name="mixer",
    )(lw['c_sinks'], x, lw['norm1_g'], lw['w_in'], lw['a_conv_w'], lw['a_log'], lw['a_dt_bias'],
      lw['a_norm_g'], lw['b_conv_w'], tabs[0], tabs[1], tabs[2], lw['d_ln_g'], lw['d_ln_b'],
      lw['d_ws'], lw['d_bias'], lw['w_out'], aconv0, bconv0, s0)


def _ffn_kernel(x_ref, n2g_ref, wg_ref, wu_ref, cw_ref, wd_ref, fconv0_ref, fng_ref,
                x_out_ref, fconv_out_ref, gbuf, *, tile, stride, final_norm):
    T = tile
    P = (FFN_CONV - 1) * stride if stride > 1 else CARRY_ROWS
    si = pl.program_id(1)

    @pl.when(si == 0)
    def _():
        gbuf[0:P, :] = fconv0_ref[0]

    x = x_ref[0]
    h = _rms(x, n2g_ref[...]).astype(BF16)
    acc = x
    for c0 in range(0, D_FF, FFN_COL_CHUNK):
        cols = slice(c0, c0 + FFN_COL_CHUNK)
        gbuf[P:P + T, cols] = _dot_w32(h, wg_ref[0, :, cols])
        up = _dot_w32(h, wu_ref[0, :, cols])
        gate = (gbuf[P:P + T, cols] * cw_ref[2:3, cols] + gbuf[P - stride:P - stride + T, cols] * cw_ref[1:2, cols]
                + gbuf[P - 2 * stride:P - 2 * stride + T, cols] * cw_ref[0:1, cols])
        acc = acc + _dot_w32((_silu(gate) * up).astype(BF16), wd_ref[0, cols, :])
    tail = gbuf[T:T + P, :]
    gbuf[0:P, :] = tail
    fconv_out_ref[0] = tail
    if final_norm:
        acc = _rms(acc, fng_ref[...])
    x_out_ref[0] = acc


def _ffn_call(x, lw, fconv0, final_g, *, layer, tile, stride, final_norm):
    bsz, seq, _ = x.shape
    T = tile
    P = (FFN_CONV - 1) * stride if stride > 1 else CARRY_ROWS
    full = lambda shape: pl.BlockSpec(shape, lambda b, s: (0,) * len(shape), pipeline_mode=pl.Buffered(1))
    per_b = lambda shape: pl.BlockSpec((1,) + shape, lambda b, s: (b,) + (0,) * len(shape))
    weight = lambda shape: pl.BlockSpec((1,) + shape, lambda b, s: (layer, 0, 0), pipeline_mode=pl.Buffered(1))
    return pl.pallas_call(
        functools.partial(_ffn_kernel, tile=T, stride=stride, final_norm=final_norm),
        grid=(bsz, seq // T),
        in_specs=[
            pl.BlockSpec((1, T, D_MODEL), lambda b, s: (b, s, 0)),
            full((1, D_MODEL)), weight((D_MODEL, D_FF)), weight((D_MODEL, D_FF)), full((FFN_CONV, D_FF)),
            weight((D_FF, D_MODEL)), per_b((P, D_FF)), full((1, D_MODEL)),
        ],
        out_specs=[pl.BlockSpec((1, T, D_MODEL), lambda b, s: (b, s, 0)), per_b((P, D_FF))],
        out_shape=[jax.ShapeDtypeStruct((bsz, seq, D_MODEL), F32), jax.ShapeDtypeStruct((bsz, P, D_FF), F32)],
        scratch_shapes=[pltpu.VMEM((T + P, D_FF), F32)],
        compiler_params=pltpu.CompilerParams(
            dimension_semantics=("arbitrary", "arbitrary"), vmem_limit_bytes=VMEM_LIMIT_BYTES),
        name="ffn",
    )(x, lw['norm2_g'], lw['ffn_w_gate'], lw['ffn_w_up'], lw['ffn_conv_w'], lw['ffn_w_down'], fconv0, final_g)


def _pack_w_in(w):
    pad = jnp.zeros((D_MODEL, 128 - 2 * A_HEADS), w.dtype)
    small = jnp.concatenate([w[:, 1028:1032], w[:, 1024:1028], pad], axis=1)
    return jnp.concatenate([w[:, 0:1024], small, w[:, 1032:]], axis=1).astype(BF16)


def _pad_lanes(v, width=128):
    return jnp.concatenate([v, jnp.zeros((width - v.shape[0],), v.dtype)])[None, :]


def _layer_weights(l, norm1_g, w_in, a_conv_w, a_log, a_dt_bias, a_norm_g, b_conv_w, c_sinks, d_ln_g, d_ln_b,
                   d_ws, d_bias, w_out, norm2_g, ffn_w_gate, ffn_w_up, ffn_conv_w, ffn_w_down):
    bias_tab = jnp.broadcast_to(d_bias[l].T[:, :, None], (D_CHUNK, D_GROUPS, HEAD_DIM)).reshape(D_CHUNK, 256)
    return {
        'norm1_g': norm1_g[l][None, :], 'w_in': _pack_w_in(w_in[l]), 'a_conv_w': a_conv_w[l],
        'a_log': _pad_lanes(a_log[l]), 'a_dt_bias': _pad_lanes(a_dt_bias[l]),
        'a_norm_g': jnp.tile(a_norm_g[l], A_HEADS)[None, :], 'b_conv_w': b_conv_w[l], 'c_sinks': c_sinks[l],
        'd_ln_g': d_ln_g[l][None, :], 'd_ln_b': d_ln_b[l][None, :],
        'd_ws': d_ws[l].reshape(D_GROUPS * D_CHUNK, D_CHUNK), 'd_bias': bias_tab,
        'w_out': w_out, 'norm2_g': norm2_g[l][None, :],
        'ffn_w_gate': ffn_w_gate, 'ffn_w_up': ffn_w_up, 'ffn_conv_w': ffn_conv_w[l], 'ffn_w_down': ffn_w_down,
    }


def _rope_tables(pos):
    half = ROPE_DIM // 2
    inv = np.power(ROPE_THETA, -np.arange(half, dtype=np.float64) * (2.0 / ROPE_DIM))
    ang = pos.astype(np.float64)[:, None] * inv[None, :]
    cos, sin = jnp.asarray(np.cos(ang), F32), jnp.asarray(np.sin(ang), F32)
    n = pos.shape[0]
    rest = HEAD_DIM - ROPE_DIM
    cos_h = jnp.concatenate([cos, cos, jnp.ones((n, rest), F32)], axis=1)
    sina_h = jnp.concatenate([-sin, jnp.zeros((n, half + rest), F32)], axis=1)
    sinb_h = jnp.concatenate([jnp.zeros((n, half), F32), sin, jnp.zeros((n, rest), F32)], axis=1)
    return tuple(jnp.concatenate([t, t], axis=1) for t in (cos_h, sina_h, sinb_h))


def _diag_blocks(s_bd):
    return jnp.stack([s_bd[:, HEAD_DIM * i:HEAD_DIM * (i + 1), HEAD_DIM * i:HEAD_DIM * (i + 1)]
                      for i in range(A_HEADS)], axis=1)


def _sample_pre_kernel(x_ref, n1g_ref, w_in_ref, aconv_w_ref, alog_ref, dtb_ref, bconv_w_ref, cos_ref, sina_ref,
                       sinb_ref, lng_ref, lnb_ref, wtab_ref, btab_ref, aconv0_ref, bconv0_ref,
                       q_ref, k_ref, v_ref, g_ref, beta_ref, z_ref, outb_ref, cq_ref, ck_ref, cv_ref, outd_ref,
                       dvn_ref, aconv_out_ref, bconv_out_ref, abuf, bbuf, *, rows, stride):
    R, S = rows, stride
    steps = R // S
    pa_rows = (A_CONV - 1) * S
    pb_rows = (B_CONV - 1) * S
    x = x_ref[...]
    h = _rms(x, n1g_ref[...]).astype(BF16)

    pa = _dot(h, w_in_ref[:, COL_A:COL_A + W_A])
    z_ref[...] = pa[:, COL_Z:COL_SMALL]
    small = pa[:, COL_SMALL:W_A]
    abuf[0:pa_rows, :] = aconv0_ref[...]
    abuf[pa_rows:pa_rows + R, :] = pa[:, 0:A_QKV_W]
    y = abuf[0:R, :] * aconv_w_ref[0:1, :]
    for j in range(1, A_CONV):
        y = y + abuf[j * S:j * S + R, :] * aconv_w_ref[j:j + 1, :]
    aconv_out_ref[...] = abuf[R:R + pa_rows, :]
    qkv = _silu(y)
    q_raw = qkv[:, 0:256]
    k_raw = qkv[:, 256:512]
    v_ref[...] = qkv[:, 512:768]
    mask_bd = (_iota((256, 256), 0) >> HEAD_SHIFT) == (_iota((256, 256), 1) >> HEAD_SHIFT)
    ones_bd = jnp.where(mask_bd, 1.0, 0.0).astype(BF16)
    q_ref[...] = q_raw * lax.rsqrt(_dot_lhs3(q_raw * q_raw, ones_bd) + EPS) * (HEAD_DIM ** -0.5)
    k_ref[...] = k_raw * lax.rsqrt(_dot_lhs3(k_raw * k_raw, ones_bd) + EPS)
    g_ref[...] = -jnp.exp(alog_ref[...]) * _softplus(small + dtb_ref[...])
    beta_ref[...] = _sigmoid(small)

    pb = _dot(h, w_in_ref[:, COL_B:COL_B + W_B])
    bbuf[0:pb_rows, :] = bconv0_ref[...]
    bbuf[pb_rows:pb_rows + R, :] = pb[:, 256:512] * pb[:, 512:768]
    bx = bbuf[0:R, :] * bconv_w_ref[0:1, :]
    for j in range(1, B_CONV):
        bx = bx + bbuf[j * S:j * S + R, :] * bconv_w_ref[j:j + 1, :]
    bconv_out_ref[...] = bbuf[R:R + pb_rows, :]
    outb_ref[...] = pb[:, 0:256] * bx

    pc = _dot(h, w_in_ref[:, COL_C:COL_C + W_C])
    cos = cos_ref[...]
    sina = sina_ref[...]
    sinb = sinb_ref[...]
    cq = pc[:, 0:256]
    cq_ref[...] = (cq * jnp.concatenate([cos, cos], axis=1)
                   + pltpu.roll(cq, 256 - ROPE_DIM // 2, 1) * jnp.concatenate([sina, sina], axis=1)
                   + pltpu.roll(cq, ROPE_DIM // 2, 1) * jnp.concatenate([sinb, sinb], axis=1))
    ck = pc[:, 256:384]
    ck_ref[...] = (ck * cos + pltpu.roll(ck, 128 - ROPE_DIM // 2, 1) * sina
                   + pltpu.roll(ck, ROPE_DIM // 2, 1) * sinb)
    cv_ref[...] = pc[:, 384:512]

    pd = _dot(h, w_in_ref[:, COL_D:COL_D + W_D])
    du = _gelu_tanh(pd[:, 0:256])
    gv = _gelu_tanh(pd[:, 256:512])
    mu = jnp.mean(gv, axis=-1, keepdims=True)
    xc = gv - mu
    dvn = xc * lax.rsqrt(jnp.mean(xc * xc, axis=-1, keepdims=True) + EPS) * lng_ref[...] + lnb_ref[...]
    dvn_ref[...] = dvn
    for t in range(steps):
        mixed = btab_ref[t:t + 1, :]
        for s in range(t + 1):
            mixed = mixed + wtab_ref[t * steps + s:t * steps + s + 1, :] * dvn[s * S:(s + 1) * S, :]
        outd_ref[t * S:(t + 1) * S, :] = du[t * S:(t + 1) * S, :] * mixed


def _sample_pre_call(x, lw, tabs, wtab, btab, aconv0, bconv0, stride):
    R = x.shape[0]
    f = lambda *shape: jax.ShapeDtypeStruct(shape, F32)
    out_shape = [f(R, 256), f(R, 256), f(R, 256), f(R, 128), f(R, 128), f(R, 256), f(R, 256), f(R, 256),
                 f(R, 128), f(R, 128), f(R, 256), f(R, 256), f((A_CONV - 1) * stride, A_QKV_W),
                 f((B_CONV - 1) * stride, 256)]
    return pl.pallas_call(
        functools.partial(_sample_pre_kernel, rows=R, stride=stride),
        out_shape=out_shape,
        scratch_shapes=[pltpu.VMEM((R + (A_CONV - 1) * stride, A_QKV_W), F32),
                        pltpu.VMEM((R + (B_CONV - 1) * stride, 256), F32)],
        compiler_params=pltpu.CompilerParams(vmem_limit_bytes=VMEM_LIMIT_BYTES),
        name="sample_pre",
    )(x, lw['norm1_g'], lw['w_in'], lw['a_conv_w'], lw['a_log'], lw['a_dt_bias'], lw['b_conv_w'],
      tabs[0], tabs[1], tabs[2], lw['d_ln_g'], lw['d_ln_b'], wtab, btab, aconv0, bconv0)


DELTA_UNROLL = 4


def _sample_delta_kernel(g_ref, beta_ref, q_ref, k_ref, v_ref, s_ref, o_ref, snew_ref,
                         qt_s, kt_s, vt_s, gt_s, bt_s, ot_s, *, steps, nb):
    h = pl.program_id(0)
    n_i = HEAD_DIM
    zeros = jnp.zeros((HEAD_DIM, nb), F32)

    @pl.when(h == 0)
    def _():
        for t in range(steps):
            rows = slice(t * nb, (t + 1) * nb)
            qt_s[t] = q_ref[rows, :].T
            kt_s[t] = k_ref[rows, :].T
            vt_s[t] = v_ref[rows, :].T
            gt_s[t] = g_ref[rows, :].T
            bt_s[t] = beta_ref[rows, :].T

    base = pl.multiple_of(h * HEAD_DIM, HEAD_DIM)
    head_rows = pl.ds(base, HEAD_DIM)

    def rows_of(i):
        return pl.ds(pl.multiple_of(i * HEAD_DIM, HEAD_DIM), HEAD_DIM)

    def decay(t):
        return jnp.exp(gt_s[t, pl.ds(h, 1), :])

    dec0 = decay(0)

    def first_pass(i, acc):
        return acc + kt_s[0, pl.ds(base + i, 1), :] * (s_ref[0, rows_of(i), :] * dec0)

    ks = lax.fori_loop(0, n_i, first_pass, zeros, unroll=DELTA_UNROLL)
    for t in range(steps):
        dec = decay(t)
        v_new = bt_s[t, pl.ds(A_HEADS + h, 1), :] * (vt_s[t, head_rows, :] - ks)
        src = s_ref if t == 0 else snew_ref
        dec_next = decay(t + 1) if t + 1 < steps else None

        def update(i, carry, t=t, dec=dec, v_new=v_new, src=src, dec_next=dec_next):
            o_acc, ks_acc = carry
            blk = src[0, rows_of(i), :] * dec + kt_s[t, pl.ds(base + i, 1), :] * v_new
            snew_ref[0, rows_of(i), :] = blk
            o_acc = o_acc + qt_s[t, pl.ds(base + i, 1), :] * blk
            if dec_next is not None:
                ks_acc = ks_acc + kt_s[t + 1, pl.ds(base + i, 1), :] * (blk * dec_next)
            return o_acc, ks_acc

        o_acc, ks = lax.fori_loop(0, n_i, update, (zeros, zeros), unroll=DELTA_UNROLL)
        ot_s[t, head_rows, :] = o_acc

    @pl.when(h == A_HEADS - 1)
    def _():
        for t in range(steps):
            o_ref[t * nb:(t + 1) * nb, :] = ot_s[t].T


def _sample_delta_call(g, beta, q, k, v, s, steps):
    nh, _, nb = s.shape
    rows = steps * nb
    whole = lambda width: pl.BlockSpec((rows, width), lambda h: (0, 0))
    st = pl.BlockSpec((1, HEAD_DIM * HEAD_DIM, nb), lambda h: (h, 0, 0))
    wide = pltpu.VMEM((steps, GROUP_WIDTH, nb), F32)
    narrow = pltpu.VMEM((steps, 128, nb), F32)
    return pl.pallas_call(
        functools.partial(_sample_delta_kernel, steps=steps, nb=nb),
        grid=(nh,),
        in_specs=[whole(128), whole(128), whole(GROUP_WIDTH), whole(GROUP_WIDTH), whole(GROUP_WIDTH), st],
        out_specs=[whole(GROUP_WIDTH), st],
        out_shape=[jax.ShapeDtypeStruct((rows, GROUP_WIDTH), F32),
                   jax.ShapeDtypeStruct((nh, HEAD_DIM * HEAD_DIM, nb), F32)],
        scratch_shapes=[wide, wide, wide, narrow, narrow, wide],
        compiler_params=pltpu.CompilerParams(dimension_semantics=("arbitrary",),
                                             vmem_limit_bytes=VMEM_LIMIT_BYTES),
        name="sample_delta",
    )(g, beta, q, k, v, s)


SAMPLE_ATTN_BLOCK = 16
NEW_KEY_ROWS = 8


def _sample_attn_kernel(sinks_ref, qm_ref, kc_ref, kn_ref, vc_ref, vn_ref, o_ref, k_out_ref, v_out_ref, *, steps):
    nq = C_HEADS * steps
    nk = WINDOW + NEW_KEY_ROWS
    BB = SAMPLE_ATTN_BLOCK
    row = _iota((BB * nq, nk), 0)
    col = _iota((BB * nq, nk), 1)
    t_q = row & (steps - 1)
    valid = ((col < WINDOW) & (col > t_q)) | ((col >= WINDOW) & (col - WINDOW <= t_q))
    head = (_iota((BB * nq, 1), 0) >> 2) & (C_HEADS - 1)
    sink = jnp.where(head == 0, sinks_ref[0],
                     jnp.where(head == 1, sinks_ref[1], jnp.where(head == 2, sinks_ref[2], sinks_ref[3])))
    scores = [_dot_nt(qm_ref[b].astype(BF16), jnp.concatenate([kc_ref[0, b], kn_ref[b]], axis=0).astype(BF16))
              for b in range(BB)]
    s = jnp.where(valid, jnp.concatenate(scores, axis=0) * (HEAD_DIM ** -0.5), NEG_INF)
    m = jnp.maximum(jnp.max(s, axis=-1, keepdims=True), sink)
    p = jnp.exp(s - m)
    denom = jnp.sum(p, axis=-1, keepdims=True) + jnp.exp(sink - m)
    p16 = (p / denom).astype(BF16)
    is_new_row = _iota((WINDOW, 128), 0) >= WINDOW - steps
    pad_rows = jnp.zeros((WINDOW - NEW_KEY_ROWS, 128), F32)

    def slide(cache, new_rows):
        kept = pltpu.roll(cache, WINDOW - steps, 0)
        tail = jnp.concatenate([pad_rows, pltpu.roll(new_rows, NEW_KEY_ROWS - steps, 0)], axis=0)
        return jnp.where(is_new_row, tail, kept)

    for b in range(BB):
        v_all = jnp.concatenate([vc_ref[0, b], vn_ref[b]], axis=0).astype(BF16)
        o_ref[b] = _dot(p16[b * nq:(b + 1) * nq, :], v_all)
        k_out_ref[b] = slide(kc_ref[0, b], kn_ref[b])
        v_out_ref[b] = slide(vc_ref[0, b], vn_ref[b])


def _sample_attn_call(sinks, qm, kc_all, kn, vc_all, vn, layer, steps):
    bs, nq, _ = qm.shape
    BB = SAMPLE_ATTN_BLOCK
    blk = lambda r: pl.BlockSpec((BB, r, 128), lambda i, *_: (i, 0, 0))
    cache = pl.BlockSpec((1, BB, WINDOW, 128), lambda i, *_: (layer, i, 0, 0))
    grid_spec = pltpu.PrefetchScalarGridSpec(
        num_scalar_prefetch=1, grid=(bs // BB,),
        in_specs=[blk(nq), cache, blk(NEW_KEY_ROWS), cache, blk(NEW_KEY_ROWS)],
        out_specs=[blk(nq), blk(WINDOW), blk(WINDOW)])
    return pl.pallas_call(
        functools.partial(_sample_attn_kernel, steps=steps),
        grid_spec=grid_spec,
        out_shape=[jax.ShapeDtypeStruct((bs, nq, 128), F32), jax.ShapeDtypeStruct((bs, WINDOW, 128), F32),
                   jax.ShapeDtypeStruct((bs, WINDOW, 128), F32)],
        compiler_params=pltpu.CompilerParams(dimension_semantics=("arbitrary",)),
        name="sample_attn",
    )(sinks, qm, kc_all, kn, vc_all, vn)


def _sample_post_kernel(x_ref, o_ref, z_ref, outb_ref, outc_ref, outd_ref, anorm_ref, w_out_ref, x_out_ref, *,
                        layer):
    mask_bd = (_iota((256, 256), 0) >> HEAD_SHIFT) == (_iota((256, 256), 1) >> HEAD_SHIFT)
    ones_bd = jnp.where(mask_bd, 1.0, 0.0).astype(BF16)
    o = o_ref[...]
    out_a = (o * lax.rsqrt(_dot_lhs3(o * o, ones_bd) * (1.0 / HEAD_DIM) + EPS) * anorm_ref[...]
             * _silu(z_ref[...]))
    cat = jnp.concatenate([out_a, outb_ref[...], outc_ref[...], outd_ref[...]], axis=1).astype(BF16)
    x_out_ref[...] = x_ref[...] + _dot_w32(cat, w_out_ref[layer])


def _sample_post_call(x, o, z, out_b, out_c, out_d, lw, layer):
    return pl.pallas_call(
        functools.partial(_sample_post_kernel, layer=layer),
        out_shape=jax.ShapeDtypeStruct(x.shape, F32),
        compiler_params=pltpu.CompilerParams(vmem_limit_bytes=VMEM_LIMIT_BYTES),
        name="sample_post",
    )(x, o, z, out_b, out_c, out_d, lw['a_norm_g'], lw['w_out'])


def _sample_mixer(x_tm, lw, tabs, d_ws_l, d_bias_l, a_state, a_conv, b_conv, c_k_all, c_v_all, layer, bs, ts):
    to_tm = lambda a: jnp.swapaxes(a, 0, 1).reshape(a.shape[1] * bs, a.shape[2])
    from_tm = lambda a, n: jnp.swapaxes(a.reshape(n, bs, a.shape[-1]), 0, 1)
    wtab = jnp.repeat(d_ws_l[:, :ts, :ts].transpose(1, 2, 0).reshape(ts * ts, D_GROUPS), HEAD_DIM, axis=1)
    btab = jnp.repeat(d_bias_l[:, :ts].T, HEAD_DIM, axis=1)
    (q, k, v, g, beta, z, out_b, cq, ck, cv, out_d, dvn, a_tail, b_tail) = _sample_pre_call(
        x_tm, lw, tabs, wtab, btab, to_tm(a_conv), to_tm(b_conv), bs)

    s_t = a_state.reshape(bs, A_HEADS, HEAD_DIM * HEAD_DIM).transpose(1, 2, 0)
    o, s_new_t = _sample_delta_call(g, beta, q, k, v, s_t, ts)
    a_state_new = s_new_t.transpose(2, 0, 1).reshape(bs, A_HEADS, HEAD_DIM, HEAD_DIM)

    cq4 = cq.reshape(ts, bs, C_HEADS, HEAD_DIM).transpose(1, 2, 0, 3)
    zq = jnp.zeros_like(cq4[:, 0])
    qm = jnp.concatenate(
        [jnp.concatenate([cq4[:, hh], zq] if hh // 2 == 0 else [zq, cq4[:, hh]], axis=-1) for hh in range(C_HEADS)],
        axis=1)
    pad_new = lambda a: jnp.concatenate([from_tm(a, ts), jnp.zeros((bs, NEW_KEY_ROWS - ts, 128), F32)], axis=1)
    o_att, c_k_new, c_v_new = _sample_attn_call(lw['c_sinks'], qm, c_k_all, pad_new(ck), c_v_all, pad_new(cv),
                                                layer, ts)
    out_c = jnp.concatenate(
        [o_att[:, hh * ts:(hh + 1) * ts, (hh // 2) * HEAD_DIM:(hh // 2 + 1) * HEAD_DIM] for hh in range(C_HEADS)],
        axis=-1)
    out_c = jnp.swapaxes(out_c, 0, 1).reshape(ts * bs, GROUP_WIDTH)
    c_k_new = c_k_new.reshape(bs, WINDOW, C_KV_HEADS, HEAD_DIM)
    c_v_new = c_v_new.reshape(bs, WINDOW, C_KV_HEADS, HEAD_DIM)

    x2 = _sample_post_call(x_tm, o, z, out_b, out_c, out_d, lw, layer)
    new = {'a_state': a_state_new, 'a_conv': from_tm(a_tail, A_CONV - 1), 'b_conv': from_tm(b_tail, B_CONV - 1),
           'c_k': c_k_new, 'c_v': c_v_new, 'd_v': from_tm(dvn, ts)}
    return x2, new


def kernel(x_prompt, x_sample, state_delta, state_delta_conv, state_shortconv, cache_win_k, cache_win_v,
           state_ffn_conv, norm1_g, w_in, a_conv_w, a_log, a_dt_bias, a_norm_g, b_conv_w, c_sinks, d_ln_g,
           d_ln_b, d_ws, d_bias, w_out, norm2_g, ffn_w_gate, ffn_w_up, ffn_conv_w, ffn_w_down, final_norm_g):
    bp, tp, _ = x_prompt.shape
    bs, ts, _ = x_sample.shape
    depth = w_in.shape[0]
    win_buf = cache_win_k.shape[2]
    pos_p = np.arange(tp, dtype=np.int32)
    pos_s = PAST_LEN + np.arange(ts, dtype=np.int32)
    assert win_buf == WINDOW and ts == 4
    tabs_p = _rope_tables(pos_p)
    tabs_s = tuple(jnp.repeat(t, bs, axis=0) for t in _rope_tables(pos_s))
    fng = final_norm_g[None, :]
    ck_all = cache_win_k.reshape(depth, bs, WINDOW, 128)
    cv_all = cache_win_v.reshape(depth, bs, WINDOW, 128)
    P = CARRY_ROWS

    hp = x_prompt
    hs = jnp.swapaxes(x_sample, 0, 1).reshape(ts * bs, D_MODEL)
    outs = {k: [] for k in ('sp', 'ss', 'acp', 'acs', 'bcp', 'bcs', 'ckp', 'cks', 'cvp', 'cvs', 'fcp', 'fcs', 'dv')}
    for l in range(depth):
        lw = _layer_weights(l, norm1_g, w_in, a_conv_w, a_log, a_dt_bias, a_norm_g, b_conv_w, c_sinks, d_ln_g,
                            d_ln_b, d_ws, d_bias, w_out, norm2_g, ffn_w_gate, ffn_w_up, ffn_conv_w, ffn_w_down)
        last = l == depth - 1
        hp, s_bd, acv, bcv, ckn, cvn = _mixer_call(
            hp, lw, tabs_p, jnp.zeros((bp, P, A_QKV_W), F32), jnp.zeros((bp, P, 256), F32),
            jnp.zeros((bp, 256, 256), F32), l)
        hp, fcv = _ffn_call(hp, lw, jnp.zeros((bp, P, D_FF), F32), fng, layer=l, tile=TILE_ROWS, stride=1,
                            final_norm=last)
        outs['sp'].append(_diag_blocks(s_bd))
        outs['acp'].append(acv[:, P - (A_CONV - 1):])
        outs['bcp'].append(bcv[:, P - (B_CONV - 1):])
        outs['ckp'].append(ckn.reshape(bp, WINDOW, C_KV_HEADS, HEAD_DIM))
        outs['cvp'].append(cvn.reshape(bp, WINDOW, C_KV_HEADS, HEAD_DIM))
        outs['fcp'].append(fcv[:, P - (FFN_CONV - 1):])
        hs, ns = _sample_mixer(hs, lw, tabs_s, d_ws[l], d_bias[l], state_delta[l], state_delta_conv[l],
                               state_shortconv[l], ck_all, cv_all, l, bs, ts)
        f0 = jnp.swapaxes(state_ffn_conv[l], 0, 1).reshape(1, (FFN_CONV - 1) * bs, D_FF)
        ys_tm, fcs = _ffn_call(hs[None], lw, f0, fng, layer=l, tile=ts * bs, stride=bs, final_norm=last)
        hs = ys_tm[0]
        outs['ss'].append(ns['a_state'])
        outs['acs'].append(ns['a_conv'])
        outs['bcs'].append(ns['b_conv'])
        outs['cks'].append(ns['c_k'])
        outs['cvs'].append(ns['c_v'])
        outs['fcs'].append(jnp.swapaxes(fcs.reshape(FFN_CONV - 1, bs, D_FF), 0, 1))
        outs['dv'].append(ns['d_v'])
    st = {k: jnp.stack(v) for k, v in outs.items()}
    hs = jnp.swapaxes(hs.reshape(ts, bs, D_MODEL), 0, 1)
    return (hp, hs, st['sp'], st['ss'], st['acp'], st['acs'], st['bcp'], st['bcs'], st['ckp'], st['cks'],
            st['cvp'], st['cvs'], st['fcp'], st['fcs'], st['dv'])
```

```python
import functools

import jax
import jax.numpy as jnp
import numpy as np
from jax import lax
from jax.experimental import pallas as pl
from jax.experimental.pallas import tpu as pltpu

F32 = jnp.float32
BF16 = jnp.bfloat16

D_MODEL = 1024
GROUP_WIDTH = 256
HEAD_DIM = 64
A_HEADS = 4
A_CONV = 4
A_CHUNK = 64
B_CONV = 3
C_HEADS = 4
C_KV_HEADS = 2
WINDOW = 128
ROPE_DIM = 16
ROPE_THETA = 500000.0
D_GROUPS = 4
D_CHUNK = 128
D_FF = 2816
FFN_CONV = 3
EPS = 1e-6
NEG_INF = -1e30
PAST_LEN = 16384

HEAD_SHIFT = 6
LANES = 128

COL_A = 0
A_QKV_W = 3 * GROUP_WIDTH
COL_Z = A_QKV_W
COL_SMALL = COL_Z + GROUP_WIDTH
W_A = COL_SMALL + LANES
COL_B = 1152
W_B = 768
COL_C = 1920
W_C = 512
COL_D = 2432
W_D = 512
P_PACKED = 2944

TILE_ROWS = 512
MIXER_TILE_ROWS = 512
CARRY_ROWS = 8
FFN_COL_CHUNK = 1408
VMEM_LIMIT_BYTES = 56 * 1024 * 1024
MIXER_VMEM_LIMIT_BYTES = 60 * 1024 * 1024


def _dot(a, b):
    return jnp.dot(a, b, preferred_element_type=F32)


def _dot_w32(a, w):
    return lax.dot_general(a, w, (((1,), (0,)), ((), ())), preferred_element_type=F32)


def _dot_nt(a, b):
    return lax.dot_general(a, b, (((1,), (1,)), ((), ())), preferred_element_type=F32)


def _dot_tn(a, b):
    return lax.dot_general(a, b, (((0,), (0,)), ((), ())), preferred_element_type=F32)


def _split3(x):
    hi = x.astype(BF16)
    r1 = x - hi.astype(F32)
    mid = r1.astype(BF16)
    lo = (r1 - mid.astype(F32)).astype(BF16)
    return hi, mid, lo


def _dot_lhs3(x, w01):
    hi, mid, lo = _split3(x)
    return _dot(hi, w01) + _dot(mid, w01) + _dot(lo, w01)


def _head_sumsq(x, ones_bd):
    return _dot((x * x).astype(BF16), ones_bd)


def _sigmoid(x):
    return 1.0 / (1.0 + jnp.exp(-x))


def _silu(x):
    return x * _sigmoid(x)


def _softplus(x):
    return jnp.maximum(x, 0.0) + jnp.log(1.0 + jnp.exp(-jnp.abs(x)))


def _gelu_tanh(x):
    return 0.5 * x * (1.0 + jnp.tanh(np.sqrt(2.0 / np.pi).astype(np.float32) * (x + 0.044715 * (x * x * x))))


def _rms(x, g):
    return x * lax.rsqrt(jnp.mean(x * x, axis=-1, keepdims=True) + EPS) * g


def _iota(shape, dim):
    return lax.broadcasted_iota(jnp.int32, shape, dim)


MIXER_SEQS = 2


def _mixer_kernel(sinks_ref, x_ref, n1g_ref, w_in_ref, aconv_w_ref, alog_ref, dtb_ref, anorm_ref,
                  bconv_w_ref, cos_ref, sina_ref, sinb_ref, lng_ref, lnb_ref, ws_ref, dbias_ref,
                  w_out_ref, aconv0_ref, bconv0_ref, s0_ref,
                  x_out_ref, s_out_ref, aconv_out_ref, bconv_out_ref, ck_out_ref, cv_out_ref,
                  *scratch, tile):
    shared = (sinks_ref, n1g_ref, w_in_ref, aconv_w_ref, alog_ref, dtb_ref, anorm_ref, bconv_w_ref, cos_ref,
              sina_ref, sinb_ref, lng_ref, lnb_ref, ws_ref, dbias_ref, w_out_ref)
    per_seq = (x_ref, aconv0_ref, bconv0_ref, s0_ref, x_out_ref, s_out_ref, aconv_out_ref, bconv_out_ref,
               ck_out_ref, cv_out_ref)
    n_scr = len(scratch) // MIXER_SEQS
    gens = [_mixer_seq(*shared, *(r.at[b] for r in per_seq), *scratch[b * n_scr:(b + 1) * n_scr], tile=tile)
            for b in range(MIXER_SEQS)]
    started, live = 1, list(gens[:1])
    while live:
        for gen in list(live):
            if next(gen, 'done') == 'done':
                live.remove(gen)
        if started < len(gens):
            live.append(gens[started])
            started += 1


def _mixer_seq(sinks_ref, n1g_ref, w_in_ref, aconv_w_ref, alog_ref, dtb_ref, anorm_ref, bconv_w_ref, cos_ref,
               sina_ref, sinb_ref, lng_ref, lnb_ref, ws_ref, dbias_ref, w_out_ref,
               x_ref, aconv0_ref, bconv0_ref, s0_ref, x_out_ref, s_out_ref, aconv_out_ref, bconv_out_ref,
               ck_out_ref, cv_out_ref,
               abuf, bbuf, kbuf, vbuf, q_s, k_s, v_s, gcb_s, bb_s, o_s, s_scr, proj_s, *, tile):
    T = tile
    si = pl.program_id(1)
    P = CARRY_ROWS

    @pl.when(si == 0)
    def _():
        abuf[0:P, :] = aconv0_ref[...]
        bbuf[0:P, :] = bconv0_ref[...]
        kbuf[0:WINDOW, :] = jnp.zeros((WINDOW, 128), F32)
        vbuf[0:WINDOW, :] = jnp.zeros((WINDOW, 128), F32)
        s_scr[...] = s0_ref[...]

    res = {}
    h = _rms(x_ref[...], n1g_ref[...]).astype(BF16)
    for lo, width in ((COL_A, W_A), (COL_B, W_B), (COL_C, W_C), (COL_D, W_D)):
        proj_s[:, lo:lo + width] = _dot(h, w_in_ref[:, lo:lo + width])
    yield

    small = proj_s[:, COL_SMALL:W_A]
    abuf[P:P + T, :] = proj_s[:, 0:A_QKV_W]

    def group_b():
        bbuf[P:P + T, :] = proj_s[:, COL_B + 256:COL_B + 512] * proj_s[:, COL_B + 512:COL_B + 768]
        bx = (bbuf[P:P + T, :] * bconv_w_ref[2:3, :] + bbuf[P - 1:P - 1 + T, :] * bconv_w_ref[1:2, :]
              + bbuf[P - 2:P - 2 + T, :] * bconv_w_ref[0:1, :])
        tail_b = bbuf[T:T + P, :]
        bbuf[0:P, :] = tail_b
        bconv_out_ref[...] = tail_b
        res['out_b'] = proj_s[:, COL_B:COL_B + 256] * bx

    def group_c_rope():
        cos = cos_ref[...]
        sina = sina_ref[...]
        sinb = sinb_ref[...]
        cq = proj_s[:, COL_C:COL_C + 256]
        res['cq'] = (cq * jnp.concatenate([cos, cos], axis=1)
                     + pltpu.roll(cq, 256 - ROPE_DIM // 2, 1) * jnp.concatenate([sina, sina], axis=1)
                     + pltpu.roll(cq, ROPE_DIM // 2, 1) * jnp.concatenate([sinb, sinb], axis=1))
        ck = proj_s[:, COL_C + 256:COL_C + 384]
        ck = (ck * cos + pltpu.roll(ck, 128 - ROPE_DIM // 2, 1) * sina
              + pltpu.roll(ck, ROPE_DIM // 2, 1) * sinb)
        kbuf[WINDOW:WINDOW + T, :] = ck
        vbuf[WINDOW:WINDOW + T, :] = proj_s[:, COL_C + 384:COL_C + 512]
        ck_out_ref[...] = kbuf[T:T + WINDOW, :]
        cv_out_ref[...] = vbuf[T:T + WINDOW, :]

    def group_d_norm():
        res['du'] = _gelu_tanh(proj_s[:, COL_D:COL_D + 256])
        gv = _gelu_tanh(proj_s[:, COL_D + 256:COL_D + 512])
        mu = jnp.mean(gv, axis=-1, keepdims=True)
        xc = gv - mu
        res['dvn'] = (xc * lax.rsqrt(jnp.mean(xc * xc, axis=-1, keepdims=True) + EPS) * lng_ref[...]
                      + lnb_ref[...])
        wr = _iota((D_GROUPS * D_CHUNK, D_CHUNK), 0) & (D_CHUNK - 1)
        wc = _iota((D_GROUPS * D_CHUNK, D_CHUNK), 1)
        res['wm'] = jnp.where(wr >= wc, ws_ref[...], 0.0).astype(BF16)

    y = (abuf[P:P + T, :] * aconv_w_ref[3:4, :] + abuf[P - 1:P - 1 + T, :] * aconv_w_ref[2:3, :]
         + abuf[P - 2:P - 2 + T, :] * aconv_w_ref[1:2, :] + abuf[P - 3:P - 3 + T, :] * aconv_w_ref[0:1, :])
    tail_a = abuf[T:T + P, :]
    abuf[0:P, :] = tail_a
    aconv_out_ref[...] = tail_a
    qkv = _silu(y)
    q_raw = qkv[:, 0:256]
    k_raw = qkv[:, 256:512]
    v_s[...] = qkv[:, 512:768]

    r256 = _iota((256, 256), 0) >> HEAD_SHIFT
    c256 = _iota((256, 256), 1) >> HEAD_SHIFT
    mask_bd = r256 == c256
    ones_bd = jnp.where(mask_bd, 1.0, 0.0).astype(BF16)
    q_s[...] = q_raw * lax.rsqrt(_head_sumsq(q_raw, ones_bd) + EPS) * (HEAD_DIM ** -0.5)
    k_s[...] = k_raw * lax.rsqrt(_head_sumsq(k_raw, ones_bd) + EPS)
    yield

    g_log = -jnp.exp(alog_ref[...]) * _softplus(small + dtb_ref[...])
    beta = _sigmoid(small)
    gbeta = jnp.where(_iota((T, 128), 1) < A_HEADS, g_log, beta)
    expand = jnp.where(_iota((128, 512), 0) == (_iota((128, 512), 1) >> HEAD_SHIFT), 1.0, 0.0).astype(BF16)
    gbb = _dot_lhs3(gbeta, expand)
    bb_s[...] = gbb[:, 256:512]

    def chunk_cumsum(xv):
        row_in_chunk = _iota(xv.shape, 0) & (A_CHUNK - 1)
        step = 1
        while step < A_CHUNK:
            xv = xv + jnp.where(row_in_chunk >= step, pltpu.roll(xv, step, 0), 0.0)
            step *= 2
        return xv

    gcb_s[...] = chunk_cumsum(gbb[:, 0:256])
    gct = chunk_cumsum(g_log).T[0:8, :]
    low_half = (_iota((1, T), 1) & A_CHUNK) == 0
    gct_r = pltpu.roll(gct, A_CHUNK, 1)
    gct_l = pltpu.roll(gct, T - A_CHUNK, 1)
    even_rows = [jnp.where(low_half, gct[a:a + 1, :], gct_r[a + 1:a + 2, :]) for a in (0, 2)]
    odd_rows = [jnp.where(low_half, gct_l[a:a + 1, :], gct[a + 1:a + 2, :]) for a in (0, 2)]
    yield

    ri = _iota((A_CHUNK, 256), 0)
    ci = _iota((A_CHUNK, 256), 1) & (A_CHUNK - 1)
    causal_t = ri >= ci
    strict_t = ri > ci
    eye_t = jnp.where(ri == ci, 1.0, 0.0)

    def bd16(x16):
        return jnp.concatenate([x16, x16, x16, x16], axis=0) * ones_bd

    n_chunks = T // A_CHUNK
    t_inv, pw, qk, qdec, e_tail, vb16, kbe16 = [], [], [], [], [], [], []
    for c in range(n_chunks):
        rows = slice(c * A_CHUNK, (c + 1) * A_CHUNK)
        blk = slice((c // 2) * 128, (c // 2) * 128 + 128)
        src = even_rows if c % 2 == 0 else odd_rows
        gc_row = jnp.concatenate([src[0][:, blk], src[1][:, blk]], axis=1)
        qc = q_s[rows, :]
        kc = k_s[rows, :]
        bbc = bb_s[rows, :]
        gcb = gcb_s[rows, :]
        decay = jnp.where(causal_t, jnp.exp(jnp.where(causal_t, gcb - gc_row, 0.0)), 0.0)
        eg = jnp.exp(gcb)
        kb = kc * bbc
        aq = _dot_nt(jnp.concatenate([kb, qc], axis=0).astype(BF16), bd16(kc.astype(BF16)))
        a_mat = jnp.where(strict_t, aq[0:A_CHUNK] * decay, 0.0)
        qk.append(aq[A_CHUNK:2 * A_CHUNK] * decay)
        t_inv.append(eye_t - a_mat)
        pw.append(a_mat)
        qdec.append(qc * eg)
        e_tail.append(jnp.exp(gcb[A_CHUNK - 1:A_CHUNK, :] - gcb))
        vb16.append((v_s[rows, :] * bbc).astype(BF16))
        kbe16.append((kb * eg).astype(BF16))
    yield
    u, w = [], []
    state = {'s': s_scr[...]}

    def level_first():
        for c in range(n_chunks):
            p16 = pw[c].astype(BF16)
            pw[c] = _dot(p16, bd16(p16))

    def level_mid():
        for c in range(n_chunks):
            p16 = pw[c].astype(BF16)
            res = _dot(jnp.concatenate([p16, t_inv[c].astype(BF16)], axis=0), bd16(p16))
            pw[c] = res[0:A_CHUNK]
            t_inv[c] = t_inv[c] + res[A_CHUNK:2 * A_CHUNK]

    def level_last():
        for c in range(n_chunks):
            t_c = t_inv[c] + _dot(t_inv[c].astype(BF16), bd16(pw[c].astype(BF16)))
            uw = _dot(t_c.astype(BF16), jnp.concatenate([bd16(vb16[c]), bd16(kbe16[c])], axis=1))
            u.append(uw[:, 0:256])
            w.append(uw[:, 256:512])

    def scan_step(c):
        rows = slice(c * A_CHUNK, (c + 1) * A_CHUNK)
        s_bd = state['s']
        wq = _dot(jnp.concatenate([w[c], qdec[c]], axis=0).astype(BF16), s_bd.astype(BF16))
        v_new = u[c] - wq[0:A_CHUNK]
        o_s[rows, :] = wq[A_CHUNK:2 * A_CHUNK] + _dot(qk[c].astype(BF16), bd16(v_new.astype(BF16)))
        kv = _dot_tn(k_s[rows, :].astype(BF16), (v_new * e_tail[c]).astype(BF16))
        g_last = gcb_s[(c + 1) * A_CHUNK - 1:(c + 1) * A_CHUNK, :]
        state['s'] = s_bd * jnp.exp(g_last) + jnp.where(mask_bd, kv, 0.0)

    chain = [level_first] + [level_mid] * 4 + [level_last] + [functools.partial(scan_step, c) for c in range(n_chunks)]

    lane128 = _iota((2 * WINDOW, 128), 1)
    low = _iota((WINDOW, 128), 1) < HEAD_DIM
    qrow = _iota((2 * WINDOW, 2 * WINDOW), 0) & (WINDOW - 1)
    kcol = _iota((2 * WINDOW, 2 * WINDOW), 1)
    band = (kcol > qrow) & (kcol <= qrow + WINDOW)
    top_half = _iota((2 * WINDOW, 1), 0) < WINDOW
    out_c_blocks = []

    def group_c_block(n):
        first_key = si * T + (n - 1) * WINDOW
        valid = band & (kcol + first_key >= 0)
        kwin = kbuf[n * WINDOW:(n + 2) * WINDOW, :]
        vwin = vbuf[n * WINDOW:(n + 2) * WINDOW, :]
        k_sw = pltpu.roll(kwin, HEAD_DIM, 1)
        v_sw = pltpu.roll(vwin, HEAD_DIM, 1)
        pair_out = []
        for g in range(C_KV_HEADS):
            own = (lane128 < HEAD_DIM) if g == 0 else (lane128 >= HEAD_DIM)
            k_dup = jnp.where(own, kwin, k_sw).astype(BF16)
            v_dup = jnp.where(own, vwin, v_sw).astype(BF16)
            qp = res['cq'][n * WINDOW:(n + 1) * WINDOW, g * 128:(g + 1) * 128]
            q_st = jnp.concatenate([jnp.where(low, qp, 0.0), jnp.where(low, 0.0, qp)], axis=0).astype(BF16)
            s = _dot_nt(q_st, k_dup) * (HEAD_DIM ** -0.5)
            s = jnp.where(valid, s, NEG_INF)
            sink = jnp.where(top_half, sinks_ref[2 * g], sinks_ref[2 * g + 1])
            m = jnp.maximum(jnp.max(s, axis=-1, keepdims=True), sink)
            p = jnp.exp(s - m)
            denom = jnp.sum(p, axis=-1, keepdims=True) + jnp.exp(sink - m)
            o2 = _dot((p / denom).astype(BF16), v_dup)
            pair_out.append(jnp.where(low, o2[0:WINDOW], o2[WINDOW:2 * WINDOW]))
        out_c_blocks.append(jnp.concatenate(pair_out, axis=1))
        if n == T // WINDOW - 1:
            kbuf[0:WINDOW, :] = kbuf[T:T + WINDOW, :]
            vbuf[0:WINDOW, :] = vbuf[T:T + WINDOW, :]

    lane_grp = _iota((D_CHUNK, 256), 1) >> HEAD_SHIFT
    out_d_blocks = []

    def group_d_block(n):
        mx = _dot(res['wm'], res['dvn'][n * D_CHUNK:(n + 1) * D_CHUNK, :].astype(BF16))
        mixed = dbias_ref[...]
        for grp in range(D_GROUPS):
            mixed = mixed + jnp.where(lane_grp == grp, mx[grp * D_CHUNK:(grp + 1) * D_CHUNK, :], 0.0)
        out_d_blocks.append(res['du'][n * D_CHUNK:(n + 1) * D_CHUNK, :] * mixed)

    c_blocks = [functools.partial(group_c_block, n) for n in range(T // WINDOW)]
    d_blocks = [functools.partial(group_d_block, n) for n in range(T // D_CHUNK)]
    fill = [group_b, group_c_rope, group_d_norm] + c_blocks + d_blocks
    for i in range(max(len(chain), len(fill))):
        if i < len(chain):
            chain[i]()
        if i < len(fill):
            fill[i]()
        yield
    out_b = res['out_b']
    out_c = jnp.concatenate(out_c_blocks, axis=0)
    out_d = jnp.concatenate(out_d_blocks, axis=0)

    s_scr[...] = state['s']
    s_out_ref[...] = state['s']
    o = o_s[...]
    out_a = (o * lax.rsqrt(_head_sumsq(o, ones_bd) * (1.0 / HEAD_DIM) + EPS) * anorm_ref[...]
             * _silu(proj_s[:, COL_Z:COL_SMALL]))

    cat = jnp.concatenate([out_a, out_b, out_c, out_d], axis=1).astype(BF16)
    x_out_ref[...] = x_ref[...] + _dot_w32(cat, w_out_ref[0])


def _mixer_call(x, lw, tabs, aconv0, bconv0, s0, layer):
    bsz, seq, _ = x.shape
    T = MIXER_TILE_ROWS
    ns = seq // T
    NB = MIXER_SEQS
    P = CARRY_ROWS
    full = lambda shape: pl.BlockSpec(shape, lambda b, s, *_: (0,) * len(shape), pipeline_mode=pl.Buffered(1))
    per_b = lambda shape: pl.BlockSpec((NB,) + shape, lambda b, s, *_: (b,) + (0,) * len(shape))
    tab = pl.BlockSpec((T, 128), lambda b, s, *_: (s, 0))
    seq_scratch = [pltpu.VMEM(shape, F32) for shape in (
        (T + P, A_QKV_W), (T + P, 256), (T + WINDOW, 128), (T + WINDOW, 128), (T, 256), (T, 256), (T, 256),
        (T, 256), (T, 256), (T, 256), (256, 256), (T, P_PACKED))]
    grid_spec = pltpu.PrefetchScalarGridSpec(
        num_scalar_prefetch=1,
        grid=(bsz // NB, ns),
        in_specs=[
            pl.BlockSpec((NB, T, D_MODEL), lambda b, s, *_: (b, s, 0)),
            full((1, D_MODEL)), full((D_MODEL, P_PACKED)), full((A_CONV, A_QKV_W)), full((1, 128)), full((1, 128)),
            full((1, 256)), full((B_CONV, 256)), tab, tab, tab, full((1, 256)), full((1, 256)),
            full((D_GROUPS * D_CHUNK, D_CHUNK)), full((D_CHUNK, 256)),
            pl.BlockSpec((1, D_MODEL, D_MODEL), lambda b, s, *_: (layer, 0, 0), pipeline_mode=pl.Buffered(1)),
            per_b((P, A_QKV_W)), per_b((P, 256)), per_b((256, 256)),
        ],
        out_specs=[
            pl.BlockSpec((NB, T, D_MODEL), lambda b, s, *_: (b, s, 0)),
            per_b((256, 256)), per_b((P, A_QKV_W)), per_b((P, 256)), per_b((WINDOW, 128)), per_b((WINDOW, 128)),
        ],
        scratch_shapes=seq_scratch * NB,
    )
    out_shape = [
        jax.ShapeDtypeStruct((bsz, seq, D_MODEL), F32),
        jax.ShapeDtypeStruct((bsz, 256, 256), F32),
        jax.ShapeDtypeStruct((bsz, P, A_QKV_W), F32),
        jax.ShapeDtypeStruct((bsz, P, 256), F32),
        jax.ShapeDtypeStruct((bsz, WINDOW, 128), F32),
        jax.ShapeDtypeStruct((bsz, WINDOW, 128), F32),
    ]
    return pl.pallas_call(
        functools.partial(_mixer_kernel, tile=T),
        grid_spec=grid_spec,
        out_shape=out_shape,
        compiler_params=pltpu.CompilerParams(
            dimension_semantics=("arbitrary", "arbitrary"), vmem_limit_bytes=MIXER_VMEM_LIMIT_BYTES),
        name="mixer",
    )(lw['c_sinks'], x, lw['norm1_g'], lw['w_in'], lw['a_conv_w'], lw['a_log'], lw['a_dt_bias'],
      lw['a_norm_g'], lw['b_conv_w'], tabs[0], tabs[1], tabs[2], lw['d_ln_g'], lw['d_ln_b'],
      lw['d_ws'], lw['d_bias'], lw['w_out'], aconv0, bconv0, s0)


def _ffn_kernel(x_ref, n2g_ref, wg_ref, wu_ref, cw_ref, wd_ref, fconv0_ref, fng_ref,
                x_out_ref, fconv_out_ref, gbuf, *, tile, stride, final_norm):
    T = tile
    P = (FFN_CONV - 1) * stride if stride > 1 else CARRY_ROWS
    si = pl.program_id(1)

    @pl.when(si == 0)
    def _():
        gbuf[0:P, :] = fconv0_ref[0]

    x = x_ref[0]
    h = _rms(x, n2g_ref[...]).astype(BF16)
    acc = x
    for c0 in range(0, D_FF, FFN_COL_CHUNK):
        cols = slice(c0, c0 + FFN_COL_CHUNK)
        gbuf[P:P + T, cols] = _dot_w32(h, wg_ref[0, :, cols])
        up = _dot_w32(h, wu_ref[0, :, cols])
        gate = (gbuf[P:P + T, cols] * cw_ref[2:3, cols] + gbuf[P - stride:P - stride + T, cols] * cw_ref[1:2, cols]
                + gbuf[P - 2 * stride:P - 2 * stride + T, cols] * cw_ref[0:1, cols])
        acc = acc + _dot_w32((_silu(gate) * up).astype(BF16), wd_ref[0, cols, :])
    tail = gbuf[T:T + P, :]
    gbuf[0:P, :] = tail
    fconv_out_ref[0] = tail
    if final_norm:
        acc = _rms(acc, fng_ref[...])
    x_out_ref[0] = acc


def _ffn_call(x, lw, fconv0, final_g, *, layer, tile, stride, final_norm):
    bsz, seq, _ = x.shape
    T = tile
    P = (FFN_CONV - 1) * stride if stride > 1 else CARRY_ROWS
    full = lambda shape: pl.BlockSpec(shape, lambda b, s: (0,) * len(shape), pipeline_mode=pl.Buffered(1))
    per_b = lambda shape: pl.BlockSpec((1,) + shape, lambda b, s: (b,) + (0,) * len(shape))
    weight = lambda shape: pl.BlockSpec((1,) + shape, lambda b, s: (layer, 0, 0), pipeline_mode=pl.Buffered(1))
    return pl.pallas_call(
        functools.partial(_ffn_kernel, tile=T, stride=stride, final_norm=final_norm),
        grid=(bsz, seq // T),
        in_specs=[
            pl.BlockSpec((1, T, D_MODEL), lambda b, s: (b, s, 0)),
            full((1, D_MODEL)), weight((D_MODEL, D_FF)), weight((D_MODEL, D_FF)), full((FFN_CONV, D_FF)),
            weight((D_FF, D_MODEL)), per_b((P, D_FF)), full((1, D_MODEL)),
        ],
        out_specs=[pl.BlockSpec((1, T, D_MODEL), lambda b, s: (b, s, 0)), per_b((P, D_FF))],
        out_shape=[jax.ShapeDtypeStruct((bsz, seq, D_MODEL), F32), jax.ShapeDtypeStruct((bsz, P, D_FF), F32)],
        scratch_shapes=[pltpu.VMEM((T + P, D_FF), F32)],
        compiler_params=pltpu.CompilerParams(
            dimension_semantics=("arbitrary", "arbitrary"), vmem_limit_bytes=VMEM_LIMIT_BYTES),
        name="ffn",
    )(x, lw['norm2_g'], lw['ffn_w_gate'], lw['ffn_w_up'], lw['ffn_conv_w'], lw['ffn_w_down'], fconv0, final_g)


def _pack_w_in(w):
    pad = jnp.zeros((D_MODEL, 128 - 2 * A_HEADS), w.dtype)
    small = jnp.concatenate([w[:, 1028:1032], w[:, 1024:1028], pad], axis=1)
    return jnp.concatenate([w[:, 0:1024], small, w[:, 1032:]], axis=1).astype(BF16)


def _pad_lanes(v, width=128):
    return jnp.concatenate([v, jnp.zeros((width - v.shape[0],), v.dtype)])[None, :]


def _layer_weights(l, norm1_g, w_in, a_conv_w, a_log, a_dt_bias, a_norm_g, b_conv_w, c_sinks, d_ln_g, d_ln_b,
                   d_ws, d_bias, w_out, norm2_g, ffn_w_gate, ffn_w_up, ffn_conv_w, ffn_w_down):
    bias_tab = jnp.broadcast_to(d_bias[l].T[:, :, None], (D_CHUNK, D_GROUPS, HEAD_DIM)).reshape(D_CHUNK, 256)
    return {
        'norm1_g': norm1_g[l][None, :], 'w_in': _pack_w_in(w_in[l]), 'a_conv_w': a_conv_w[l],
        'a_log': _pad_lanes(a_log[l]), 'a_dt_bias': _pad_lanes(a_dt_bias[l]),
        'a_norm_g': jnp.tile(a_norm_g[l], A_HEADS)[None, :], 'b_conv_w': b_conv_w[l], 'c_sinks': c_sinks[l],
        'd_ln_g': d_ln_g[l][None, :], 'd_ln_b': d_ln_b[l][None, :],
        'd_ws': d_ws[l].reshape(D_GROUPS * D_CHUNK, D_CHUNK), 'd_bias': bias_tab,
        'w_out': w_out, 'norm2_g': norm2_g[l][None, :],
        'ffn_w_gate': ffn_w_gate, 'ffn_w_up': ffn_w_up, 'ffn_conv_w': ffn_conv_w[l], 'ffn_w_down': ffn_w_down,
    }


def _rope_tables(pos):
    half = ROPE_DIM // 2
    inv = np.power(ROPE_THETA, -np.arange(half, dtype=np.float64) * (2.0 / ROPE_DIM))
    ang = pos.astype(np.float64)[:, None] * inv[None, :]
    cos, sin = jnp.asarray(np.cos(ang), F32), jnp.asarray(np.sin(ang), F32)
    n = pos.shape[0]
    rest = HEAD_DIM - ROPE_DIM
    cos_h = jnp.concatenate([cos, cos, jnp.ones((n, rest), F32)], axis=1)
    sina_h = jnp.concatenate([-sin, jnp.zeros((n, half + rest), F32)], axis=1)
    sinb_h = jnp.concatenate([jnp.zeros((n, half), F32), sin, jnp.zeros((n, rest), F32)], axis=1)
    return tuple(jnp.concatenate([t, t], axis=1) for t in (cos_h, sina_h, sinb_h))


def _diag_blocks(s_bd):
    return jnp.stack([s_bd[:, HEAD_DIM * i:HEAD_DIM * (i + 1), HEAD_DIM * i:HEAD_DIM * (i + 1)]
                      for i in range(A_HEADS)], axis=1)


def _sample_pre_kernel(x_ref, n1g_ref, w_in_ref, aconv_w_ref, alog_ref, dtb_ref, bconv_w_ref, cos_ref, sina_ref,
                       sinb_ref, lng_ref, lnb_ref, wtab_ref, btab_ref, aconv0_ref, bconv0_ref,
                       q_ref, k_ref, v_ref, g_ref, beta_ref, z_ref, outb_ref, cq_ref, ck_ref, cv_ref, outd_ref,
                       dvn_ref, aconv_out_ref, bconv_out_ref, abuf, bbuf, *, rows, stride):
    R, S = rows, stride
    steps = R // S
    pa_rows = (A_CONV - 1) * S
    pb_rows = (B_CONV - 1) * S
    x = x_ref[...]
    h = _rms(x, n1g_ref[...]).astype(BF16)

    pa = _dot(h, w_in_ref[:, COL_A:COL_A + W_A])
    z_ref[...] = pa[:, COL_Z:COL_SMALL]
    small = pa[:, COL_SMALL:W_A]
    abuf[0:pa_rows, :] = aconv0_ref[...]
    abuf[pa_rows:pa_rows + R, :] = pa[:, 0:A_QKV_W]
    y = abuf[0:R, :] * aconv_w_ref[0:1, :]
    for j in range(1, A_CONV):
        y = y + abuf[j * S:j * S + R, :] * aconv_w_ref[j:j + 1, :]
    aconv_out_ref[...] = abuf[R:R + pa_rows, :]
    qkv = _silu(y)
    q_raw = qkv[:, 0:256]
    k_raw = qkv[:, 256:512]
    v_ref[...] = qkv[:, 512:768]
    mask_bd = (_iota((256, 256), 0) >> HEAD_SHIFT) == (_iota((256, 256), 1) >> HEAD_SHIFT)
    ones_bd = jnp.where(mask_bd, 1.0, 0.0).astype(BF16)
    q_ref[...] = q_raw * lax.rsqrt(_dot_lhs3(q_raw * q_raw, ones_bd) + EPS) * (HEAD_DIM ** -0.5)
    k_ref[...] = k_raw * lax.rsqrt(_dot_lhs3(k_raw * k_raw, ones_bd) + EPS)
    g_ref[...] = -jnp.exp(alog_ref[...]) * _softplus(small + dtb_ref[...])
    beta_ref[...] = _sigmoid(small)

    pb = _dot(h, w_in_ref[:, COL_B:COL_B + W_B])
    bbuf[0:pb_rows, :] = bconv0_ref[...]
    bbuf[pb_rows:pb_rows + R, :] = pb[:, 256:512] * pb[:, 512:768]
    bx = bbuf[0:R, :] * bconv_w_ref[0:1, :]
    for j in range(1, B_CONV):
        bx = bx + bbuf[j * S:j * S + R, :] * bconv_w_ref[j:j + 1, :]
    bconv_out_ref[...] = bbuf[R:R + pb_rows, :]
    outb_ref[...] = pb[:, 0:256] * bx

    pc = _dot(h, w_in_ref[:, COL_C:COL_C + W_C])
    cos = cos_ref[...]
    sina = sina_ref[...]
    sinb = sinb_ref[...]
    cq = pc[:, 0:256]
    cq_ref[...] = (cq * jnp.concatenate([cos, cos], axis=1)
                   + pltpu.roll(cq, 256 - ROPE_DIM // 2, 1) * jnp.concatenate([sina, sina], axis=1)
                   + pltpu.roll(cq, ROPE_DIM // 2, 1) * jnp.concatenate([sinb, sinb], axis=1))
    ck = pc[:, 256:384]
    ck_ref[...] = (ck * cos + pltpu.roll(ck, 128 - ROPE_DIM // 2, 1) * sina
                   + pltpu.roll(ck, ROPE_DIM // 2, 1) * sinb)
    cv_ref[...] = pc[:, 384:512]

    pd = _dot(h, w_in_ref[:, COL_D:COL_D + W_D])
    du = _gelu_tanh(pd[:, 0:256])
    gv = _gelu_tanh(pd[:, 256:512])
    mu = jnp.mean(gv, axis=-1, keepdims=True)
    xc = gv - mu
    dvn = xc * lax.rsqrt(jnp.mean(xc * xc, axis=-1, keepdims=True) + EPS) * lng_ref[...] + lnb_ref[...]
    dvn_ref[...] = dvn
    for t in range(steps):
        mixed = btab_ref[t:t + 1, :]
        for s in range(t + 1):
            mixed = mixed + wtab_ref[t * steps + s:t * steps + s + 1, :] * dvn[s * S:(s + 1) * S, :]
        outd_ref[t * S:(t + 1) * S, :] = du[t * S:(t + 1) * S, :] * mixed


def _sample_pre_call(x, lw, tabs, wtab, btab, aconv0, bconv0, stride):
    R = x.shape[0]
    f = lambda *shape: jax.ShapeDtypeStruct(shape, F32)
    out_shape = [f(R, 256), f(R, 256), f(R, 256), f(R, 128), f(R, 128), f(R, 256), f(R, 256), f(R, 256),
                 f(R, 128), f(R, 128), f(R, 256), f(R, 256), f((A_CONV - 1) * stride, A_QKV_W),
                 f((B_CONV - 1) * stride, 256)]
    return pl.pallas_call(
        functools.partial(_sample_pre_kernel, rows=R, stride=stride),
        out_shape=out_shape,
        scratch_shapes=[pltpu.VMEM((R + (A_CONV - 1) * stride, A_QKV_W), F32),
                        pltpu.VMEM((R + (B_CONV - 1) * stride, 256), F32)],
        compiler_params=pltpu.CompilerParams(vmem_limit_bytes=VMEM_LIMIT_BYTES),
        name="sample_pre",
    )(x, lw['norm1_g'], lw['w_in'], lw['a_conv_w'], lw['a_log'], lw['a_dt_bias'], lw['b_conv_w'],
      tabs[0], tabs[1], tabs[2], lw['d_ln_g'], lw['d_ln_b'], wtab, btab, aconv0, bconv0)


DELTA_UNROLL = 4


def _sample_delta_kernel(g_ref, beta_ref, q_ref, k_ref, v_ref, s_ref, o_ref, snew_ref,
                         qt_s, kt_s, vt_s, gt_s, bt_s, ot_s, *, steps, nb):
    h = pl.program_id(0)
    n_i = HEAD_DIM
    zeros = jnp.zeros((HEAD_DIM, nb), F32)

    @pl.when(h == 0)
    def _():
        for t in range(steps):
            rows = slice(t * nb, (t + 1) * nb)
            qt_s[t] = q_ref[rows, :].T
            kt_s[t] = k_ref[rows, :].T
            vt_s[t] = v_ref[rows, :].T
            gt_s[t] = g_ref[rows, :].T
            bt_s[t] = beta_ref[rows, :].T

    base = pl.multiple_of(h * HEAD_DIM, HEAD_DIM)
    head_rows = pl.ds(base, HEAD_DIM)

    def rows_of(i):
        return pl.ds(pl.multiple_of(i * HEAD_DIM, HEAD_DIM), HEAD_DIM)

    def decay(t):
        return jnp.exp(gt_s[t, pl.ds(h, 1), :])

    dec0 = decay(0)

    def first_pass(i, acc):
        return acc + kt_s[0, pl.ds(base + i, 1), :] * (s_ref[0, rows_of(i), :] * dec0)

    ks = lax.fori_loop(0, n_i, first_pass, zeros, unroll=DELTA_UNROLL)
    for t in range(steps):
        dec = decay(t)
        v_new = bt_s[t, pl.ds(A_HEADS + h, 1), :] * (vt_s[t, head_rows, :] - ks)
        src = s_ref if t == 0 else snew_ref
        dec_next = decay(t + 1) if t + 1 < steps else None

        def update(i, carry, t=t, dec=dec, v_new=v_new, src=src, dec_next=dec_next):
            o_acc, ks_acc = carry
            blk = src[0, rows_of(i), :] * dec + kt_s[t, pl.ds(base + i, 1), :] * v_new
            snew_ref[0, rows_of(i), :] = blk
            o_acc = o_acc + qt_s[t, pl.ds(base + i, 1), :] * blk
            if dec_next is not None:
                ks_acc = ks_acc + kt_s[t + 1, pl.ds(base + i, 1), :] * (blk * dec_next)
            return o_acc, ks_acc

        o_acc, ks = lax.fori_loop(0, n_i, update, (zeros, zeros), unroll=DELTA_UNROLL)
        ot_s[t, head_rows, :] = o_acc

    @pl.when(h == A_HEADS - 1)
    def _():
        for t in range(steps):
            o_ref[t * nb:(t + 1) * nb, :] = ot_s[t].T


def _sample_delta_call(g, beta, q, k, v, s, steps):
    nh, _, nb = s.shape
    rows = steps * nb
    whole = lambda width: pl.BlockSpec((rows, width), lambda h: (0, 0))
    st = pl.BlockSpec((1, HEAD_DIM * HEAD_DIM, nb), lambda h: (h, 0, 0))
    wide = pltpu.VMEM((steps, GROUP_WIDTH, nb), F32)
    narrow = pltpu.VMEM((steps, 128, nb), F32)
    return pl.pallas_call(
        functools.partial(_sample_delta_kernel, steps=steps, nb=nb),
        grid=(nh,),
        in_specs=[whole(128), whole(128), whole(GROUP_WIDTH), whole(GROUP_WIDTH), whole(GROUP_WIDTH), st],
        out_specs=[whole(GROUP_WIDTH), st],
        out_shape=[jax.ShapeDtypeStruct((rows, GROUP_WIDTH), F32),
                   jax.ShapeDtypeStruct((nh, HEAD_DIM * HEAD_DIM, nb), F32)],
        scratch_shapes=[wide, wide, wide, narrow, narrow, wide],
        compiler_params=pltpu.CompilerParams(dimension_semantics=("arbitrary",),
                                             vmem_limit_bytes=VMEM_LIMIT_BYTES),
        name="sample_delta",
    )(g, beta, q, k, v, s)


SAMPLE_ATTN_BLOCK = 16
NEW_KEY_ROWS = 8


def _sample_attn_kernel(sinks_ref, qm_ref, kc_ref, kn_ref, vc_ref, vn_ref, o_ref, k_out_ref, v_out_ref, *, steps):
    nq = C_HEADS * steps
    nk = WINDOW + NEW_KEY_ROWS
    BB = SAMPLE_ATTN_BLOCK
    row = _iota((BB * nq, nk), 0)
    col = _iota((BB * nq, nk), 1)
    t_q = row & (steps - 1)
    valid = ((col < WINDOW) & (col > t_q)) | ((col >= WINDOW) & (col - WINDOW <= t_q))
    head = (_iota((BB * nq, 1), 0) >> 2) & (C_HEADS - 1)
    sink = jnp.where(head == 0, sinks_ref[0],
                     jnp.where(head == 1, sinks_ref[1], jnp.where(head == 2, sinks_ref[2], sinks_ref[3])))
    scores = [_dot_nt(qm_ref[b].astype(BF16), jnp.concatenate([kc_ref[0, b], kn_ref[b]], axis=0).astype(BF16))
              for b in range(BB)]
    s = jnp.where(valid, jnp.concatenate(scores, axis=0) * (HEAD_DIM ** -0.5), NEG_INF)
    m = jnp.maximum(jnp.max(s, axis=-1, keepdims=True), sink)
    p = jnp.exp(s - m)
    denom = jnp.sum(p, axis=-1, keepdims=True) + jnp.exp(sink - m)
    p16 = (p / denom).astype(BF16)
    is_new_row = _iota((WINDOW, 128), 0) >= WINDOW - steps
    pad_rows = jnp.zeros((WINDOW - NEW_KEY_ROWS, 128), F32)

    def slide(cache, new_rows):
        kept = pltpu.roll(cache, WINDOW - steps, 0)
        tail = jnp.concatenate([pad_rows, pltpu.roll(new_rows, NEW_KEY_ROWS - steps, 0)], axis=0)
        return jnp.where(is_new_row, tail, kept)

    for b in range(BB):
        v_all = jnp.concatenate([vc_ref[0, b], vn_ref[b]], axis=0).astype(BF16)
        o_ref[b] = _dot(p16[b * nq:(b + 1) * nq, :], v_all)
        k_out_ref[b] = slide(kc_ref[0, b], kn_ref[b])
        v_out_ref[b] = slide(vc_ref[0, b], vn_ref[b])


def _sample_attn_call(sinks, qm, kc_all, kn, vc_all, vn, layer, steps):
    bs, nq, _ = qm.shape
    BB = SAMPLE_ATTN_BLOCK
    blk = lambda r: pl.BlockSpec((BB, r, 128), lambda i, *_: (i, 0, 0))
    cache = pl.BlockSpec((1, BB, WINDOW, 128), lambda i, *_: (layer, i, 0, 0))
    grid_spec = pltpu.PrefetchScalarGridSpec(
        num_scalar_prefetch=1, grid=(bs // BB,),
        in_specs=[blk(nq), cache, blk(NEW_KEY_ROWS), cache, blk(NEW_KEY_ROWS)],
        out_specs=[blk(nq), blk(WINDOW), blk(WINDOW)])
    return pl.pallas_call(
        functools.partial(_sample_attn_kernel, steps=steps),
        grid_spec=grid_spec,
        out_shape=[jax.ShapeDtypeStruct((bs, nq, 128), F32), jax.ShapeDtypeStruct((bs, WINDOW, 128), F32),
                   jax.ShapeDtypeStruct((bs, WINDOW, 128), F32)],
        compiler_params=pltpu.CompilerParams(dimension_semantics=("arbitrary",)),
        name="sample_attn",
    )(sinks, qm, kc_all, kn, vc_all, vn)


def _sample_post_kernel(x_ref, o_ref, z_ref, outb_ref, outc_ref, outd_ref, anorm_ref, w_out_ref, x_out_ref, *,
                        layer):
    mask_bd = (_iota((256, 256), 0) >> HEAD_SHIFT) == (_iota((256, 256), 1) >> HEAD_SHIFT)
    ones_bd = jnp.where(mask_bd, 1.0, 0.0).astype(BF16)
    o = o_ref[...]
    out_a = (o * lax.rsqrt(_dot_lhs3(o * o, ones_bd) * (1.0 / HEAD_DIM) + EPS) * anorm_ref[...]
             * _silu(z_ref[...]))
    cat = jnp.concatenate([out_a, outb_ref[...], outc_ref[...], outd_ref[...]], axis=1).astype(BF16)
    x_out_ref[...] = x_ref[...] + _dot_w32(cat, w_out_ref[layer])


def _sample_post_call(x, o, z, out_b, out_c, out_d, lw, layer):
    return pl.pallas_call(
        functools.partial(_sample_post_kernel, layer=layer),
        out_shape=jax.ShapeDtypeStruct(x.shape, F32),
        compiler_params=pltpu.CompilerParams(vmem_limit_bytes=VMEM_LIMIT_BYTES),
        name="sample_post",
    )(x, o, z, out_b, out_c, out_d, lw['a_norm_g'], lw['w_out'])


def _sample_mixer(x_tm, lw, tabs, d_ws_l, d_bias_l, a_state, a_conv, b_conv, c_k_all, c_v_all, layer, bs, ts):
    to_tm = lambda a: jnp.swapaxes(a, 0, 1).reshape(a.shape[1] * bs, a.shape[2])
    from_tm = lambda a, n: jnp.swapaxes(a.reshape(n, bs, a.shape[-1]), 0, 1)
    wtab = jnp.repeat(d_ws_l[:, :ts, :ts].transpose(1, 2, 0).reshape(ts * ts, D_GROUPS), HEAD_DIM, axis=1)
    btab = jnp.repeat(d_bias_l[:, :ts].T, HEAD_DIM, axis=1)
    (q, k, v, g, beta, z, out_b, cq, ck, cv, out_d, dvn, a_tail, b_tail) = _sample_pre_call(
        x_tm, lw, tabs, wtab, btab, to_tm(a_conv), to_tm(b_conv), bs)

    s_t = a_state.reshape(bs, A_HEADS, HEAD_DIM * HEAD_DIM).transpose(1, 2, 0)
    o, s_new_t = _sample_delta_call(g, beta, q, k, v, s_t, ts)
    a_state_new = s_new_t

    cq4 = cq.reshape(ts, bs, C_HEADS, HEAD_DIM).transpose(1, 2, 0, 3)
    zq = jnp.zeros_like(cq4[:, 0])
    qm = jnp.concatenate(
        [jnp.concatenate([cq4[:, hh], zq] if hh // 2 == 0 else [zq, cq4[:, hh]], axis=-1) for hh in range(C_HEADS)],
        axis=1)
    pad_new = lambda a: jnp.concatenate([from_tm(a, ts), jnp.zeros((bs, NEW_KEY_ROWS - ts, 128), F32)], axis=1)
    o_att, c_k_new, c_v_new = _sample_attn_call(lw['c_sinks'], qm, c_k_all, pad_new(ck), c_v_all, pad_new(cv),
                                                layer, ts)
    out_c = jnp.concatenate(
        [o_att[:, hh * ts:(hh + 1) * ts, (hh // 2) * HEAD_DIM:(hh // 2 + 1) * HEAD_DIM] for hh in range(C_HEADS)],
        axis=-1)
    out_c = jnp.swapaxes(out_c, 0, 1).reshape(ts * bs, GROUP_WIDTH)

    x2 = _sample_post_call(x_tm, o, z, out_b, out_c, out_d, lw, layer)
    new = {'a_state': a_state_new, 'a_conv': from_tm(a_tail, A_CONV - 1), 'b_conv': from_tm(b_tail, B_CONV - 1),
           'c_k': c_k_new, 'c_v': c_v_new, 'd_v': from_tm(dvn, ts)}
    return x2, new


def kernel(x_prompt, x_sample, state_delta, state_delta_conv, state_shortconv, cache_win_k, cache_win_v,
           state_ffn_conv, norm1_g, w_in, a_conv_w, a_log, a_dt_bias, a_norm_g, b_conv_w, c_sinks, d_ln_g,
           d_ln_b, d_ws, d_bias, w_out, norm2_g, ffn_w_gate, ffn_w_up, ffn_conv_w, ffn_w_down, final_norm_g):
    bp, tp, _ = x_prompt.shape
    bs, ts, _ = x_sample.shape
    depth = w_in.shape[0]
    win_buf = cache_win_k.shape[2]
    pos_p = np.arange(tp, dtype=np.int32)
    pos_s = PAST_LEN + np.arange(ts, dtype=np.int32)
    assert win_buf == WINDOW and ts == 4
    tabs_p = _rope_tables(pos_p)
    tabs_s = tuple(jnp.repeat(t, bs, axis=0) for t in _rope_tables(pos_s))
    fng = final_norm_g[None, :]
    ck_all = cache_win_k.reshape(depth, bs, WINDOW, 128)
    cv_all = cache_win_v.reshape(depth, bs, WINDOW, 128)
    P = CARRY_ROWS

    hp = x_prompt
    hs = jnp.swapaxes(x_sample, 0, 1).reshape(ts * bs, D_MODEL)
    outs = {k: [] for k in ('sp', 'ss', 'acp', 'acs', 'bcp', 'bcs', 'ckp', 'cks', 'cvp', 'cvs', 'fcp', 'fcs', 'dv')}
    for l in range(depth):
        lw = _layer_weights(l, norm1_g, w_in, a_conv_w, a_log, a_dt_bias, a_norm_g, b_conv_w, c_sinks, d_ln_g,
                            d_ln_b, d_ws, d_bias, w_out, norm2_g, ffn_w_gate, ffn_w_up, ffn_conv_w, ffn_w_down)
        last = l == depth - 1
        hp, s_bd, acv, bcv, ckn, cvn = _mixer_call(
            hp, lw, tabs_p, jnp.zeros((bp, P, A_QKV_W), F32), jnp.zeros((bp, P, 256), F32),
            jnp.zeros((bp, 256, 256), F32), l)
        hp, fcv = _ffn_call(hp, lw, jnp.zeros((bp, P, D_FF), F32), fng, layer=l, tile=TILE_ROWS, stride=1,
                            final_norm=last)
        outs['sp'].append(_diag_blocks(s_bd))
        outs['acp'].append(acv[:, P - (A_CONV - 1):])
        outs['bcp'].append(bcv[:, P - (B_CONV - 1):])
        outs['ckp'].append(ckn.reshape(bp, WINDOW, C_KV_HEADS, HEAD_DIM))
        outs['cvp'].append(cvn.reshape(bp, WINDOW, C_KV_HEADS, HEAD_DIM))
        outs['fcp'].append(fcv[:, P - (FFN_CONV - 1):])
        hs, ns = _sample_mixer(hs, lw, tabs_s, d_ws[l], d_bias[l], state_delta[l], state_delta_conv[l],
                               state_shortconv[l], ck_all, cv_all, l, bs, ts)
        f0 = jnp.swapaxes(state_ffn_conv[l], 0, 1).reshape(1, (FFN_CONV - 1) * bs, D_FF)
        ys_tm, fcs = _ffn_call(hs[None], lw, f0, fng, layer=l, tile=ts * bs, stride=bs, final_norm=last)
        hs = ys_tm[0]
        outs['ss'].append(ns['a_state'])
        outs['acs'].append(ns['a_conv'])
        outs['bcs'].append(ns['b_conv'])
        outs['cks'].append(ns['c_k'])
        outs['cvs'].append(ns['c_v'])
        outs['fcs'].append(jnp.swapaxes(fcs.reshape(FFN_CONV - 1, bs, D_FF), 0, 1))
        outs['dv'].append(ns['d_v'])
    st = {k: jnp.stack(v) for k, v in outs.items()}
    st['ss'] = st['ss'].transpose(0, 3, 1, 2).reshape(depth, bs, A_HEADS, HEAD_DIM, HEAD_DIM)
    st['cks'] = st['cks'].reshape(depth, bs, WINDOW, C_KV_HEADS, HEAD_DIM)
    st['cvs'] = st['cvs'].reshape(depth, bs, WINDOW, C_KV_HEADS, HEAD_DIM)
    hs = jnp.swapaxes(hs.reshape(ts, bs, D_MODEL), 0, 1)
    return (hp, hs, st['sp'], st['ss'], st['acp'], st['acs'], st['bcp'], st['bcs'], st['ckp'], st['cks'],
            st['cvp'], st['cvs'], st['fcp'], st['fcs'], st['dv'])
```

```python
import functools

import jax
import jax.numpy as jnp
import numpy as np
from jax import lax
from jax.experimental import pallas as pl
from jax.experimental.pallas import tpu as pltpu

F32 = jnp.float32
BF16 = jnp.bfloat16

D_MODEL = 1024
GROUP_WIDTH = 256
HEAD_DIM = 64
A_HEADS = 4
A_CONV = 4
A_CHUNK = 64
B_CONV = 3
C_HEADS = 4
C_KV_HEADS = 2
WINDOW = 128
ROPE_DIM = 16
ROPE_THETA = 500000.0
D_GROUPS = 4
D_CHUNK = 128
D_FF = 2816
FFN_CONV = 3
EPS = 1e-6
NEG_INF = -1e30
PAST_LEN = 16384

HEAD_SHIFT = 6
LANES = 128

COL_A = 0
A_QKV_W = 3 * GROUP_WIDTH
COL_Z = A_QKV_W
COL_SMALL = COL_Z + GROUP_WIDTH
W_A = COL_SMALL + LANES
COL_B = 1152
W_B = 768
COL_C = 1920
W_C = 512
COL_D = 2432
W_D = 512
P_PACKED = 2944

TILE_ROWS = 512
MIXER_TILE_ROWS = 512
CARRY_ROWS = 8
FFN_COL_CHUNK = 1408
VMEM_LIMIT_BYTES = 56 * 1024 * 1024
MIXER_VMEM_LIMIT_BYTES = 60 * 1024 * 1024


def _dot(a, b):
    return jnp.dot(a, b, preferred_element_type=F32)


def _dot_w32(a, w):
    return lax.dot_general(a, w, (((1,), (0,)), ((), ())), preferred_element_type=F32)


def _dot_nt(a, b):
    return lax.dot_general(a, b, (((1,), (1,)), ((), ())), preferred_element_type=F32)


def _dot_tn(a, b):
    return lax.dot_general(a, b, (((0,), (0,)), ((), ())), preferred_element_type=F32)


def _split3(x):
    hi = x.astype(BF16)
    r1 = x - hi.astype(F32)
    mid = r1.astype(BF16)
    lo = (r1 - mid.astype(F32)).astype(BF16)
    return hi, mid, lo


def _dot_lhs3(x, w01):
    hi, mid, lo = _split3(x)
    return _dot(hi, w01) + _dot(mid, w01) + _dot(lo, w01)


def _head_sumsq(x, ones_bd):
    return _dot((x * x).astype(BF16), ones_bd)


def _sigmoid(x):
    return 1.0 / (1.0 + jnp.exp(-x))


def _silu(x):
    return x * _sigmoid(x)


def _softplus(x):
    return jnp.maximum(x, 0.0) + jnp.log(1.0 + jnp.exp(-jnp.abs(x)))


def _gelu_tanh(x):
    return 0.5 * x * (1.0 + jnp.tanh(np.sqrt(2.0 / np.pi).astype(np.float32) * (x + 0.044715 * (x * x * x))))


def _rms(x, g):
    return x * lax.rsqrt(jnp.mean(x * x, axis=-1, keepdims=True) + EPS) * g


def _iota(shape, dim):
    return lax.broadcasted_iota(jnp.int32, shape, dim)


MIXER_SEQS = 2


def _mixer_kernel(sinks_ref, x_ref, n1g_ref, w_in_ref, aconv_w_ref, alog_ref, dtb_ref, anorm_ref,
                  bconv_w_ref, cos_ref, sina_ref, sinb_ref, lng_ref, lnb_ref, ws_ref, dbias_ref,
                  w_out_ref, aconv0_ref, bconv0_ref, s0_ref,
                  x_out_ref, s_out_ref, aconv_out_ref, bconv_out_ref, ck_out_ref, cv_out_ref,
                  *scratch, tile):
    shared = (sinks_ref, n1g_ref, w_in_ref, aconv_w_ref, alog_ref, dtb_ref, anorm_ref, bconv_w_ref, cos_ref,
              sina_ref, sinb_ref, lng_ref, lnb_ref, ws_ref, dbias_ref, w_out_ref)
    per_seq = (x_ref, aconv0_ref, bconv0_ref, s0_ref, x_out_ref, s_out_ref, aconv_out_ref, bconv_out_ref,
               ck_out_ref, cv_out_ref)
    n_scr = len(scratch) // MIXER_SEQS
    gens = [_mixer_seq(*shared, *(r.at[b] for r in per_seq), *scratch[b * n_scr:(b + 1) * n_scr], tile=tile)
            for b in range(MIXER_SEQS)]
    started, live = 1, list(gens[:1])
    while live:
        for gen in list(live):
            if next(gen, 'done') == 'done':
                live.remove(gen)
        if started < len(gens):
            live.append(gens[started])
            started += 1


def _mixer_seq(sinks_ref, n1g_ref, w_in_ref, aconv_w_ref, alog_ref, dtb_ref, anorm_ref, bconv_w_ref, cos_ref,
               sina_ref, sinb_ref, lng_ref, lnb_ref, ws_ref, dbias_ref, w_out_ref,
               x_ref, aconv0_ref, bconv0_ref, s0_ref, x_out_ref, s_out_ref, aconv_out_ref, bconv_out_ref,
               ck_out_ref, cv_out_ref,
               abuf, bbuf, kbuf, vbuf, q_s, k_s, v_s, gcb_s, bb_s, o_s, s_scr, proj_s, *, tile):
    T = tile
    si = pl.program_id(1)
    P = CARRY_ROWS

    @pl.when(si == 0)
    def _():
        abuf[0:P, :] = aconv0_ref[...]
        bbuf[0:P, :] = bconv0_ref[...]
        kbuf[0:WINDOW, :] = jnp.zeros((WINDOW, 128), F32)
        vbuf[0:WINDOW, :] = jnp.zeros((WINDOW, 128), F32)
        s_scr[...] = s0_ref[...]

    res = {}
    h = _rms(x_ref[...], n1g_ref[...]).astype(BF16)
    for lo, width in ((COL_A, W_A), (COL_B, W_B), (COL_C, W_C), (COL_D, W_D)):
        proj_s[:, lo:lo + width] = _dot(h, w_in_ref[:, lo:lo + width])
    yield

    small = proj_s[:, COL_SMALL:W_A]
    abuf[P:P + T, :] = proj_s[:, 0:A_QKV_W]

    def group_b():
        bbuf[P:P + T, :] = proj_s[:, COL_B + 256:COL_B + 512] * proj_s[:, COL_B + 512:COL_B + 768]
        bx = (bbuf[P:P + T, :] * bconv_w_ref[2:3, :] + bbuf[P - 1:P - 1 + T, :] * bconv_w_ref[1:2, :]
              + bbuf[P - 2:P - 2 + T, :] * bconv_w_ref[0:1, :])
        tail_b = bbuf[T:T + P, :]
        bbuf[0:P, :] = tail_b
        bconv_out_ref[...] = tail_b
        res['out_b'] = proj_s[:, COL_B:COL_B + 256] * bx

    def group_c_rope():
        cos = cos_ref[...]
        sina = sina_ref[...]
        sinb = sinb_ref[...]
        cq = proj_s[:, COL_C:COL_C + 256]
        res['cq'] = (cq * jnp.concatenate([cos, cos], axis=1)
                     + pltpu.roll(cq, 256 - ROPE_DIM // 2, 1) * jnp.concatenate([sina, sina], axis=1)
                     + pltpu.roll(cq, ROPE_DIM // 2, 1) * jnp.concatenate([sinb, sinb], axis=1))
        ck = proj_s[:, COL_C + 256:COL_C + 384]
        ck = (ck * cos + pltpu.roll(ck, 128 - ROPE_DIM // 2, 1) * sina
              + pltpu.roll(ck, ROPE_DIM // 2, 1) * sinb)
        kbuf[WINDOW:WINDOW + T, :] = ck
        vbuf[WINDOW:WINDOW + T, :] = proj_s[:, COL_C + 384:COL_C + 512]
        ck_out_ref[...] = kbuf[T:T + WINDOW, :]
        cv_out_ref[...] = vbuf[T:T + WINDOW, :]

    def group_d_norm():
        res['du'] = _gelu_tanh(proj_s[:, COL_D:COL_D + 256])
        gv = _gelu_tanh(proj_s[:, COL_D + 256:COL_D + 512])
        mu = jnp.mean(gv, axis=-1, keepdims=True)
        xc = gv - mu
        res['dvn'] = (xc * lax.rsqrt(jnp.mean(xc * xc, axis=-1, keepdims=True) + EPS) * lng_ref[...]
                      + lnb_ref[...])
        wr = _iota((D_GROUPS * D_CHUNK, D_CHUNK), 0) & (D_CHUNK - 1)
        wc = _iota((D_GROUPS * D_CHUNK, D_CHUNK), 1)
        res['wm'] = jnp.where(wr >= wc, ws_ref[...], 0.0).astype(BF16)

    y = (abuf[P:P + T, :] * aconv_w_ref[3:4, :] + abuf[P - 1:P - 1 + T, :] * aconv_w_ref[2:3, :]
         + abuf[P - 2:P - 2 + T, :] * aconv_w_ref[1:2, :] + abuf[P - 3:P - 3 + T, :] * aconv_w_ref[0:1, :])
    tail_a = abuf[T:T + P, :]
    abuf[0:P, :] = tail_a
    aconv_out_ref[...] = tail_a
    qkv = _silu(y)
    q_raw = qkv[:, 0:256]
    k_raw = qkv[:, 256:512]
    v_s[...] = qkv[:, 512:768]

    r256 = _iota((256, 256), 0) >> HEAD_SHIFT
    c256 = _iota((256, 256), 1) >> HEAD_SHIFT
    mask_bd = r256 == c256
    ones_bd = jnp.where(mask_bd, 1.0, 0.0).astype(BF16)
    q_s[...] = q_raw * lax.rsqrt(_head_sumsq(q_raw, ones_bd) + EPS) * (HEAD_DIM ** -0.5)
    k_s[...] = k_raw * lax.rsqrt(_head_sumsq(k_raw, ones_bd) + EPS)
    yield

    g_log = -jnp.exp(alog_ref[...]) * _softplus(small + dtb_ref[...])
    beta = _sigmoid(small)
    gbeta = jnp.where(_iota((T, 128), 1) < A_HEADS, g_log, beta)
    expand = jnp.where(_iota((128, 512), 0) == (_iota((128, 512), 1) >> HEAD_SHIFT), 1.0, 0.0).astype(BF16)
    gbb = _dot_lhs3(gbeta, expand)
    bb_s[...] = gbb[:, 256:512]

    def chunk_cumsum(xv):
        row_in_chunk = _iota(xv.shape, 0) & (A_CHUNK - 1)
        step = 1
        while step < A_CHUNK:
            xv = xv + jnp.where(row_in_chunk >= step, pltpu.roll(xv, step, 0), 0.0)
            step *= 2
        return xv

    gcb_s[...] = chunk_cumsum(gbb[:, 0:256])
    gct = chunk_cumsum(g_log).T[0:8, :]
    low_half = (_iota((1, T), 1) & A_CHUNK) == 0
    gct_r = pltpu.roll(gct, A_CHUNK, 1)
    gct_l = pltpu.roll(gct, T - A_CHUNK, 1)
    even_rows = [jnp.where(low_half, gct[a:a + 1, :], gct_r[a + 1:a + 2, :]) for a in (0, 2)]
    odd_rows = [jnp.where(low_half, gct_l[a:a + 1, :], gct[a + 1:a + 2, :]) for a in (0, 2)]
    yield

    ri = _iota((A_CHUNK, 256), 0)
    ci = _iota((A_CHUNK, 256), 1) & (A_CHUNK - 1)
    causal_t = ri >= ci
    strict_t = ri > ci
    eye_t = jnp.where(ri == ci, 1.0, 0.0)

    def bd16(x16):
        return jnp.concatenate([x16, x16, x16, x16], axis=0) * ones_bd

    n_chunks = T // A_CHUNK
    t_inv, pw, qk, qdec, e_tail, vb16, kbe16 = [], [], [], [], [], [], []
    for c in range(n_chunks):
        rows = slice(c * A_CHUNK, (c + 1) * A_CHUNK)
        blk = slice((c // 2) * 128, (c // 2) * 128 + 128)
        src = even_rows if c % 2 == 0 else odd_rows
        gc_row = jnp.concatenate([src[0][:, blk], src[1][:, blk]], axis=1)
        qc = q_s[rows, :]
        kc = k_s[rows, :]
        bbc = bb_s[rows, :]
        gcb = gcb_s[rows, :]
        decay = jnp.where(causal_t, jnp.exp(jnp.where(causal_t, gcb - gc_row, 0.0)), 0.0)
        eg = jnp.exp(gcb)
        kb = kc * bbc
        aq = _dot_nt(jnp.concatenate([kb, qc], axis=0).astype(BF16), bd16(kc.astype(BF16)))
        a_mat = jnp.where(strict_t, aq[0:A_CHUNK] * decay, 0.0)
        qk.append(aq[A_CHUNK:2 * A_CHUNK] * decay)
        t_inv.append(eye_t - a_mat)
        pw.append(a_mat)
        qdec.append(qc * eg)
        e_tail.append(jnp.exp(gcb[A_CHUNK - 1:A_CHUNK, :] - gcb))
        vb16.append((v_s[rows, :] * bbc).astype(BF16))
        kbe16.append((kb * eg).astype(BF16))
    yield
    u, w = [], []
    state = {'s': s_scr[...]}

    def level_first():
        for c in range(n_chunks):
            p16 = pw[c].astype(BF16)
            pw[c] = _dot(p16, bd16(p16))

    def level_mid():
        for c in range(n_chunks):
            p16 = pw[c].astype(BF16)
            res = _dot(jnp.concatenate([p16, t_inv[c].astype(BF16)], axis=0), bd16(p16))
            pw[c] = res[0:A_CHUNK]
            t_inv[c] = t_inv[c] + res[A_CHUNK:2 * A_CHUNK]

    def level_last():
        for c in range(n_chunks):
            t_c = t_inv[c] + _dot(t_inv[c].astype(BF16), bd16(pw[c].astype(BF16)))
            uw = _dot(t_c.astype(BF16), jnp.concatenate([bd16(vb16[c]), bd16(kbe16[c])], axis=1))
            u.append(uw[:, 0:256])
            w.append(uw[:, 256:512])

    def scan_step(c):
        rows = slice(c * A_CHUNK, (c + 1) * A_CHUNK)
        s_bd = state['s']
        wq = _dot(jnp.concatenate([w[c], qdec[c]], axis=0).astype(BF16), s_bd.astype(BF16))
        v_new = u[c] - wq[0:A_CHUNK]
        o_s[rows, :] = wq[A_CHUNK:2 * A_CHUNK] + _dot(qk[c].astype(BF16), bd16(v_new.astype(BF16)))
        kv = _dot_tn(k_s[rows, :].astype(BF16), (v_new * e_tail[c]).astype(BF16))
        g_last = gcb_s[(c + 1) * A_CHUNK - 1:(c + 1) * A_CHUNK, :]
        state['s'] = s_bd * jnp.exp(g_last) + jnp.where(mask_bd, kv, 0.0)

    chain = [level_first] + [level_mid] * 4 + [level_last] + [functools.partial(scan_step, c) for c in range(n_chunks)]

    lane128 = _iota((2 * WINDOW, 128), 1)
    low = _iota((WINDOW, 128), 1) < HEAD_DIM
    qrow = _iota((2 * WINDOW, 2 * WINDOW), 0) & (WINDOW - 1)
    kcol = _iota((2 * WINDOW, 2 * WINDOW), 1)
    band = (kcol > qrow) & (kcol <= qrow + WINDOW)
    top_half = _iota((2 * WINDOW, 1), 0) < WINDOW
    out_c_blocks = []

    def group_c_block(n):
        first_key = si * T + (n - 1) * WINDOW
        valid = band & (kcol + first_key >= 0)
        kwin = kbuf[n * WINDOW:(n + 2) * WINDOW, :]
        vwin = vbuf[n * WINDOW:(n + 2) * WINDOW, :]
        k_sw = pltpu.roll(kwin, HEAD_DIM, 1)
        v_sw = pltpu.roll(vwin, HEAD_DIM, 1)
        pair_out = []
        for g in range(C_KV_HEADS):
            own = (lane128 < HEAD_DIM) if g == 0 else (lane128 >= HEAD_DIM)
            k_dup = jnp.where(own, kwin, k_sw).astype(BF16)
            v_dup = jnp.where(own, vwin, v_sw).astype(BF16)
            qp = res['cq'][n * WINDOW:(n + 1) * WINDOW, g * 128:(g + 1) * 128]
            q_st = jnp.concatenate([jnp.where(low, qp, 0.0), jnp.where(low, 0.0, qp)], axis=0).astype(BF16)
            s = _dot_nt(q_st, k_dup) * (HEAD_DIM ** -0.5)
            s = jnp.where(valid, s, NEG_INF)
            sink = jnp.where(top_half, sinks_ref[2 * g], sinks_ref[2 * g + 1])
            m = jnp.maximum(jnp.max(s, axis=-1, keepdims=True), sink)
            p = jnp.exp(s - m)
            denom = jnp.sum(p, axis=-1, keepdims=True) + jnp.exp(sink - m)
            o2 = _dot((p / denom).astype(BF16), v_dup)
            pair_out.append(jnp.where(low, o2[0:WINDOW], o2[WINDOW:2 * WINDOW]))
        out_c_blocks.append(jnp.concatenate(pair_out, axis=1))
        if n == T // WINDOW - 1:
            kbuf[0:WINDOW, :] = kbuf[T:T + WINDOW, :]
            vbuf[0:WINDOW, :] = vbuf[T:T + WINDOW, :]

    lane_grp = _iota((D_CHUNK, 256), 1) >> HEAD_SHIFT
    out_d_blocks = []

    def group_d_block(n):
        mx = _dot(res['wm'], res['dvn'][n * D_CHUNK:(n + 1) * D_CHUNK, :].astype(BF16))
        mixed = dbias_ref[...]
        for grp in range(D_GROUPS):
            mixed = mixed + jnp.where(lane_grp == grp, mx[grp * D_CHUNK:(grp + 1) * D_CHUNK, :], 0.0)
        out_d_blocks.append(res['du'][n * D_CHUNK:(n + 1) * D_CHUNK, :] * mixed)

    c_blocks = [functools.partial(group_c_block, n) for n in range(T // WINDOW)]
    d_blocks = [functools.partial(group_d_block, n) for n in range(T // D_CHUNK)]
    fill = [group_b, group_c_rope, group_d_norm] + c_blocks + d_blocks
    for i in range(max(len(chain), len(fill))):
        if i < len(chain):
            chain[i]()
        if i < len(fill):
            fill[i]()
        yield
    out_b = res['out_b']
    out_c = jnp.concatenate(out_c_blocks, axis=0)
    out_d = jnp.concatenate(out_d_blocks, axis=0)

    s_scr[...] = state['s']
    s_out_ref[...] = state['s']
    o = o_s[...]
    out_a = (o * lax.rsqrt(_head_sumsq(o, ones_bd) * (1.0 / HEAD_DIM) + EPS) * anorm_ref[...]
             * _silu(proj_s[:, COL_Z:COL_SMALL]))

    cat = jnp.concatenate([out_a, out_b, out_c, out_d], axis=1).astype(BF16)
    x_out_ref[...] = x_ref[...] + _dot_w32(cat, w_out_ref[0])


def _mixer_call(x, lw, tabs, aconv0, bconv0, s0, layer):
    bsz, seq, _ = x.shape
    T = MIXER_TILE_ROWS
    ns = seq // T
    NB = MIXER_SEQS
    P = CARRY_ROWS
    full = lambda shape: pl.BlockSpec(shape, lambda b, s, *_: (0,) * len(shape), pipeline_mode=pl.Buffered(1))
    per_b = lambda shape: pl.BlockSpec((NB,) + shape, lambda b, s, *_: (b,) + (0,) * len(shape))
    tab = pl.BlockSpec((T, 128), lambda b, s, *_: (s, 0))
    seq_scratch = [pltpu.VMEM(shape, F32) for shape in (
        (T + P, A_QKV_W), (T + P, 256), (T + WINDOW, 128), (T + WINDOW, 128), (T, 256), (T, 256), (T, 256),
        (T, 256), (T, 256), (T, 256), (256, 256), (T, P_PACKED))]
    grid_spec = pltpu.PrefetchScalarGridSpec(
        num_scalar_prefetch=1,
        grid=(bsz // NB, ns),
        in_specs=[
            pl.BlockSpec((NB, T, D_MODEL), lambda b, s, *_: (b, s, 0)),
            full((1, D_MODEL)), full((D_MODEL, P_PACKED)), full((A_CONV, A_QKV_W)), full((1, 128)), full((1, 128)),
            full((1, 256)), full((B_CONV, 256)), tab, tab, tab, full((1, 256)), full((1, 256)),
            full((D_GROUPS * D_CHUNK, D_CHUNK)), full((D_CHUNK, 256)),
            pl.BlockSpec((1, D_MODEL, D_MODEL), lambda b, s, *_: (layer, 0, 0), pipeline_mode=pl.Buffered(1)),
            per_b((P, A_QKV_W)), per_b((P, 256)), per_b((256, 256)),
        ],
        out_specs=[
            pl.BlockSpec((NB, T, D_MODEL), lambda b, s, *_: (b, s, 0)),
            per_b((256, 256)), per_b((P, A_QKV_W)), per_b((P, 256)), per_b((WINDOW, 128)), per_b((WINDOW, 128)),
        ],
        scratch_shapes=seq_scratch * NB,
    )
    out_shape = [
        jax.ShapeDtypeStruct((bsz, seq, D_MODEL), F32),
        jax.ShapeDtypeStruct((bsz, 256, 256), F32),
        jax.ShapeDtypeStruct((bsz, P, A_QKV_W), F32),
        jax.ShapeDtypeStruct((bsz, P, 256), F32),
        jax.ShapeDtypeStruct((bsz, WINDOW, 128), F32),
        jax.ShapeDtypeStruct((bsz, WINDOW, 128), F32),
    ]
    return pl.pallas_call(
        functools.partial(_mixer_kernel, tile=T),
        grid_spec=grid_spec,
        out_shape=out_shape,
        compiler_params=pltpu.CompilerParams(
            dimension_semantics=("arbitrary", "arbitrary"), vmem_limit_bytes=MIXER_VMEM_LIMIT_BYTES),
        name="mixer",
    )(lw['c_sinks'], x, lw['norm1_g'], lw['w_in'], lw['a_conv_w'], lw['a_log'], lw['a_dt_bias'],
      lw['a_norm_g'], lw['b_conv_w'], tabs[0], tabs[1], tabs[2], lw['d_ln_g'], lw['d_ln_b'],
      lw['d_ws'], lw['d_bias'], lw['w_out'], aconv0, bconv0, s0)


def _ffn_kernel(x_ref, n2g_ref, wg_ref, wu_ref, cw_ref, wd_ref, fconv0_ref, fng_ref,
                x_out_ref, fconv_out_ref, gbuf, *, tile, stride, final_norm):
    T = tile
    P = (FFN_CONV - 1) * stride if stride > 1 else CARRY_ROWS
    si = pl.program_id(1)

    @pl.when(si == 0)
    def _():
        gbuf[0:P, :] = fconv0_ref[0]

    x = x_ref[0]
    h = _rms(x, n2g_ref[...]).astype(BF16)
    acc = x
    for c0 in range(0, D_FF, FFN_COL_CHUNK):
        cols = slice(c0, c0 + FFN_COL_CHUNK)
        gbuf[P:P + T, cols] = _dot_w32(h, wg_ref[0, :, cols])
        up = _dot_w32(h, wu_ref[0, :, cols])
        gate = (gbuf[P:P + T, cols] * cw_ref[2:3, cols] + gbuf[P - stride:P - stride + T, cols] * cw_ref[1:2, cols]
                + gbuf[P - 2 * stride:P - 2 * stride + T, cols] * cw_ref[0:1, cols])
        acc = acc + _dot_w32((_silu(gate) * up).astype(BF16), wd_ref[0, cols, :])
    tail = gbuf[T:T + P, :]
    gbuf[0:P, :] = tail
    fconv_out_ref[0] = tail
    if final_norm:
        acc = _rms(acc, fng_ref[...])
    x_out_ref[0] = acc


FFN_STREAM_COLS = 256


def _ffn_stream_kernel(x_ref, n2g_ref, wg_ref, wu_ref, cw_ref, wd_ref, fconv0_ref, fng_ref,
                       x_out_ref, fconv_out_ref, gbuf, *, tile, stride, final_norm, n_steps):
    T = tile
    P = (FFN_CONV - 1) * stride
    c = pl.program_id(0)
    x = x_ref[0]
    h = _rms(x, n2g_ref[...]).astype(BF16)
    gbuf[0:P, :] = fconv0_ref[0]
    gbuf[P:P + T, :] = _dot_w32(h, wg_ref[0])
    up = _dot_w32(h, wu_ref[0])
    gate = (gbuf[P:P + T, :] * cw_ref[2:3, :] + gbuf[P - stride:P - stride + T, :] * cw_ref[1:2, :]
            + gbuf[P - 2 * stride:P - 2 * stride + T, :] * cw_ref[0:1, :])
    part = _dot_w32((_silu(gate) * up).astype(BF16), wd_ref[0])
    fconv_out_ref[0] = gbuf[T:T + P, :]

    @pl.when(c == 0)
    def _():
        x_out_ref[0] = x + part

    @pl.when(c > 0)
    def _():
        x_out_ref[0] = x_out_ref[0] + part

    if final_norm:
        @pl.when(c == n_steps - 1)
        def _():
            x_out_ref[0] = _rms(x_out_ref[0], fng_ref[...])


def _ffn_stream_call(x, lw, fconv0, final_g, *, layer, stride, final_norm):
    _, T, _ = x.shape
    P = (FFN_CONV - 1) * stride
    C = FFN_STREAM_COLS
    n_steps = D_FF // C
    const = lambda shape: pl.BlockSpec(shape, lambda c: (0,) * len(shape))
    return pl.pallas_call(
        functools.partial(_ffn_stream_kernel, tile=T, stride=stride, final_norm=final_norm, n_steps=n_steps),
        grid=(n_steps,),
        in_specs=[
            const((1, T, D_MODEL)), const((1, D_MODEL)),
            pl.BlockSpec((1, D_MODEL, C), lambda c: (layer, 0, c)), pl.BlockSpec((1, D_MODEL, C), lambda c: (layer, 0, c)),
            pl.BlockSpec((FFN_CONV, C), lambda c: (0, c)), pl.BlockSpec((1, C, D_MODEL), lambda c: (layer, c, 0)),
            pl.BlockSpec((1, P, C), lambda c: (0, 0, c)), const((1, D_MODEL)),
        ],
        out_specs=[const((1, T, D_MODEL)), pl.BlockSpec((1, P, C), lambda c: (0, 0, c))],
        out_shape=[jax.ShapeDtypeStruct((1, T, D_MODEL), F32), jax.ShapeDtypeStruct((1, P, D_FF), F32)],
        scratch_shapes=[pltpu.VMEM((T + P, C), F32)],
        compiler_params=pltpu.CompilerParams(dimension_semantics=("arbitrary",), vmem_limit_bytes=VMEM_LIMIT_BYTES),
        name="ffn_stream",
    )(x, lw['norm2_g'], lw['ffn_w_gate'], lw['ffn_w_up'], lw['ffn_conv_w'], lw['ffn_w_down'], fconv0, final_g)


def _ffn_call(x, lw, fconv0, final_g, *, layer, tile, stride, final_norm):
    bsz, seq, _ = x.shape
    T = tile
    P = (FFN_CONV - 1) * stride if stride > 1 else CARRY_ROWS
    full = lambda shape: pl.BlockSpec(shape, lambda b, s: (0,) * len(shape), pipeline_mode=pl.Buffered(1))
    per_b = lambda shape: pl.BlockSpec((1,) + shape, lambda b, s: (b,) + (0,) * len(shape))
    weight = lambda shape: pl.BlockSpec((1,) + shape, lambda b, s: (layer, 0, 0), pipeline_mode=pl.Buffered(1))
    return pl.pallas_call(
        functools.partial(_ffn_kernel, tile=T, stride=stride, final_norm=final_norm),
        grid=(bsz, seq // T),
        in_specs=[
            pl.BlockSpec((1, T, D_MODEL), lambda b, s: (b, s, 0)),
            full((1, D_MODEL)), weight((D_MODEL, D_FF)), weight((D_MODEL, D_FF)), full((FFN_CONV, D_FF)),
            weight((D_FF, D_MODEL)), per_b((P, D_FF)), full((1, D_MODEL)),
        ],
        out_specs=[pl.BlockSpec((1, T, D_MODEL), lambda b, s: (b, s, 0)), per_b((P, D_FF))],
        out_shape=[jax.ShapeDtypeStruct((bsz, seq, D_MODEL), F32), jax.ShapeDtypeStruct((bsz, P, D_FF), F32)],
        scratch_shapes=[pltpu.VMEM((T + P, D_FF), F32)],
        compiler_params=pltpu.CompilerParams(
            dimension_semantics=("arbitrary", "arbitrary"), vmem_limit_bytes=VMEM_LIMIT_BYTES),
        name="ffn",
    )(x, lw['norm2_g'], lw['ffn_w_gate'], lw['ffn_w_up'], lw['ffn_conv_w'], lw['ffn_w_down'], fconv0, final_g)


def _pack_w_in(w):
    pad = jnp.zeros((D_MODEL, 128 - 2 * A_HEADS), w.dtype)
    small = jnp.concatenate([w[:, 1028:1032], w[:, 1024:1028], pad], axis=1)
    return jnp.concatenate([w[:, 0:1024], small, w[:, 1032:]], axis=1).astype(BF16)


def _pad_lanes(v, width=128):
    return jnp.concatenate([v, jnp.zeros((width - v.shape[0],), v.dtype)])[None, :]


def _layer_weights(l, norm1_g, w_in, a_conv_w, a_log, a_dt_bias, a_norm_g, b_conv_w, c_sinks, d_ln_g, d_ln_b,
                   d_ws, d_bias, w_out, norm2_g, ffn_w_gate, ffn_w_up, ffn_conv_w, ffn_w_down):
    bias_tab = jnp.broadcast_to(d_bias[l].T[:, :, None], (D_CHUNK, D_GROUPS, HEAD_DIM)).reshape(D_CHUNK, 256)
    return {
        'norm1_g': norm1_g[l][None, :], 'w_in': _pack_w_in(w_in[l]), 'a_conv_w': a_conv_w[l],
        'a_log': _pad_lanes(a_log[l]), 'a_dt_bias': _pad_lanes(a_dt_bias[l]),
        'a_norm_g': jnp.tile(a_norm_g[l], A_HEADS)[None, :], 'b_conv_w': b_conv_w[l], 'c_sinks': c_sinks[l],
        'd_ln_g': d_ln_g[l][None, :], 'd_ln_b': d_ln_b[l][None, :],
        'd_ws': d_ws[l].reshape(D_GROUPS * D_CHUNK, D_CHUNK), 'd_bias': bias_tab,
        'w_out': w_out, 'norm2_g': norm2_g[l][None, :],
        'ffn_w_gate': ffn_w_gate, 'ffn_w_up': ffn_w_up, 'ffn_conv_w': ffn_conv_w[l], 'ffn_w_down': ffn_w_down,
    }


def _rope_tables(pos):
    half = ROPE_DIM // 2
    inv = np.power(ROPE_THETA, -np.arange(half, dtype=np.float64) * (2.0 / ROPE_DIM))
    ang = pos.astype(np.float64)[:, None] * inv[None, :]
    cos, sin = jnp.asarray(np.cos(ang), F32), jnp.asarray(np.sin(ang), F32)
    n = pos.shape[0]
    rest = HEAD_DIM - ROPE_DIM
    cos_h = jnp.concatenate([cos, cos, jnp.ones((n, rest), F32)], axis=1)
    sina_h = jnp.concatenate([-sin, jnp.zeros((n, half + rest), F32)], axis=1)
    sinb_h = jnp.concatenate([jnp.zeros((n, half), F32), sin, jnp.zeros((n, rest), F32)], axis=1)
    return tuple(jnp.concatenate([t, t], axis=1) for t in (cos_h, sina_h, sinb_h))


def _diag_blocks(s_bd):
    return jnp.stack([s_bd[:, HEAD_DIM * i:HEAD_DIM * (i + 1), HEAD_DIM * i:HEAD_DIM * (i + 1)]
                      for i in range(A_HEADS)], axis=1)


def _sample_pre_kernel(x_ref, n1g_ref, w_in_ref, aconv_w_ref, alog_ref, dtb_ref, bconv_w_ref, cos_ref, sina_ref,
                       sinb_ref, lng_ref, lnb_ref, wtab_ref, btab_ref, aconv0_ref, bconv0_ref,
                       q_ref, k_ref, v_ref, g_ref, beta_ref, z_ref, outb_ref, cq_ref, ck_ref, cv_ref, outd_ref,
                       dvn_ref, aconv_out_ref, bconv_out_ref, abuf, bbuf, *, rows, stride):
    R, S = rows, stride
    steps = R // S
    pa_rows = (A_CONV - 1) * S
    pb_rows = (B_CONV - 1) * S
    x = x_ref[...]
    h = _rms(x, n1g_ref[...]).astype(BF16)

    pa = _dot(h, w_in_ref[:, COL_A:COL_A + W_A])
    z_ref[...] = pa[:, COL_Z:COL_SMALL]
    small = pa[:, COL_SMALL:W_A]
    abuf[0:pa_rows, :] = aconv0_ref[...]
    abuf[pa_rows:pa_rows + R, :] = pa[:, 0:A_QKV_W]
    y = abuf[0:R, :] * aconv_w_ref[0:1, :]
    for j in range(1, A_CONV):
        y = y + abuf[j * S:j * S + R, :] * aconv_w_ref[j:j + 1, :]
    aconv_out_ref[...] = abuf[R:R + pa_rows, :]
    qkv = _silu(y)
    q_raw = qkv[:, 0:256]
    k_raw = qkv[:, 256:512]
    v_ref[...] = qkv[:, 512:768]
    mask_bd = (_iota((256, 256), 0) >> HEAD_SHIFT) == (_iota((256, 256), 1) >> HEAD_SHIFT)
    ones_bd = jnp.where(mask_bd, 1.0, 0.0).astype(BF16)
    q_ref[...] = q_raw * lax.rsqrt(_dot_lhs3(q_raw * q_raw, ones_bd) + EPS) * (HEAD_DIM ** -0.5)
    k_ref[...] = k_raw * lax.rsqrt(_dot_lhs3(k_raw * k_raw, ones_bd) + EPS)
    g_ref[...] = -jnp.exp(alog_ref[...]) * _softplus(small + dtb_ref[...])
    beta_ref[...] = _sigmoid(small)

    pb = _dot(h, w_in_ref[:, COL_B:COL_B + W_B])
    bbuf[0:pb_rows, :] = bconv0_ref[...]
    bbuf[pb_rows:pb_rows + R, :] = pb[:, 256:512] * pb[:, 512:768]
    bx = bbuf[0:R, :] * bconv_w_ref[0:1, :]
    for j in range(1, B_CONV):
        bx = bx + bbuf[j * S:j * S + R, :] * bconv_w_ref[j:j + 1, :]
    bconv_out_ref[...] = bbuf[R:R + pb_rows, :]
    outb_ref[...] = pb[:, 0:256] * bx

    pc = _dot(h, w_in_ref[:, COL_C:COL_C + W_C])
    cos = cos_ref[...]
    sina = sina_ref[...]
    sinb = sinb_ref[...]
    cq = pc[:, 0:256]
    cq_ref[...] = (cq * jnp.concatenate([cos, cos], axis=1)
                   + pltpu.roll(cq, 256 - ROPE_DIM // 2, 1) * jnp.concatenate([sina, sina], axis=1)
                   + pltpu.roll(cq, ROPE_DIM // 2, 1) * jnp.concatenate([sinb, sinb], axis=1))
    ck = pc[:, 256:384]
    ck_ref[...] = (ck * cos + pltpu.roll(ck, 128 - ROPE_DIM // 2, 1) * sina
                   + pltpu.roll(ck, ROPE_DIM // 2, 1) * sinb)
    cv_ref[...] = pc[:, 384:512]

    pd = _dot(h, w_in_ref[:, COL_D:COL_D + W_D])
    du = _gelu_tanh(pd[:, 0:256])
    gv = _gelu_tanh(pd[:, 256:512])
    mu = jnp.mean(gv, axis=-1, keepdims=True)
    xc = gv - mu
    dvn = xc * lax.rsqrt(jnp.mean(xc * xc, axis=-1, keepdims=True) + EPS) * lng_ref[...] + lnb_ref[...]
    dvn_ref[...] = dvn
    for t in range(steps):
        mixed = btab_ref[t:t + 1, :]
        for s in range(t + 1):
            mixed = mixed + wtab_ref[t * steps + s:t * steps + s + 1, :] * dvn[s * S:(s + 1) * S, :]
        outd_ref[t * S:(t + 1) * S, :] = du[t * S:(t + 1) * S, :] * mixed


def _sample_pre_call(x, lw, tabs, wtab, btab, aconv0, bconv0, stride):
    R = x.shape[0]
    f = lambda *shape: jax.ShapeDtypeStruct(shape, F32)
    out_shape = [f(R, 256), f(R, 256), f(R, 256), f(R, 128), f(R, 128), f(R, 256), f(R, 256), f(R, 256),
                 f(R, 128), f(R, 128), f(R, 256), f(R, 256), f((A_CONV - 1) * stride, A_QKV_W),
                 f((B_CONV - 1) * stride, 256)]
    return pl.pallas_call(
        functools.partial(_sample_pre_kernel, rows=R, stride=stride),
        out_shape=out_shape,
        scratch_shapes=[pltpu.VMEM((R + (A_CONV - 1) * stride, A_QKV_W), F32),
                        pltpu.VMEM((R + (B_CONV - 1) * stride, 256), F32)],
        compiler_params=pltpu.CompilerParams(vmem_limit_bytes=VMEM_LIMIT_BYTES),
        name="sample_pre",
    )(x, lw['norm1_g'], lw['w_in'], lw['a_conv_w'], lw['a_log'], lw['a_dt_bias'], lw['b_conv_w'],
      tabs[0], tabs[1], tabs[2], lw['d_ln_g'], lw['d_ln_b'], wtab, btab, aconv0, bconv0)


DELTA_UNROLL = 4


def _sample_delta_kernel(g_ref, beta_ref, q_ref, k_ref, v_ref, s_ref, o_ref, snew_ref,
                         qt_s, kt_s, vt_s, gt_s, bt_s, ot_s, *, steps, nb):
    h = pl.program_id(0)
    n_i = HEAD_DIM
    zeros = jnp.zeros((HEAD_DIM, nb), F32)

    @pl.when(h == 0)
    def _():
        for t in range(steps):
            rows = slice(t * nb, (t + 1) * nb)
            qt_s[t] = q_ref[rows, :].T
            kt_s[t] = k_ref[rows, :].T
            vt_s[t] = v_ref[rows, :].T
            gt_s[t] = g_ref[rows, :].T
            bt_s[t] = beta_ref[rows, :].T

    base = pl.multiple_of(h * HEAD_DIM, HEAD_DIM)
    head_rows = pl.ds(base, HEAD_DIM)

    def rows_of(i):
        return pl.ds(pl.multiple_of(i * HEAD_DIM, HEAD_DIM), HEAD_DIM)

    def decay(t):
        return jnp.exp(gt_s[t, pl.ds(h, 1), :])

    dec0 = decay(0)

    def first_pass(i, acc):
        return acc + kt_s[0, pl.ds(base + i, 1), :] * (s_ref[0, rows_of(i), :] * dec0)

    ks = lax.fori_loop(0, n_i, first_pass, zeros, unroll=DELTA_UNROLL)
    for t in range(steps):
        dec = decay(t)
        v_new = bt_s[t, pl.ds(A_HEADS + h, 1), :] * (vt_s[t, head_rows, :] - ks)
        src = s_ref if t == 0 else snew_ref
        dec_next = decay(t + 1) if t + 1 < steps else None

        def update(i, carry, t=t, dec=dec, v_new=v_new, src=src, dec_next=dec_next):
            o_acc, ks_acc = carry
            blk = src[0, rows_of(i), :] * dec + kt_s[t, pl.ds(base + i, 1), :] * v_new
            snew_ref[0, rows_of(i), :] = blk
            o_acc = o_acc + qt_s[t, pl.ds(base + i, 1), :] * blk
            if dec_next is not None:
                ks_acc = ks_acc + kt_s[t + 1, pl.ds(base + i, 1), :] * (blk * dec_next)
            return o_acc, ks_acc

        o_acc, ks = lax.fori_loop(0, n_i, update, (zeros, zeros), unroll=DELTA_UNROLL)
        ot_s[t, head_rows, :] = o_acc

    @pl.when(h == A_HEADS - 1)
    def _():
        for t in range(steps):
            o_ref[t * nb:(t + 1) * nb, :] = ot_s[t].T


def _sample_delta_call(g, beta, q, k, v, s, steps):
    nh, _, nb = s.shape
    rows = steps * nb
    whole = lambda width: pl.BlockSpec((rows, width), lambda h: (0, 0))
    st = pl.BlockSpec((1, HEAD_DIM * HEAD_DIM, nb), lambda h: (h, 0, 0))
    wide = pltpu.VMEM((steps, GROUP_WIDTH, nb), F32)
    narrow = pltpu.VMEM((steps, 128, nb), F32)
    return pl.pallas_call(
        functools.partial(_sample_delta_kernel, steps=steps, nb=nb),
        grid=(nh,),
        in_specs=[whole(128), whole(128), whole(GROUP_WIDTH), whole(GROUP_WIDTH), whole(GROUP_WIDTH), st],
        out_specs=[whole(GROUP_WIDTH), st],
        out_shape=[jax.ShapeDtypeStruct((rows, GROUP_WIDTH), F32),
                   jax.ShapeDtypeStruct((nh, HEAD_DIM * HEAD_DIM, nb), F32)],
        scratch_shapes=[wide, wide, wide, narrow, narrow, wide],
        compiler_params=pltpu.CompilerParams(dimension_semantics=("arbitrary",),
                                             vmem_limit_bytes=VMEM_LIMIT_BYTES),
        name="sample_delta",
    )(g, beta, q, k, v, s)


SAMPLE_ATTN_BLOCK = 16
NEW_KEY_ROWS = 8


def _sample_attn_kernel(sinks_ref, qm_ref, kc_ref, kn_ref, vc_ref, vn_ref, o_ref, k_out_ref, v_out_ref, *, steps):
    nq = C_HEADS * steps
    nk = WINDOW + NEW_KEY_ROWS
    BB = SAMPLE_ATTN_BLOCK
    row = _iota((BB * nq, nk), 0)
    col = _iota((BB * nq, nk), 1)
    t_q = row & (steps - 1)
    valid = ((col < WINDOW) & (col > t_q)) | ((col >= WINDOW) & (col - WINDOW <= t_q))
    head = (_iota((BB * nq, 1), 0) >> 2) & (C_HEADS - 1)
    sink = jnp.where(head == 0, sinks_ref[0],
                     jnp.where(head == 1, sinks_ref[1], jnp.where(head == 2, sinks_ref[2], sinks_ref[3])))
    scores = [_dot_nt(qm_ref[b].astype(BF16), jnp.concatenate([kc_ref[0, b], kn_ref[b]], axis=0).astype(BF16))
              for b in range(BB)]
    s = jnp.where(valid, jnp.concatenate(scores, axis=0) * (HEAD_DIM ** -0.5), NEG_INF)
    m = jnp.maximum(jnp.max(s, axis=-1, keepdims=True), sink)
    p = jnp.exp(s - m)
    denom = jnp.sum(p, axis=-1, keepdims=True) + jnp.exp(sink - m)
    p16 = (p / denom).astype(BF16)
    is_new_row = _iota((WINDOW, 128), 0) >= WINDOW - steps
    pad_rows = jnp.zeros((WINDOW - NEW_KEY_ROWS, 128), F32)

    def slide(cache, new_rows):
        kept = pltpu.roll(cache, WINDOW - steps, 0)
        tail = jnp.concatenate([pad_rows, pltpu.roll(new_rows, NEW_KEY_ROWS - steps, 0)], axis=0)
        return jnp.where(is_new_row, tail, kept)

    for b in range(BB):
        v_all = jnp.concatenate([vc_ref[0, b], vn_ref[b]], axis=0).astype(BF16)
        o_ref[b] = _dot(p16[b * nq:(b + 1) * nq, :], v_all)
        k_out_ref[b] = slide(kc_ref[0, b], kn_ref[b])
        v_out_ref[b] = slide(vc_ref[0, b], vn_ref[b])


def _sample_attn_call(sinks, qm, kc_all, kn, vc_all, vn, layer, steps):
    bs, nq, _ = qm.shape
    BB = SAMPLE_ATTN_BLOCK
    blk = lambda r: pl.BlockSpec((BB, r, 128), lambda i, *_: (i, 0, 0))
    cache = pl.BlockSpec((1, BB, WINDOW, 128), lambda i, *_: (layer, i, 0, 0))
    grid_spec = pltpu.PrefetchScalarGridSpec(
        num_scalar_prefetch=1, grid=(bs // BB,),
        in_specs=[blk(nq), cache, blk(NEW_KEY_ROWS), cache, blk(NEW_KEY_ROWS)],
        out_specs=[blk(nq), blk(WINDOW), blk(WINDOW)])
    return pl.pallas_call(
        functools.partial(_sample_attn_kernel, steps=steps),
        grid_spec=grid_spec,
        out_shape=[jax.ShapeDtypeStruct((bs, nq, 128), F32), jax.ShapeDtypeStruct((bs, WINDOW, 128), F32),
                   jax.ShapeDtypeStruct((bs, WINDOW, 128), F32)],
        compiler_params=pltpu.CompilerParams(dimension_semantics=("arbitrary",)),
        name="sample_attn",
    )(sinks, qm, kc_all, kn, vc_all, vn)


def _sample_post_kernel(x_ref, o_ref, z_ref, outb_ref, outc_ref, outd_ref, anorm_ref, w_out_ref, x_out_ref, *,
                        layer):
    mask_bd = (_iota((256, 256), 0) >> HEAD_SHIFT) == (_iota((256, 256), 1) >> HEAD_SHIFT)
    ones_bd = jnp.where(mask_bd, 1.0, 0.0).astype(BF16)
    o = o_ref[...]
    out_a = (o * lax.rsqrt(_dot_lhs3(o * o, ones_bd) * (1.0 / HEAD_DIM) + EPS) * anorm_ref[...]
             * _silu(z_ref[...]))
    cat = jnp.concatenate([out_a, outb_ref[...], outc_ref[...], outd_ref[...]], axis=1).astype(BF16)
    x_out_ref[...] = x_ref[...] + _dot_w32(cat, w_out_ref[layer])


def _sample_post_call(x, o, z, out_b, out_c, out_d, lw, layer):
    return pl.pallas_call(
        functools.partial(_sample_post_kernel, layer=layer),
        out_shape=jax.ShapeDtypeStruct(x.shape, F32),
        compiler_params=pltpu.CompilerParams(vmem_limit_bytes=VMEM_LIMIT_BYTES),
        name="sample_post",
    )(x, o, z, out_b, out_c, out_d, lw['a_norm_g'], lw['w_out'])


def _sample_mixer(x_tm, lw, tabs, d_ws_l, d_bias_l, a_state, a_conv, b_conv, c_k_all, c_v_all, layer, bs, ts):
    to_tm = lambda a: jnp.swapaxes(a, 0, 1).reshape(a.shape[1] * bs, a.shape[2])
    from_tm = lambda a, n: jnp.swapaxes(a.reshape(n, bs, a.shape[-1]), 0, 1)
    wtab = jnp.repeat(d_ws_l[:, :ts, :ts].transpose(1, 2, 0).reshape(ts * ts, D_GROUPS), HEAD_DIM, axis=1)
    btab = jnp.repeat(d_bias_l[:, :ts].T, HEAD_DIM, axis=1)
    (q, k, v, g, beta, z, out_b, cq, ck, cv, out_d, dvn, a_tail, b_tail) = _sample_pre_call(
        x_tm, lw, tabs, wtab, btab, to_tm(a_conv), to_tm(b_conv), bs)

    s_t = a_state.reshape(bs, A_HEADS, HEAD_DIM * HEAD_DIM).transpose(1, 2, 0)
    o, s_new_t = _sample_delta_call(g, beta, q, k, v, s_t, ts)
    a_state_new = s_new_t.transpose(2, 0, 1).reshape(bs, A_HEADS, HEAD_DIM, HEAD_DIM)

    cq4 = cq.reshape(ts, bs, C_HEADS, HEAD_DIM).transpose(1, 2, 0, 3)
    zq = jnp.zeros_like(cq4[:, 0])
    qm = jnp.concatenate(
        [jnp.concatenate([cq4[:, hh], zq] if hh // 2 == 0 else [zq, cq4[:, hh]], axis=-1) for hh in range(C_HEADS)],
        axis=1)
    pad_new = lambda a: jnp.concatenate([from_tm(a, ts), jnp.zeros((bs, NEW_KEY_ROWS - ts, 128), F32)], axis=1)
    o_att, c_k_new, c_v_new = _sample_attn_call(lw['c_sinks'], qm, c_k_all, pad_new(ck), c_v_all, pad_new(cv),
                                                layer, ts)
    out_c = jnp.concatenate(
        [o_att[:, hh * ts:(hh + 1) * ts, (hh // 2) * HEAD_DIM:(hh // 2 + 1) * HEAD_DIM] for hh in range(C_HEADS)],
        axis=-1)
    out_c = jnp.swapaxes(out_c, 0, 1).reshape(ts * bs, GROUP_WIDTH)
    c_k_new = c_k_new.reshape(bs, WINDOW, C_KV_HEADS, HEAD_DIM)
    c_v_new = c_v_new.reshape(bs, WINDOW, C_KV_HEADS, HEAD_DIM)

    x2 = _sample_post_call(x_tm, o, z, out_b, out_c, out_d, lw, layer)
    new = {'a_state': a_state_new, 'a_conv': from_tm(a_tail, A_CONV - 1), 'b_conv': from_tm(b_tail, B_CONV - 1),
           'c_k': c_k_new, 'c_v': c_v_new, 'd_v': from_tm(dvn, ts)}
    return x2, new


def kernel(x_prompt, x_sample, state_delta, state_delta_conv, state_shortconv, cache_win_k, cache_win_v,
           state_ffn_conv, norm1_g, w_in, a_conv_w, a_log, a_dt_bias, a_norm_g, b_conv_w, c_sinks, d_ln_g,
           d_ln_b, d_ws, d_bias, w_out, norm2_g, ffn_w_gate, ffn_w_up, ffn_conv_w, ffn_w_down, final_norm_g):
    bp, tp, _ = x_prompt.shape
    bs, ts, _ = x_sample.shape
    depth = w_in.shape[0]
    win_buf = cache_win_k.shape[2]
    pos_p = np.arange(tp, dtype=np.int32)
    pos_s = PAST_LEN + np.arange(ts, dtype=np.int32)
    assert win_buf == WINDOW and ts == 4
    tabs_p = _rope_tables(pos_p)
    tabs_s = tuple(jnp.repeat(t, bs, axis=0) for t in _rope_tables(pos_s))
    fng = final_norm_g[None, :]
    ck_all = cache_win_k.reshape(depth, bs, WINDOW, 128)
    cv_all = cache_win_v.reshape(depth, bs, WINDOW, 128)
    P = CARRY_ROWS

    hp = x_prompt
    hs = jnp.swapaxes(x_sample, 0, 1).reshape(ts * bs, D_MODEL)
    outs = {k: [] for k in ('sp', 'ss', 'acp', 'acs', 'bcp', 'bcs', 'ckp', 'cks', 'cvp', 'cvs', 'fcp', 'fcs', 'dv')}
    for l in range(depth):
        lw = _layer_weights(l, norm1_g, w_in, a_conv_w, a_log, a_dt_bias, a_norm_g, b_conv_w, c_sinks, d_ln_g,
                            d_ln_b, d_ws, d_bias, w_out, norm2_g, ffn_w_gate, ffn_w_up, ffn_conv_w, ffn_w_down)
        last = l == depth - 1
        hp, s_bd, acv, bcv, ckn, cvn = _mixer_call(
            hp, lw, tabs_p, jnp.zeros((bp, P, A_QKV_W), F32), jnp.zeros((bp, P, 256), F32),
            jnp.zeros((bp, 256, 256), F32), l)
        hp, fcv = _ffn_call(hp, lw, jnp.zeros((bp, P, D_FF), F32), fng, layer=l, tile=TILE_ROWS, stride=1,
                            final_norm=last)
        outs['sp'].append(_diag_blocks(s_bd))
        outs['acp'].append(acv[:, P - (A_CONV - 1):])
        outs['bcp'].append(bcv[:, P - (B_CONV - 1):])
        outs['ckp'].append(ckn.reshape(bp, WINDOW, C_KV_HEADS, HEAD_DIM))
        outs['cvp'].append(cvn.reshape(bp, WINDOW, C_KV_HEADS, HEAD_DIM))
        outs['fcp'].append(fcv[:, P - (FFN_CONV - 1):])
        hs, ns = _sample_mixer(hs, lw, tabs_s, d_ws[l], d_bias[l], state_delta[l], state_delta_conv[l],
                               state_shortconv[l], ck_all, cv_all, l, bs, ts)
        f0 = jnp.swapaxes(state_ffn_conv[l], 0, 1).reshape(1, (FFN_CONV - 1) * bs, D_FF)
        ys_tm, fcs = _ffn_stream_call(hs[None], lw, f0, fng, layer=l, stride=bs, final_norm=last)
        hs = ys_tm[0]
        outs['ss'].append(ns['a_state'])
        outs['acs'].append(ns['a_conv'])
        outs['bcs'].append(ns['b_conv'])
        outs['cks'].append(ns['c_k'])
        outs['cvs'].append(ns['c_v'])
        outs['fcs'].append(jnp.swapaxes(fcs.reshape(FFN_CONV - 1, bs, D_FF), 0, 1))
        outs['dv'].append(ns['d_v'])
    st = {k: jnp.stack(v) for k, v in outs.items()}
    hs = jnp.swapaxes(hs.reshape(ts, bs, D_MODEL), 0, 1)
    return (hp, hs, st['sp'], st['ss'], st['acp'], st['acs'], st['bcp'], st['bcs'], st['ckp'], st['cks'],
            st['cvp'], st['cvs'], st['fcp'], st['fcs'], st['dv'])
```
